```python
import math
import jax, jax.numpy as jnp
from jax import lax
import numpy as np

D_MODEL = 1024
BATCH = 16
SEQ = 2048
DEPTH = 2

N_EVEN = (DEPTH + 1) // 2
N_ODD = DEPTH // 2

MIX_WIDTH = 1024
A_GROUPS = 4
A_GROUP_DIM = 128
A_WIDTH = A_GROUPS * A_GROUP_DIM
CHUNK = 128
SGU_LN_EPS = 1e-5
B_HEADS = 8
B_HEAD_DIM = 64
B_WIDTH = B_HEADS * B_HEAD_DIM
DECAY_LORA = 64
ICLR_LORA = 64
GATE_LORA = 128
B_IN = 3 * B_WIDTH + DECAY_LORA + ICLR_LORA + GATE_LORA
RWKV_GN_EPS = 64e-5
IN0 = 2 * A_WIDTH + B_IN
C_HEADS = 8
C_KV_HEADS = 2
C_GROUP = C_HEADS // C_KV_HEADS
C_HEAD_DIM = 64
C_WIDTH = C_HEADS * C_HEAD_DIM
KV_WIDTH = C_KV_HEADS * C_HEAD_DIM
N_BRANCH = 3
CMP_LEN = 32
CMP_STRIDE = 16
CMP_HIDDEN = 256
SLC_BLK = 64
SEL_TOPK = 8
WIN = 512
QBLK = 128
ROT_DIM = C_HEAD_DIM // 4
ROPE_THETA = 500000.0
D_WIDTH = 512
CONV_W = 3
IN1 = C_WIDTH + 6 * KV_WIDTH + C_HEADS * N_BRANCH + 3 * D_WIDTH
FFN_DIM = 2816
N_EXPERTS = 8
TOP_K = 2
EXPERT_DIM = 1408
MOE_BLK = 256
NORM_EPS = 1e-6
NEG_INF = -1e30
FORCE_BONUS = 1e6

kernel_name = "hybrid_gmlp_rwkv7_nsa_shortconv_moe"


def _split(p, sizes):
    offs = [int(o) for o in np.cumsum(sizes)[:-1]]
    return jnp.split(p, offs, axis=-1)


def _rms_norm(x, g):
    xf = x.astype(jnp.float32)
    y = xf * lax.rsqrt(jnp.mean(xf * xf, axis=-1, keepdims=True) + NORM_EPS)
    return (y * g.astype(jnp.float32)).astype(x.dtype)


def _masked_softmax(s, mask):
    s = jnp.where(mask, s, NEG_INF)
    m = jnp.max(s, axis=-1, keepdims=True)
    p = jnp.where(mask, jnp.exp(s - m), 0.0)
    return p / jnp.maximum(jnp.sum(p, axis=-1, keepdims=True), 1e-30)


def _rope_partial(x, cos, sin):
    half = ROT_DIM // 2
    c = cos[None, :, None, :].astype(x.dtype)
    s = sin[None, :, None, :].astype(x.dtype)
    x1 = x[..., :half]
    x2 = x[..., half:ROT_DIM]
    return jnp.concatenate([x1 * c - x2 * s, x1 * s + x2 * c, x[..., ROT_DIM:]], axis=-1)


def _chunked_sgu(u, v, ln_g, ln_b, w_s, b_s):
    bsz, t, _ = v.shape
    vf = v.reshape(bsz, t, A_GROUPS, A_GROUP_DIM).astype(jnp.float32)
    mu = jnp.mean(vf, axis=-1, keepdims=True)
    var = jnp.mean(jnp.square(vf - mu), axis=-1, keepdims=True)
    vn = ((vf - mu) * lax.rsqrt(var + SGU_LN_EPS)).reshape(bsz, t, A_WIDTH)
    vn = (vn * ln_g + ln_b).astype(v.dtype)
    vc = vn.reshape(bsz, t // CHUNK, CHUNK, A_GROUPS, A_GROUP_DIM)
    causal = jnp.tril(jnp.ones((CHUNK, CHUNK), dtype=bool))
    wm = jnp.where(causal[None], w_s, 0.0).astype(v.dtype)
    mixed = jnp.einsum('gts,bcsgd->bctgd', wm, vc) + b_s.T[:, :, None]
    return u * mixed.reshape(bsz, t, A_WIDTH)


def _rwkv7_scan(r, w, k, v, kk, a):
    bsz, _, h, n = r.shape

    def step(S, inp):
        r_t, w_t, k_t, v_t, kk_t, a_t = inp
        sa = jnp.einsum('bhij,bhj->bhi', S, kk_t)
        S = (S * w_t[:, :, None, :]
             - sa[..., None] * (kk_t * a_t)[:, :, None, :]
             + v_t[..., None] * k_t[:, :, None, :])
        return S, jnp.einsum('bhij,bhj->bhi', S, r_t)

    xs = tuple(jnp.moveaxis(z, 1, 0) for z in (r, w, k, v, kk, a))
    S0 = jnp.zeros((bsz, h, n, n), jnp.float32)
    _, ys = lax.scan(step, S0, xs)
    return jnp.moveaxis(ys, 0, 1)


def _rwkv7_time_mix(pb, mu, w0, w2, a0, a2, g2, k_k, k_a, r_k, gn_g, gn_b):
    bsz, t, _ = pb.shape
    prev = jnp.pad(pb, ((0, 0), (1, 0), (0, 0)))[:, :-1]
    pb = pb + (prev - pb) * mu
    r, k, v, wl, al, gl = _split(pb, [B_WIDTH, B_WIDTH, B_WIDTH, DECAY_LORA, ICLR_LORA, GATE_LORA])
    w = -jax.nn.softplus(-(w0 + jnp.tanh(wl) @ w2)) - 0.5
    decay = jnp.exp(-jnp.exp(w.astype(jnp.float32)))
    a = jax.nn.sigmoid(a0 + al @ a2)
    g = jax.nn.sigmoid(gl) @ g2

    def heads(z):
        return z.reshape(bsz, t, B_HEADS, B_HEAD_DIM).astype(jnp.float32)

    kk = heads(k * k_k)
    kk = kk / jnp.maximum(jnp.sqrt(jnp.sum(kk * kk, axis=-1, keepdims=True)), 1e-12)
    k = k * (1.0 + (a - 1.0) * k_a)
    rh, kh, vh, ah, wh = heads(r), heads(k), heads(v), heads(a), heads(decay)
    y = _rwkv7_scan(rh, wh, kh, vh, kk, ah)
    ym = jnp.mean(y, axis=-1, keepdims=True)
    yv = jnp.mean(jnp.square(y - ym), axis=-1, keepdims=True)
    y = ((y - ym) * lax.rsqrt(yv + RWKV_GN_EPS)).reshape(bsz, t, B_WIDTH)
    y = y * gn_g.astype(jnp.float32) + gn_b.astype(jnp.float32)
    bonus = jnp.sum(rh * kh * r_k.astype(jnp.float32), axis=-1, keepdims=True) * vh
    y = (y + bonus.reshape(bsz, t, B_WIDTH)) * g.astype(jnp.float32)
    return y.astype(pb.dtype)


def _nsa(q, kc, vc, ks, vs, kw, vw, gl, pos_k, pos_v, ck1, ck2, cv1, cv2, cos, sin):
    bsz, t, _ = q.shape
    f32 = jnp.float32
    scale = C_HEAD_DIM ** -0.5
    q = q.reshape(bsz, t, C_HEADS, C_HEAD_DIM)
    kc, vc, ks, vs, kw, vw = (z.reshape(bsz, t, C_KV_HEADS, C_HEAD_DIM) for z in (kc, vc, ks, vs, kw, vw))

    n_cmp = (t - CMP_LEN) // CMP_STRIDE + 1
    idx = np.arange(n_cmp)[:, None] * CMP_STRIDE + np.arange(CMP_LEN)[None, :]

    def compress(z, pos, w1, w2):
        zb = z[:, idx] + pos[None, None, :, None, :]
        zb = zb.transpose(0, 3, 1, 2, 4).reshape(bsz, C_KV_HEADS, n_cmp, CMP_LEN * C_HEAD_DIM)
        return jax.nn.gelu(zb @ w1) @ w2

    k_cmp = compress(kc, pos_k, ck1, ck2)
    v_cmp = compress(vc, pos_v, cv1, cv2).astype(f32)
    cmp_end = jnp.arange(n_cmp) * CMP_STRIDE + CMP_LEN - 1
    n_slc = t // SLC_BLK
    ci = jnp.arange(n_cmp)[:, None] * CMP_STRIDE
    sj = jnp.arange(n_slc)[None, :] * SLC_BLK
    overlap = ((ci < sj + SLC_BLK) & (ci + CMP_LEN > sj)).astype(f32)
    n_sel = min(SEL_TOPK, n_slc)

    q_rot = _rope_partial(q, cos, sin)
    ks = _rope_partial(ks, cos, sin)
    kw = _rope_partial(kw, cos, sin)
    n_qb = t // QBLK

    def to_qblocks(z):
        return z.reshape(bsz, n_qb, QBLK, C_KV_HEADS, C_GROUP, z.shape[-1]).transpose(1, 0, 3, 4, 2, 5)

    qr_b = to_qblocks(q_rot)
    qn_b = to_qblocks(q)
    g_b = to_qblocks(jax.nn.sigmoid(gl.reshape(bsz, t, C_HEADS, N_BRANCH)))
    ks_blk = ks.reshape(bsz, n_slc, SLC_BLK, C_KV_HEADS, C_HEAD_DIM).transpose(0, 3, 1, 2, 4)
    vs_blk = vs.reshape(bsz, n_slc, SLC_BLK, C_KV_HEADS, C_HEAD_DIM).transpose(0, 3, 1, 2, 4)
    pad = ((0, 0), (0, 0), (WIN, 0), (0, 0))
    kw_pad = jnp.pad(kw.transpose(0, 2, 1, 3), pad)
    vw_pad = jnp.pad(vw.transpose(0, 2, 1, 3), pad)
    b_ix = jnp.arange(bsz)[:, None, None, None]
    h_ix = jnp.arange(C_KV_HEADS)[None, :, None, None]
    jb = jnp.arange(n_slc)

    def body(args):
        qr, qn, gt, blk = args
        s0 = blk * QBLK
        t_pos = s0 + jnp.arange(QBLK)
        sc = jnp.einsum('bhgqd,bhnd->bhgqn', qn, k_cmp).astype(f32) * scale
        pc = _masked_softmax(sc, cmp_end[None, :] <= t_pos[:, None])
        o_c = jnp.einsum('bhgqn,bhnd->bhgqd', pc, v_cmp)
        imp = jnp.einsum('bhgqn,nj->bhqj', pc, overlap)
        cur = t_pos // SLC_BLK
        forced = (jb[None] == 0) | (jb[None] == cur[:, None]) | (jb[None] == cur[:, None] - 1)
        imp = jnp.where(jb[None] * SLC_BLK <= t_pos[:, None],
                        imp + jnp.where(forced, FORCE_BONUS, 0.0), NEG_INF)
        _, sel = lax.top_k(imp, n_sel)
        kg = ks_blk[b_ix, h_ix, sel]
        vg = vs_blk[b_ix, h_ix, sel]
        tok = sel[..., None] * SLC_BLK + jnp.arange(SLC_BLK)
        ms = (tok <= t_pos[:, None, None]).reshape(bsz, C_KV_HEADS, 1, QBLK, n_sel * SLC_BLK)
        ss = jnp.einsum('bhgqd,bhqnld->bhgqnl', qr, kg).astype(f32)
        ps = _masked_softmax(ss.reshape(bsz, C_KV_HEADS, C_GROUP, QBLK, n_sel * SLC_BLK) * scale, ms)
        o_s = jnp.einsum('bhgqm,bhqmd->bhgqd', ps,
                         vg.reshape(bsz, C_KV_HEADS, QBLK, n_sel * SLC_BLK, C_HEAD_DIM).astype(f32))
        kwb = lax.dynamic_slice_in_dim(kw_pad, s0, QBLK + WIN, axis=2)
        vwb = lax.dynamic_slice_in_dim(vw_pad, s0, QBLK + WIN, axis=2)
        key_pos = s0 - WIN + jnp.arange(QBLK + WIN)
        diff = t_pos[:, None] - key_pos[None, :]
        mw = (diff >= 0) & (diff < WIN) & (key_pos[None, :] >= 0)
        sw = jnp.einsum('bhgqd,bhkd->bhgqk', qr, kwb).astype(f32) * scale
        pw = _masked_softmax(sw, mw)
        o_w = jnp.einsum('bhgqk,bhkd->bhgqd', pw, vwb.astype(f32))
        gt = gt.astype(f32)
        return (gt[..., 0:1] * o_c + gt[..., 1:2] * o_s + gt[..., 2:3] * o_w).astype(qr.dtype)

    out = lax.map(body, (qr_b, qn_b, g_b, jnp.arange(n_qb)))
    return out.transpose(1, 0, 4, 2, 3, 5).reshape(bsz, t, C_WIDTH)


def _short_conv(bg, cg, hd, conv_w):
    z = cg * hd
    y = lax.conv_general_dilated(z, conv_w[:, None, :].astype(z.dtype), window_strides=(1,),
                                 padding=[(CONV_W - 1, 0)],
                                 dimension_numbers=('NWC', 'WIO', 'NWC'),
                                 feature_group_count=D_WIDTH)
    return bg * y


def _swiglu(h, w_gate, w_up, w_down):
    return (jax.nn.silu(h @ w_gate) * (h @ w_up)) @ w_down


def _moe_swiglu(h, w_router, b_router, w_gate, w_up, w_down):
    bsz, t, d = h.shape
    n = bsz * t
    nk = n * TOP_K
    xf = h.reshape(n, d)
    logits = (xf @ w_router).astype(jnp.float32) + b_router.astype(jnp.float32)
    top_val, top_idx = lax.top_k(logits, TOP_K)
    gates = jax.nn.softmax(top_val, axis=-1)
    e_flat = top_idx.reshape(-1)
    tok_flat = jnp.repeat(jnp.arange(n, dtype=jnp.int32), TOP_K)
    g_flat = gates.reshape(-1)
    order = jnp.argsort(e_flat)
    e_sorted = e_flat[order]
    counts = jax.ops.segment_sum(jnp.ones_like(e_flat), e_flat, num_segments=N_EXPERTS)
    padded = ((counts + MOE_BLK - 1) // MOE_BLK) * MOE_BLK
    p_end = jnp.cumsum(padded)
    p_start = p_end - padded
    u_start = jnp.cumsum(counts) - counts
    dest = p_start[e_sorted] + jnp.arange(nk) - u_start[e_sorted]
    n_blk = (nk + MOE_BLK - 1) // MOE_BLK + N_EXPERTS
    p_rows = n_blk * MOE_BLK
    row_tok = jnp.zeros((p_rows,), jnp.int32).at[dest].set(tok_flat[order])
    row_w = jnp.zeros((p_rows,), jnp.float32).at[dest].set(g_flat[order])
    blk_e = jnp.minimum(jnp.searchsorted(p_end, jnp.arange(n_blk) * MOE_BLK, side='right'), N_EXPERTS - 1)
    xs = xf[row_tok].reshape(n_blk, MOE_BLK, d)

    def expert_block(args):
        xb, e = args
        return (jax.nn.silu(xb @ w_gate[e]) * (xb @ w_up[e])) @ w_down[e]

    yb = lax.map(expert_block, (xs, blk_e)).reshape(p_rows, d)
    out = jnp.zeros((n, d), h.dtype).at[row_tok].add(yb * row_w[:, None].astype(yb.dtype))
    return out.reshape(bsz, t, d)


def setup_inputs(seed: int = 0) -> dict:
    key = jax.random.key(seed)
    keys = iter(jax.random.split(key, 48))

    def nrm(shape, scale):
        return scale * jax.random.normal(next(keys), shape, jnp.float32)

    def gain(shape):
        return 1.0 + nrm(shape, 0.1)

    def unif(shape, lo, hi):
        return jax.random.uniform(next(keys), shape, jnp.float32, lo, hi)

    E, O, D = N_EVEN, N_ODD, D_MODEL
    return {
        "x": nrm((BATCH, SEQ, D), 1.0),
        "e_norm_mix": gain((E, D)),
        "e_w_in": nrm((E, D, IN0), D ** -0.5),
        "sgu_ln_g": gain((E, A_WIDTH)),
        "sgu_ln_b": nrm((E, A_WIDTH), 0.02),
        "sgu_w": nrm((E, A_GROUPS, CHUNK, CHUNK), CHUNK ** -0.5),
        "sgu_b": gain((E, A_GROUPS, CHUNK)),
        "rwkv_mu": unif((E, B_IN), 0.0, 1.0),
        "rwkv_w0": unif((E, B_WIDTH), -2.5, 0.5),
        "rwkv_w2": nrm((E, DECAY_LORA, B_WIDTH), 0.1 * DECAY_LORA ** -0.5),
        "rwkv_a0": nrm((E, B_WIDTH), 0.5),
        "rwkv_a2": nrm((E, ICLR_LORA, B_WIDTH), 0.5 * ICLR_LORA ** -0.5),
        "rwkv_g2": nrm((E, GATE_LORA, B_WIDTH), GATE_LORA ** -0.5),
        "rwkv_k_k": 0.85 + nrm((E, B_WIDTH), 0.1),
        "rwkv_k_a": gain((E, B_WIDTH)),
        "rwkv_r_k": nrm((E, B_HEADS, B_HEAD_DIM), 0.1),
        "rwkv_gn_g": gain((E, B_WIDTH)),
        "rwkv_gn_b": nrm((E, B_WIDTH), 0.02),
        "e_w_out": nrm((E, MIX_WIDTH, D), MIX_WIDTH ** -0.5),
        "e_norm_ffn": gain((E, D)),
        "ffn_w_gate": nrm((E, D, FFN_DIM), D ** -0.5),
        "ffn_w_up": nrm((E, D, FFN_DIM), D ** -0.5),
        "ffn_w_down": nrm((E, FFN_DIM, D), FFN_DIM ** -0.5),
        "o_norm_mix": gain((O, D)),
        "o_w_in": nrm((O, D, IN1), D ** -0.5),
        "nsa_cmp_pos_k": nrm((O, CMP_LEN, C_HEAD_DIM), 0.1),
        "nsa_cmp_pos_v": nrm((O, CMP_LEN, C_HEAD_DIM), 0.1),
        "nsa_cmp_k_w1": nrm((O, CMP_LEN * C_HEAD_DIM, CMP_HIDDEN), (CMP_LEN * C_HEAD_DIM) ** -0.5),
        "nsa_cmp_k_w2": nrm((O, CMP_HIDDEN, C_HEAD_DIM), CMP_HIDDEN ** -0.5),
        "nsa_cmp_v_w1": nrm((O, CMP_LEN * C_HEAD_DIM, CMP_HIDDEN), (CMP_LEN * C_HEAD_DIM) ** -0.5),
        "nsa_cmp_v_w2": nrm((O, CMP_HIDDEN, C_HEAD_DIM), CMP_HIDDEN ** -0.5),
        "conv_w": nrm((O, CONV_W, D_WIDTH), CONV_W ** -0.5),
        "o_w_out": nrm((O, MIX_WIDTH, D), MIX_WIDTH ** -0.5),
        "o_norm_ffn": gain((O, D)),
        "moe_router": nrm((O, D, N_EXPERTS), D ** -0.5),
        "moe_router_b": nrm((O, N_EXPERTS), 0.01),
        "moe_w_gate": nrm((O, N_EXPERTS, D, EXPERT_DIM), D ** -0.5),
        "moe_w_up": nrm((O, N_EXPERTS, D, EXPERT_DIM), D ** -0.5),
        "moe_w_down": nrm((O, N_EXPERTS, EXPERT_DIM, D), EXPERT_DIM ** -0.5),
        "final_norm": gain((D,)),
    }


def reference(x, e_norm_mix, e_w_in, sgu_ln_g, sgu_ln_b, sgu_w, sgu_b, rwkv_mu, rwkv_w0, rwkv_w2,
              rwkv_a0, rwkv_a2, rwkv_g2, rwkv_k_k, rwkv_k_a, rwkv_r_k, rwkv_gn_g, rwkv_gn_b, e_w_out,
              e_norm_ffn, ffn_w_gate, ffn_w_up, ffn_w_down, o_norm_mix, o_w_in, nsa_cmp_pos_k,
              nsa_cmp_pos_v, nsa_cmp_k_w1, nsa_cmp_k_w2, nsa_cmp_v_w1, nsa_cmp_v_w2, conv_w, o_w_out,
              o_norm_ffn, moe_router, moe_router_b, moe_w_gate, moe_w_up, moe_w_down, final_norm):
    t = x.shape[1]
    pos = jnp.arange(t, dtype=jnp.float32)
    inv_freq = ROPE_THETA ** (-jnp.arange(0, ROT_DIM, 2, dtype=jnp.float32) / ROT_DIM)
    ang = pos[:, None] * inv_freq[None, :]
    cos, sin = jnp.cos(ang), jnp.sin(ang)

    h = x
    for layer in range(DEPTH):
        i = layer // 2
        if layer % 2 == 0:
            hn = _rms_norm(h, e_norm_mix[i])
            p = hn @ e_w_in[i]
            pu, pv, pb = _split(p, [A_WIDTH, A_WIDTH, B_IN])
            ya = _chunked_sgu(jax.nn.gelu(pu), jax.nn.gelu(pv), sgu_ln_g[i], sgu_ln_b[i], sgu_w[i], sgu_b[i])
            yb = _rwkv7_time_mix(pb, rwkv_mu[i], rwkv_w0[i], rwkv_w2[i], rwkv_a0[i], rwkv_a2[i],
                                 rwkv_g2[i], rwkv_k_k[i], rwkv_k_a[i], rwkv_r_k[i],
                                 rwkv_gn_g[i], rwkv_gn_b[i])
            h = h + jnp.concatenate([ya.astype(h.dtype), yb.astype(h.dtype)], axis=-1) @ e_w_out[i]
            hn = _rms_norm(h, e_norm_ffn[i])
            h = h + _swiglu(hn, ffn_w_gate[i], ffn_w_up[i], ffn_w_down[i])
        else:
            hn = _rms_norm(h, o_norm_mix[i])
            p = hn @ o_w_in[i]
            q, kc, vc, ks, vs, kw, vw, gl, bg, cg, hd = _split(
                p, [C_WIDTH] + [KV_WIDTH] * 6 + [C_HEADS * N_BRANCH] + [D_WIDTH] * 3)
            yc = _nsa(q, kc, vc, ks, vs, kw, vw, gl, nsa_cmp_pos_k[i], nsa_cmp_pos_v[i],
                      nsa_cmp_k_w1[i], nsa_cmp_k_w2[i], nsa_cmp_v_w1[i], nsa_cmp_v_w2[i], cos, sin)
            yd = _short_conv(bg, cg, hd, conv_w[i])
            h = h + jnp.concatenate([yc.astype(h.dtype), yd.astype(h.dtype)], axis=-1) @ o_w_out[i]
            hn = _rms_norm(h, o_norm_ffn[i])
            h = h + _moe_swiglu(hn, moe_router[i], moe_router_b[i], moe_w_gate[i], moe_w_up[i], moe_w_down[i])
    return _rms_norm(h, final_norm)
```

```python
import functools

import jax
import jax.numpy as jnp
import numpy as np
from jax import lax
from jax.experimental import pallas as pl
from jax.experimental.pallas import tpu as pltpu

F32 = jnp.float32
BF16 = jnp.bfloat16

D_MODEL = 1024
A_GROUPS = 4
A_GROUP_DIM = 128
A_WIDTH = A_GROUPS * A_GROUP_DIM
CHUNK = 128
SGU_LN_EPS = 1e-5
B_HEADS = 8
B_HEAD_DIM = 64
B_WIDTH = B_HEADS * B_HEAD_DIM
DECAY_LORA = 64
ICLR_LORA = 64
GATE_LORA = 128
B_IN = 3 * B_WIDTH + DECAY_LORA + ICLR_LORA + GATE_LORA
RWKV_GN_EPS = 64e-5
C_HEADS = 8
C_KV_HEADS = 2
C_GROUP = C_HEADS // C_KV_HEADS
C_HEAD_DIM = 64
C_WIDTH = C_HEADS * C_HEAD_DIM
KV_WIDTH = C_KV_HEADS * C_HEAD_DIM
N_BRANCH = 3
CMP_LEN = 32
CMP_STRIDE = 16
CMP_HIDDEN = 256
SLC_BLK = 64
SEL_TOPK = 8
WIN = 512
QBLK = 128
ROT_DIM = C_HEAD_DIM // 4
ROPE_THETA = 500000.0
D_WIDTH = 512
CONV_W = 3
FFN_DIM = 2816
N_EXPERTS = 8
TOP_K = 2
EXPERT_DIM = 1408
NORM_EPS = 1e-6
NEG_INF = -1e30
FORCE_BONUS = 1e6

V7X_LANES = 128
V7X_SUBLANES = 8
V7X_VMEM_LIMIT = 56 * 1024 * 1024

MOE_ROWS = 256
DMA_ROWS = 512


def _params(*sem):
    return pltpu.CompilerParams(dimension_semantics=sem, vmem_limit_bytes=V7X_VMEM_LIMIT)


def _const_spec(shape):
    zeros = (0,) * len(shape)
    return pl.BlockSpec(shape, lambda *_: zeros)


def _rows_spec(tm, width):
    return pl.BlockSpec((tm, width), lambda i: (i, 0))


def _rms(x, g):
    return x * lax.rsqrt(jnp.mean(x * x, axis=-1, keepdims=True) + NORM_EPS) * g


def _gelu(x):
    return x * (0.5 * (1.0 + jnp.tanh(0.7978845608028654 * (x + 0.044715 * (x * x * x)))))


def _sigmoid(x):
    return 1.0 / (1.0 + jnp.exp(-x))


def _dot(a, b):
    return jnp.dot(a.astype(BF16), b.astype(BF16), preferred_element_type=F32)


def _dot_nt(a, b):
    return lax.dot_general(a.astype(BF16), b.astype(BF16), (((1,), (1,)), ((), ())),
                           preferred_element_type=F32)


def _split_bf16(a):
    hi = a.astype(BF16)
    lo = (a - hi.astype(F32)).astype(BF16)
    return hi, lo


def _dot_f32(a, b):
    ah, al = _split_bf16(a)
    bh, bl = _split_bf16(b)
    d = functools.partial(jnp.dot, preferred_element_type=F32)
    return d(ah, bh) + (d(al, bh) + d(ah, bl))


def _masked_softmax(s, mask):
    s = jnp.where(mask, s, NEG_INF)
    m = jnp.max(s, axis=-1, keepdims=True)
    p = jnp.where(mask, jnp.exp(s - m), 0.0)
    return p / jnp.maximum(jnp.sum(p, axis=-1, keepdims=True), 1e-30)


def _norm_matmul_body(x_ref, g_ref, *refs):
    k = len(refs) // 2
    xn = _rms(x_ref[...], g_ref[...]).astype(BF16)
    for w_ref, o_ref in zip(refs[:k], refs[k:]):
        o_ref[...] = jnp.dot(xn, w_ref[...], preferred_element_type=F32).astype(o_ref.dtype)


def _norm_matmul(x, gain, ws, out_dtypes, tm=256):
    n, d = x.shape
    return pl.pallas_call(
        _norm_matmul_body,
        grid=(n // tm,),
        in_specs=[_rows_spec(tm, d), _const_spec((1, d))] + [_const_spec(w.shape) for w in ws],
        out_specs=[_rows_spec(tm, w.shape[1]) for w in ws],
        out_shape=[jax.ShapeDtypeStruct((n, w.shape[1]), dt) for w, dt in zip(ws, out_dtypes)],
        compiler_params=_params("parallel"),
        name="norm_matmul",
    )(x, gain.reshape(1, d), *ws)


def _sgu_body(p_ref, lng_ref, lnb_ref, w_ref, b_ref, o_ref, *, n_chunks):
    row = lax.broadcasted_iota(jnp.int32, (CHUNK, CHUNK), 0)
    col = lax.broadcasted_iota(jnp.int32, (CHUNK, CHUNK), 1)
    causal = col <= row
    for c in range(n_chunks):
        rows = slice(c * CHUNK, (c + 1) * CHUNK)
        u = _gelu(p_ref[rows, :A_WIDTH])
        v = _gelu(p_ref[rows, A_WIDTH:])
        outs = []
        for g in range(A_GROUPS):
            cols = slice(g * A_GROUP_DIM, (g + 1) * A_GROUP_DIM)
            vg = v[:, cols]
            mu = jnp.mean(vg, axis=-1, keepdims=True)
            dv = vg - mu
            var = jnp.mean(dv * dv, axis=-1, keepdims=True)
            vn = dv * lax.rsqrt(var + SGU_LN_EPS) * lng_ref[:, cols] + lnb_ref[:, cols]
            wm = jnp.where(causal, w_ref[g], 0.0)
            mixed = _dot(wm, vn) + b_ref[:, g:g + 1]
            outs.append(u[:, cols] * mixed)
        o_ref[rows, :] = jnp.concatenate(outs, axis=1).astype(o_ref.dtype)


def _sgu(p_uv, ln_g, ln_b, w_s, b_s, tm=512):
    n = p_uv.shape[0]
    return pl.pallas_call(
        functools.partial(_sgu_body, n_chunks=tm // CHUNK),
        grid=(n // tm,),
        in_specs=[_rows_spec(tm, 2 * A_WIDTH), _const_spec((1, A_WIDTH)), _const_spec((1, A_WIDTH)),
                  _const_spec((A_GROUPS, CHUNK, CHUNK)), _const_spec((CHUNK, A_GROUPS))],
        out_specs=_rows_spec(tm, A_WIDTH),
        out_shape=jax.ShapeDtypeStruct((n, A_WIDTH), BF16),
        compiler_params=_params("parallel"),
        name="sgu",
    )(p_uv, ln_g.reshape(1, -1), ln_b.reshape(1, -1), w_s, b_s.T)


def _softplus(x):
    return jnp.maximum(x, 0.0) + jnp.log(1.0 + jnp.exp(-jnp.abs(x)))


def _rwkv_prep_body(pb_ref, prev_ref, mu_ref, w0_ref, w2_ref, a0_ref, a2_ref, g2_ref,
                    r_o, w_o, k_o, v_o, a_o, g_o, *, tiles_per_seq):
    tm = pb_ref.shape[0]
    x = pb_ref[...]
    first = (pl.program_id(0) % tiles_per_seq) == 0
    prev_row = jnp.where(first, 0.0, prev_ref[V7X_SUBLANES - 1:V7X_SUBLANES, :])
    rowid = lax.broadcasted_iota(jnp.int32, (tm, 1), 0)
    shifted = jnp.where(rowid == 0, prev_row, pltpu.roll(x, 1, axis=0))
    xm = x + (shifted - x) * mu_ref[...]
    o = 3 * B_WIDTH
    wl = xm[:, o:o + DECAY_LORA]
    al = xm[:, o + DECAY_LORA:o + DECAY_LORA + ICLR_LORA]
    gl = xm[:, o + DECAY_LORA + ICLR_LORA:]
    w = -_softplus(-(w0_ref[...] + _dot(jnp.tanh(wl), w2_ref[...]))) - 0.5
    r_o[...] = xm[:, :B_WIDTH]
    w_o[...] = jnp.exp(-jnp.exp(w))
    k_o[...] = xm[:, B_WIDTH:2 * B_WIDTH]
    v_o[...] = xm[:, 2 * B_WIDTH:3 * B_WIDTH]
    a_o[...] = _sigmoid(a0_ref[...] + _dot(al, a2_ref[...]))
    g_o[...] = _dot(_sigmoid(gl), g2_ref[...])


def _rwkv_prep(p_b, seq, mu, w0, w2, a0, a2, g2, tm=256):
    n = p_b.shape[0]
    per8 = tm // V7X_SUBLANES
    outs = [jax.ShapeDtypeStruct((n, B_WIDTH), F32)] * 6
    return pl.pallas_call(
        functools.partial(_rwkv_prep_body, tiles_per_seq=seq // tm),
        grid=(n // tm,),
        in_specs=[_rows_spec(tm, B_IN),
                  pl.BlockSpec((V7X_SUBLANES, B_IN), lambda i: (jnp.maximum(i * per8 - 1, 0), 0)),
                  _const_spec((1, B_IN)), _const_spec((1, B_WIDTH)), _const_spec((DECAY_LORA, B_WIDTH)),
                  _const_spec((1, B_WIDTH)), _const_spec((ICLR_LORA, B_WIDTH)),
                  _const_spec((GATE_LORA, B_WIDTH))],
        out_specs=[_rows_spec(tm, B_WIDTH)] * 6,
        out_shape=outs,
        compiler_params=_params("parallel"),
        name="rwkv_prep",
    )(p_b, p_b, mu.reshape(1, -1), w0.reshape(1, -1), w2, a0.reshape(1, -1), a2, g2)


def _rwkv_scan_body(r_ref, w_ref, k0_ref, v_ref, a_ref, kkp_ref, kap_ref, rk_ref, gng_ref, gnb_ref,
                    y_ref, s_ref, kkn_ref, ka_ref, km_ref):
    tt, n = r_ref.shape[0], r_ref.shape[1]

    @pl.when(pl.program_id(0) == 0)
    def _():
        s_ref[...] = jnp.zeros_like(s_ref)

    k0 = k0_ref[...]
    a = a_ref[...]
    kk = k0 * kkp_ref[...][None]
    kkn = kk / jnp.maximum(jnp.sqrt(jnp.sum(kk * kk, axis=1, keepdims=True)), 1e-12)
    kkn_ref[...] = kkn
    ka_ref[...] = kkn * a
    km_ref[...] = k0 * (1.0 + (a - 1.0) * kap_ref[...][None])

    zero = jnp.zeros((n, r_ref.shape[2]), F32)

    def sa_init(j, acc):
        return acc + s_ref[j] * kkn_ref[0, pl.ds(j, 1), :]

    sa0 = lax.fori_loop(0, n, sa_init, zero)

    def step(t, sa):
        v_t = v_ref[t]
        tn = jnp.minimum(t + 1, tt - 1)

        def jbody(j, carry):
            y, san = carry
            row = pl.ds(j, 1)
            sn = s_ref[j] * w_ref[t, row, :] + (v_t * km_ref[t, row, :] - sa * ka_ref[t, row, :])
            s_ref[j] = sn
            return y + sn * r_ref[t, row, :], san + sn * kkn_ref[tn, row, :]

        y, san = lax.fori_loop(0, n, jbody, (zero, zero), unroll=8)
        y_ref[t] = y
        return san

    lax.fori_loop(0, tt, step, sa0)

    y = y_ref[...]
    ym = jnp.mean(y, axis=1, keepdims=True)
    dy = y - ym
    yv = jnp.mean(dy * dy, axis=1, keepdims=True)
    yn = dy * lax.rsqrt(yv + RWKV_GN_EPS) * gng_ref[...][None] + gnb_ref[...][None]
    bonus = jnp.sum(r_ref[...] * km_ref[...] * rk_ref[...][None], axis=1, keepdims=True) * v_ref[...]
    y_ref[...] = yn + bonus


def _rwkv_scan(r, w, k0, v, a, k_k, k_a, r_k, gn_g, gn_b, bsz, seq, tt=32):
    n = B_HEAD_DIM
    lanes = bsz * B_HEADS

    def lane_param(p):
        return jnp.tile(p.reshape(B_HEADS, n).T, (1, bsz))

    seq_spec = pl.BlockSpec((tt, n, lanes), lambda c: (c, 0, 0))
    par_spec = _const_spec((n, lanes))
    return pl.pallas_call(
        _rwkv_scan_body,
        grid=(seq // tt,),
        in_specs=[seq_spec] * 5 + [par_spec] * 5,
        out_specs=seq_spec,
        out_shape=jax.ShapeDtypeStruct((seq, n, lanes), F32),
        scratch_shapes=[pltpu.VMEM((n, n, lanes), F32)] + [pltpu.VMEM((tt, n, lanes), F32)] * 3,
        compiler_params=_params("arbitrary"),
        name="rwkv_scan",
    )(r, w, k0, v, a, lane_param(k_k), lane_param(k_a), lane_param(r_k.reshape(-1)),
      lane_param(gn_g), lane_param(gn_b))


def _to_scan_layout(z, bsz, seq):
    return z.reshape(bsz, seq, B_HEADS, B_HEAD_DIM).transpose(1, 3, 0, 2).reshape(
        seq, B_HEAD_DIM, bsz * B_HEADS)


def _from_scan_layout(y, bsz, seq):
    return y.reshape(seq, B_HEAD_DIM, bsz, B_HEADS).transpose(2, 0, 3, 1).reshape(
        bsz * seq, B_WIDTH)


def _out_proj0_body(h_ref, ya_ref, ys_ref, g_ref, w_ref, o_ref):
    yb = ys_ref[...] * g_ref[...]
    o_ref[...] = h_ref[...] + (_dot(ya_ref[...], w_ref[:A_WIDTH, :]) + _dot(yb, w_ref[A_WIDTH:, :]))


def _out_proj0(h, ya, ys, g, w_out, tm=512):
    n = h.shape[0]
    return pl.pallas_call(
        _out_proj0_body,
        grid=(n // tm,),
        in_specs=[_rows_spec(tm, D_MODEL), _rows_spec(tm, A_WIDTH), _rows_spec(tm, B_WIDTH),
                  _rows_spec(tm, B_WIDTH), _const_spec(w_out.shape)],
        out_specs=_rows_spec(tm, D_MODEL),
        out_shape=jax.ShapeDtypeStruct((n, D_MODEL), F32),
        compiler_params=_params("parallel"),
        name="out_proj0",
    )(h, ya, ys, g, w_out)


def _out_proj1_body(h_ref, yc_ref, bcd_ref, prev_ref, cw_ref, w_ref, o_ref, *, tiles_per_seq):
    tm = h_ref.shape[0]
    first = (pl.program_id(0) % tiles_per_seq) == 0
    z = bcd_ref[:, D_WIDTH:2 * D_WIDTH] * bcd_ref[:, 2 * D_WIDTH:]
    zp = jnp.where(first, 0.0, prev_ref[:, D_WIDTH:2 * D_WIDTH] * prev_ref[:, 2 * D_WIDTH:])
    rowid = lax.broadcasted_iota(jnp.int32, (tm, 1), 0)
    z1 = jnp.where(rowid == 0, zp[7:8, :], pltpu.roll(z, 1, axis=0))
    z2 = pltpu.roll(z, 2, axis=0)
    z2 = jnp.where(rowid == 0, zp[6:7, :], jnp.where(rowid == 1, zp[7:8, :], z2))
    y = cw_ref[0:1, :] * z2 + cw_ref[1:2, :] * z1 + cw_ref[2:3, :] * z
    yd = bcd_ref[:, :D_WIDTH] * y
    o_ref[...] = h_ref[...] + (_dot(yc_ref[...], w_ref[:C_WIDTH, :]) + _dot(yd, w_ref[C_WIDTH:, :]))


def _out_proj1(h, yc, bcd, conv_w, w_out, seq, tm=512):
    n = h.shape[0]
    per8 = tm // V7X_SUBLANES
    return pl.pallas_call(
        functools.partial(_out_proj1_body, tiles_per_seq=seq // tm),
        grid=(n // tm,),
        in_specs=[_rows_spec(tm, D_MODEL), _rows_spec(tm, C_WIDTH), _rows_spec(tm, 3 * D_WIDTH),
                  pl.BlockSpec((V7X_SUBLANES, 3 * D_WIDTH), lambda i: (jnp.maximum(i * per8 - 1, 0), 0)),
                  _const_spec((CONV_W, D_WIDTH)), _const_spec(w_out.shape)],
        out_specs=_rows_spec(tm, D_MODEL),
        out_shape=jax.ShapeDtypeStruct((n, D_MODEL), F32),
        compiler_params=_params("parallel"),
        name="out_proj1",
    )(h, yc, bcd, bcd, conv_w, w_out)


def _ffn_body(h_ref, g_ref, wg_ref, wu_ref, wd_ref, o_ref):
    h = h_ref[...]
    hn = _rms(h, g_ref[...]).astype(BF16)
    gate = jnp.dot(hn, wg_ref[...], preferred_element_type=F32)
    up = jnp.dot(hn, wu_ref[...], preferred_element_type=F32)
    act = (gate * _sigmoid(gate)) * up
    o_ref[...] = h + _dot(act, wd_ref[...])


def _ffn(h, gain, wg, wu, wd, tm=256):
    n = h.shape[0]
    return pl.pallas_call(
        _ffn_body,
        grid=(n // tm,),
        in_specs=[_rows_spec(tm, D_MODEL), _const_spec((1, D_MODEL)), _const_spec(wg.shape),
                  _const_spec(wu.shape), _const_spec(wd.shape)],
        out_specs=_rows_spec(tm, D_MODEL),
        out_shape=jax.ShapeDtypeStruct((n, D_MODEL), F32),
        compiler_params=_params("parallel"),
        name="ffn",
    )(h, gain.reshape(1, -1), wg, wu, wd)


def _rope(x, c, s_up, s_dn):
    half = ROT_DIM // 2
    return x * c + pltpu.roll(x, half, axis=1) * s_up + pltpu.roll(x, V7X_LANES - half, axis=1) * s_dn


def _in_proj1_body(x_ref, g_ref, c_ref, su_ref, sd_ref, wq_ref, wkc_ref, wkv_ref, wgl_ref, wbcd_ref,
                   qn_o, qr_o, kc_o, kv_o, gl_o, bcd_o):
    xn = _rms(x_ref[...], g_ref[...]).astype(BF16)
    d = functools.partial(jnp.dot, preferred_element_type=F32)
    c, su, sd = c_ref[...], su_ref[...], sd_ref[...]
    q = d(xn, wq_ref[...])
    qn_o[...] = q.astype(BF16)
    qr_o[...] = jnp.concatenate(
        [_rope(q[:, i * V7X_LANES:(i + 1) * V7X_LANES], c, su, sd) for i in range(C_WIDTH // V7X_LANES)],
        axis=1).astype(BF16)
    kc_o[...] = d(xn, wkc_ref[...])
    kv = d(xn, wkv_ref[...])
    parts = [kv[:, i * KV_WIDTH:(i + 1) * KV_WIDTH] for i in range(4)]
    parts[0] = _rope(parts[0], c, su, sd)
    parts[2] = _rope(parts[2], c, su, sd)
    kv_o[...] = jnp.concatenate(parts, axis=1).astype(BF16)
    gl_o[...] = d(xn, wgl_ref[...])
    bcd_o[...] = d(xn, wbcd_ref[...])


def _rope_tables(seq):
    half = ROT_DIM // 2
    pos = jnp.arange(seq, dtype=F32)
    inv_freq = ROPE_THETA ** (-jnp.arange(0, ROT_DIM, 2, dtype=F32) / ROT_DIM)
    ang = pos[:, None] * inv_freq[None, :]
    cos, sin = jnp.cos(ang), jnp.sin(ang)
    pad = jnp.zeros((seq, C_HEAD_DIM - ROT_DIM), F32)
    zeros = jnp.zeros((seq, half), F32)
    c = jnp.concatenate([cos, cos, pad + 1.0], axis=1)
    s_up = jnp.concatenate([zeros, sin, pad], axis=1)
    s_dn = jnp.concatenate([-sin, zeros, pad], axis=1)
    rep = V7X_LANES // C_HEAD_DIM
    return tuple(jnp.tile(z, (1, rep)) for z in (c, s_up, s_dn))


def _in_proj1(x, gain, w_in, seq, tm=256):
    n, d = x.shape
    o = np.cumsum([0, C_WIDTH] + [KV_WIDTH] * 6 + [C_HEADS * N_BRANCH] + [D_WIDTH] * 3)
    wq = w_in[:, o[0]:o[1]].astype(BF16)
    wkc = w_in[:, o[1]:o[3]].astype(BF16)
    wkv = w_in[:, o[3]:o[7]].astype(BF16)
    wgl = jnp.pad(w_in[:, o[7]:o[8]], ((0, 0), (0, V7X_LANES - C_HEADS * N_BRANCH))).astype(BF16)
    wbcd = w_in[:, o[8]:o[11]].astype(BF16)
    ws = [wq, wkc, wkv, wgl, wbcd]
    dts = [BF16, BF16, F32, BF16, F32, F32]
    widths = [C_WIDTH, C_WIDTH, 2 * KV_WIDTH, 4 * KV_WIDTH, V7X_LANES, 3 * D_WIDTH]
    tps = seq // tm
    tab_spec = pl.BlockSpec((tm, V7X_LANES), lambda i: (i % tps, 0))
    return pl.pallas_call(
        _in_proj1_body,
        grid=(n // tm,),
        in_specs=[_rows_spec(tm, d), _const_spec((1, d)), tab_spec, tab_spec, tab_spec]
                 + [_const_spec(w.shape) for w in ws],
        out_specs=[_rows_spec(tm, wd) for wd in widths],
        out_shape=[jax.ShapeDtypeStruct((n, wd), dt) for wd, dt in zip(widths, dts)],
        compiler_params=_params("parallel"),
        name="in_proj1",
    )(x, gain.reshape(1, d), *_rope_tables(seq), *ws)


def _compress_body(r_ref, p_ref, w1_ref, w2_ref, o_ref):
    half = CMP_STRIDE * C_HEAD_DIM
    r = r_ref[0, 0]
    lo = _dot(r + p_ref[:, :half], w1_ref[:half, :])
    hi = _dot(r + p_ref[:, half:], w1_ref[half:, :])
    n_rows = r.shape[0]
    hidden = _gelu(lo + pltpu.roll(hi, n_rows - 1, axis=0))
    o_ref[0, 0] = _dot(hidden, w2_ref[...])


def _compress(z, pos, w1, w2, bsz, seq):
    n_rows = seq // CMP_STRIDE
    r = z.reshape(bsz, n_rows, CMP_STRIDE, C_KV_HEADS, C_HEAD_DIM).transpose(0, 3, 1, 2, 4).reshape(
        bsz, C_KV_HEADS, n_rows, CMP_STRIDE * C_HEAD_DIM)
    blk = lambda w: pl.BlockSpec((1, 1, n_rows, w), lambda b, h: (b, h, 0, 0))
    return pl.pallas_call(
        _compress_body,
        grid=(bsz, C_KV_HEADS),
        in_specs=[blk(CMP_STRIDE * C_HEAD_DIM), _const_spec((1, CMP_LEN * C_HEAD_DIM)),
                  _const_spec(w1.shape), _const_spec(w2.shape)],
        out_specs=blk(C_HEAD_DIM),
        out_shape=jax.ShapeDtypeStruct((bsz, C_KV_HEADS, n_rows, C_HEAD_DIM), F32),
        compiler_params=_params("parallel", "parallel"),
        name="nsa_compress",
    )(r, pos.reshape(1, -1), w1.astype(BF16), w2.astype(BF16))


def _nsa_body(qn_ref, qr_ref, gl_ref, kcmp_ref, vcmp_ref, ks_ref, vs_ref, kw_ref, vw_ref, ovl_ref,
              o_ref, *, n_slc):
    scale = C_HEAD_DIM ** -0.5
    n_cmp = kcmp_ref.shape[2]
    qb = pl.program_id(1)
    s0 = qb * QBLK
    t_pos = s0 + lax.broadcasted_iota(jnp.int32, (QBLK, 1), 0)
    gate = _sigmoid(gl_ref[0])
    cmp_end = lax.broadcasted_iota(jnp.int32, (1, n_cmp), 1) * CMP_STRIDE + (CMP_LEN - 1)
    mask_c = cmp_end <= t_pos
    jb = lax.broadcasted_iota(jnp.int32, (1, n_slc), 1)
    cur = t_pos // SLC_BLK
    forced = (jb == 0) | (jb == cur) | (jb == cur - 1)
    slc_ok = jb * SLC_BLK <= t_pos
    win_start = pl.multiple_of(jnp.maximum(s0 - WIN, 0), QBLK)
    key_w = win_start + lax.broadcasted_iota(jnp.int32, (1, QBLK + WIN), 1)
    diff_w = t_pos - key_w
    mask_w = (diff_w >= 0) & (diff_w < WIN)
    e_row = lax.broadcasted_iota(jnp.int32, (n_slc, QBLK), 0)
    e_col = lax.broadcasted_iota(jnp.int32, (n_slc, QBLK), 1) // SLC_BLK
    key_lane = lax.broadcasted_iota(jnp.int32, (1, QBLK), 1)

    outs = []
    for hk in range(C_KV_HEADS):
        kvc = slice(hk * C_HEAD_DIM, (hk + 1) * C_HEAD_DIM)
        heads = [hk * C_GROUP + g for g in range(C_GROUP)]
        hcols = [slice(h * C_HEAD_DIM, (h + 1) * C_HEAD_DIM) for h in heads]
        kc = kcmp_ref[0, hk].astype(BF16)
        vc = vcmp_ref[0, hk].astype(BF16)
        pc_sum = jnp.zeros((QBLK, n_cmp), F32)
        o_c = []
        for hc in hcols:
            pc = _masked_softmax(_dot_nt(qn_ref[0, :, hc], kc) * scale, mask_c)
            o_c.append(_dot(pc, vc))
            pc_sum = pc_sum + pc
        imp = _dot_f32(pc_sum, ovl_ref[...])
        imp = jnp.where(slc_ok, imp + jnp.where(forced, FORCE_BONUS, 0.0), NEG_INF)
        rank = jnp.zeros((QBLK, n_slc), F32)
        for k in range(n_slc):
            ck = imp[:, k:k + 1]
            beats = (ck > imp) | ((ck == imp) & (jb > k))
            rank = rank + jnp.where(beats, 1.0, 0.0)
        sel = jnp.where(rank < float(min(SEL_TOPK, n_slc)), 1.0, 0.0).astype(BF16)
        qr = [qr_ref[0, :, hc] for hc in hcols]

        def kt_body(kt, carry):
            k0 = pl.multiple_of(kt * QBLK, QBLK)
            k = ks_ref[0, pl.ds(k0, QBLK), kvc]
            v = vs_ref[0, pl.ds(k0, QBLK), kvc]
            expand = jnp.where(e_row == kt * (QBLK // SLC_BLK) + e_col, 1.0, 0.0).astype(BF16)
            chosen = jnp.dot(sel, expand, preferred_element_type=F32)
            mask = (chosen > 0.5) & (k0 + key_lane <= t_pos)
            new = []
            for g in range(C_GROUP):
                m, l, acc = carry[g]
                s = jnp.where(mask, _dot_nt(qr[g], k) * scale, NEG_INF)
                m_new = jnp.maximum(m, jnp.max(s, axis=-1, keepdims=True))
                alpha = jnp.exp(m - m_new)
                p = jnp.where(mask, jnp.exp(s - m_new), 0.0)
                new.append((m_new, alpha * l + jnp.sum(p, axis=-1, keepdims=True),
                            alpha * acc + _dot(p, v)))
            return tuple(new)

        init = tuple((jnp.full((QBLK, 1), NEG_INF, F32), jnp.zeros((QBLK, 1), F32),
                      jnp.zeros((QBLK, C_HEAD_DIM), F32)) for _ in range(C_GROUP))
        fin = lax.fori_loop(0, qb + 1, kt_body, init)
        o_s = [acc / jnp.maximum(l, 1e-30) for (_, l, acc) in fin]
        kwb = kw_ref[0, pl.ds(win_start, QBLK + WIN), kvc]
        vwb = vw_ref[0, pl.ds(win_start, QBLK + WIN), kvc]
        for g, h in enumerate(heads):
            pw = _masked_softmax(_dot_nt(qr[g], kwb) * scale, mask_w)
            o_w = _dot(pw, vwb)
            gc = h * N_BRANCH
            outs.append(gate[:, gc:gc + 1] * o_c[g] + gate[:, gc + 1:gc + 2] * o_s[g]
                        + gate[:, gc + 2:gc + 3] * o_w)
    o_ref[0] = jnp.concatenate(outs, axis=1).astype(o_ref.dtype)


def _nsa(qn, qr, glp, k_cmp, v_cmp, kv, bsz, seq):
    n_cmp = seq // CMP_STRIDE
    n_slc = seq // SLC_BLK
    ci = np.arange(n_cmp)[:, None] * CMP_STRIDE
    sj = np.arange(n_slc)[None, :] * SLC_BLK
    overlap = jnp.asarray(((ci < sj + SLC_BLK) & (ci + CMP_LEN > sj)).astype(np.float32))
    q_spec = pl.BlockSpec((1, QBLK, C_WIDTH), lambda b, i: (b, i, 0))
    cmp_spec = pl.BlockSpec((1, C_KV_HEADS, n_cmp, C_HEAD_DIM), lambda b, i: (b, 0, 0, 0))
    kv_spec = lambda c: pl.BlockSpec((1, seq, KV_WIDTH), lambda b, i: (b, 0, c))
    kv3 = kv.reshape(bsz, seq, 4 * KV_WIDTH)
    return pl.pallas_call(
        functools.partial(_nsa_body, n_slc=n_slc),
        grid=(bsz, seq // QBLK),
        in_specs=[q_spec, q_spec, pl.BlockSpec((1, QBLK, V7X_LANES), lambda b, i: (b, i, 0)),
                  cmp_spec, cmp_spec, kv_spec(0), kv_spec(1), kv_spec(2), kv_spec(3),
                  _const_spec((n_cmp, n_slc))],
        out_specs=q_spec,
        out_shape=jax.ShapeDtypeStruct((bsz, seq, C_WIDTH), BF16),
        compiler_params=_params("parallel", "arbitrary"),
        name="nsa_attention",
    )(qn.reshape(bsz, seq, C_WIDTH), qr.reshape(bsz, seq, C_WIDTH), glp.reshape(bsz, seq, V7X_LANES),
      k_cmp, v_cmp, kv3, kv3, kv3, kv3, overlap)


def _router_body(h_ref, g_ref, wr_ref, br_ref, hn_o, idx_o, gate_o):
    hn = _rms(h_ref[...], g_ref[...])
    hn_o[...] = hn
    logits = _dot_f32(hn, wr_ref[...]) + br_ref[...]
    lane = lax.broadcasted_iota(jnp.int32, logits.shape, 1)
    m1 = jnp.max(logits, axis=-1, keepdims=True)
    i1 = jnp.min(jnp.where(logits == m1, lane, V7X_LANES), axis=-1, keepdims=True)
    rest = jnp.where(lane == i1, NEG_INF, logits)
    m2 = jnp.max(rest, axis=-1, keepdims=True)
    i2 = jnp.min(jnp.where(rest == m2, lane, V7X_LANES), axis=-1, keepdims=True)
    e2 = jnp.exp(m2 - m1)
    den = 1.0 + e2
    idx_o[...] = jnp.where(lane == 0, i1, jnp.where(lane == 1, i2, 0))
    gate_o[...] = jnp.where(lane == 0, 1.0 / den, jnp.where(lane == 1, e2 / den, 0.0))


def _router(h, gain, w_router, b_router, tm=512):
    n = h.shape[0]
    wr = jnp.pad(w_router, ((0, 0), (0, V7X_LANES - N_EXPERTS)))
    br = jnp.pad(b_router.reshape(1, -1), ((0, 0), (0, V7X_LANES - N_EXPERTS)), constant_values=NEG_INF)
    return pl.pallas_call(
        _router_body,
        grid=(n // tm,),
        in_specs=[_rows_spec(tm, D_MODEL), _const_spec((1, D_MODEL)), _const_spec(wr.shape),
                  _const_spec(br.shape)],
        out_specs=[_rows_spec(tm, D_MODEL), _rows_spec(tm, V7X_LANES), _rows_spec(tm, V7X_LANES)],
        out_shape=[jax.ShapeDtypeStruct((n, D_MODEL), F32), jax.ShapeDtypeStruct((n, V7X_LANES), jnp.int32),
                   jax.ShapeDtypeStruct((n, V7X_LANES), F32)],
        compiler_params=_params("parallel"),
        name="moe_router",
    )(h, gain.reshape(1, -1), wr, br)


def _row_copy(src, dst, s, d, sem):
    return pltpu.make_async_copy(src.at[s], dst.at[d], sem)


def _dispatch_body(dest_ref, hn_ref, _, xs_ref, sem):
    base = pl.program_id(0) * DMA_ROWS

    def issue(r, c):
        for k in range(TOP_K):
            _row_copy(hn_ref, xs_ref, base + r, dest_ref[TOP_K * r + k], sem).start()
        return c

    lax.fori_loop(0, DMA_ROWS, issue, 0)

    def drain(r, c):
        for k in range(TOP_K):
            _row_copy(hn_ref, xs_ref, 0, 0, sem).wait()
        return c

    lax.fori_loop(0, DMA_ROWS, drain, 0)


def _dispatch(dest, hn_rows, p_rows):
    n = hn_rows.shape[0]
    return pl.pallas_call(
        _dispatch_body,
        grid=(n // DMA_ROWS,),
        in_specs=[pl.BlockSpec((TOP_K * DMA_ROWS,), lambda i: (i,), memory_space=pltpu.SMEM),
                  pl.BlockSpec(memory_space=pl.ANY), pl.BlockSpec(memory_space=pl.ANY)],
        out_specs=pl.BlockSpec(memory_space=pl.ANY),
        out_shape=jax.ShapeDtypeStruct((p_rows, 1, D_MODEL), F32),
        scratch_shapes=[pltpu.SemaphoreType.DMA(())],
        input_output_aliases={2: 0},
        compiler_params=_params("arbitrary"),
        name="moe_dispatch",
    )(dest, hn_rows, jnp.zeros((p_rows, 1, D_MODEL), F32))


def _collect_body(dest_ref, yb_ref, out_ref, sem):
    base = pl.program_id(0) * DMA_ROWS

    def issue(r, c):
        _row_copy(yb_ref, out_ref, dest_ref[r], base + r, sem).start()
        return c

    lax.fori_loop(0, DMA_ROWS, issue, 0)

    def drain(r, c):
        _row_copy(yb_ref, out_ref, 0, 0, sem).wait()
        return c

    lax.fori_loop(0, DMA_ROWS, drain, 0)


def _collect(dest, yb_rows):
    nk = dest.shape[0]
    return pl.pallas_call(
        _collect_body,
        grid=(nk // DMA_ROWS,),
        in_specs=[pl.BlockSpec((DMA_ROWS,), lambda i: (i,), memory_space=pltpu.SMEM),
                  pl.BlockSpec(memory_space=pl.ANY)],
        out_specs=pl.BlockSpec(memory_space=pl.ANY),
        out_shape=jax.ShapeDtypeStruct((nk, 1, D_MODEL), F32),
        scratch_shapes=[pltpu.SemaphoreType.DMA(())],
        compiler_params=_params("arbitrary"),
        name="moe_collect",
    )(dest, yb_rows)


def _experts_body(blk_e_ref, n_used_ref, x_ref, wg_ref, wu_ref, wd_ref, o_ref):
    i = pl.program_id(0)

    @pl.when(i < n_used_ref[0])
    def _():
        x = x_ref[...].astype(BF16)
        gate = jnp.dot(x, wg_ref[0], preferred_element_type=F32)
        up = jnp.dot(x, wu_ref[0], preferred_element_type=F32)
        act = (gate * _sigmoid(gate)) * up
        o_ref[...] = _dot(act, wd_ref[0])

    @pl.when(i >= n_used_ref[0])
    def _():
        o_ref[...] = jnp.zeros_like(o_ref)


def _experts(blk_e, n_used, xs, wg, wu, wd):
    p_rows = xs.shape[0]
    x_spec = pl.BlockSpec((MOE_ROWS, D_MODEL), lambda i, be, nu: (i, 0))
    w_spec = lambda w: pl.BlockSpec((1,) + w.shape[1:], lambda i, be, nu: (be[i], 0, 0))
    return pl.pallas_call(
        _experts_body,
        grid_spec=pltpu.PrefetchScalarGridSpec(
            num_scalar_prefetch=2,
            grid=(p_rows // MOE_ROWS,),
            in_specs=[x_spec, w_spec(wg), w_spec(wu), w_spec(wd)],
            out_specs=x_spec),
        out_shape=jax.ShapeDtypeStruct((p_rows, D_MODEL), F32),
        compiler_params=_params("arbitrary"),
        name="moe_experts",
    )(blk_e, n_used, xs, wg, wu, wd)


def _combine_body(h_ref, y_ref, gate_ref, g_ref, o_ref):
    gate = gate_ref[...]
    moe = gate[:, 0:1] * y_ref[:, :D_MODEL] + gate[:, 1:2] * y_ref[:, D_MODEL:]
    o_ref[...] = _rms(h_ref[...] + moe, g_ref[...])


def _combine(h, y_pairs, gates, gain, tm=512):
    n = h.shape[0]
    return pl.pallas_call(
        _combine_body,
        grid=(n // tm,),
        in_specs=[_rows_spec(tm, D_MODEL), _rows_spec(tm, TOP_K * D_MODEL), _rows_spec(tm, V7X_LANES),
                  _const_spec((1, D_MODEL))],
        out_specs=_rows_spec(tm, D_MODEL),
        out_shape=jax.ShapeDtypeStruct((n, D_MODEL), F32),
        compiler_params=_params("parallel"),
        name="moe_combine_norm",
    )(h, y_pairs, gates, gain.reshape(1, -1))


def _moe_layout(idx):
    n = idx.shape[0]
    nk = n * TOP_K
    e_flat = idx[:, :TOP_K].reshape(nk)
    onehot = (e_flat[:, None] == jnp.arange(N_EXPERTS, dtype=jnp.int32)[None, :]).astype(jnp.int32)
    csum = jnp.cumsum(onehot, axis=0)
    counts = csum[-1]
    rank = jnp.sum(csum * onehot, axis=1) - 1
    padded = ((counts + MOE_ROWS - 1) // MOE_ROWS) * MOE_ROWS
    p_end = jnp.cumsum(padded)
    p_start = p_end - padded
    dest = jnp.sum(onehot * p_start[None, :], axis=1) + rank
    n_blk = (nk + MOE_ROWS - 1) // MOE_ROWS + N_EXPERTS
    blk_e = jnp.minimum(jnp.searchsorted(p_end, jnp.arange(n_blk) * MOE_ROWS, side="right"),
                        N_EXPERTS - 1).astype(jnp.int32)
    n_used = (p_end[-1:] // MOE_ROWS).astype(jnp.int32)
    return dest.astype(jnp.int32), blk_e, n_used, n_blk * MOE_ROWS


def _moe_final(h, norm_g, w_router, b_router, wg, wu, wd, final_g):
    n = h.shape[0]
    hn, idx, gates = _router(h, norm_g, w_router, b_router)
    dest, blk_e, n_used, p_rows = _moe_layout(idx)
    xs = _dispatch(dest, hn.reshape(n, 1, D_MODEL), p_rows)
    yb = _experts(blk_e, n_used, xs.reshape(p_rows, D_MODEL), wg.astype(BF16), wu.astype(BF16),
                  wd.astype(BF16))
    pairs = _collect(dest, yb.reshape(p_rows, 1, D_MODEL))
    return _combine(h, pairs.reshape(n, TOP_K * D_MODEL), gates, final_g)


def kernel(x, e_norm_mix, e_w_in, sgu_ln_g, sgu_ln_b, sgu_w, sgu_b, rwkv_mu, rwkv_w0, rwkv_w2,
           rwkv_a0, rwkv_a2, rwkv_g2, rwkv_k_k, rwkv_k_a, rwkv_r_k, rwkv_gn_g, rwkv_gn_b, e_w_out,
           e_norm_ffn, ffn_w_gate, ffn_w_up, ffn_w_down, o_norm_mix, o_w_in, nsa_cmp_pos_k,
           nsa_cmp_pos_v, nsa_cmp_k_w1, nsa_cmp_k_w2, nsa_cmp_v_w1, nsa_cmp_v_w2, conv_w, o_w_out,
           o_norm_ffn, moe_router, moe_router_b, moe_w_gate, moe_w_up, moe_w_down, final_norm):
    bsz, seq, d = x.shape
    n = bsz * seq
    h = x.reshape(n, d)

    w_in = e_w_in[0].astype(BF16)
    p_uv, p_b = _norm_matmul(h, e_norm_mix[0], [w_in[:, :2 * A_WIDTH], w_in[:, 2 * A_WIDTH:]], [F32, F32])
    ya = _sgu(p_uv, sgu_ln_g[0], sgu_ln_b[0], sgu_w[0], sgu_b[0])
    r, w, k0, v, a, g = _rwkv_prep(p_b, seq, rwkv_mu[0], rwkv_w0[0], rwkv_w2[0], rwkv_a0[0],
                                   rwkv_a2[0], rwkv_g2[0])
    scan_in = [_to_scan_layout(z, bsz, seq) for z in (r, w, k0, v, a)]
    ys = _rwkv_scan(*scan_in, rwkv_k_k[0], rwkv_k_a[0], rwkv_r_k[0], rwkv_gn_g[0], rwkv_gn_b[0],
                    bsz, seq)
    h = _out_proj0(h, ya, _from_scan_layout(ys, bsz, seq), g, e_w_out[0].astype(BF16))
    h = _ffn(h, e_norm_ffn[0], ffn_w_gate[0].astype(BF16), ffn_w_up[0].astype(BF16),
             ffn_w_down[0].astype(BF16))

    qn, qr, kcvc, kv, glp, bcd = _in_proj1(h, o_norm_mix[0], o_w_in[0], seq)
    k_cmp = _compress(kcvc[:, :KV_WIDTH], nsa_cmp_pos_k[0], nsa_cmp_k_w1[0], nsa_cmp_k_w2[0], bsz, seq)
    v_cmp = _compress(kcvc[:, KV_WIDTH:], nsa_cmp_pos_v[0], nsa_cmp_v_w1[0], nsa_cmp_v_w2[0], bsz, seq)
    yc = _nsa(qn, qr, glp, k_cmp, v_cmp, kv, bsz, seq).reshape(n, C_WIDTH)
    h = _out_proj1(h, yc, bcd, conv_w[0], o_w_out[0].astype(BF16), seq)
    out = _moe_final(h, o_norm_ffn[0], moe_router[0], moe_router_b[0], moe_w_gate[0], moe_w_up[0],
                     moe_w_down[0], final_norm)
    return out.reshape(bsz, seq, d)
```

```python
import functools

import jax
import jax.numpy as jnp
import numpy as np
from jax import lax
from jax.experimental import pallas as pl
from jax.experimental.pallas import tpu as pltpu

F32 = jnp.float32
BF16 = jnp.bfloat16

D_MODEL = 1024
A_GROUPS = 4
A_GROUP_DIM = 128
A_WIDTH = A_GROUPS * A_GROUP_DIM
CHUNK = 128
SGU_LN_EPS = 1e-5
B_HEADS = 8
B_HEAD_DIM = 64
B_WIDTH = B_HEADS * B_HEAD_DIM
DECAY_LORA = 64
ICLR_LORA = 64
GATE_LORA = 128
B_IN = 3 * B_WIDTH + DECAY_LORA + ICLR_LORA + GATE_LORA
RWKV_GN_EPS = 64e-5
C_HEADS = 8
C_KV_HEADS = 2
C_GROUP = C_HEADS // C_KV_HEADS
C_HEAD_DIM = 64
C_WIDTH = C_HEADS * C_HEAD_DIM
KV_WIDTH = C_KV_HEADS * C_HEAD_DIM
N_BRANCH = 3
CMP_LEN = 32
CMP_STRIDE = 16
CMP_HIDDEN = 256
SLC_BLK = 64
SEL_TOPK = 8
WIN = 512
QBLK = 128
ROT_DIM = C_HEAD_DIM // 4
ROPE_THETA = 500000.0
D_WIDTH = 512
CONV_W = 3
FFN_DIM = 2816
N_EXPERTS = 8
TOP_K = 2
EXPERT_DIM = 1408
NORM_EPS = 1e-6
NEG_INF = -1e30
FORCE_BONUS = 1e6

V7X_LANES = 128
V7X_SUBLANES = 8
V7X_VMEM_LIMIT = 56 * 1024 * 1024

MOE_ROWS = 256
DMA_ROWS = 512


def _params(*sem):
    return pltpu.CompilerParams(dimension_semantics=sem, vmem_limit_bytes=V7X_VMEM_LIMIT)


def _const_spec(shape):
    zeros = (0,) * len(shape)
    return pl.BlockSpec(shape, lambda *_: zeros)


def _rows_spec(tm, width):
    return pl.BlockSpec((tm, width), lambda i: (i, 0))


def _rms(x, g):
    return x * lax.rsqrt(jnp.mean(x * x, axis=-1, keepdims=True) + NORM_EPS) * g


def _gelu(x):
    return x * (0.5 * (1.0 + jnp.tanh(0.7978845608028654 * (x + 0.044715 * (x * x * x)))))


def _sigmoid(x):
    return 1.0 / (1.0 + jnp.exp(-x))


def _dot(a, b):
    return jnp.dot(a.astype(BF16), b.astype(BF16), preferred_element_type=F32)


def _dot_nt(a, b):
    return lax.dot_general(a.astype(BF16), b.astype(BF16), (((1,), (1,)), ((), ())),
                           preferred_element_type=F32)


def _split_bf16(a):
    hi = a.astype(BF16)
    lo = (a - hi.astype(F32)).astype(BF16)
    return hi, lo


def _dot_f32(a, b):
    ah, al = _split_bf16(a)
    bh, bl = _split_bf16(b)
    d = functools.partial(jnp.dot, preferred_element_type=F32)
    return d(ah, bh) + (d(al, bh) + d(ah, bl))


def _masked_softmax(s, mask):
    s = jnp.where(mask, s, NEG_INF)
    m = jnp.max(s, axis=-1, keepdims=True)
    p = jnp.where(mask, jnp.exp(s - m), 0.0)
    return p / jnp.maximum(jnp.sum(p, axis=-1, keepdims=True), 1e-30)


def _norm_matmul_body(x_ref, g_ref, *refs):
    k = len(refs) // 2
    xn = _rms(x_ref[...], g_ref[...]).astype(BF16)
    for w_ref, o_ref in zip(refs[:k], refs[k:]):
        o_ref[...] = jnp.dot(xn, w_ref[...], preferred_element_type=F32).astype(o_ref.dtype)


def _norm_matmul(x, gain, ws, out_dtypes, tm=256):
    n, d = x.shape
    return pl.pallas_call(
        _norm_matmul_body,
        grid=(n // tm,),
        in_specs=[_rows_spec(tm, d), _const_spec((1, d))] + [_const_spec(w.shape) for w in ws],
        out_specs=[_rows_spec(tm, w.shape[1]) for w in ws],
        out_shape=[jax.ShapeDtypeStruct((n, w.shape[1]), dt) for w, dt in zip(ws, out_dtypes)],
        compiler_params=_params("parallel"),
        name="norm_matmul",
    )(x, gain.reshape(1, d), *ws)


def _sgu_body(p_ref, lng_ref, lnb_ref, w_ref, b_ref, o_ref, *, n_chunks):
    row = lax.broadcasted_iota(jnp.int32, (CHUNK, CHUNK), 0)
    col = lax.broadcasted_iota(jnp.int32, (CHUNK, CHUNK), 1)
    causal = col <= row
    for c in range(n_chunks):
        rows = slice(c * CHUNK, (c + 1) * CHUNK)
        u = _gelu(p_ref[rows, :A_WIDTH])
        v = _gelu(p_ref[rows, A_WIDTH:])
        outs = []
        for g in range(A_GROUPS):
            cols = slice(g * A_GROUP_DIM, (g + 1) * A_GROUP_DIM)
            vg = v[:, cols]
            mu = jnp.mean(vg, axis=-1, keepdims=True)
            dv = vg - mu
            var = jnp.mean(dv * dv, axis=-1, keepdims=True)
            vn = dv * lax.rsqrt(var + SGU_LN_EPS) * lng_ref[:, cols] + lnb_ref[:, cols]
            wm = jnp.where(causal, w_ref[g], 0.0)
            mixed = _dot(wm, vn) + b_ref[:, g:g + 1]
            outs.append(u[:, cols] * mixed)
        o_ref[rows, :] = jnp.concatenate(outs, axis=1).astype(o_ref.dtype)


def _sgu(p_uv, ln_g, ln_b, w_s, b_s, tm=512):
    n = p_uv.shape[0]
    return pl.pallas_call(
        functools.partial(_sgu_body, n_chunks=tm // CHUNK),
        grid=(n // tm,),
        in_specs=[_rows_spec(tm, 2 * A_WIDTH), _const_spec((1, A_WIDTH)), _const_spec((1, A_WIDTH)),
                  _const_spec((A_GROUPS, CHUNK, CHUNK)), _const_spec((CHUNK, A_GROUPS))],
        out_specs=_rows_spec(tm, A_WIDTH),
        out_shape=jax.ShapeDtypeStruct((n, A_WIDTH), BF16),
        compiler_params=_params("parallel"),
        name="sgu",
    )(p_uv, ln_g.reshape(1, -1), ln_b.reshape(1, -1), w_s, b_s.T)


def _softplus(x):
    return jnp.maximum(x, 0.0) + jnp.log(1.0 + jnp.exp(-jnp.abs(x)))


def _rwkv_prep_body(pb_ref, prev_ref, mu_ref, w0_ref, w2_ref, a0_ref, a2_ref, g2_ref,
                    r_o, w_o, k_o, v_o, a_o, g_o, *, tiles_per_seq):
    tm = pb_ref.shape[0]
    x = pb_ref[...]
    first = (pl.program_id(0) % tiles_per_seq) == 0
    prev_row = jnp.where(first, 0.0, prev_ref[V7X_SUBLANES - 1:V7X_SUBLANES, :])
    rowid = lax.broadcasted_iota(jnp.int32, (tm, 1), 0)
    shifted = jnp.where(rowid == 0, prev_row, pltpu.roll(x, 1, axis=0))
    xm = x + (shifted - x) * mu_ref[...]
    o = 3 * B_WIDTH
    wl = xm[:, o:o + DECAY_LORA]
    al = xm[:, o + DECAY_LORA:o + DECAY_LORA + ICLR_LORA]
    gl = xm[:, o + DECAY_LORA + ICLR_LORA:]
    w = -_softplus(-(w0_ref[...] + _dot(jnp.tanh(wl), w2_ref[...]))) - 0.5
    r_o[...] = xm[:, :B_WIDTH]
    w_o[...] = jnp.exp(-jnp.exp(w))
    k_o[...] = xm[:, B_WIDTH:2 * B_WIDTH]
    v_o[...] = xm[:, 2 * B_WIDTH:3 * B_WIDTH]
    a_o[...] = _sigmoid(a0_ref[...] + _dot(al, a2_ref[...]))
    g_o[...] = _dot(_sigmoid(gl), g2_ref[...])


def _rwkv_prep(p_b, seq, mu, w0, w2, a0, a2, g2, tm=256):
    n = p_b.shape[0]
    per8 = tm // V7X_SUBLANES
    outs = [jax.ShapeDtypeStruct((n, B_WIDTH), F32)] * 6
    return pl.pallas_call(
        functools.partial(_rwkv_prep_body, tiles_per_seq=seq // tm),
        grid=(n // tm,),
        in_specs=[_rows_spec(tm, B_IN),
                  pl.BlockSpec((V7X_SUBLANES, B_IN), lambda i: (jnp.maximum(i * per8 - 1, 0), 0)),
                  _const_spec((1, B_IN)), _const_spec((1, B_WIDTH)), _const_spec((DECAY_LORA, B_WIDTH)),
                  _const_spec((1, B_WIDTH)), _const_spec((ICLR_LORA, B_WIDTH)),
                  _const_spec((GATE_LORA, B_WIDTH))],
        out_specs=[_rows_spec(tm, B_WIDTH)] * 6,
        out_shape=outs,
        compiler_params=_params("parallel"),
        name="rwkv_prep",
    )(p_b, p_b, mu.reshape(1, -1), w0.reshape(1, -1), w2, a0.reshape(1, -1), a2, g2)


def _rwkv_scan_body(r_ref, w_ref, k0_ref, v_ref, a_ref, kkp_ref, kap_ref, rk_ref, gng_ref, gnb_ref,
                    y_ref, s_ref, kkn_ref, ka_ref, km_ref):
    tt, n = r_ref.shape[0], r_ref.shape[1]

    @pl.when(pl.program_id(0) == 0)
    def _():
        s_ref[...] = jnp.zeros_like(s_ref)

    k0 = k0_ref[...]
    a = a_ref[...]
    kk = k0 * kkp_ref[...][None]
    kkn = kk / jnp.maximum(jnp.sqrt(jnp.sum(kk * kk, axis=1, keepdims=True)), 1e-12)
    kkn_ref[...] = kkn
    ka_ref[...] = kkn * a
    km_ref[...] = k0 * (1.0 + (a - 1.0) * kap_ref[...][None])

    zero = jnp.zeros((n, r_ref.shape[2]), F32)

    def sa_init(j, acc):
        return acc + s_ref[j] * kkn_ref[0, pl.ds(j, 1), :]

    sa0 = lax.fori_loop(0, n, sa_init, zero)

    def step(t, sa):
        v_t = v_ref[t]
        tn = jnp.minimum(t + 1, tt - 1)

        def jbody(j, carry):
            y, san = carry
            row = pl.ds(j, 1)
            sn = s_ref[j] * w_ref[t, row, :] + (v_t * km_ref[t, row, :] - sa * ka_ref[t, row, :])
            s_ref[j] = sn
            return y + sn * r_ref[t, row, :], san + sn * kkn_ref[tn, row, :]

        y, san = lax.fori_loop(0, n, jbody, (zero, zero), unroll=8)
        y_ref[t] = y
        return san

    lax.fori_loop(0, tt, step, sa0)

    y = y_ref[...]
    ym = jnp.mean(y, axis=1, keepdims=True)
    dy = y - ym
    yv = jnp.mean(dy * dy, axis=1, keepdims=True)
    yn = dy * lax.rsqrt(yv + RWKV_GN_EPS) * gng_ref[...][None] + gnb_ref[...][None]
    bonus = jnp.sum(r_ref[...] * km_ref[...] * rk_ref[...][None], axis=1, keepdims=True) * v_ref[...]
    y_ref[...] = yn + bonus


def _rwkv_scan(r, w, k0, v, a, k_k, k_a, r_k, gn_g, gn_b, bsz, seq, tt=32):
    n = B_HEAD_DIM
    lanes = bsz * B_HEADS

    def lane_param(p):
        return jnp.tile(p.reshape(B_HEADS, n).T, (1, bsz))

    seq_spec = pl.BlockSpec((tt, n, lanes), lambda c: (c, 0, 0))
    par_spec = _const_spec((n, lanes))
    return pl.pallas_call(
        _rwkv_scan_body,
        grid=(seq // tt,),
        in_specs=[seq_spec] * 5 + [par_spec] * 5,
        out_specs=seq_spec,
        out_shape=jax.ShapeDtypeStruct((seq, n, lanes), F32),
        scratch_shapes=[pltpu.VMEM((n, n, lanes), F32)] + [pltpu.VMEM((tt, n, lanes), F32)] * 3,
        compiler_params=_params("arbitrary"),
        name="rwkv_scan",
    )(r, w, k0, v, a, lane_param(k_k), lane_param(k_a), lane_param(r_k.reshape(-1)),
      lane_param(gn_g), lane_param(gn_b))


def _to_scan_layout(z, bsz, seq):
    return z.reshape(bsz, seq, B_HEADS, B_HEAD_DIM).transpose(1, 3, 0, 2).reshape(
        seq, B_HEAD_DIM, bsz * B_HEADS)


def _from_scan_layout(y, bsz, seq):
    return y.reshape(seq, B_HEAD_DIM, bsz, B_HEADS).transpose(2, 0, 3, 1).reshape(
        bsz * seq, B_WIDTH)


def _out_proj0_body(h_ref, ya_ref, ys_ref, g_ref, w_ref, o_ref):
    yb = ys_ref[...] * g_ref[...]
    o_ref[...] = h_ref[...] + (_dot(ya_ref[...], w_ref[:A_WIDTH, :]) + _dot(yb, w_ref[A_WIDTH:, :]))


def _out_proj0(h, ya, ys, g, w_out, tm=512):
    n = h.shape[0]
    return pl.pallas_call(
        _out_proj0_body,
        grid=(n // tm,),
        in_specs=[_rows_spec(tm, D_MODEL), _rows_spec(tm, A_WIDTH), _rows_spec(tm, B_WIDTH),
                  _rows_spec(tm, B_WIDTH), _const_spec(w_out.shape)],
        out_specs=_rows_spec(tm, D_MODEL),
        out_shape=jax.ShapeDtypeStruct((n, D_MODEL), F32),
        compiler_params=_params("parallel"),
        name="out_proj0",
    )(h, ya, ys, g, w_out)


def _out_proj1_body(h_ref, yc_ref, bcd_ref, prev_ref, cw_ref, w_ref, o_ref, *, tiles_per_seq):
    tm = h_ref.shape[0]
    first = (pl.program_id(0) % tiles_per_seq) == 0
    z = bcd_ref[:, D_WIDTH:2 * D_WIDTH] * bcd_ref[:, 2 * D_WIDTH:]
    zp = jnp.where(first, 0.0, prev_ref[:, D_WIDTH:2 * D_WIDTH] * prev_ref[:, 2 * D_WIDTH:])
    rowid = lax.broadcasted_iota(jnp.int32, (tm, 1), 0)
    z1 = jnp.where(rowid == 0, zp[7:8, :], pltpu.roll(z, 1, axis=0))
    z2 = pltpu.roll(z, 2, axis=0)
    z2 = jnp.where(rowid == 0, zp[6:7, :], jnp.where(rowid == 1, zp[7:8, :], z2))
    y = cw_ref[0:1, :] * z2 + cw_ref[1:2, :] * z1 + cw_ref[2:3, :] * z
    yd = bcd_ref[:, :D_WIDTH] * y
    o_ref[...] = h_ref[...] + (_dot(yc_ref[...], w_ref[:C_WIDTH, :]) + _dot(yd, w_ref[C_WIDTH:, :]))


def _out_proj1(h, yc, bcd, conv_w, w_out, seq, tm=512):
    n = h.shape[0]
    per8 = tm // V7X_SUBLANES
    return pl.pallas_call(
        functools.partial(_out_proj1_body, tiles_per_seq=seq // tm),
        grid=(n // tm,),
        in_specs=[_rows_spec(tm, D_MODEL), _rows_spec(tm, C_WIDTH), _rows_spec(tm, 3 * D_WIDTH),
                  pl.BlockSpec((V7X_SUBLANES, 3 * D_WIDTH), lambda i: (jnp.maximum(i * per8 - 1, 0), 0)),
                  _const_spec((CONV_W, D_WIDTH)), _const_spec(w_out.shape)],
        out_specs=_rows_spec(tm, D_MODEL),
        out_shape=jax.ShapeDtypeStruct((n, D_MODEL), F32),
        compiler_params=_params("parallel"),
        name="out_proj1",
    )(h, yc, bcd, bcd, conv_w, w_out)


def _ffn_body(h_ref, g_ref, wg_ref, wu_ref, wd_ref, o_ref):
    h = h_ref[...]
    hn = _rms(h, g_ref[...]).astype(BF16)
    gate = jnp.dot(hn, wg_ref[...], preferred_element_type=F32)
    up = jnp.dot(hn, wu_ref[...], preferred_element_type=F32)
    act = (gate * _sigmoid(gate)) * up
    o_ref[...] = h + _dot(act, wd_ref[...])


def _ffn(h, gain, wg, wu, wd, tm=256):
    n = h.shape[0]
    return pl.pallas_call(
        _ffn_body,
        grid=(n // tm,),
        in_specs=[_rows_spec(tm, D_MODEL), _const_spec((1, D_MODEL)), _const_spec(wg.shape),
                  _const_spec(wu.shape), _const_spec(wd.shape)],
        out_specs=_rows_spec(tm, D_MODEL),
        out_shape=jax.ShapeDtypeStruct((n, D_MODEL), F32),
        compiler_params=_params("parallel"),
        name="ffn",
    )(h, gain.reshape(1, -1), wg, wu, wd)


def _rope(x, c, s_up, s_dn):
    half = ROT_DIM // 2
    return x * c + pltpu.roll(x, half, axis=1) * s_up + pltpu.roll(x, V7X_LANES - half, axis=1) * s_dn


def _lane_blocks(x):
    return [x[:, i * V7X_LANES:(i + 1) * V7X_LANES] for i in range(x.shape[1] // V7X_LANES)]


def _in_proj1_body(x_ref, g_ref, c_ref, su_ref, sd_ref, kb_ref, one_ref, wq_ref, wkc_ref, wkv_ref,
                   wgl_ref, wbcd_ref, qn_o, qr_o, kc_o, ks_o, vs_o, kw_o, vw_o, gl_o, bcd_o):
    xn = _rms(x_ref[...], g_ref[...]).astype(BF16)
    d = functools.partial(jnp.dot, preferred_element_type=F32)
    c, su, sd = c_ref[...], su_ref[...], sd_ref[...]
    rope = lambda z: _rope(z, c, su, sd)
    q = d(xn, wq_ref[...]) * (C_HEAD_DIM ** -0.5)
    qn_o[...] = q.astype(BF16)
    qr_o[...] = jnp.concatenate([rope(z) for z in _lane_blocks(q)], axis=1).astype(BF16)
    kc_o[...] = d(xn, wkc_ref[...])
    kv = _lane_blocks(d(xn, wkv_ref[...]))
    hk = C_KV_HEADS
    ks_o[...] = jnp.concatenate([rope(z) + kb_ref[...] for z in kv[:hk]], axis=1).astype(BF16)
    vs_o[...] = jnp.concatenate([z + one_ref[...] for z in kv[hk:2 * hk]], axis=1).astype(BF16)
    kw_o[...] = jnp.concatenate([rope(z) for z in kv[2 * hk:3 * hk]], axis=1).astype(BF16)
    vw_o[...] = jnp.concatenate([z + one_ref[...] for z in kv[3 * hk:]], axis=1).astype(BF16)
    gl_o[...] = d(xn, wgl_ref[...])
    bcd_o[...] = d(xn, wbcd_ref[...])


def _head_tables(seq):
    half = ROT_DIM // 2
    pos = jnp.arange(seq, dtype=F32)
    inv_freq = ROPE_THETA ** (-jnp.arange(0, ROT_DIM, 2, dtype=F32) / ROT_DIM)
    ang = pos[:, None] * inv_freq[None, :]
    cos, sin = jnp.cos(ang), jnp.sin(ang)
    pad = jnp.zeros((seq, V7X_LANES - ROT_DIM), F32)
    zeros = jnp.zeros((seq, half), F32)
    c = jnp.concatenate([cos, cos, pad + 1.0], axis=1)
    s_up = jnp.concatenate([zeros, sin, pad], axis=1)
    s_dn = jnp.concatenate([-sin, zeros, pad], axis=1)
    lane = jnp.arange(V7X_LANES)[None, :]
    blk = (jnp.arange(seq) // SLC_BLK)[:, None]
    k_bias = jnp.where(lane == C_HEAD_DIM + blk, NEG_INF, 0.0).astype(F32)
    ones = (lane == C_HEAD_DIM).astype(F32)
    return c, s_up, s_dn, k_bias, ones


def _pad_heads(w, n_heads):
    width = w.shape[1] // n_heads
    w = w.reshape(w.shape[0], n_heads, width)
    return jnp.pad(w, ((0, 0), (0, 0), (0, V7X_LANES - width))).reshape(w.shape[0], n_heads * V7X_LANES)


def _in_proj1(x, gain, w_in, seq, tm=256):
    n, d = x.shape
    o = np.cumsum([0, C_WIDTH] + [KV_WIDTH] * 6 + [C_HEADS * N_BRANCH] + [D_WIDTH] * 3)
    wq = _pad_heads(w_in[:, o[0]:o[1]], C_HEADS).astype(BF16)
    wkc = w_in[:, o[1]:o[3]].astype(BF16)
    wkv = _pad_heads(w_in[:, o[3]:o[7]], 4 * C_KV_HEADS).astype(BF16)
    wgl = _pad_heads(w_in[:, o[7]:o[8]], C_KV_HEADS).astype(BF16)
    wbcd = w_in[:, o[8]:o[11]].astype(BF16)
    ws = [wq, wkc, wkv, wgl, wbcd]
    kvw = C_KV_HEADS * V7X_LANES
    widths = [C_HEADS * V7X_LANES] * 2 + [2 * KV_WIDTH] + [kvw] * 4 + [kvw, 3 * D_WIDTH]
    dts = [BF16, BF16, F32, BF16, BF16, BF16, BF16, F32, F32]
    tps = seq // tm
    tab_spec = pl.BlockSpec((tm, V7X_LANES), lambda i: (i % tps, 0))
    c, s_up, s_dn, k_bias, ones = _head_tables(seq)
    return pl.pallas_call(
        _in_proj1_body,
        grid=(n // tm,),
        in_specs=[_rows_spec(tm, d), _const_spec((1, d)), tab_spec, tab_spec, tab_spec, tab_spec,
                  _const_spec((1, V7X_LANES))] + [_const_spec(w.shape) for w in ws],
        out_specs=[_rows_spec(tm, wd) for wd in widths],
        out_shape=[jax.ShapeDtypeStruct((n, wd), dt) for wd, dt in zip(widths, dts)],
        compiler_params=_params("parallel"),
        name="in_proj1",
    )(x, gain.reshape(1, d), c, s_up, s_dn, k_bias, ones, *ws)


def _compress_body(r_ref, p_ref, w1_ref, w2_ref, o_ref):
    half = CMP_STRIDE * C_HEAD_DIM
    r = r_ref[0, 0]
    lo = _dot(r + p_ref[:, :half], w1_ref[:half, :])
    hi = _dot(r + p_ref[:, half:], w1_ref[half:, :])
    n_rows = r.shape[0]
    hidden = _gelu(lo + pltpu.roll(hi, n_rows - 1, axis=0))
    o_ref[0, 0] = _dot(hidden, w2_ref[...]).astype(o_ref.dtype)


def _compress(z, pos, w1, w2, bsz, seq):
    n_rows = seq // CMP_STRIDE
    r = z.reshape(bsz, n_rows, CMP_STRIDE, C_KV_HEADS, C_HEAD_DIM).transpose(0, 3, 1, 2, 4).reshape(
        bsz, C_KV_HEADS, n_rows, CMP_STRIDE * C_HEAD_DIM)
    w2p = _pad_heads(w2, 1).astype(BF16)
    blk = lambda w: pl.BlockSpec((1, 1, n_rows, w), lambda b, h: (b, h, 0, 0))
    return pl.pallas_call(
        _compress_body,
        grid=(bsz, C_KV_HEADS),
        in_specs=[blk(CMP_STRIDE * C_HEAD_DIM), _const_spec((1, CMP_LEN * C_HEAD_DIM)),
                  _const_spec(w1.shape), _const_spec(w2p.shape)],
        out_specs=blk(V7X_LANES),
        out_shape=jax.ShapeDtypeStruct((bsz, C_KV_HEADS, n_rows, V7X_LANES), BF16),
        compiler_params=_params("parallel", "parallel"),
        name="nsa_compress",
    )(r, pos.reshape(1, -1), w1.astype(BF16), w2p)


SEL_CHUNK = 4 * QBLK


def _stack_heads(blk):
    return jnp.concatenate(_lane_blocks(blk), axis=0)


def _rows4(x):
    return jnp.concatenate([x] * C_GROUP, axis=0)


def _nsa_body(qn_ref, qr_ref, gl_ref, kc_ref, vc_ref, ks_ref, vs_ref, kw_ref, vw_ref, ovl_ref, place_ref,
              o_ref):
    n_cmp = kc_ref.shape[2]
    n_slc = ovl_ref.shape[0]
    hd = C_HEAD_DIM
    qb = pl.program_id(2)
    s0 = qb * QBLK
    t_pos = s0 + lax.broadcasted_iota(jnp.int32, (QBLK, 1), 0)
    qn4 = _stack_heads(qn_ref[0])
    qr4 = _stack_heads(qr_ref[0])

    cmp_end = lax.broadcasted_iota(jnp.int32, (1, n_cmp), 1) * CMP_STRIDE + (CMP_LEN - 1)
    pc = _masked_softmax(_dot_nt(qn4, kc_ref[0, 0]), _rows4(cmp_end <= t_pos))
    o_c = _dot(pc, vc_ref[0, 0])
    pc_sum = (pc[:QBLK] + pc[QBLK:2 * QBLK]) + (pc[2 * QBLK:3 * QBLK] + pc[3 * QBLK:])

    p_hi, p_lo = _split_bf16(pc_sum)
    ovl = ovl_ref[...]
    imp = _dot_nt(ovl, p_hi) + _dot_nt(ovl, p_lo)
    jb = lax.broadcasted_iota(jnp.int32, (n_slc, 1), 0)
    t_row = s0 + lax.broadcasted_iota(jnp.int32, (1, QBLK), 1)
    cur = t_row // SLC_BLK
    forced = (jb == 0) | (jb == cur) | (jb == cur - 1)
    imp = jnp.where(jb * SLC_BLK <= t_row, imp + jnp.where(forced, FORCE_BONUS, 0.0), NEG_INF)
    rank = jnp.zeros((n_slc, QBLK), F32)
    for k in range(n_slc):
        ck = imp[k:k + 1, :]
        beats = (ck > imp) | ((ck == imp) & (jb > k))
        rank = rank + jnp.where(beats, 1.0, 0.0)
    not_sel = jnp.where(rank < float(min(SEL_TOPK, n_slc)), 0.0, 1.0).astype(BF16)
    not_sel_q = lax.dot_general(not_sel, place_ref[...], (((0,), (0,)), ((), ())),
                                preferred_element_type=F32)
    q_sel = qr4 + _rows4(not_sel_q.astype(BF16))

    def sel_chunk(c, m, acc, causal):
        k0 = pl.multiple_of(c * SEL_CHUNK, SEL_CHUNK)
        s = _dot_nt(q_sel, ks_ref[0, pl.ds(k0, SEL_CHUNK), :])
        if causal:
            key = k0 + lax.broadcasted_iota(jnp.int32, (1, SEL_CHUNK), 1)
            s = s + _rows4(jnp.where(key <= t_pos, 0.0, NEG_INF))
        m_new = jnp.maximum(m, jnp.max(s, axis=-1, keepdims=True))
        p = jnp.exp(s - m_new)
        acc = jnp.exp(m - m_new) * acc + _dot(p, vs_ref[0, pl.ds(k0, SEL_CHUNK), :])
        return m_new, acc

    rows = C_GROUP * QBLK
    last = qb // (SEL_CHUNK // QBLK)
    m, acc = lax.fori_loop(0, last, lambda c, ma: sel_chunk(c, ma[0], ma[1], False),
                           (jnp.full((rows, 1), NEG_INF, F32), jnp.zeros((rows, V7X_LANES), F32)))
    _, acc = sel_chunk(last, m, acc, True)
    o_s = acc[:, :hd] / jnp.maximum(acc[:, hd:hd + 1], 1e-30)

    band = QBLK + WIN
    w0 = pl.multiple_of(jnp.maximum(s0 - WIN, 0), QBLK)
    diff = t_pos - (w0 + lax.broadcasted_iota(jnp.int32, (1, band), 1))
    s = _dot_nt(qr4, kw_ref[0, pl.ds(w0, band), :]) + _rows4(
        jnp.where((diff >= 0) & (diff < WIN), 0.0, NEG_INF))
    p = jnp.exp(s - jnp.max(s, axis=-1, keepdims=True))
    acc = _dot(p, vw_ref[0, pl.ds(w0, band), :])
    o_w = acc[:, :hd] / jnp.maximum(acc[:, hd:hd + 1], 1e-30)

    gate = _sigmoid(gl_ref[0])
    outs = []
    for g in range(C_GROUP):
        r = slice(g * QBLK, (g + 1) * QBLK)
        gc = g * N_BRANCH
        outs.append(gate[:, gc:gc + 1] * o_c[r, :hd] + gate[:, gc + 1:gc + 2] * o_s[r]
                    + gate[:, gc + 2:gc + 3] * o_w[r])
    o_ref[0] = jnp.concatenate(outs, axis=1).astype(o_ref.dtype)


def _nsa(qn, qr, gl, k_cmp, v_cmp, ks, vs, kw, vw, bsz, seq):
    n_cmp = seq // CMP_STRIDE
    n_slc = seq // SLC_BLK
    assert C_HEAD_DIM + n_slc <= V7X_LANES and seq % SEL_CHUNK == 0
    ci = np.arange(n_cmp)[None, :] * CMP_STRIDE
    sj = np.arange(n_slc)[:, None] * SLC_BLK
    overlap_t = jnp.asarray(((ci < sj + SLC_BLK) & (ci + CMP_LEN > sj)).astype(np.float32)).astype(BF16)
    place = jnp.asarray(np.eye(n_slc, V7X_LANES, k=C_HEAD_DIM, dtype=np.float32)).astype(BF16)
    gw = C_GROUP * V7X_LANES
    q_spec = pl.BlockSpec((1, QBLK, gw), lambda b, h, i: (b, i, h))
    gl_spec = pl.BlockSpec((1, QBLK, V7X_LANES), lambda b, h, i: (b, i, h))
    cmp_spec = pl.BlockSpec((1, 1, n_cmp, V7X_LANES), lambda b, h, i: (b, h, 0, 0))
    kv_spec = pl.BlockSpec((1, seq, V7X_LANES), lambda b, h, i: (b, 0, h))
    as3 = lambda z: z.reshape(bsz, seq, z.shape[-1])
    return pl.pallas_call(
        _nsa_body,
        grid=(bsz, C_KV_HEADS, seq // QBLK),
        in_specs=[q_spec, q_spec, gl_spec, cmp_spec, cmp_spec, kv_spec, kv_spec, kv_spec, kv_spec,
                  _const_spec((n_slc, n_cmp)), _const_spec((n_slc, V7X_LANES))],
        out_specs=pl.BlockSpec((1, QBLK, C_GROUP * C_HEAD_DIM), lambda b, h, i: (b, i, h)),
        out_shape=jax.ShapeDtypeStruct((bsz, seq, C_WIDTH), BF16),
        compiler_params=_params("parallel", "parallel", "arbitrary"),
        name="nsa_attention",
    )(as3(qn), as3(qr), as3(gl), k_cmp, v_cmp, as3(ks), as3(vs), as3(kw), as3(vw), overlap_t, place)


def _router_body(h_ref, g_ref, wr_ref, br_ref, hn_o, idx_o, gate_o):
    hn = _rms(h_ref[...], g_ref[...])
    hn_o[...] = hn
    logits = _dot_f32(hn, wr_ref[...]) + br_ref[...]
    lane = lax.broadcasted_iota(jnp.int32, logits.shape, 1)
    m1 = jnp.max(logits, axis=-1, keepdims=True)
    i1 = jnp.min(jnp.where(logits == m1, lane, V7X_LANES), axis=-1, keepdims=True)
    rest = jnp.where(lane == i1, NEG_INF, logits)
    m2 = jnp.max(rest, axis=-1, keepdims=True)
    i2 = jnp.min(jnp.where(rest == m2, lane, V7X_LANES), axis=-1, keepdims=True)
    e2 = jnp.exp(m2 - m1)
    den = 1.0 + e2
    idx_o[...] = jnp.where(lane == 0, i1, jnp.where(lane == 1, i2, 0))
    gate_o[...] = jnp.where(lane == 0, 1.0 / den, jnp.where(lane == 1, e2 / den, 0.0))


def _router(h, gain, w_router, b_router, tm=512):
    n = h.shape[0]
    wr = jnp.pad(w_router, ((0, 0), (0, V7X_LANES - N_EXPERTS)))
    br = jnp.pad(b_router.reshape(1, -1), ((0, 0), (0, V7X_LANES - N_EXPERTS)), constant_values=NEG_INF)
    return pl.pallas_call(
        _router_body,
        grid=(n // tm,),
        in_specs=[_rows_spec(tm, D_MODEL), _const_spec((1, D_MODEL)), _const_spec(wr.shape),
                  _const_spec(br.shape)],
        out_specs=[_rows_spec(tm, D_MODEL), _rows_spec(tm, V7X_LANES), _rows_spec(tm, V7X_LANES)],
        out_shape=[jax.ShapeDtypeStruct((n, D_MODEL), F32), jax.ShapeDtypeStruct((n, V7X_LANES), jnp.int32),
                   jax.ShapeDtypeStruct((n, V7X_LANES), F32)],
        compiler_params=_params("parallel"),
        name="moe_router",
    )(h, gain.reshape(1, -1), wr, br)


def _row_copy(src, dst, s, d, sem):
    return pltpu.make_async_copy(src.at[s], dst.at[d], sem)


def _dispatch_body(dest_ref, hn_ref, _, xs_ref, sem):
    def issue(r, c):
        for k in range(TOP_K):
            _row_copy(hn_ref, xs_ref, r, dest_ref[TOP_K * r + k], sem).start()
        return c

    lax.fori_loop(0, DMA_ROWS, issue, 0)

    def drain(r, c):
        for k in range(TOP_K):
            _row_copy(hn_ref, xs_ref, 0, 0, sem).wait()
        return c

    lax.fori_loop(0, DMA_ROWS, drain, 0)


def _dispatch(dest, hn_rows, p_rows):
    n = hn_rows.shape[0]
    return pl.pallas_call(
        _dispatch_body,
        grid=(n // DMA_ROWS,),
        in_specs=[pl.BlockSpec((TOP_K * DMA_ROWS,), lambda i: (i,), memory_space=pltpu.SMEM),
                  pl.BlockSpec((DMA_ROWS, 1, D_MODEL), lambda i: (i, 0, 0)),
                  pl.BlockSpec(memory_space=pl.ANY)],
        out_specs=pl.BlockSpec(memory_space=pl.ANY),
        out_shape=jax.ShapeDtypeStruct((p_rows, 1, D_MODEL), F32),
        scratch_shapes=[pltpu.SemaphoreType.DMA(())],
        input_output_aliases={2: 0},
        compiler_params=_params("arbitrary"),
        name="moe_dispatch",
    )(dest, hn_rows, jnp.zeros((p_rows, 1, D_MODEL), F32))


def _collect_body(dest_ref, yb_ref, out_ref, sem):
    def issue(r, c):
        _row_copy(yb_ref, out_ref, dest_ref[r], r, sem).start()
        return c

    lax.fori_loop(0, DMA_ROWS, issue, 0)

    def drain(r, c):
        _row_copy(yb_ref, out_ref, 0, 0, sem).wait()
        return c

    lax.fori_loop(0, DMA_ROWS, drain, 0)


def _collect(dest, yb_rows):
    nk = dest.shape[0]
    return pl.pallas_call(
        _collect_body,
        grid=(nk // DMA_ROWS,),
        in_specs=[pl.BlockSpec((DMA_ROWS,), lambda i: (i,), memory_space=pltpu.SMEM),
                  pl.BlockSpec(memory_space=pl.ANY)],
        out_specs=pl.BlockSpec((DMA_ROWS, 1, D_MODEL), lambda i: (i, 0, 0)),
        out_shape=jax.ShapeDtypeStruct((nk, 1, D_MODEL), F32),
        scratch_shapes=[pltpu.SemaphoreType.DMA(())],
        compiler_params=_params("arbitrary"),
        name="moe_collect",
    )(dest, yb_rows)


def _experts_body(blk_e_ref, n_used_ref, x_ref, wg_ref, wu_ref, wd_ref, o_ref):
    i = pl.program_id(0)

    @pl.when(i < n_used_ref[0])
    def _():
        x = x_ref[...].astype(BF16)
        gate = jnp.dot(x, wg_ref[0], preferred_element_type=F32)
        up = jnp.dot(x, wu_ref[0], preferred_element_type=F32)
        act = (gate * _sigmoid(gate)) * up
        o_ref[...] = _dot(act, wd_ref[0])

    @pl.when(i >= n_used_ref[0])
    def _():
        o_ref[...] = jnp.zeros_like(o_ref)


def _experts(blk_e, n_used, xs, wg, wu, wd):
    p_rows = xs.shape[0]
    x_spec = pl.BlockSpec((MOE_ROWS, D_MODEL), lambda i, be, nu: (i, 0))
    w_spec = lambda w: pl.BlockSpec((1,) + w.shape[1:], lambda i, be, nu: (be[i], 0, 0))
    return pl.pallas_call(
        _experts_body,
        grid_spec=pltpu.PrefetchScalarGridSpec(
            num_scalar_prefetch=2,
            grid=(p_rows // MOE_ROWS,),
            in_specs=[x_spec, w_spec(wg), w_spec(wu), w_spec(wd)],
            out_specs=x_spec),
        out_shape=jax.ShapeDtypeStruct((p_rows, D_MODEL), F32),
        compiler_params=_params("arbitrary"),
        name="moe_experts",
    )(blk_e, n_used, xs, wg, wu, wd)


def _combine_body(h_ref, y_ref, gate_ref, g_ref, o_ref):
    gate = gate_ref[...]
    moe = gate[:, 0:1] * y_ref[:, :D_MODEL] + gate[:, 1:2] * y_ref[:, D_MODEL:]
    o_ref[...] = _rms(h_ref[...] + moe, g_ref[...])


def _combine(h, y_pairs, gates, gain, tm=512):
    n = h.shape[0]
    return pl.pallas_call(
        _combine_body,
        grid=(n // tm,),
        in_specs=[_rows_spec(tm, D_MODEL), _rows_spec(tm, TOP_K * D_MODEL), _rows_spec(tm, V7X_LANES),
                  _const_spec((1, D_MODEL))],
        out_specs=_rows_spec(tm, D_MODEL),
        out_shape=jax.ShapeDtypeStruct((n, D_MODEL), F32),
        compiler_params=_params("parallel"),
        name="moe_combine_norm",
    )(h, y_pairs, gates, gain.reshape(1, -1))


def _moe_layout(idx):
    n = idx.shape[0]
    nk = n * TOP_K
    e_flat = idx[:, :TOP_K].reshape(nk)
    onehot = (e_flat[:, None] == jnp.arange(N_EXPERTS, dtype=jnp.int32)[None, :]).astype(jnp.int32)
    csum = jnp.cumsum(onehot, axis=0)
    counts = csum[-1]
    rank = jnp.sum(csum * onehot, axis=1) - 1
    padded = ((counts + MOE_ROWS - 1) // MOE_ROWS) * MOE_ROWS
    p_end = jnp.cumsum(padded)
    p_start = p_end - padded
    dest = jnp.sum(onehot * p_start[None, :], axis=1) + rank
    n_blk = (nk + MOE_ROWS - 1) // MOE_ROWS + N_EXPERTS
    blk_e = jnp.minimum(jnp.searchsorted(p_end, jnp.arange(n_blk) * MOE_ROWS, side="right"),
                        N_EXPERTS - 1).astype(jnp.int32)
    n_used = (p_end[-1:] // MOE_ROWS).astype(jnp.int32)
    return dest.astype(jnp.int32), blk_e, n_used, n_blk * MOE_ROWS


def _moe_final(h, norm_g, w_router, b_router, wg, wu, wd, final_g):
    n = h.shape[0]
    hn, idx, gates = _router(h, norm_g, w_router, b_router)
    dest, blk_e, n_used, p_rows = _moe_layout(idx)
    xs = _dispatch(dest, hn.reshape(n, 1, D_MODEL), p_rows)
    yb = _experts(blk_e, n_used, xs.reshape(p_rows, D_MODEL), wg.astype(BF16), wu.astype(BF16),
                  wd.astype(BF16))
    pairs = _collect(dest, yb.reshape(p_rows, 1, D_MODEL))
    return _combine(h, pairs.reshape(n, TOP_K * D_MODEL), gates, final_g)


def kernel(x, e_norm_mix, e_w_in, sgu_ln_g, sgu_ln_b, sgu_w, sgu_b, rwkv_mu, rwkv_w0, rwkv_w2,
           rwkv_a0, rwkv_a2, rwkv_g2, rwkv_k_k, rwkv_k_a, rwkv_r_k, rwkv_gn_g, rwkv_gn_b, e_w_out,
           e_norm_ffn, ffn_w_gate, ffn_w_up, ffn_w_down, o_norm_mix, o_w_in, nsa_cmp_pos_k,
           nsa_cmp_pos_v, nsa_cmp_k_w1, nsa_cmp_k_w2, nsa_cmp_v_w1, nsa_cmp_v_w2, conv_w, o_w_out,
           o_norm_ffn, moe_router, moe_router_b, moe_w_gate, moe_w_up, moe_w_down, final_norm):
    bsz, seq, d = x.shape
    n = bsz * seq
    h = x.reshape(n, d)

    w_in = e_w_in[0].astype(BF16)
    p_uv, p_b = _norm_matmul(h, e_norm_mix[0], [w_in[:, :2 * A_WIDTH], w_in[:, 2 * A_WIDTH:]], [F32, F32])
    ya = _sgu(p_uv, sgu_ln_g[0], sgu_ln_b[0], sgu_w[0], sgu_b[0])
    r, w, k0, v, a, g = _rwkv_prep(p_b, seq, rwkv_mu[0], rwkv_w0[0], rwkv_w2[0], rwkv_a0[0],
                                   rwkv_a2[0], rwkv_g2[0])
    scan_in = [_to_scan_layout(z, bsz, seq) for z in (r, w, k0, v, a)]
    ys = _rwkv_scan(*scan_in, rwkv_k_k[0], rwkv_k_a[0], rwkv_r_k[0], rwkv_gn_g[0], rwkv_gn_b[0],
                    bsz, seq)
    h = _out_proj0(h, ya, _from_scan_layout(ys, bsz, seq), g, e_w_out[0].astype(BF16))
    h = _ffn(h, e_norm_ffn[0], ffn_w_gate[0].astype(BF16), ffn_w_up[0].astype(BF16),
             ffn_w_down[0].astype(BF16))

    qn, qr, kcvc, ks, vs, kw, vw, gl, bcd = _in_proj1(h, o_norm_mix[0], o_w_in[0], seq)
    k_cmp = _compress(kcvc[:, :KV_WIDTH], nsa_cmp_pos_k[0], nsa_cmp_k_w1[0], nsa_cmp_k_w2[0], bsz, seq)
    v_cmp = _compress(kcvc[:, KV_WIDTH:], nsa_cmp_pos_v[0], nsa_cmp_v_w1[0], nsa_cmp_v_w2[0], bsz, seq)
    yc = _nsa(qn, qr, gl, k_cmp, v_cmp, ks, vs, kw, vw, bsz, seq).reshape(n, C_WIDTH)
    h = _out_proj1(h, yc, bcd, conv_w[0], o_w_out[0].astype(BF16), seq)
    out = _moe_final(h, o_norm_ffn[0], moe_router[0], moe_router_b[0], moe_w_gate[0], moe_w_up[0],
                     moe_w_down[0], final_norm)
    return out.reshape(bsz, seq, d)
```

```python
import functools

import jax
import jax.numpy as jnp
import numpy as np
from jax import lax
from jax.experimental import pallas as pl
from jax.experimental.pallas import tpu as pltpu

F32 = jnp.float32
BF16 = jnp.bfloat16

D_MODEL = 1024
A_GROUPS = 4
A_GROUP_DIM = 128
A_WIDTH = A_GROUPS * A_GROUP_DIM
CHUNK = 128
SGU_LN_EPS = 1e-5
B_HEADS = 8
B_HEAD_DIM = 64
B_WIDTH = B_HEADS * B_HEAD_DIM
DECAY_LORA = 64
ICLR_LORA = 64
GATE_LORA = 128
B_IN = 3 * B_WIDTH + DECAY_LORA + ICLR_LORA + GATE_LORA
RWKV_GN_EPS = 64e-5
C_HEADS = 8
C_KV_HEADS = 2
C_GROUP = C_HEADS // C_KV_HEADS
C_HEAD_DIM = 64
C_WIDTH = C_HEADS * C_HEAD_DIM
KV_WIDTH = C_KV_HEADS * C_HEAD_DIM
N_BRANCH = 3
CMP_LEN = 32
CMP_STRIDE = 16
CMP_HIDDEN = 256
SLC_BLK = 64
SEL_TOPK = 8
WIN = 512
QBLK = 128
ROT_DIM = C_HEAD_DIM // 4
ROPE_THETA = 500000.0
D_WIDTH = 512
CONV_W = 3
FFN_DIM = 2816
N_EXPERTS = 8
TOP_K = 2
EXPERT_DIM = 1408
NORM_EPS = 1e-6
NEG_INF = -1e30
FORCE_BONUS = 1e6

V7X_LANES = 128
V7X_SUBLANES = 8
V7X_VMEM_LIMIT = 56 * 1024 * 1024

MOE_ROWS = 256
DMA_ROWS = 512


def _params(*sem):
    return pltpu.CompilerParams(dimension_semantics=sem, vmem_limit_bytes=V7X_VMEM_LIMIT)


def _const_spec(shape):
    zeros = (0,) * len(shape)
    return pl.BlockSpec(shape, lambda *_: zeros)


def _rows_spec(tm, width):
    return pl.BlockSpec((tm, width), lambda i: (i, 0))


def _rms(x, g):
    return x * lax.rsqrt(jnp.mean(x * x, axis=-1, keepdims=True) + NORM_EPS) * g


def _gelu(x):
    return x * (0.5 * (1.0 + jnp.tanh(0.7978845608028654 * (x + 0.044715 * (x * x * x)))))


def _sigmoid(x):
    return 1.0 / (1.0 + jnp.exp(-x))


def _dot(a, b):
    return jnp.dot(a.astype(BF16), b.astype(BF16), preferred_element_type=F32)


def _dot_nt(a, b):
    return lax.dot_general(a.astype(BF16), b.astype(BF16), (((1,), (1,)), ((), ())),
                           preferred_element_type=F32)


def _split_bf16(a):
    hi = a.astype(BF16)
    lo = (a - hi.astype(F32)).astype(BF16)
    return hi, lo


def _dot_f32(a, b):
    ah, al = _split_bf16(a)
    bh, bl = _split_bf16(b)
    d = functools.partial(jnp.dot, preferred_element_type=F32)
    return d(ah, bh) + (d(al, bh) + d(ah, bl))


def _masked_softmax(s, mask):
    s = jnp.where(mask, s, NEG_INF)
    m = jnp.max(s, axis=-1, keepdims=True)
    p = jnp.where(mask, jnp.exp(s - m), 0.0)
    return p / jnp.maximum(jnp.sum(p, axis=-1, keepdims=True), 1e-30)


def _norm_matmul_body(x_ref, g_ref, *refs):
    k = len(refs) // 2
    xn = _rms(x_ref[...], g_ref[...]).astype(BF16)
    for w_ref, o_ref in zip(refs[:k], refs[k:]):
        o_ref[...] = jnp.dot(xn, w_ref[...], preferred_element_type=F32).astype(o_ref.dtype)


def _norm_matmul(x, gain, ws, out_dtypes, tm=256):
    n, d = x.shape
    return pl.pallas_call(
        _norm_matmul_body,
        grid=(n // tm,),
        in_specs=[_rows_spec(tm, d), _const_spec((1, d))] + [_const_spec(w.shape) for w in ws],
        out_specs=[_rows_spec(tm, w.shape[1]) for w in ws],
        out_shape=[jax.ShapeDtypeStruct((n, w.shape[1]), dt) for w, dt in zip(ws, out_dtypes)],
        compiler_params=_params("parallel"),
        name="norm_matmul",
    )(x, gain.reshape(1, d), *ws)


def _sgu_body(p_ref, lng_ref, lnb_ref, w_ref, b_ref, o_ref, *, n_chunks):
    row = lax.broadcasted_iota(jnp.int32, (CHUNK, CHUNK), 0)
    col = lax.broadcasted_iota(jnp.int32, (CHUNK, CHUNK), 1)
    causal = col <= row
    for c in range(n_chunks):
        rows = slice(c * CHUNK, (c + 1) * CHUNK)
        u = _gelu(p_ref[rows, :A_WIDTH])
        v = _gelu(p_ref[rows, A_WIDTH:])
        outs = []
        for g in range(A_GROUPS):
            cols = slice(g * A_GROUP_DIM, (g + 1) * A_GROUP_DIM)
            vg = v[:, cols]
            mu = jnp.mean(vg, axis=-1, keepdims=True)
            dv = vg - mu
            var = jnp.mean(dv * dv, axis=-1, keepdims=True)
            vn = dv * lax.rsqrt(var + SGU_LN_EPS) * lng_ref[:, cols] + lnb_ref[:, cols]
            wm = jnp.where(causal, w_ref[g], 0.0)
            mixed = _dot(wm, vn) + b_ref[:, g:g + 1]
            outs.append(u[:, cols] * mixed)
        o_ref[rows, :] = jnp.concatenate(outs, axis=1).astype(o_ref.dtype)


def _sgu(p_uv, ln_g, ln_b, w_s, b_s, tm=512):
    n = p_uv.shape[0]
    return pl.pallas_call(
        functools.partial(_sgu_body, n_chunks=tm // CHUNK),
        grid=(n // tm,),
        in_specs=[_rows_spec(tm, 2 * A_WIDTH), _const_spec((1, A_WIDTH)), _const_spec((1, A_WIDTH)),
                  _const_spec((A_GROUPS, CHUNK, CHUNK)), _const_spec((CHUNK, A_GROUPS))],
        out_specs=_rows_spec(tm, A_WIDTH),
        out_shape=jax.ShapeDtypeStruct((n, A_WIDTH), BF16),
        compiler_params=_params("parallel"),
        name="sgu",
    )(p_uv, ln_g.reshape(1, -1), ln_b.reshape(1, -1), w_s, b_s.T)


def _softplus(x):
    return jnp.maximum(x, 0.0) + jnp.log(1.0 + jnp.exp(-jnp.abs(x)))


def _rwkv_prep_body(pb_ref, prev_ref, mu_ref, w0_ref, w2_ref, a0_ref, a2_ref, g2_ref,
                    r_o, w_o, k_o, v_o, a_o, g_o, *, tiles_per_seq):
    tm = pb_ref.shape[0]
    x = pb_ref[...]
    first = (pl.program_id(0) % tiles_per_seq) == 0
    prev_row = jnp.where(first, 0.0, prev_ref[V7X_SUBLANES - 1:V7X_SUBLANES, :])
    rowid = lax.broadcasted_iota(jnp.int32, (tm, 1), 0)
    shifted = jnp.where(rowid == 0, prev_row, pltpu.roll(x, 1, axis=0))
    xm = x + (shifted - x) * mu_ref[...]
    o = 3 * B_WIDTH
    wl = xm[:, o:o + DECAY_LORA]
    al = xm[:, o + DECAY_LORA:o + DECAY_LORA + ICLR_LORA]
    gl = xm[:, o + DECAY_LORA + ICLR_LORA:]
    w = -_softplus(-(w0_ref[...] + _dot(jnp.tanh(wl), w2_ref[...]))) - 0.5
    r_o[...] = xm[:, :B_WIDTH]
    w_o[...] = jnp.exp(-jnp.exp(w))
    k_o[...] = xm[:, B_WIDTH:2 * B_WIDTH]
    v_o[...] = xm[:, 2 * B_WIDTH:3 * B_WIDTH]
    a_o[...] = _sigmoid(a0_ref[...] + _dot(al, a2_ref[...]))
    g_o[...] = _dot(_sigmoid(gl), g2_ref[...])


def _rwkv_prep(p_b, seq, mu, w0, w2, a0, a2, g2, tm=256):
    n = p_b.shape[0]
    per8 = tm // V7X_SUBLANES
    outs = [jax.ShapeDtypeStruct((n, B_WIDTH), F32)] * 6
    return pl.pallas_call(
        functools.partial(_rwkv_prep_body, tiles_per_seq=seq // tm),
        grid=(n // tm,),
        in_specs=[_rows_spec(tm, B_IN),
                  pl.BlockSpec((V7X_SUBLANES, B_IN), lambda i: (jnp.maximum(i * per8 - 1, 0), 0)),
                  _const_spec((1, B_IN)), _const_spec((1, B_WIDTH)), _const_spec((DECAY_LORA, B_WIDTH)),
                  _const_spec((1, B_WIDTH)), _const_spec((ICLR_LORA, B_WIDTH)),
                  _const_spec((GATE_LORA, B_WIDTH))],
        out_specs=[_rows_spec(tm, B_WIDTH)] * 6,
        out_shape=outs,
        compiler_params=_params("parallel"),
        name="rwkv_prep",
    )(p_b, p_b, mu.reshape(1, -1), w0.reshape(1, -1), w2, a0.reshape(1, -1), a2, g2)


def _rwkv_scan_body(r_ref, w_ref, k0_ref, v_ref, a_ref, kkp_ref, kap_ref, rk_ref, gng_ref, gnb_ref,
                    y_ref, s_ref, kkn_ref, ka_ref, km_ref):
    tt, n = r_ref.shape[0], r_ref.shape[1]

    @pl.when(pl.program_id(0) == 0)
    def _():
        s_ref[...] = jnp.zeros_like(s_ref)

    k0 = k0_ref[...]
    a = a_ref[...]
    kk = k0 * kkp_ref[...][None]
    kkn = kk / jnp.maximum(jnp.sqrt(jnp.sum(kk * kk, axis=1, keepdims=True)), 1e-12)
    kkn_ref[...] = kkn
    ka_ref[...] = kkn * a
    km_ref[...] = k0 * (1.0 + (a - 1.0) * kap_ref[...][None])

    zero = jnp.zeros((n, r_ref.shape[2]), F32)

    def sa_init(j, acc):
        return acc + s_ref[j] * kkn_ref[0, pl.ds(j, 1), :]

    sa0 = lax.fori_loop(0, n, sa_init, zero)

    def step(t, sa):
        v_t = v_ref[t]
        tn = jnp.minimum(t + 1, tt - 1)

        def jbody(j, carry):
            y, san = carry
            row = pl.ds(j, 1)
            sn = s_ref[j] * w_ref[t, row, :] + (v_t * km_ref[t, row, :] - sa * ka_ref[t, row, :])
            s_ref[j] = sn
            return y + sn * r_ref[t, row, :], san + sn * kkn_ref[tn, row, :]

        y, san = lax.fori_loop(0, n, jbody, (zero, zero), unroll=8)
        y_ref[t] = y
        return san

    lax.fori_loop(0, tt, step, sa0)

    y = y_ref[...]
    ym = jnp.mean(y, axis=1, keepdims=True)
    dy = y - ym
    yv = jnp.mean(dy * dy, axis=1, keepdims=True)
    yn = dy * lax.rsqrt(yv + RWKV_GN_EPS) * gng_ref[...][None] + gnb_ref[...][None]
    bonus = jnp.sum(r_ref[...] * km_ref[...] * rk_ref[...][None], axis=1, keepdims=True) * v_ref[...]
    y_ref[...] = yn + bonus


def _rwkv_scan(r, w, k0, v, a, k_k, k_a, r_k, gn_g, gn_b, bsz, seq, tt=32):
    n = B_HEAD_DIM
    lanes = bsz * B_HEADS

    def lane_param(p):
        return jnp.tile(p.reshape(B_HEADS, n).T, (1, bsz))

    seq_spec = pl.BlockSpec((tt, n, lanes), lambda c: (c, 0, 0))
    par_spec = _const_spec((n, lanes))
    return pl.pallas_call(
        _rwkv_scan_body,
        grid=(seq // tt,),
        in_specs=[seq_spec] * 5 + [par_spec] * 5,
        out_specs=seq_spec,
        out_shape=jax.ShapeDtypeStruct((seq, n, lanes), F32),
        scratch_shapes=[pltpu.VMEM((n, n, lanes), F32)] + [pltpu.VMEM((tt, n, lanes), F32)] * 3,
        compiler_params=_params("arbitrary"),
        name="rwkv_scan",
    )(r, w, k0, v, a, lane_param(k_k), lane_param(k_a), lane_param(r_k.reshape(-1)),
      lane_param(gn_g), lane_param(gn_b))


def _to_scan_layout(z, bsz, seq):
    return z.reshape(bsz, seq, B_HEADS, B_HEAD_DIM).transpose(1, 3, 0, 2).reshape(
        seq, B_HEAD_DIM, bsz * B_HEADS)


def _from_scan_layout(y, bsz, seq):
    return y.reshape(seq, B_HEAD_DIM, bsz, B_HEADS).transpose(2, 0, 3, 1).reshape(
        bsz * seq, B_WIDTH)


def _out_proj0_body(h_ref, ya_ref, ys_ref, g_ref, w_ref, o_ref):
    yb = ys_ref[...] * g_ref[...]
    o_ref[...] = h_ref[...] + (_dot(ya_ref[...], w_ref[:A_WIDTH, :]) + _dot(yb, w_ref[A_WIDTH:, :]))


def _out_proj0(h, ya, ys, g, w_out, tm=512):
    n = h.shape[0]
    return pl.pallas_call(
        _out_proj0_body,
        grid=(n // tm,),
        in_specs=[_rows_spec(tm, D_MODEL), _rows_spec(tm, A_WIDTH), _rows_spec(tm, B_WIDTH),
                  _rows_spec(tm, B_WIDTH), _const_spec(w_out.shape)],
        out_specs=_rows_spec(tm, D_MODEL),
        out_shape=jax.ShapeDtypeStruct((n, D_MODEL), F32),
        compiler_params=_params("parallel"),
        name="out_proj0",
    )(h, ya, ys, g, w_out)


def _out_proj1_body(h_ref, yc_ref, bcd_ref, prev_ref, cw_ref, w_ref, o_ref, *, tiles_per_seq):
    tm = h_ref.shape[0]
    first = (pl.program_id(0) % tiles_per_seq) == 0
    z = bcd_ref[:, D_WIDTH:2 * D_WIDTH] * bcd_ref[:, 2 * D_WIDTH:]
    zp = jnp.where(first, 0.0, prev_ref[:, D_WIDTH:2 * D_WIDTH] * prev_ref[:, 2 * D_WIDTH:])
    rowid = lax.broadcasted_iota(jnp.int32, (tm, 1), 0)
    z1 = jnp.where(rowid == 0, zp[7:8, :], pltpu.roll(z, 1, axis=0))
    z2 = pltpu.roll(z, 2, axis=0)
    z2 = jnp.where(rowid == 0, zp[6:7, :], jnp.where(rowid == 1, zp[7:8, :], z2))
    y = cw_ref[0:1, :] * z2 + cw_ref[1:2, :] * z1 + cw_ref[2:3, :] * z
    yd = bcd_ref[:, :D_WIDTH] * y
    o_ref[...] = h_ref[...] + (_dot(yc_ref[...], w_ref[:C_WIDTH, :]) + _dot(yd, w_ref[C_WIDTH:, :]))


def _out_proj1(h, yc, bcd, conv_w, w_out, seq, tm=512):
    n = h.shape[0]
    per8 = tm // V7X_SUBLANES
    return pl.pallas_call(
        functools.partial(_out_proj1_body, tiles_per_seq=seq // tm),
        grid=(n // tm,),
        in_specs=[_rows_spec(tm, D_MODEL), _rows_spec(tm, C_WIDTH), _rows_spec(tm, 3 * D_WIDTH),
                  pl.BlockSpec((V7X_SUBLANES, 3 * D_WIDTH), lambda i: (jnp.maximum(i * per8 - 1, 0), 0)),
                  _const_spec((CONV_W, D_WIDTH)), _const_spec(w_out.shape)],
        out_specs=_rows_spec(tm, D_MODEL),
        out_shape=jax.ShapeDtypeStruct((n, D_MODEL), F32),
        compiler_params=_params("parallel"),
        name="out_proj1",
    )(h, yc, bcd, bcd, conv_w, w_out)


def _ffn_body(h_ref, g_ref, wg_ref, wu_ref, wd_ref, o_ref):
    h = h_ref[...]
    hn = _rms(h, g_ref[...]).astype(BF16)
    gate = jnp.dot(hn, wg_ref[...], preferred_element_type=F32)
    up = jnp.dot(hn, wu_ref[...], preferred_element_type=F32)
    act = (gate * _sigmoid(gate)) * up
    o_ref[...] = h + _dot(act, wd_ref[...])


def _ffn(h, gain, wg, wu, wd, tm=256):
    n = h.shape[0]
    return pl.pallas_call(
        _ffn_body,
        grid=(n // tm,),
        in_specs=[_rows_spec(tm, D_MODEL), _const_spec((1, D_MODEL)), _const_spec(wg.shape),
                  _const_spec(wu.shape), _const_spec(wd.shape)],
        out_specs=_rows_spec(tm, D_MODEL),
        out_shape=jax.ShapeDtypeStruct((n, D_MODEL), F32),
        compiler_params=_params("parallel"),
        name="ffn",
    )(h, gain.reshape(1, -1), wg, wu, wd)


def _rope(x, c, s_up, s_dn):
    half = ROT_DIM // 2
    return x * c + pltpu.roll(x, half, axis=1) * s_up + pltpu.roll(x, V7X_LANES - half, axis=1) * s_dn


def _lane_blocks(x):
    return [x[:, i * V7X_LANES:(i + 1) * V7X_LANES] for i in range(x.shape[1] // V7X_LANES)]


def _in_proj1_body(x_ref, g_ref, c_ref, su_ref, sd_ref, kb_ref, one_ref, wq_ref, wkc_ref, wkv_ref,
                   wgl_ref, wbcd_ref, qn_o, qr_o, kc_o, ks_o, vs_o, kw_o, vw_o, gl_o, bcd_o):
    xn = _rms(x_ref[...], g_ref[...]).astype(BF16)
    d = functools.partial(jnp.dot, preferred_element_type=F32)
    c, su, sd = c_ref[...], su_ref[...], sd_ref[...]
    rope = lambda z: _rope(z, c, su, sd)
    q = d(xn, wq_ref[...]) * (C_HEAD_DIM ** -0.5)
    qn_o[...] = q.astype(BF16)
    qr_o[...] = jnp.concatenate([rope(z) for z in _lane_blocks(q)], axis=1).astype(BF16)
    kc_o[...] = d(xn, wkc_ref[...])
    kv = _lane_blocks(d(xn, wkv_ref[...]))
    hk = C_KV_HEADS
    ks_o[...] = jnp.concatenate([rope(z) + kb_ref[...] for z in kv[:hk]], axis=1).astype(BF16)
    vs_o[...] = jnp.concatenate([z + one_ref[...] for z in kv[hk:2 * hk]], axis=1).astype(BF16)
    kw_o[...] = jnp.concatenate([rope(z) for z in kv[2 * hk:3 * hk]], axis=1).astype(BF16)
    vw_o[...] = jnp.concatenate([z + one_ref[...] for z in kv[3 * hk:]], axis=1).astype(BF16)
    gl_o[...] = d(xn, wgl_ref[...])
    bcd_o[...] = d(xn, wbcd_ref[...])


def _head_tables(seq):
    half = ROT_DIM // 2
    pos = jnp.arange(seq, dtype=F32)
    inv_freq = ROPE_THETA ** (-jnp.arange(0, ROT_DIM, 2, dtype=F32) / ROT_DIM)
    ang = pos[:, None] * inv_freq[None, :]
    cos, sin = jnp.cos(ang), jnp.sin(ang)
    pad = jnp.zeros((seq, V7X_LANES - ROT_DIM), F32)
    zeros = jnp.zeros((seq, half), F32)
    c = jnp.concatenate([cos, cos, pad + 1.0], axis=1)
    s_up = jnp.concatenate([zeros, sin, pad], axis=1)
    s_dn = jnp.concatenate([-sin, zeros, pad], axis=1)
    lane = jnp.arange(V7X_LANES)[None, :]
    blk = (jnp.arange(seq) // SLC_BLK)[:, None]
    k_bias = jnp.where(lane == C_HEAD_DIM + blk, NEG_INF, 0.0).astype(F32)
    ones = (lane == C_HEAD_DIM).astype(F32)
    return c, s_up, s_dn, k_bias, ones


def _pad_heads(w, n_heads):
    width = w.shape[1] // n_heads
    w = w.reshape(w.shape[0], n_heads, width)
    return jnp.pad(w, ((0, 0), (0, 0), (0, V7X_LANES - width))).reshape(w.shape[0], n_heads * V7X_LANES)


def _in_proj1(x, gain, w_in, seq, tm=256):
    n, d = x.shape
    o = np.cumsum([0, C_WIDTH] + [KV_WIDTH] * 6 + [C_HEADS * N_BRANCH] + [D_WIDTH] * 3)
    wq = _pad_heads(w_in[:, o[0]:o[1]], C_HEADS).astype(BF16)
    wkc = w_in[:, o[1]:o[3]].astype(BF16)
    wkv = _pad_heads(w_in[:, o[3]:o[7]], 4 * C_KV_HEADS).astype(BF16)
    wgl = _pad_heads(w_in[:, o[7]:o[8]], C_KV_HEADS).astype(BF16)
    wbcd = w_in[:, o[8]:o[11]].astype(BF16)
    ws = [wq, wkc, wkv, wgl, wbcd]
    kvw = C_KV_HEADS * V7X_LANES
    widths = [C_HEADS * V7X_LANES] * 2 + [2 * KV_WIDTH] + [kvw] * 4 + [kvw, 3 * D_WIDTH]
    dts = [BF16, BF16, F32, BF16, BF16, BF16, BF16, F32, F32]
    tps = seq // tm
    tab_spec = pl.BlockSpec((tm, V7X_LANES), lambda i: (i % tps, 0))
    c, s_up, s_dn, k_bias, ones = _head_tables(seq)
    return pl.pallas_call(
        _in_proj1_body,
        grid=(n // tm,),
        in_specs=[_rows_spec(tm, d), _const_spec((1, d)), tab_spec, tab_spec, tab_spec, tab_spec,
                  _const_spec((1, V7X_LANES))] + [_const_spec(w.shape) for w in ws],
        out_specs=[_rows_spec(tm, wd) for wd in widths],
        out_shape=[jax.ShapeDtypeStruct((n, wd), dt) for wd, dt in zip(widths, dts)],
        compiler_params=_params("parallel"),
        name="in_proj1",
    )(x, gain.reshape(1, d), c, s_up, s_dn, k_bias, ones, *ws)


def _compress_body(r_ref, p_ref, w1_ref, w2_ref, o_ref):
    half = CMP_STRIDE * C_HEAD_DIM
    r = r_ref[0, 0]
    lo = _dot(r + p_ref[:, :half], w1_ref[:half, :])
    hi = _dot(r + p_ref[:, half:], w1_ref[half:, :])
    n_rows = r.shape[0]
    hidden = _gelu(lo + pltpu.roll(hi, n_rows - 1, axis=0))
    o_ref[0, 0] = _dot(hidden, w2_ref[...]).astype(o_ref.dtype)


def _compress(z, pos, w1, w2, bsz, seq):
    n_rows = seq // CMP_STRIDE
    r = z.reshape(bsz, n_rows, CMP_STRIDE, C_KV_HEADS, C_HEAD_DIM).transpose(0, 3, 1, 2, 4).reshape(
        bsz, C_KV_HEADS, n_rows, CMP_STRIDE * C_HEAD_DIM)
    w2p = _pad_heads(w2, 1).astype(BF16)
    blk = lambda w: pl.BlockSpec((1, 1, n_rows, w), lambda b, h: (b, h, 0, 0))
    return pl.pallas_call(
        _compress_body,
        grid=(bsz, C_KV_HEADS),
        in_specs=[blk(CMP_STRIDE * C_HEAD_DIM), _const_spec((1, CMP_LEN * C_HEAD_DIM)),
                  _const_spec(w1.shape), _const_spec(w2p.shape)],
        out_specs=blk(V7X_LANES),
        out_shape=jax.ShapeDtypeStruct((bsz, C_KV_HEADS, n_rows, V7X_LANES), BF16),
        compiler_params=_params("parallel", "parallel"),
        name="nsa_compress",
    )(r, pos.reshape(1, -1), w1.astype(BF16), w2p)


SEL_CHUNK = 4 * QBLK


def _stack_heads(blk):
    return jnp.concatenate(_lane_blocks(blk), axis=0)


def _rows4(x):
    return jnp.concatenate([x] * C_GROUP, axis=0)


def _nsa_body(qn_ref, qr_ref, gl_ref, kc_ref, vc_ref, ks_ref, vs_ref, kw_ref, vw_ref, ovl_ref, place_ref,
              o_ref):
    n_cmp = kc_ref.shape[2]
    n_slc = ovl_ref.shape[0]
    hd = C_HEAD_DIM
    qb = pl.program_id(2)
    s0 = qb * QBLK
    t_pos = s0 + lax.broadcasted_iota(jnp.int32, (QBLK, 1), 0)
    qn4 = _stack_heads(qn_ref[0])
    qr4 = _stack_heads(qr_ref[0])

    cmp_end = lax.broadcasted_iota(jnp.int32, (1, n_cmp), 1) * CMP_STRIDE + (CMP_LEN - 1)
    pc = _masked_softmax(_dot_nt(qn4, kc_ref[0, 0]), _rows4(cmp_end <= t_pos))
    o_c = _dot(pc, vc_ref[0, 0])
    pc_sum = (pc[:QBLK] + pc[QBLK:2 * QBLK]) + (pc[2 * QBLK:3 * QBLK] + pc[3 * QBLK:])

    p_hi, p_lo = _split_bf16(pc_sum)
    ovl = ovl_ref[...]
    imp = _dot_nt(ovl, p_hi) + _dot_nt(ovl, p_lo)
    jb = lax.broadcasted_iota(jnp.int32, (n_slc, 1), 0)
    t_row = s0 + lax.broadcasted_iota(jnp.int32, (1, QBLK), 1)
    cur = t_row // SLC_BLK
    forced = (jb == 0) | (jb == cur) | (jb == cur - 1)
    imp = jnp.where(jb * SLC_BLK <= t_row, imp + jnp.where(forced, FORCE_BONUS, 0.0), NEG_INF)
    rank = jnp.zeros((n_slc, QBLK), F32)
    for k in range(n_slc):
        ck = imp[k:k + 1, :]
        beats = (ck > imp) | ((ck == imp) & (jb > k))
        rank = rank + jnp.where(beats, 1.0, 0.0)
    not_sel = jnp.where(rank < float(min(SEL_TOPK, n_slc)), 0.0, 1.0).astype(BF16)
    not_sel_q = lax.dot_general(not_sel, place_ref[...], (((0,), (0,)), ((), ())),
                                preferred_element_type=F32)
    q_sel = qr4 + _rows4(not_sel_q.astype(BF16))

    def sel_chunk(c, m, acc, causal):
        k0 = pl.multiple_of(c * SEL_CHUNK, SEL_CHUNK)
        s = _dot_nt(q_sel, ks_ref[0, pl.ds(k0, SEL_CHUNK), :])
        if causal:
            key = k0 + lax.broadcasted_iota(jnp.int32, (1, SEL_CHUNK), 1)
            s = s + _rows4(jnp.where(key <= t_pos, 0.0, NEG_INF))
        m_new = jnp.maximum(m, jnp.max(s, axis=-1, keepdims=True))
        p = jnp.exp(s - m_new)
        acc = jnp.exp(m - m_new) * acc + _dot(p, vs_ref[0, pl.ds(k0, SEL_CHUNK), :])
        return m_new, acc

    rows = C_GROUP * QBLK
    last = qb // (SEL_CHUNK // QBLK)
    m, acc = lax.fori_loop(0, last, lambda c, ma: sel_chunk(c, ma[0], ma[1], False),
                           (jnp.full((rows, 1), NEG_INF, F32), jnp.zeros((rows, V7X_LANES), F32)))
    _, acc = sel_chunk(last, m, acc, True)
    o_s = acc[:, :hd] / jnp.maximum(acc[:, hd:hd + 1], 1e-30)

    band = QBLK + WIN
    w0 = pl.multiple_of(jnp.maximum(s0 - WIN, 0), QBLK)
    diff = t_pos - (w0 + lax.broadcasted_iota(jnp.int32, (1, band), 1))
    s = _dot_nt(qr4, kw_ref[0, pl.ds(w0, band), :]) + _rows4(
        jnp.where((diff >= 0) & (diff < WIN), 0.0, NEG_INF))
    p = jnp.exp(s - jnp.max(s, axis=-1, keepdims=True))
    acc = _dot(p, vw_ref[0, pl.ds(w0, band), :])
    o_w = acc[:, :hd] / jnp.maximum(acc[:, hd:hd + 1], 1e-30)

    gate = _sigmoid(gl_ref[0])
    outs = []
    for g in range(C_GROUP):
        r = slice(g * QBLK, (g + 1) * QBLK)
        gc = g * N_BRANCH
        outs.append(gate[:, gc:gc + 1] * o_c[r, :hd] + gate[:, gc + 1:gc + 2] * o_s[r]
                    + gate[:, gc + 2:gc + 3] * o_w[r])
    o_ref[0] = jnp.concatenate(outs, axis=1).astype(o_ref.dtype)


def _nsa(qn, qr, gl, k_cmp, v_cmp, ks, vs, kw, vw, bsz, seq):
    n_cmp = seq // CMP_STRIDE
    n_slc = seq // SLC_BLK
    assert C_HEAD_DIM + n_slc <= V7X_LANES and seq % SEL_CHUNK == 0
    ci = np.arange(n_cmp)[None, :] * CMP_STRIDE
    sj = np.arange(n_slc)[:, None] * SLC_BLK
    overlap_t = jnp.asarray(((ci < sj + SLC_BLK) & (ci + CMP_LEN > sj)).astype(np.float32)).astype(BF16)
    place = jnp.asarray(np.eye(n_slc, V7X_LANES, k=C_HEAD_DIM, dtype=np.float32)).astype(BF16)
    gw = C_GROUP * V7X_LANES
    q_spec = pl.BlockSpec((1, QBLK, gw), lambda b, h, i: (b, i, h))
    gl_spec = pl.BlockSpec((1, QBLK, V7X_LANES), lambda b, h, i: (b, i, h))
    cmp_spec = pl.BlockSpec((1, 1, n_cmp, V7X_LANES), lambda b, h, i: (b, h, 0, 0))
    kv_spec = pl.BlockSpec((1, seq, V7X_LANES), lambda b, h, i: (b, 0, h))
    as3 = lambda z: z.reshape(bsz, seq, z.shape[-1])
    return pl.pallas_call(
        _nsa_body,
        grid=(bsz, C_KV_HEADS, seq // QBLK),
        in_specs=[q_spec, q_spec, gl_spec, cmp_spec, cmp_spec, kv_spec, kv_spec, kv_spec, kv_spec,
                  _const_spec((n_slc, n_cmp)), _const_spec((n_slc, V7X_LANES))],
        out_specs=pl.BlockSpec((1, QBLK, C_GROUP * C_HEAD_DIM), lambda b, h, i: (b, i, h)),
        out_shape=jax.ShapeDtypeStruct((bsz, seq, C_WIDTH), BF16),
        compiler_params=_params("parallel", "parallel", "arbitrary"),
        name="nsa_attention",
    )(as3(qn), as3(qr), as3(gl), k_cmp, v_cmp, as3(ks), as3(vs), as3(kw), as3(vw), overlap_t, place)


ROW_CHUNKS = D_MODEL // V7X_LANES
ROW_TILE = (ROW_CHUNKS, V7X_LANES)
DMA_UNROLL = 8
PAD_SPANS = N_EXPERTS + 1


def _store_row_tiles(ref, x):
    for c in range(ROW_CHUNKS):
        ref[:, c, :] = x[:, c * V7X_LANES:(c + 1) * V7X_LANES]


def _load_row_tiles(ref):
    return jnp.concatenate([ref[:, c, :] for c in range(ROW_CHUNKS)], axis=1)


def _router_body(h_ref, g_ref, wr_ref, br_ref, tri_ref, hn_o, idx_o, gate_o, cnt_o, cnt_scr):
    @pl.when(pl.program_id(0) == 0)
    def _():
        cnt_scr[...] = jnp.zeros_like(cnt_scr)

    hn = _rms(h_ref[...], g_ref[...])
    _store_row_tiles(hn_o, hn)
    logits = _dot_f32(hn, wr_ref[...]) + br_ref[...]
    lane = lax.broadcasted_iota(jnp.int32, logits.shape, 1)
    m1 = jnp.max(logits, axis=-1, keepdims=True)
    i1 = jnp.min(jnp.where(logits == m1, lane, V7X_LANES), axis=-1, keepdims=True)
    rest = jnp.where(lane == i1, NEG_INF, logits)
    m2 = jnp.max(rest, axis=-1, keepdims=True)
    i2 = jnp.min(jnp.where(rest == m2, lane, V7X_LANES), axis=-1, keepdims=True)
    e2 = jnp.exp(m2 - m1)
    den = 1.0 + e2
    hit1, hit2 = lane == i1, lane == i2
    hits = jnp.where(hit1 | hit2, 1.0, 0.0)
    before = jnp.dot(tri_ref[...], hits.astype(BF16), preferred_element_type=F32) + cnt_scr[...]
    r1 = jnp.sum(jnp.where(hit1, before, 0.0), axis=-1, keepdims=True).astype(jnp.int32)
    r2 = jnp.sum(jnp.where(hit2, before, 0.0), axis=-1, keepdims=True).astype(jnp.int32)
    cnt_scr[...] = cnt_scr[...] + jnp.sum(hits, axis=0, keepdims=True)
    cnt_o[...] = cnt_scr[...]
    idx_o[...] = jnp.where(lane == 0, i1, jnp.where(lane == 1, i2, jnp.where(lane == 2, r1,
                           jnp.where(lane == 3, r2, 0))))
    gate_o[...] = jnp.where(lane == 0, 1.0 / den, jnp.where(lane == 1, e2 / den, 0.0))


def _router(h, gain, w_router, b_router, tm=512):
    n = h.shape[0]
    wr = jnp.pad(w_router, ((0, 0), (0, V7X_LANES - N_EXPERTS)))
    br = jnp.pad(b_router.reshape(1, -1), ((0, 0), (0, V7X_LANES - N_EXPERTS)), constant_values=NEG_INF)
    tri = jnp.asarray(np.tril(np.ones((tm, tm), np.float32), k=-1)).astype(BF16)
    return pl.pallas_call(
        _router_body,
        grid=(n // tm,),
        in_specs=[_rows_spec(tm, D_MODEL), _const_spec((1, D_MODEL)), _const_spec(wr.shape),
                  _const_spec(br.shape), _const_spec(tri.shape)],
        out_specs=[pl.BlockSpec((tm,) + ROW_TILE, lambda i: (i, 0, 0)), _rows_spec(tm, V7X_LANES),
                   _rows_spec(tm, V7X_LANES), _const_spec((1, V7X_LANES))],
        out_shape=[jax.ShapeDtypeStruct((n,) + ROW_TILE, F32), jax.ShapeDtypeStruct((n, V7X_LANES), jnp.int32),
                   jax.ShapeDtypeStruct((n, V7X_LANES), F32), jax.ShapeDtypeStruct((1, V7X_LANES), F32)],
        scratch_shapes=[pltpu.VMEM((1, V7X_LANES), F32)],
        compiler_params=_params("arbitrary"),
        name="moe_router",
    )(h, gain.reshape(1, -1), wr, br, tri)


def _row_copy(src, dst, sem):
    return pltpu.make_async_copy(src, dst, sem)


def _dispatch_body(pad_ref, dest_ref, hn_ref, xs_ref, zero_scr, sem):
    @pl.when(pl.program_id(0) == 0)
    def _():
        zero_scr[...] = jnp.zeros_like(zero_scr)
        for e in range(PAD_SPANS):
            start, count = pad_ref[e], pad_ref[PAD_SPANS + e]

            def fill(r, c):
                _row_copy(zero_scr.at[0], xs_ref.at[start + r], sem).start()
                return c

            def filled(r, c):
                _row_copy(zero_scr.at[0], xs_ref.at[0], sem).wait()
                return c

            lax.fori_loop(0, count, fill, 0)
            lax.fori_loop(0, count, filled, 0)

    def issue(blk, c):
        for u in range(DMA_UNROLL):
            r = blk * DMA_UNROLL + u
            for k in range(TOP_K):
                _row_copy(hn_ref.at[r], xs_ref.at[dest_ref[TOP_K * r + k]], sem).start(priority=k)
        return c

    lax.fori_loop(0, DMA_ROWS // DMA_UNROLL, issue, 0)

    def drain(r, c):
        for k in range(TOP_K):
            _row_copy(hn_ref.at[0], xs_ref.at[0], sem).wait()
        return c

    lax.fori_loop(0, DMA_ROWS, drain, 0)


def _dispatch(pad_info, dest, hn_rows, p_rows):
    n = hn_rows.shape[0]
    return pl.pallas_call(
        _dispatch_body,
        grid_spec=pltpu.PrefetchScalarGridSpec(
            num_scalar_prefetch=1,
            grid=(n // DMA_ROWS,),
            in_specs=[pl.BlockSpec((TOP_K * DMA_ROWS,), lambda i, pad: (i,), memory_space=pltpu.SMEM),
                      pl.BlockSpec((DMA_ROWS,) + ROW_TILE, lambda i, pad: (i, 0, 0))],
            out_specs=pl.BlockSpec(memory_space=pl.ANY),
            scratch_shapes=[pltpu.VMEM((1,) + ROW_TILE, F32), pltpu.SemaphoreType.DMA(())]),
        out_shape=jax.ShapeDtypeStruct((p_rows,) + ROW_TILE, F32),
        compiler_params=_params("arbitrary"),
        name="moe_dispatch",
    )(pad_info, dest, hn_rows)


def _experts_body(blk_e_ref, n_used_ref, x_ref, wg_ref, wu_ref, wd_ref, o_ref):
    i = pl.program_id(0)

    @pl.when(i < n_used_ref[0])
    def _():
        x = _load_row_tiles(x_ref).astype(BF16)
        gate = jnp.dot(x, wg_ref[0], preferred_element_type=F32)
        up = jnp.dot(x, wu_ref[0], preferred_element_type=F32)
        act = (gate * _sigmoid(gate)) * up
        _store_row_tiles(o_ref, _dot(act, wd_ref[0]))

    @pl.when(i >= n_used_ref[0])
    def _():
        o_ref[...] = jnp.zeros_like(o_ref)


def _experts(blk_e, n_used, xs, wg, wu, wd):
    p_rows = xs.shape[0]
    x_spec = pl.BlockSpec((MOE_ROWS,) + ROW_TILE, lambda i, be, nu: (jnp.minimum(i, nu[0] - 1), 0, 0))
    o_spec = pl.BlockSpec((MOE_ROWS,) + ROW_TILE, lambda i, be, nu: (i, 0, 0))
    w_spec = lambda w: pl.BlockSpec((1,) + w.shape[1:], lambda i, be, nu: (be[i], 0, 0))
    return pl.pallas_call(
        _experts_body,
        grid_spec=pltpu.PrefetchScalarGridSpec(
            num_scalar_prefetch=2,
            grid=(p_rows // MOE_ROWS,),
            in_specs=[x_spec, w_spec(wg), w_spec(wu), w_spec(wd)],
            out_specs=o_spec),
        out_shape=jax.ShapeDtypeStruct((p_rows,) + ROW_TILE, F32),
        compiler_params=_params("arbitrary"),
        name="moe_experts",
    )(blk_e, n_used, xs, wg, wu, wd)


def _combine_body(dest_ref, h_ref, gate_ref, g_ref, yb_ref, o_ref, y_scr, sem):
    tm = h_ref.shape[0]

    def issue(blk, c):
        for u in range(DMA_UNROLL):
            t = blk * DMA_UNROLL + u
            for k in range(TOP_K):
                _row_copy(yb_ref.at[dest_ref[TOP_K * t + k]], y_scr.at[k, t], sem).start(priority=k)
        return c

    lax.fori_loop(0, tm // DMA_UNROLL, issue, 0)

    def drain(t, c):
        for k in range(TOP_K):
            _row_copy(yb_ref.at[0], y_scr.at[0, 0], sem).wait()
        return c

    lax.fori_loop(0, tm, drain, 0)
    gate = gate_ref[...]
    moe = gate[:, 0:1] * _load_row_tiles(y_scr.at[0]) + gate[:, 1:2] * _load_row_tiles(y_scr.at[1])
    o_ref[...] = _rms(h_ref[...] + moe, g_ref[...])


def _combine(dest, h, gates, gain, yb_rows, tm=256):
    n = h.shape[0]
    return pl.pallas_call(
        _combine_body,
        grid=(n // tm,),
        in_specs=[pl.BlockSpec((TOP_K * tm,), lambda i: (i,), memory_space=pltpu.SMEM),
                  _rows_spec(tm, D_MODEL), _rows_spec(tm, V7X_LANES), _const_spec((1, D_MODEL)),
                  pl.BlockSpec(memory_space=pl.ANY)],
        out_specs=_rows_spec(tm, D_MODEL),
        out_shape=jax.ShapeDtypeStruct((n, D_MODEL), F32),
        scratch_shapes=[pltpu.VMEM((TOP_K, tm) + ROW_TILE, F32), pltpu.SemaphoreType.DMA(())],
        compiler_params=_params("arbitrary"),
        name="moe_combine_norm",
    )(dest, h, gates, gain.reshape(1, -1), yb_rows)


def _moe_layout(idx, counts):
    n = idx.shape[0]
    nk = n * TOP_K
    counts = counts[0, :N_EXPERTS].astype(jnp.int32)
    padded = ((counts + MOE_ROWS - 1) // MOE_ROWS) * MOE_ROWS
    p_end = jnp.cumsum(padded)
    p_start = p_end - padded
    experts = jnp.arange(N_EXPERTS, dtype=jnp.int32)[None, None, :]
    first = jnp.sum(jnp.where(idx[:, :TOP_K, None] == experts, p_start[None, None, :], 0), axis=-1)
    dest = (first + idx[:, TOP_K:2 * TOP_K]).reshape(nk)
    n_blk = (nk + MOE_ROWS - 1) // MOE_ROWS + N_EXPERTS
    blk_e = jnp.minimum(jnp.searchsorted(p_end, jnp.arange(n_blk) * MOE_ROWS, side="right"),
                        N_EXPERTS - 1).astype(jnp.int32)
    n_used = (p_end[-1:] // MOE_ROWS).astype(jnp.int32)
    p_rows = n_blk * MOE_ROWS
    pad_info = jnp.concatenate([p_start + counts, p_end[-1:], padded - counts,
                                p_rows - p_end[-1:]]).astype(jnp.int32)
    return dest.astype(jnp.int32), blk_e, n_used, pad_info, p_rows


def _moe_final(h, norm_g, w_router, b_router, wg, wu, wd, final_g):
    hn, idx, gates, counts = _router(h, norm_g, w_router, b_router)
    dest, blk_e, n_used, pad_info, p_rows = _moe_layout(idx, counts)
    xs = _dispatch(pad_info, dest, hn, p_rows)
    yb = _experts(blk_e, n_used, xs, wg.astype(BF16), wu.astype(BF16), wd.astype(BF16))
    return _combine(dest, h, gates, final_g, yb)


def kernel(x, e_norm_mix, e_w_in, sgu_ln_g, sgu_ln_b, sgu_w, sgu_b, rwkv_mu, rwkv_w0, rwkv_w2,
           rwkv_a0, rwkv_a2, rwkv_g2, rwkv_k_k, rwkv_k_a, rwkv_r_k, rwkv_gn_g, rwkv_gn_b, e_w_out,
           e_norm_ffn, ffn_w_gate, ffn_w_up, ffn_w_down, o_norm_mix, o_w_in, nsa_cmp_pos_k,
           nsa_cmp_pos_v, nsa_cmp_k_w1, nsa_cmp_k_w2, nsa_cmp_v_w1, nsa_cmp_v_w2, conv_w, o_w_out,
           o_norm_ffn, moe_router, moe_router_b, moe_w_gate, moe_w_up, moe_w_down, final_norm):
    bsz, seq, d = x.shape
    n = bsz * seq
    h = x.reshape(n, d)

    w_in = e_w_in[0].astype(BF16)
    p_uv, p_b = _norm_matmul(h, e_norm_mix[0], [w_in[:, :2 * A_WIDTH], w_in[:, 2 * A_WIDTH:]], [F32, F32])
    ya = _sgu(p_uv, sgu_ln_g[0], sgu_ln_b[0], sgu_w[0], sgu_b[0])
    r, w, k0, v, a, g = _rwkv_prep(p_b, seq, rwkv_mu[0], rwkv_w0[0], rwkv_w2[0], rwkv_a0[0],
                                   rwkv_a2[0], rwkv_g2[0])
    scan_in = [_to_scan_layout(z, bsz, seq) for z in (r, w, k0, v, a)]
    ys = _rwkv_scan(*scan_in, rwkv_k_k[0], rwkv_k_a[0], rwkv_r_k[0], rwkv_gn_g[0], rwkv_gn_b[0],
                    bsz, seq)
    h = _out_proj0(h, ya, _from_scan_layout(ys, bsz, seq), g, e_w_out[0].astype(BF16))
    h = _ffn(h, e_norm_ffn[0], ffn_w_gate[0].astype(BF16), ffn_w_up[0].astype(BF16),
             ffn_w_down[0].astype(BF16))

    qn, qr, kcvc, ks, vs, kw, vw, gl, bcd = _in_proj1(h, o_norm_mix[0], o_w_in[0], seq)
    k_cmp = _compress(kcvc[:, :KV_WIDTH], nsa_cmp_pos_k[0], nsa_cmp_k_w1[0], nsa_cmp_k_w2[0], bsz, seq)
    v_cmp = _compress(kcvc[:, KV_WIDTH:], nsa_cmp_pos_v[0], nsa_cmp_v_w1[0], nsa_cmp_v_w2[0], bsz, seq)
    yc = _nsa(qn, qr, gl, k_cmp, v_cmp, ks, vs, kw, vw, bsz, seq).reshape(n, C_WIDTH)
    h = _out_proj1(h, yc, bcd, conv_w[0], o_w_out[0].astype(BF16), seq)
    out = _moe_final(h, o_norm_ffn[0], moe_router[0], moe_router_b[0], moe_w_gate[0], moe_w_up[0],
                     moe_w_down[0], final_norm)
    return out.reshape(bsz, seq, d)
```

```python
import functools

import jax
import jax.numpy as jnp
import numpy as np
from jax import lax
from jax.experimental import pallas as pl
from jax.experimental.pallas import tpu as pltpu

F32 = jnp.float32
BF16 = jnp.bfloat16

D_MODEL = 1024
A_GROUPS = 4
A_GROUP_DIM = 128
A_WIDTH = A_GROUPS * A_GROUP_DIM
CHUNK = 128
SGU_LN_EPS = 1e-5
B_HEADS = 8
B_HEAD_DIM = 64
B_WIDTH = B_HEADS * B_HEAD_DIM
DECAY_LORA = 64
ICLR_LORA = 64
GATE_LORA = 128
B_IN = 3 * B_WIDTH + DECAY_LORA + ICLR_LORA + GATE_LORA
RWKV_GN_EPS = 64e-5
C_HEADS = 8
C_KV_HEADS = 2
C_GROUP = C_HEADS // C_KV_HEADS
C_HEAD_DIM = 64
C_WIDTH = C_HEADS * C_HEAD_DIM
KV_WIDTH = C_KV_HEADS * C_HEAD_DIM
N_BRANCH = 3
CMP_LEN = 32
CMP_STRIDE = 16
CMP_HIDDEN = 256
SLC_BLK = 64
SEL_TOPK = 8
WIN = 512
NSA_TQ = 256
ROT_DIM = C_HEAD_DIM // 4
ROPE_THETA = 500000.0
D_WIDTH = 512
CONV_W = 3
FFN_DIM = 2816
N_EXPERTS = 8
TOP_K = 2
EXPERT_DIM = 1408
NORM_EPS = 1e-6
NEG_INF = -1e30
FORCE_BONUS = 1e6

V7X_LANES = 128
V7X_SUBLANES = 8
V7X_VMEM_LIMIT = 56 * 1024 * 1024

MOE_ROWS = 512
DMA_ROWS = 512


def _params(*sem):
    return pltpu.CompilerParams(dimension_semantics=sem, vmem_limit_bytes=V7X_VMEM_LIMIT)


def _const_spec(shape):
    zeros = (0,) * len(shape)
    return pl.BlockSpec(shape, lambda *_: zeros)


def _rows_spec(tm, width):
    return pl.BlockSpec((tm, width), lambda i: (i, 0))


def _rms(x, g):
    return x * lax.rsqrt(jnp.mean(x * x, axis=-1, keepdims=True) + NORM_EPS) * g


def _gelu(x):
    return x * (0.5 * (1.0 + jnp.tanh(0.7978845608028654 * (x + 0.044715 * (x * x * x)))))


def _sigmoid(x):
    return 1.0 / (1.0 + jnp.exp(-x))


def _dot(a, b):
    return jnp.dot(a.astype(BF16), b.astype(BF16), preferred_element_type=F32)


def _dot_nt(a, b):
    return lax.dot_general(a.astype(BF16), b.astype(BF16), (((1,), (1,)), ((), ())),
                           preferred_element_type=F32)


def _split_bf16(a):
    hi = a.astype(BF16)
    lo = (a - hi.astype(F32)).astype(BF16)
    return hi, lo


def _dot_f32(a, b):
    ah, al = _split_bf16(a)
    bh, bl = _split_bf16(b)
    d = functools.partial(jnp.dot, preferred_element_type=F32)
    return d(ah, bh) + (d(al, bh) + d(ah, bl))


def _masked_softmax(s, mask):
    s = jnp.where(mask, s, NEG_INF)
    m = jnp.max(s, axis=-1, keepdims=True)
    p = jnp.where(mask, jnp.exp(s - m), 0.0)
    return p / jnp.maximum(jnp.sum(p, axis=-1, keepdims=True), 1e-30)


def _norm_matmul_body(x_ref, g_ref, *refs):
    k = len(refs) // 2
    xn = _rms(x_ref[...], g_ref[...]).astype(BF16)
    for w_ref, o_ref in zip(refs[:k], refs[k:]):
        o_ref[...] = jnp.dot(xn, w_ref[...], preferred_element_type=F32).astype(o_ref.dtype)


def _norm_matmul(x, gain, ws, out_dtypes, tm=256):
    n, d = x.shape
    return pl.pallas_call(
        _norm_matmul_body,
        grid=(n // tm,),
        in_specs=[_rows_spec(tm, d), _const_spec((1, d))] + [_const_spec(w.shape) for w in ws],
        out_specs=[_rows_spec(tm, w.shape[1]) for w in ws],
        out_shape=[jax.ShapeDtypeStruct((n, w.shape[1]), dt) for w, dt in zip(ws, out_dtypes)],
        compiler_params=_params("parallel"),
        name="norm_matmul",
    )(x, gain.reshape(1, d), *ws)


def _sgu_body(p_ref, lng_ref, lnb_ref, w_ref, b_ref, o_ref, *, n_chunks):
    row = lax.broadcasted_iota(jnp.int32, (CHUNK, CHUNK), 0)
    col = lax.broadcasted_iota(jnp.int32, (CHUNK, CHUNK), 1)
    causal = col <= row
    for c in range(n_chunks):
        rows = slice(c * CHUNK, (c + 1) * CHUNK)
        u = _gelu(p_ref[rows, :A_WIDTH])
        v = _gelu(p_ref[rows, A_WIDTH:])
        outs = []
        for g in range(A_GROUPS):
            cols = slice(g * A_GROUP_DIM, (g + 1) * A_GROUP_DIM)
            vg = v[:, cols]
            mu = jnp.mean(vg, axis=-1, keepdims=True)
            dv = vg - mu
            var = jnp.mean(dv * dv, axis=-1, keepdims=True)
            vn = dv * lax.rsqrt(var + SGU_LN_EPS) * lng_ref[:, cols] + lnb_ref[:, cols]
            wm = jnp.where(causal, w_ref[g], 0.0)
            mixed = _dot(wm, vn) + b_ref[:, g:g + 1]
            outs.append(u[:, cols] * mixed)
        o_ref[rows, :] = jnp.concatenate(outs, axis=1).astype(o_ref.dtype)


def _sgu(p_uv, ln_g, ln_b, w_s, b_s, tm=512):
    n = p_uv.shape[0]
    return pl.pallas_call(
        functools.partial(_sgu_body, n_chunks=tm // CHUNK),
        grid=(n // tm,),
        in_specs=[_rows_spec(tm, 2 * A_WIDTH), _const_spec((1, A_WIDTH)), _const_spec((1, A_WIDTH)),
                  _const_spec((A_GROUPS, CHUNK, CHUNK)), _const_spec((CHUNK, A_GROUPS))],
        out_specs=_rows_spec(tm, A_WIDTH),
        out_shape=jax.ShapeDtypeStruct((n, A_WIDTH), BF16),
        compiler_params=_params("parallel"),
        name="sgu",
    )(p_uv, ln_g.reshape(1, -1), ln_b.reshape(1, -1), w_s, b_s.T)


def _softplus(x):
    return jnp.maximum(x, 0.0) + jnp.log(1.0 + jnp.exp(-jnp.abs(x)))


def _rwkv_prep_body(pb_ref, prev_ref, mu_ref, w0_ref, w2_ref, a0_ref, a2_ref, g2_ref,
                    s_o, g_o, *, tiles_per_seq):
    tm = pb_ref.shape[0]
    x = pb_ref[...]
    first = (pl.program_id(0) % tiles_per_seq) == 0
    prev_row = jnp.where(first, 0.0, prev_ref[V7X_SUBLANES - 1:V7X_SUBLANES, :])
    rowid = lax.broadcasted_iota(jnp.int32, (tm, 1), 0)
    shifted = jnp.where(rowid == 0, prev_row, pltpu.roll(x, 1, axis=0))
    xm = x + (shifted - x) * mu_ref[...]
    o = 3 * B_WIDTH
    wl = xm[:, o:o + DECAY_LORA]
    al = xm[:, o + DECAY_LORA:o + DECAY_LORA + ICLR_LORA]
    gl = xm[:, o + DECAY_LORA + ICLR_LORA:]
    w = -_softplus(-(w0_ref[...] + _dot(jnp.tanh(wl), w2_ref[...]))) - 0.5
    s_o[0] = xm[:, :B_WIDTH]
    s_o[1] = jnp.exp(-jnp.exp(w))
    s_o[2] = xm[:, B_WIDTH:2 * B_WIDTH]
    s_o[3] = xm[:, 2 * B_WIDTH:3 * B_WIDTH]
    s_o[4] = _sigmoid(a0_ref[...] + _dot(al, a2_ref[...]))
    g_o[...] = _dot(_sigmoid(gl), g2_ref[...])


SCAN_OPERANDS = 5


def _rwkv_prep(p_b, seq, mu, w0, w2, a0, a2, g2, tm=256):
    n = p_b.shape[0]
    per8 = tm // V7X_SUBLANES
    outs = [jax.ShapeDtypeStruct((SCAN_OPERANDS, n, B_WIDTH), F32), jax.ShapeDtypeStruct((n, B_WIDTH), F32)]
    return pl.pallas_call(
        functools.partial(_rwkv_prep_body, tiles_per_seq=seq // tm),
        grid=(n // tm,),
        in_specs=[_rows_spec(tm, B_IN),
                  pl.BlockSpec((V7X_SUBLANES, B_IN), lambda i: (jnp.maximum(i * per8 - 1, 0), 0)),
                  _const_spec((1, B_IN)), _const_spec((1, B_WIDTH)), _const_spec((DECAY_LORA, B_WIDTH)),
                  _const_spec((1, B_WIDTH)), _const_spec((ICLR_LORA, B_WIDTH)),
                  _const_spec((GATE_LORA, B_WIDTH))],
        out_specs=[pl.BlockSpec((SCAN_OPERANDS, tm, B_WIDTH), lambda i: (0, i, 0)), _rows_spec(tm, B_WIDTH)],
        out_shape=outs,
        compiler_params=_params("parallel"),
        name="rwkv_prep",
    )(p_b, p_b, mu.reshape(1, -1), w0.reshape(1, -1), w2, a0.reshape(1, -1), a2, g2)


def _rwkv_scan_body(r_ref, w_ref, k0_ref, v_ref, a_ref, kkp_ref, kap_ref, rk_ref, gng_ref, gnb_ref,
                    y_ref, s_ref, kkn_ref, ka_ref, km_ref):
    tt, n = r_ref.shape[0], r_ref.shape[1]

    @pl.when(pl.program_id(0) == 0)
    def _():
        s_ref[...] = jnp.zeros_like(s_ref)

    k0 = k0_ref[...]
    a = a_ref[...]
    kk = k0 * kkp_ref[...][None]
    kkn = kk / jnp.maximum(jnp.sqrt(jnp.sum(kk * kk, axis=1, keepdims=True)), 1e-12)
    kkn_ref[...] = kkn
    ka_ref[...] = kkn * a
    km_ref[...] = k0 * (1.0 + (a - 1.0) * kap_ref[...][None])

    zero = jnp.zeros((n, r_ref.shape[2]), F32)

    def sa_init(j, acc):
        return acc + s_ref[j] * kkn_ref[0, pl.ds(j, 1), :]

    sa0 = lax.fori_loop(0, n, sa_init, zero)

    def step(t, sa):
        v_t = v_ref[t]
        tn = jnp.minimum(t + 1, tt - 1)

        def jbody(j, carry):
            y, san = carry
            row = pl.ds(j, 1)
            sn = s_ref[j] * w_ref[t, row, :] + (v_t * km_ref[t, row, :] - sa * ka_ref[t, row, :])
            s_ref[j] = sn
            return y + sn * r_ref[t, row, :], san + sn * kkn_ref[tn, row, :]

        y, san = lax.fori_loop(0, n, jbody, (zero, zero), unroll=8)
        y_ref[t] = y
        return san

    lax.fori_loop(0, tt, step, sa0)

    y = y_ref[...]
    ym = jnp.mean(y, axis=1, keepdims=True)
    dy = y - ym
    yv = jnp.mean(dy * dy, axis=1, keepdims=True)
    yn = dy * lax.rsqrt(yv + RWKV_GN_EPS) * gng_ref[...][None] + gnb_ref[...][None]
    bonus = jnp.sum(r_ref[...] * km_ref[...] * rk_ref[...][None], axis=1, keepdims=True) * v_ref[...]
    y_ref[...] = yn + bonus


def _rwkv_scan(ops, k_k, k_a, r_k, gn_g, gn_b, bsz, seq, tt=32):
    n = B_HEAD_DIM
    lanes = bsz * B_HEADS

    def lane_param(p):
        return jnp.tile(p.reshape(B_HEADS, n).T, (1, bsz))

    op_spec = lambda a: pl.BlockSpec((None, tt, n, lanes), lambda c: (a, c, 0, 0))
    par_spec = _const_spec((n, lanes))
    return pl.pallas_call(
        _rwkv_scan_body,
        grid=(seq // tt,),
        in_specs=[op_spec(a) for a in range(SCAN_OPERANDS)] + [par_spec] * 5,
        out_specs=pl.BlockSpec((tt, n, lanes), lambda c: (c, 0, 0)),
        out_shape=jax.ShapeDtypeStruct((seq, n, lanes), F32),
        scratch_shapes=[pltpu.VMEM((n, n, lanes), F32)] + [pltpu.VMEM((tt, n, lanes), F32)] * 3,
        compiler_params=_params("arbitrary"),
        name="rwkv_scan",
    )(*([ops] * SCAN_OPERANDS), lane_param(k_k), lane_param(k_a), lane_param(r_k.reshape(-1)),
      lane_param(gn_g), lane_param(gn_b))


SCAN_TT = 128
SCAN_UNROLL = 8


def _to_scan_body(x_ref, o_ref, a_ref):
    bsz = x_ref.shape[0]
    for b in range(bsz):
        for blk in range(B_WIDTH // V7X_LANES):
            cols = slice(blk * V7X_LANES, (blk + 1) * V7X_LANES)
            a_ref[b, cols, :] = x_ref[b, :, cols].T

    def dims(i, carry):
        for u in range(SCAN_UNROLL):
            d = i * SCAN_UNROLL + u
            z = jnp.concatenate([a_ref[b, pl.ds(d, B_HEADS, stride=B_HEAD_DIM), :] for b in range(bsz)],
                                axis=0)
            o_ref[:, d, :] = z.T
        return carry

    lax.fori_loop(0, B_HEAD_DIM // SCAN_UNROLL, dims, 0)


def _to_scan_layout(z, bsz, seq):
    na = z.shape[0]
    tt = min(SCAN_TT, seq)
    return pl.pallas_call(
        _to_scan_body,
        grid=(na, seq // tt),
        in_specs=[pl.BlockSpec((None, bsz, tt, B_WIDTH), lambda a, i: (a, 0, i, 0))],
        out_specs=pl.BlockSpec((None, tt, B_HEAD_DIM, bsz * B_HEADS), lambda a, i: (a, i, 0, 0)),
        out_shape=jax.ShapeDtypeStruct((na, seq, B_HEAD_DIM, bsz * B_HEADS), F32),
        scratch_shapes=[pltpu.VMEM((bsz, B_WIDTH, tt), F32)],
        compiler_params=_params("parallel", "arbitrary"),
        name="to_scan_layout",
    )(z.reshape(na, bsz, seq, B_WIDTH))


def _from_scan_body(y_ref, o_ref, a_ref):
    bsz = o_ref.shape[0]

    def dims(i, carry):
        for u in range(SCAN_UNROLL):
            d = i * SCAN_UNROLL + u
            zt = y_ref[:, d, :].T
            for b in range(bsz):
                a_ref[b, pl.ds(d, B_HEADS, stride=B_HEAD_DIM), :] = zt[b * B_HEADS:(b + 1) * B_HEADS, :]
        return carry

    lax.fori_loop(0, B_HEAD_DIM // SCAN_UNROLL, dims, 0)
    for b in range(bsz):
        for blk in range(B_WIDTH // V7X_LANES):
            cols = slice(blk * V7X_LANES, (blk + 1) * V7X_LANES)
            o_ref[b, :, cols] = a_ref[b, cols, :].T


def _from_scan_layout(y, bsz, seq):
    tt = min(SCAN_TT, seq)
    return pl.pallas_call(
        _from_scan_body,
        grid=(seq // tt,),
        in_specs=[pl.BlockSpec((tt, B_HEAD_DIM, bsz * B_HEADS), lambda i: (i, 0, 0))],
        out_specs=pl.BlockSpec((bsz, tt, B_WIDTH), lambda i: (0, i, 0)),
        out_shape=jax.ShapeDtypeStruct((bsz, seq, B_WIDTH), F32),
        scratch_shapes=[pltpu.VMEM((bsz, B_WIDTH, tt), F32)],
        compiler_params=_params("arbitrary"),
        name="from_scan_layout",
    )(y).reshape(bsz * seq, B_WIDTH)


def _out_proj0_body(h_ref, ya_ref, ys_ref, g_ref, w_ref, o_ref):
    yb = ys_ref[...] * g_ref[...]
    o_ref[...] = h_ref[...] + (_dot(ya_ref[...], w_ref[:A_WIDTH, :]) + _dot(yb, w_ref[A_WIDTH:, :]))


def _out_proj0(h, ya, ys, g, w_out, tm=512):
    n = h.shape[0]
    return pl.pallas_call(
        _out_proj0_body,
        grid=(n // tm,),
        in_specs=[_rows_spec(tm, D_MODEL), _rows_spec(tm, A_WIDTH), _rows_spec(tm, B_WIDTH),
                  _rows_spec(tm, B_WIDTH), _const_spec(w_out.shape)],
        out_specs=_rows_spec(tm, D_MODEL),
        out_shape=jax.ShapeDtypeStruct((n, D_MODEL), F32),
        compiler_params=_params("parallel"),
        name="out_proj0",
    )(h, ya, ys, g, w_out)


def _out_proj1_body(h_ref, yc_ref, bcd_ref, prev_ref, cw_ref, w_ref, o_ref, *, tiles_per_seq):
    tm = h_ref.shape[0]
    first = (pl.program_id(0) % tiles_per_seq) == 0
    z = bcd_ref[:, D_WIDTH:2 * D_WIDTH] * bcd_ref[:, 2 * D_WIDTH:]
    zp = jnp.where(first, 0.0, prev_ref[:, D_WIDTH:2 * D_WIDTH] * prev_ref[:, 2 * D_WIDTH:])
    rowid = lax.broadcasted_iota(jnp.int32, (tm, 1), 0)
    z1 = jnp.where(rowid == 0, zp[7:8, :], pltpu.roll(z, 1, axis=0))
    z2 = pltpu.roll(z, 2, axis=0)
    z2 = jnp.where(rowid == 0, zp[6:7, :], jnp.where(rowid == 1, zp[7:8, :], z2))
    y = cw_ref[0:1, :] * z2 + cw_ref[1:2, :] * z1 + cw_ref[2:3, :] * z
    yd = bcd_ref[:, :D_WIDTH] * y
    o_ref[...] = h_ref[...] + (_dot(yc_ref[...], w_ref[:C_WIDTH, :]) + _dot(yd, w_ref[C_WIDTH:, :]))


def _out_proj1(h, yc, bcd, conv_w, w_out, seq, tm=512):
    n = h.shape[0]
    per8 = tm // V7X_SUBLANES
    return pl.pallas_call(
        functools.partial(_out_proj1_body, tiles_per_seq=seq // tm),
        grid=(n // tm,),
        in_specs=[_rows_spec(tm, D_MODEL), _rows_spec(tm, C_WIDTH), _rows_spec(tm, 3 * D_WIDTH),
                  pl.BlockSpec((V7X_SUBLANES, 3 * D_WIDTH), lambda i: (jnp.maximum(i * per8 - 1, 0), 0)),
                  _const_spec((CONV_W, D_WIDTH)), _const_spec(w_out.shape)],
        out_specs=_rows_spec(tm, D_MODEL),
        out_shape=jax.ShapeDtypeStruct((n, D_MODEL), F32),
        compiler_params=_params("parallel"),
        name="out_proj1",
    )(h, yc, bcd, bcd, conv_w, w_out)


def _ffn_body(h_ref, g_ref, wg_ref, wu_ref, wd_ref, o_ref):
    h = h_ref[...]
    hn = _rms(h, g_ref[...]).astype(BF16)
    gate = jnp.dot(hn, wg_ref[...], preferred_element_type=F32)
    up = jnp.dot(hn, wu_ref[...], preferred_element_type=F32)
    act = (gate * _sigmoid(gate)) * up
    o_ref[...] = h + _dot(act, wd_ref[...])


def _ffn(h, gain, wg, wu, wd, tm=256):
    n = h.shape[0]
    return pl.pallas_call(
        _ffn_body,
        grid=(n // tm,),
        in_specs=[_rows_spec(tm, D_MODEL), _const_spec((1, D_MODEL)), _const_spec(wg.shape),
                  _const_spec(wu.shape), _const_spec(wd.shape)],
        out_specs=_rows_spec(tm, D_MODEL),
        out_shape=jax.ShapeDtypeStruct((n, D_MODEL), F32),
        compiler_params=_params("parallel"),
        name="ffn",
    )(h, gain.reshape(1, -1), wg, wu, wd)


def _rope(x, c, s_up, s_dn):
    half = ROT_DIM // 2
    return x * c + pltpu.roll(x, half, axis=1) * s_up + pltpu.roll(x, V7X_LANES - half, axis=1) * s_dn


def _lane_blocks(x):
    return [x[:, i * V7X_LANES:(i + 1) * V7X_LANES] for i in range(x.shape[1] // V7X_LANES)]


def _in_proj1_body(x_ref, g_ref, c_ref, su_ref, sd_ref, kb_ref, one_ref, wq_ref, wkc_ref, wkv_ref,
                   wgl_ref, wbcd_ref, qn_o, qr_o, kc_o, ks_o, vs_o, kw_o, vw_o, gl_o, bcd_o):
    xn = _rms(x_ref[...], g_ref[...]).astype(BF16)
    d = functools.partial(jnp.dot, preferred_element_type=F32)
    c, su, sd = c_ref[...], su_ref[...], sd_ref[...]
    rope = lambda z: _rope(z, c, su, sd)
    q = d(xn, wq_ref[...]) * (C_HEAD_DIM ** -0.5)
    qn_o[...] = q.astype(BF16)
    qr_o[...] = jnp.concatenate([rope(z) for z in _lane_blocks(q)], axis=1).astype(BF16)
    kc_o[...] = d(xn, wkc_ref[...])
    kv = _lane_blocks(d(xn, wkv_ref[...]))
    hk = C_KV_HEADS
    ks_o[...] = jnp.concatenate([rope(z) + kb_ref[...] for z in kv[:hk]], axis=1).astype(BF16)
    vs_o[...] = jnp.concatenate([z + one_ref[...] for z in kv[hk:2 * hk]], axis=1).astype(BF16)
    kw_o[...] = jnp.concatenate([rope(z) for z in kv[2 * hk:3 * hk]], axis=1).astype(BF16)
    vw_o[...] = jnp.concatenate([z + one_ref[...] for z in kv[3 * hk:]], axis=1).astype(BF16)
    gl_o[...] = d(xn, wgl_ref[...])
    bcd_o[...] = d(xn, wbcd_ref[...])


def _head_tables(seq):
    half = ROT_DIM // 2
    pos = jnp.arange(seq, dtype=F32)
    inv_freq = ROPE_THETA ** (-jnp.arange(0, ROT_DIM, 2, dtype=F32) / ROT_DIM)
    ang = pos[:, None] * inv_freq[None, :]
    cos, sin = jnp.cos(ang), jnp.sin(ang)
    pad = jnp.zeros((seq, V7X_LANES - ROT_DIM), F32)
    zeros = jnp.zeros((seq, half), F32)
    c = jnp.concatenate([cos, cos, pad + 1.0], axis=1)
    s_up = jnp.concatenate([zeros, sin, pad], axis=1)
    s_dn = jnp.concatenate([-sin, zeros, pad], axis=1)
    lane = jnp.arange(V7X_LANES)[None, :]
    blk = (jnp.arange(seq) // SLC_BLK)[:, None]
    k_bias = jnp.where(lane == C_HEAD_DIM + blk, NEG_INF, 0.0).astype(F32)
    ones = (lane == C_HEAD_DIM).astype(F32)
    return c, s_up, s_dn, k_bias, ones


def _pad_heads(w, n_heads):
    width = w.shape[1] // n_heads
    w = w.reshape(w.shape[0], n_heads, width)
    return jnp.pad(w, ((0, 0), (0, 0), (0, V7X_LANES - width))).reshape(w.shape[0], n_heads * V7X_LANES)


def _in_proj1(x, gain, w_in, seq, tm=256):
    n, d = x.shape
    o = np.cumsum([0, C_WIDTH] + [KV_WIDTH] * 6 + [C_HEADS * N_BRANCH] + [D_WIDTH] * 3)
    wq = _pad_heads(w_in[:, o[0]:o[1]], C_HEADS).astype(BF16)
    wkc = w_in[:, o[1]:o[3]].astype(BF16)
    wkv = _pad_heads(w_in[:, o[3]:o[7]], 4 * C_KV_HEADS).astype(BF16)
    wgl = _pad_heads(w_in[:, o[7]:o[8]], C_KV_HEADS).astype(BF16)
    wbcd = w_in[:, o[8]:o[11]].astype(BF16)
    ws = [wq, wkc, wkv, wgl, wbcd]
    kvw = C_KV_HEADS * V7X_LANES
    widths = [C_HEADS * V7X_LANES] * 2 + [2 * KV_WIDTH] + [kvw] * 4 + [kvw, 3 * D_WIDTH]
    dts = [BF16, BF16, F32, BF16, BF16, BF16, BF16, F32, F32]
    tps = seq // tm
    tab_spec = pl.BlockSpec((tm, V7X_LANES), lambda i: (i % tps, 0))
    c, s_up, s_dn, k_bias, ones = _head_tables(seq)
    return pl.pallas_call(
        _in_proj1_body,
        grid=(n // tm,),
        in_specs=[_rows_spec(tm, d), _const_spec((1, d)), tab_spec, tab_spec, tab_spec, tab_spec,
                  _const_spec((1, V7X_LANES))] + [_const_spec(w.shape) for w in ws],
        out_specs=[_rows_spec(tm, wd) for wd in widths],
        out_shape=[jax.ShapeDtypeStruct((n, wd), dt) for wd, dt in zip(widths, dts)],
        compiler_params=_params("parallel"),
        name="in_proj1",
    )(x, gain.reshape(1, d), c, s_up, s_dn, k_bias, ones, *ws)


def _compress_body(r_ref, p_ref, w1_ref, w2_ref, o_ref):
    half = CMP_STRIDE * C_HEAD_DIM
    r = r_ref[0, 0]
    lo = _dot(r + p_ref[:, :half], w1_ref[:half, :])
    hi = _dot(r + p_ref[:, half:], w1_ref[half:, :])
    n_rows = r.shape[0]
    hidden = _gelu(lo + pltpu.roll(hi, n_rows - 1, axis=0))
    o_ref[0, 0] = _dot(hidden, w2_ref[...]).astype(o_ref.dtype)


def _compress(z, pos, w1, w2, bsz, seq):
    n_rows = seq // CMP_STRIDE
    r = z.reshape(bsz, n_rows, CMP_STRIDE, C_KV_HEADS, C_HEAD_DIM).transpose(0, 3, 1, 2, 4).reshape(
        bsz, C_KV_HEADS, n_rows, CMP_STRIDE * C_HEAD_DIM)
    w2p = _pad_heads(w2, 1).astype(BF16)
    blk = lambda w: pl.BlockSpec((1, 1, n_rows, w), lambda b, h: (b, h, 0, 0))
    return pl.pallas_call(
        _compress_body,
        grid=(bsz, C_KV_HEADS),
        in_specs=[blk(CMP_STRIDE * C_HEAD_DIM), _const_spec((1, CMP_LEN * C_HEAD_DIM)),
                  _const_spec(w1.shape), _const_spec(w2p.shape)],
        out_specs=blk(V7X_LANES),
        out_shape=jax.ShapeDtypeStruct((bsz, C_KV_HEADS, n_rows, V7X_LANES), BF16),
        compiler_params=_params("parallel", "parallel"),
        name="nsa_compress",
    )(r, pos.reshape(1, -1), w1.astype(BF16), w2p)


SEL_CHUNK = 512


def _stack_heads(blk):
    return jnp.concatenate(_lane_blocks(blk), axis=0)


def _rows4(x):
    return jnp.concatenate([x] * C_GROUP, axis=0)


def _nsa_body(qn_ref, qr_ref, gl_ref, kc_ref, vc_ref, ks_ref, vs_ref, kw_ref, vw_ref, ovl_ref, place_ref,
              o_ref):
    n_cmp = kc_ref.shape[2]
    n_slc = ovl_ref.shape[0]
    hd = C_HEAD_DIM
    qb = pl.program_id(2)
    s0 = qb * NSA_TQ
    t_pos = s0 + lax.broadcasted_iota(jnp.int32, (NSA_TQ, 1), 0)
    qn4 = _stack_heads(qn_ref[0])
    qr4 = _stack_heads(qr_ref[0])

    cmp_end = lax.broadcasted_iota(jnp.int32, (1, n_cmp), 1) * CMP_STRIDE + (CMP_LEN - 1)
    pc = _masked_softmax(_dot_nt(qn4, kc_ref[0, 0]), _rows4(cmp_end <= t_pos))
    o_c = _dot(pc, vc_ref[0, 0])
    pc_sum = (pc[:NSA_TQ] + pc[NSA_TQ:2 * NSA_TQ]) + (pc[2 * NSA_TQ:3 * NSA_TQ] + pc[3 * NSA_TQ:])

    p_hi, p_lo = _split_bf16(pc_sum)
    ovl = ovl_ref[...]
    imp = _dot_nt(ovl, p_hi) + _dot_nt(ovl, p_lo)
    jb = lax.broadcasted_iota(jnp.int32, (n_slc, 1), 0)
    t_row = s0 + lax.broadcasted_iota(jnp.int32, (1, NSA_TQ), 1)
    cur = t_row // SLC_BLK
    forced = (jb == 0) | (jb == cur) | (jb == cur - 1)
    imp = jnp.where(jb * SLC_BLK <= t_row, imp + jnp.where(forced, FORCE_BONUS, 0.0), NEG_INF)
    rank = jnp.zeros((n_slc, NSA_TQ), F32)
    for k in range(n_slc):
        ck = imp[k:k + 1, :]
        beats = (ck > imp) | ((ck == imp) & (jb > k))
        rank = rank + jnp.where(beats, 1.0, 0.0)
    not_sel = jnp.where(rank < float(min(SEL_TOPK, n_slc)), 0.0, 1.0).astype(BF16)
    not_sel_q = lax.dot_general(not_sel, place_ref[...], (((0,), (0,)), ((), ())),
                                preferred_element_type=F32)
    q_sel = qr4 + _rows4(not_sel_q.astype(BF16))

    def sel_chunk(c, m, acc, causal):
        k0 = pl.multiple_of(c * SEL_CHUNK, SEL_CHUNK)
        s = _dot_nt(q_sel, ks_ref[0, pl.ds(k0, SEL_CHUNK), :])
        if causal:
            key = k0 + lax.broadcasted_iota(jnp.int32, (1, SEL_CHUNK), 1)
            s = s + _rows4(jnp.where(key <= t_pos, 0.0, NEG_INF))
        m_new = jnp.maximum(m, jnp.max(s, axis=-1, keepdims=True))
        p = jnp.exp(s - m_new)
        acc = jnp.exp(m - m_new) * acc + _dot(p, vs_ref[0, pl.ds(k0, SEL_CHUNK), :])
        return m_new, acc

    rows = C_GROUP * NSA_TQ
    last = qb // (SEL_CHUNK // NSA_TQ)
    m, acc = lax.fori_loop(0, last, lambda c, ma: sel_chunk(c, ma[0], ma[1], False),
                           (jnp.full((rows, 1), NEG_INF, F32), jnp.zeros((rows, V7X_LANES), F32)))
    _, acc = sel_chunk(last, m, acc, True)
    o_s = acc[:, :hd] / jnp.maximum(acc[:, hd:hd + 1], 1e-30)

    band = NSA_TQ + WIN
    w0 = pl.multiple_of(jnp.maximum(s0 - WIN, 0), NSA_TQ)
    diff = t_pos - (w0 + lax.broadcasted_iota(jnp.int32, (1, band), 1))
    s = _dot_nt(qr4, kw_ref[0, pl.ds(w0, band), :]) + _rows4(
        jnp.where((diff >= 0) & (diff < WIN), 0.0, NEG_INF))
    p = jnp.exp(s - jnp.max(s, axis=-1, keepdims=True))
    acc = _dot(p, vw_ref[0, pl.ds(w0, band), :])
    o_w = acc[:, :hd] / jnp.maximum(acc[:, hd:hd + 1], 1e-30)

    gate = _sigmoid(gl_ref[0])
    outs = []
    for g in range(C_GROUP):
        r = slice(g * NSA_TQ, (g + 1) * NSA_TQ)
        gc = g * N_BRANCH
        outs.append(gate[:, gc:gc + 1] * o_c[r, :hd] + gate[:, gc + 1:gc + 2] * o_s[r]
                    + gate[:, gc + 2:gc + 3] * o_w[r])
    o_ref[0] = jnp.concatenate(outs, axis=1).astype(o_ref.dtype)


def _nsa(qn, qr, gl, k_cmp, v_cmp, ks, vs, kw, vw, bsz, seq):
    n_cmp = seq // CMP_STRIDE
    n_slc = seq // SLC_BLK
    assert C_HEAD_DIM + n_slc <= V7X_LANES and seq % SEL_CHUNK == 0
    ci = np.arange(n_cmp)[None, :] * CMP_STRIDE
    sj = np.arange(n_slc)[:, None] * SLC_BLK
    overlap_t = jnp.asarray(((ci < sj + SLC_BLK) & (ci + CMP_LEN > sj)).astype(np.float32)).astype(BF16)
    place = jnp.asarray(np.eye(n_slc, V7X_LANES, k=C_HEAD_DIM, dtype=np.float32)).astype(BF16)
    gw = C_GROUP * V7X_LANES
    q_spec = pl.BlockSpec((1, NSA_TQ, gw), lambda b, h, i: (b, i, h))
    gl_spec = pl.BlockSpec((1, NSA_TQ, V7X_LANES), lambda b, h, i: (b, i, h))
    cmp_spec = pl.BlockSpec((1, 1, n_cmp, V7X_LANES), lambda b, h, i: (b, h, 0, 0))
    kv_spec = pl.BlockSpec((1, seq, V7X_LANES), lambda b, h, i: (b, 0, h))
    as3 = lambda z: z.reshape(bsz, seq, z.shape[-1])
    return pl.pallas_call(
        _nsa_body,
        grid=(bsz, C_KV_HEADS, seq // NSA_TQ),
        in_specs=[q_spec, q_spec, gl_spec, cmp_spec, cmp_spec, kv_spec, kv_spec, kv_spec, kv_spec,
                  _const_spec((n_slc, n_cmp)), _const_spec((n_slc, V7X_LANES))],
        out_specs=pl.BlockSpec((1, NSA_TQ, C_GROUP * C_HEAD_DIM), lambda b, h, i: (b, i, h)),
        out_shape=jax.ShapeDtypeStruct((bsz, seq, C_WIDTH), BF16),
        compiler_params=_params("parallel", "parallel", "arbitrary"),
        name="nsa_attention",
    )(as3(qn), as3(qr), as3(gl), k_cmp, v_cmp, as3(ks), as3(vs), as3(kw), as3(vw), overlap_t, place)


ROW_CHUNKS = D_MODEL // V7X_LANES
ROW_TILE = (ROW_CHUNKS, V7X_LANES)
DMA_UNROLL = 8
PAD_SPANS = N_EXPERTS + 1


def _store_row_tiles(ref, x):
    for c in range(ROW_CHUNKS):
        ref[:, c, :] = x[:, c * V7X_LANES:(c + 1) * V7X_LANES]


def _load_row_tiles(ref):
    return jnp.concatenate([ref[:, c, :] for c in range(ROW_CHUNKS)], axis=1)


def _router_body(h_ref, g_ref, wr_ref, br_ref, tri_ref, hn_o, idx_o, gate_o, cnt_o, cnt_scr):
    @pl.when(pl.program_id(0) == 0)
    def _():
        cnt_scr[...] = jnp.zeros_like(cnt_scr)

    hn = _rms(h_ref[...], g_ref[...])
    _store_row_tiles(hn_o, hn)
    logits = _dot_f32(hn, wr_ref[...]) + br_ref[...]
    lane = lax.broadcasted_iota(jnp.int32, logits.shape, 1)
    m1 = jnp.max(logits, axis=-1, keepdims=True)
    i1 = jnp.min(jnp.where(logits == m1, lane, V7X_LANES), axis=-1, keepdims=True)
    rest = jnp.where(lane == i1, NEG_INF, logits)
    m2 = jnp.max(rest, axis=-1, keepdims=True)
    i2 = jnp.min(jnp.where(rest == m2, lane, V7X_LANES), axis=-1, keepdims=True)
    e2 = jnp.exp(m2 - m1)
    den = 1.0 + e2
    hit1, hit2 = lane == i1, lane == i2
    hits = jnp.where(hit1 | hit2, 1.0, 0.0)
    before = jnp.dot(tri_ref[...], hits.astype(BF16), preferred_element_type=F32) + cnt_scr[...]
    r1 = jnp.sum(jnp.where(hit1, before, 0.0), axis=-1, keepdims=True).astype(jnp.int32)
    r2 = jnp.sum(jnp.where(hit2, before, 0.0), axis=-1, keepdims=True).astype(jnp.int32)
    cnt_scr[...] = cnt_scr[...] + jnp.sum(hits, axis=0, keepdims=True)
    cnt_o[...] = cnt_scr[...]
    idx_o[...] = jnp.where(lane == 0, i1, jnp.where(lane == 1, i2, jnp.where(lane == 2, r1,
                           jnp.where(lane == 3, r2, 0))))
    gate_o[...] = jnp.where(lane == 0, 1.0 / den, jnp.where(lane == 1, e2 / den, 0.0))


def _router(h, gain, w_router, b_router, tm=512):
    n = h.shape[0]
    wr = jnp.pad(w_router, ((0, 0), (0, V7X_LANES - N_EXPERTS)))
    br = jnp.pad(b_router.reshape(1, -1), ((0, 0), (0, V7X_LANES - N_EXPERTS)), constant_values=NEG_INF)
    tri = jnp.asarray(np.tril(np.ones((tm, tm), np.float32), k=-1)).astype(BF16)
    return pl.pallas_call(
        _router_body,
        grid=(n // tm,),
        in_specs=[_rows_spec(tm, D_MODEL), _const_spec((1, D_MODEL)), _const_spec(wr.shape),
                  _const_spec(br.shape), _const_spec(tri.shape)],
        out_specs=[pl.BlockSpec((tm,) + ROW_TILE, lambda i: (i, 0, 0)), _rows_spec(tm, V7X_LANES),
                   _rows_spec(tm, V7X_LANES), _const_spec((1, V7X_LANES))],
        out_shape=[jax.ShapeDtypeStruct((n,) + ROW_TILE, F32), jax.ShapeDtypeStruct((n, V7X_LANES), jnp.int32),
                   jax.ShapeDtypeStruct((n, V7X_LANES), F32), jax.ShapeDtypeStruct((1, V7X_LANES), F32)],
        scratch_shapes=[pltpu.VMEM((1, V7X_LANES), F32)],
        compiler_params=_params("arbitrary"),
        name="moe_router",
    )(h, gain.reshape(1, -1), wr, br, tri)


def _row_copy(src, dst, sem):
    return pltpu.make_async_copy(src, dst, sem)


def _dispatch_body(pad_ref, dest_ref, hn_ref, xs_ref, zero_scr, sem):
    @pl.when(pl.program_id(0) == 0)
    def _():
        zero_scr[...] = jnp.zeros_like(zero_scr)
        for e in range(PAD_SPANS):
            start, count = pad_ref[e], pad_ref[PAD_SPANS + e]

            def fill(r, c):
                _row_copy(zero_scr.at[0], xs_ref.at[start + r], sem).start()
                return c

            def filled(r, c):
                _row_copy(zero_scr.at[0], xs_ref.at[0], sem).wait()
                return c

            lax.fori_loop(0, count, fill, 0)
            lax.fori_loop(0, count, filled, 0)

    def issue(blk, c):
        for u in range(DMA_UNROLL):
            r = blk * DMA_UNROLL + u
            for k in range(TOP_K):
                _row_copy(hn_ref.at[r], xs_ref.at[dest_ref[TOP_K * r + k]], sem).start(priority=k)
        return c

    lax.fori_loop(0, DMA_ROWS // DMA_UNROLL, issue, 0)

    def drain(r, c):
        for k in range(TOP_K):
            _row_copy(hn_ref.at[0], xs_ref.at[0], sem).wait()
        return c

    lax.fori_loop(0, DMA_ROWS, drain, 0)


def _dispatch(pad_info, dest, hn_rows, p_rows):
    n = hn_rows.shape[0]
    return pl.pallas_call(
        _dispatch_body,
        grid_spec=pltpu.PrefetchScalarGridSpec(
            num_scalar_prefetch=1,
            grid=(n // DMA_ROWS,),
            in_specs=[pl.BlockSpec((TOP_K * DMA_ROWS,), lambda i, pad: (i,), memory_space=pltpu.SMEM),
                      pl.BlockSpec((DMA_ROWS,) + ROW_TILE, lambda i, pad: (i, 0, 0))],
            out_specs=pl.BlockSpec(memory_space=pl.ANY),
            scratch_shapes=[pltpu.VMEM((1,) + ROW_TILE, F32), pltpu.SemaphoreType.DMA(())]),
        out_shape=jax.ShapeDtypeStruct((p_rows,) + ROW_TILE, F32),
        compiler_params=_params("arbitrary"),
        name="moe_dispatch",
    )(pad_info, dest, hn_rows)


def _experts_body(blk_e_ref, n_used_ref, x_ref, wg_ref, wu_ref, wd_ref, o_ref):
    i = pl.program_id(0)

    @pl.when(i < n_used_ref[0])
    def _():
        x = _load_row_tiles(x_ref).astype(BF16)
        gate = jnp.dot(x, wg_ref[0], preferred_element_type=F32)
        up = jnp.dot(x, wu_ref[0], preferred_element_type=F32)
        act = (gate * _sigmoid(gate)) * up
        _store_row_tiles(o_ref, _dot(act, wd_ref[0]))

    @pl.when(i >= n_used_ref[0])
    def _():
        o_ref[...] = jnp.zeros_like(o_ref)


def _experts(blk_e, n_used, xs, wg, wu, wd):
    p_rows = xs.shape[0]
    x_spec = pl.BlockSpec((MOE_ROWS,) + ROW_TILE, lambda i, be, nu: (jnp.minimum(i, nu[0] - 1), 0, 0))
    o_spec = pl.BlockSpec((MOE_ROWS,) + ROW_TILE, lambda i, be, nu: (i, 0, 0))
    w_spec = lambda w: pl.BlockSpec((1,) + w.shape[1:], lambda i, be, nu: (be[i], 0, 0))
    return pl.pallas_call(
        _experts_body,
        grid_spec=pltpu.PrefetchScalarGridSpec(
            num_scalar_prefetch=2,
            grid=(p_rows // MOE_ROWS,),
            in_specs=[x_spec, w_spec(wg), w_spec(wu), w_spec(wd)],
            out_specs=o_spec),
        out_shape=jax.ShapeDtypeStruct((p_rows,) + ROW_TILE, F32),
        compiler_params=_params("arbitrary"),
        name="moe_experts",
    )(blk_e, n_used, xs, wg, wu, wd)


def _combine_body(dest_ref, h_ref, gate_ref, g_ref, yb_ref, o_ref, y_scr, sem):
    tm = h_ref.shape[0]

    def issue(blk, c):
        for u in range(DMA_UNROLL):
            t = blk * DMA_UNROLL + u
            for k in range(TOP_K):
                _row_copy(yb_ref.at[dest_ref[TOP_K * t + k]], y_scr.at[k, t], sem).start(priority=k)
        return c

    lax.fori_loop(0, tm // DMA_UNROLL, issue, 0)

    def drain(t, c):
        for k in range(TOP_K):
            _row_copy(yb_ref.at[0], y_scr.at[0, 0], sem).wait()
        return c

    lax.fori_loop(0, tm, drain, 0)
    gate = gate_ref[...]
    moe = gate[:, 0:1] * _load_row_tiles(y_scr.at[0]) + gate[:, 1:2] * _load_row_tiles(y_scr.at[1])
    o_ref[...] = _rms(h_ref[...] + moe, g_ref[...])


def _combine(dest, h, gates, gain, yb_rows, tm=256):
    n = h.shape[0]
    return pl.pallas_call(
        _combine_body,
        grid=(n // tm,),
        in_specs=[pl.BlockSpec((TOP_K * tm,), lambda i: (i,), memory_space=pltpu.SMEM),
                  _rows_spec(tm, D_MODEL), _rows_spec(tm, V7X_LANES), _const_spec((1, D_MODEL)),
                  pl.BlockSpec(memory_space=pl.ANY)],
        out_specs=_rows_spec(tm, D_MODEL),
        out_shape=jax.ShapeDtypeStruct((n, D_MODEL), F32),
        scratch_shapes=[pltpu.VMEM((TOP_K, tm) + ROW_TILE, F32), pltpu.SemaphoreType.DMA(())],
        compiler_params=_params("arbitrary"),
        name="moe_combine_norm",
    )(dest, h, gates, gain.reshape(1, -1), yb_rows)


def _moe_layout(idx, counts):
    n = idx.shape[0]
    nk = n * TOP_K
    counts = counts[0, :N_EXPERTS].astype(jnp.int32)
    padded = ((counts + MOE_ROWS - 1) // MOE_ROWS) * MOE_ROWS
    p_end = jnp.cumsum(padded)
    p_start = p_end - padded
    experts = jnp.arange(N_EXPERTS, dtype=jnp.int32)[None, None, :]
    first = jnp.sum(jnp.where(idx[:, :TOP_K, None] == experts, p_start[None, None, :], 0), axis=-1)
    dest = (first + idx[:, TOP_K:2 * TOP_K]).reshape(nk)
    n_blk = (nk + MOE_ROWS - 1) // MOE_ROWS + N_EXPERTS
    blk_e = jnp.minimum(jnp.searchsorted(p_end, jnp.arange(n_blk) * MOE_ROWS, side="right"),
                        N_EXPERTS - 1).astype(jnp.int32)
    n_used = (p_end[-1:] // MOE_ROWS).astype(jnp.int32)
    p_rows = n_blk * MOE_ROWS
    pad_info = jnp.concatenate([p_start + counts, p_end[-1:], padded - counts,
                                p_rows - p_end[-1:]]).astype(jnp.int32)
    return dest.astype(jnp.int32), blk_e, n_used, pad_info, p_rows


def _moe_final(h, norm_g, w_router, b_router, wg, wu, wd, final_g):
    hn, idx, gates, counts = _router(h, norm_g, w_router, b_router)
    dest, blk_e, n_used, pad_info, p_rows = _moe_layout(idx, counts)
    xs = _dispatch(pad_info, dest, hn, p_rows)
    yb = _experts(blk_e, n_used, xs, wg.astype(BF16), wu.astype(BF16), wd.astype(BF16))
    return _combine(dest, h, gates, final_g, yb)


def kernel(x, e_norm_mix, e_w_in, sgu_ln_g, sgu_ln_b, sgu_w, sgu_b, rwkv_mu, rwkv_w0, rwkv_w2,
           rwkv_a0, rwkv_a2, rwkv_g2, rwkv_k_k, rwkv_k_a, rwkv_r_k, rwkv_gn_g, rwkv_gn_b, e_w_out,
           e_norm_ffn, ffn_w_gate, ffn_w_up, ffn_w_down, o_norm_mix, o_w_in, nsa_cmp_pos_k,
           nsa_cmp_pos_v, nsa_cmp_k_w1, nsa_cmp_k_w2, nsa_cmp_v_w1, nsa_cmp_v_w2, conv_w, o_w_out,
           o_norm_ffn, moe_router, moe_router_b, moe_w_gate, moe_w_up, moe_w_down, final_norm):
    bsz, seq, d = x.shape
    n = bsz * seq
    h = x.reshape(n, d)

    w_in = e_w_in[0].astype(BF16)
    p_uv, p_b = _norm_matmul(h, e_norm_mix[0], [w_in[:, :2 * A_WIDTH], w_in[:, 2 * A_WIDTH:]], [F32, F32])
    ya = _sgu(p_uv, sgu_ln_g[0], sgu_ln_b[0], sgu_w[0], sgu_b[0])
    scan_ops, g = _rwkv_prep(p_b, seq, rwkv_mu[0], rwkv_w0[0], rwkv_w2[0], rwkv_a0[0],
                             rwkv_a2[0], rwkv_g2[0])
    ys = _rwkv_scan(_to_scan_layout(scan_ops, bsz, seq), rwkv_k_k[0], rwkv_k_a[0], rwkv_r_k[0],
                    rwkv_gn_g[0], rwkv_gn_b[0], bsz, seq)
    h = _out_proj0(h, ya, _from_scan_layout(ys, bsz, seq), g, e_w_out[0].astype(BF16))
    h = _ffn(h, e_norm_ffn[0], ffn_w_gate[0].astype(BF16), ffn_w_up[0].astype(BF16),
             ffn_w_down[0].astype(BF16))

    qn, qr, kcvc, ks, vs, kw, vw, gl, bcd = _in_proj1(h, o_norm_mix[0], o_w_in[0], seq)
    k_cmp = _compress(kcvc[:, :KV_WIDTH], nsa_cmp_pos_k[0], nsa_cmp_k_w1[0], nsa_cmp_k_w2[0], bsz, seq)
    v_cmp = _compress(kcvc[:, KV_WIDTH:], nsa_cmp_pos_v[0], nsa_cmp_v_w1[0], nsa_cmp_v_w2[0], bsz, seq)
    yc = _nsa(qn, qr, gl, k_cmp, v_cmp, ks, vs, kw, vw, bsz, seq).reshape(n, C_WIDTH)
    h = _out_proj1(h, yc, bcd, conv_w[0], o_w_out[0].astype(BF16), seq)
    out = _moe_final(h, o_norm_ffn[0], moe_router[0], moe_router_b[0], moe_w_gate[0], moe_w_up[0],
                     moe_w_down[0], final_norm)
    return out.reshape(bsz, seq, d)
```

```python
import functools

import jax
import jax.numpy as jnp
import numpy as np
from jax import lax
from jax.experimental import pallas as pl
from jax.experimental.pallas import tpu as pltpu

F32 = jnp.float32
BF16 = jnp.bfloat16

D_MODEL = 1024
A_GROUPS = 4
A_GROUP_DIM = 128
A_WIDTH = A_GROUPS * A_GROUP_DIM
CHUNK = 128
SGU_LN_EPS = 1e-5
B_HEADS = 8
B_HEAD_DIM = 64
B_WIDTH = B_HEADS * B_HEAD_DIM
DECAY_LORA = 64
ICLR_LORA = 64
GATE_LORA = 128
B_IN = 3 * B_WIDTH + DECAY_LORA + ICLR_LORA + GATE_LORA
RWKV_GN_EPS = 64e-5
C_HEADS = 8
C_KV_HEADS = 2
C_GROUP = C_HEADS // C_KV_HEADS
C_HEAD_DIM = 64
C_WIDTH = C_HEADS * C_HEAD_DIM
KV_WIDTH = C_KV_HEADS * C_HEAD_DIM
N_BRANCH = 3
CMP_LEN = 32
CMP_STRIDE = 16
CMP_HIDDEN = 256
SLC_BLK = 64
SEL_TOPK = 8
WIN = 512
NSA_TQ = 256
ROT_DIM = C_HEAD_DIM // 4
ROPE_THETA = 500000.0
D_WIDTH = 512
CONV_W = 3
FFN_DIM = 2816
N_EXPERTS = 8
TOP_K = 2
EXPERT_DIM = 1408
NORM_EPS = 1e-6
NEG_INF = -1e30
FORCE_BONUS = 1e6

V7X_LANES = 128
V7X_SUBLANES = 8
V7X_VMEM_LIMIT = 56 * 1024 * 1024

MOE_ROWS = 512
DMA_ROWS = 512


def _params(*sem):
    return pltpu.CompilerParams(dimension_semantics=sem, vmem_limit_bytes=V7X_VMEM_LIMIT)


def _const_spec(shape):
    zeros = (0,) * len(shape)
    return pl.BlockSpec(shape, lambda *_: zeros)


def _rows_spec(tm, width):
    return pl.BlockSpec((tm, width), lambda i: (i, 0))


def _rms(x, g):
    return x * lax.rsqrt(jnp.mean(x * x, axis=-1, keepdims=True) + NORM_EPS) * g


def _gelu(x):
    return x * (0.5 * (1.0 + jnp.tanh(0.7978845608028654 * (x + 0.044715 * (x * x * x)))))


def _sigmoid(x):
    return 1.0 / (1.0 + jnp.exp(-x))


def _dot(a, b):
    return jnp.dot(a.astype(BF16), b.astype(BF16), preferred_element_type=F32)


def _dot_nt(a, b):
    return lax.dot_general(a.astype(BF16), b.astype(BF16), (((1,), (1,)), ((), ())),
                           preferred_element_type=F32)


def _split_bf16(a):
    hi = a.astype(BF16)
    lo = (a - hi.astype(F32)).astype(BF16)
    return hi, lo


def _dot_f32(a, b):
    ah, al = _split_bf16(a)
    bh, bl = _split_bf16(b)
    d = functools.partial(jnp.dot, preferred_element_type=F32)
    return d(ah, bh) + (d(al, bh) + d(ah, bl))


def _masked_softmax(s, mask):
    s = jnp.where(mask, s, NEG_INF)
    m = jnp.max(s, axis=-1, keepdims=True)
    p = jnp.where(mask, jnp.exp(s - m), 0.0)
    return p / jnp.maximum(jnp.sum(p, axis=-1, keepdims=True), 1e-30)


def _norm_matmul_body(x_ref, g_ref, *refs):
    k = len(refs) // 2
    xn = _rms(x_ref[...], g_ref[...]).astype(BF16)
    for w_ref, o_ref in zip(refs[:k], refs[k:]):
        o_ref[...] = jnp.dot(xn, w_ref[...], preferred_element_type=F32).astype(o_ref.dtype)


def _norm_matmul(x, gain, ws, out_dtypes, tm=256):
    n, d = x.shape
    return pl.pallas_call(
        _norm_matmul_body,
        grid=(n // tm,),
        in_specs=[_rows_spec(tm, d), _const_spec((1, d))] + [_const_spec(w.shape) for w in ws],
        out_specs=[_rows_spec(tm, w.shape[1]) for w in ws],
        out_shape=[jax.ShapeDtypeStruct((n, w.shape[1]), dt) for w, dt in zip(ws, out_dtypes)],
        compiler_params=_params("parallel"),
        name="norm_matmul",
    )(x, gain.reshape(1, d), *ws)


def _sgu_body(p_ref, lng_ref, lnb_ref, w_ref, b_ref, o_ref, *, n_chunks):
    row = lax.broadcasted_iota(jnp.int32, (CHUNK, CHUNK), 0)
    col = lax.broadcasted_iota(jnp.int32, (CHUNK, CHUNK), 1)
    causal = col <= row
    for c in range(n_chunks):
        rows = slice(c * CHUNK, (c + 1) * CHUNK)
        u = _gelu(p_ref[rows, :A_WIDTH])
        v = _gelu(p_ref[rows, A_WIDTH:])
        outs = []
        for g in range(A_GROUPS):
            cols = slice(g * A_GROUP_DIM, (g + 1) * A_GROUP_DIM)
            vg = v[:, cols]
            mu = jnp.mean(vg, axis=-1, keepdims=True)
            dv = vg - mu
            var = jnp.mean(dv * dv, axis=-1, keepdims=True)
            vn = dv * lax.rsqrt(var + SGU_LN_EPS) * lng_ref[:, cols] + lnb_ref[:, cols]
            wm = jnp.where(causal, w_ref[g], 0.0)
            mixed = _dot(wm, vn) + b_ref[:, g:g + 1]
            outs.append(u[:, cols] * mixed)
        o_ref[rows, :] = jnp.concatenate(outs, axis=1).astype(o_ref.dtype)


def _sgu(p_uv, ln_g, ln_b, w_s, b_s, tm=512):
    n = p_uv.shape[0]
    return pl.pallas_call(
        functools.partial(_sgu_body, n_chunks=tm // CHUNK),
        grid=(n // tm,),
        in_specs=[_rows_spec(tm, 2 * A_WIDTH), _const_spec((1, A_WIDTH)), _const_spec((1, A_WIDTH)),
                  _const_spec((A_GROUPS, CHUNK, CHUNK)), _const_spec((CHUNK, A_GROUPS))],
        out_specs=_rows_spec(tm, A_WIDTH),
        out_shape=jax.ShapeDtypeStruct((n, A_WIDTH), BF16),
        compiler_params=_params("parallel"),
        name="sgu",
    )(p_uv, ln_g.reshape(1, -1), ln_b.reshape(1, -1), w_s, b_s.T)


def _softplus(x):
    return jnp.maximum(x, 0.0) + jnp.log(1.0 + jnp.exp(-jnp.abs(x)))


def _rwkv_prep_body(pb_ref, prev_ref, mu_ref, w0_ref, w2_ref, a0_ref, a2_ref, g2_ref,
                    s_o, g_o, *, tiles_per_seq):
    tm = pb_ref.shape[0]
    x = pb_ref[...]
    first = (pl.program_id(0) % tiles_per_seq) == 0
    prev_row = jnp.where(first, 0.0, prev_ref[V7X_SUBLANES - 1:V7X_SUBLANES, :])
    rowid = lax.broadcasted_iota(jnp.int32, (tm, 1), 0)
    shifted = jnp.where(rowid == 0, prev_row, pltpu.roll(x, 1, axis=0))
    xm = x + (shifted - x) * mu_ref[...]
    o = 3 * B_WIDTH
    wl = xm[:, o:o + DECAY_LORA]
    al = xm[:, o + DECAY_LORA:o + DECAY_LORA + ICLR_LORA]
    gl = xm[:, o + DECAY_LORA + ICLR_LORA:]
    w = -_softplus(-(w0_ref[...] + _dot(jnp.tanh(wl), w2_ref[...]))) - 0.5
    s_o[0] = xm[:, :B_WIDTH]
    s_o[1] = jnp.exp(-jnp.exp(w))
    s_o[2] = xm[:, B_WIDTH:2 * B_WIDTH]
    s_o[3] = xm[:, 2 * B_WIDTH:3 * B_WIDTH]
    s_o[4] = _sigmoid(a0_ref[...] + _dot(al, a2_ref[...]))
    g_o[...] = _dot(_sigmoid(gl), g2_ref[...])


SCAN_OPERANDS = 5


def _rwkv_prep(p_b, seq, mu, w0, w2, a0, a2, g2, tm=256):
    n = p_b.shape[0]
    per8 = tm // V7X_SUBLANES
    outs = [jax.ShapeDtypeStruct((SCAN_OPERANDS, n, B_WIDTH), F32), jax.ShapeDtypeStruct((n, B_WIDTH), F32)]
    return pl.pallas_call(
        functools.partial(_rwkv_prep_body, tiles_per_seq=seq // tm),
        grid=(n // tm,),
        in_specs=[_rows_spec(tm, B_IN),
                  pl.BlockSpec((V7X_SUBLANES, B_IN), lambda i: (jnp.maximum(i * per8 - 1, 0), 0)),
                  _const_spec((1, B_IN)), _const_spec((1, B_WIDTH)), _const_spec((DECAY_LORA, B_WIDTH)),
                  _const_spec((1, B_WIDTH)), _const_spec((ICLR_LORA, B_WIDTH)),
                  _const_spec((GATE_LORA, B_WIDTH))],
        out_specs=[pl.BlockSpec((SCAN_OPERANDS, tm, B_WIDTH), lambda i: (0, i, 0)), _rows_spec(tm, B_WIDTH)],
        out_shape=outs,
        compiler_params=_params("parallel"),
        name="rwkv_prep",
    )(p_b, p_b, mu.reshape(1, -1), w0.reshape(1, -1), w2, a0.reshape(1, -1), a2, g2)


def _rwkv_scan_body(r_ref, w_ref, k0_ref, v_ref, a_ref, kkp_ref, kap_ref, rk_ref, gng_ref, gnb_ref,
                    y_ref, s_ref, kkn_ref, ka_ref, km_ref):
    tt, n = r_ref.shape[0], r_ref.shape[1]

    @pl.when(pl.program_id(0) == 0)
    def _():
        s_ref[...] = jnp.zeros_like(s_ref)

    k0 = k0_ref[...]
    a = a_ref[...]
    kk = k0 * kkp_ref[...][None]
    kkn = kk / jnp.maximum(jnp.sqrt(jnp.sum(kk * kk, axis=1, keepdims=True)), 1e-12)
    kkn_ref[...] = kkn
    ka_ref[...] = kkn * a
    km_ref[...] = k0 * (1.0 + (a - 1.0) * kap_ref[...][None])

    zero = jnp.zeros((n, r_ref.shape[2]), F32)

    def sa_init(j, acc):
        return acc + s_ref[j] * kkn_ref[0, pl.ds(j, 1), :]

    sa0 = lax.fori_loop(0, n, sa_init, zero)

    def step(t, sa):
        v_t = v_ref[t]
        tn = jnp.minimum(t + 1, tt - 1)

        def jbody(j, carry):
            y, san = carry
            row = pl.ds(j, 1)
            sn = s_ref[j] * w_ref[t, row, :] + (v_t * km_ref[t, row, :] - sa * ka_ref[t, row, :])
            s_ref[j] = sn
            return y + sn * r_ref[t, row, :], san + sn * kkn_ref[tn, row, :]

        y, san = lax.fori_loop(0, n, jbody, (zero, zero), unroll=8)
        y_ref[t] = y
        return san

    lax.fori_loop(0, tt, step, sa0)

    y = y_ref[...]
    ym = jnp.mean(y, axis=1, keepdims=True)
    dy = y - ym
    yv = jnp.mean(dy * dy, axis=1, keepdims=True)
    yn = dy * lax.rsqrt(yv + RWKV_GN_EPS) * gng_ref[...][None] + gnb_ref[...][None]
    bonus = jnp.sum(r_ref[...] * km_ref[...] * rk_ref[...][None], axis=1, keepdims=True) * v_ref[...]
    y_ref[...] = yn + bonus


def _rwkv_scan(ops, k_k, k_a, r_k, gn_g, gn_b, bsz, seq, tt=32):
    n = B_HEAD_DIM
    lanes = bsz * B_HEADS

    def lane_param(p):
        return jnp.tile(p.reshape(B_HEADS, n).T, (1, bsz))

    op_spec = lambda a: pl.BlockSpec((None, tt, n, lanes), lambda c: (a, c, 0, 0))
    par_spec = _const_spec((n, lanes))
    return pl.pallas_call(
        _rwkv_scan_body,
        grid=(seq // tt,),
        in_specs=[op_spec(a) for a in range(SCAN_OPERANDS)] + [par_spec] * 5,
        out_specs=pl.BlockSpec((tt, n, lanes), lambda c: (c, 0, 0)),
        out_shape=jax.ShapeDtypeStruct((seq, n, lanes), F32),
        scratch_shapes=[pltpu.VMEM((n, n, lanes), F32)] + [pltpu.VMEM((tt, n, lanes), F32)] * 3,
        compiler_params=_params("arbitrary"),
        name="rwkv_scan",
    )(*([ops] * SCAN_OPERANDS), lane_param(k_k), lane_param(k_a), lane_param(r_k.reshape(-1)),
      lane_param(gn_g), lane_param(gn_b))


SCAN_TT = 128
SCAN_UNROLL = 8


def _to_scan_body(x_ref, o_ref, a_ref):
    bsz = x_ref.shape[0]
    for b in range(bsz):
        for blk in range(B_WIDTH // V7X_LANES):
            cols = slice(blk * V7X_LANES, (blk + 1) * V7X_LANES)
            a_ref[b, cols, :] = x_ref[b, :, cols].T

    def dims(i, carry):
        for u in range(SCAN_UNROLL):
            d = i * SCAN_UNROLL + u
            z = jnp.concatenate([a_ref[b, pl.ds(d, B_HEADS, stride=B_HEAD_DIM), :] for b in range(bsz)],
                                axis=0)
            o_ref[:, d, :] = z.T
        return carry

    lax.fori_loop(0, B_HEAD_DIM // SCAN_UNROLL, dims, 0)


def _to_scan_layout(z, bsz, seq):
    na = z.shape[0]
    tt = min(SCAN_TT, seq)
    return pl.pallas_call(
        _to_scan_body,
        grid=(na, seq // tt),
        in_specs=[pl.BlockSpec((None, bsz, tt, B_WIDTH), lambda a, i: (a, 0, i, 0))],
        out_specs=pl.BlockSpec((None, tt, B_HEAD_DIM, bsz * B_HEADS), lambda a, i: (a, i, 0, 0)),
        out_shape=jax.ShapeDtypeStruct((na, seq, B_HEAD_DIM, bsz * B_HEADS), F32),
        scratch_shapes=[pltpu.VMEM((bsz, B_WIDTH, tt), F32)],
        compiler_params=_params("parallel", "arbitrary"),
        name="to_scan_layout",
    )(z.reshape(na, bsz, seq, B_WIDTH))


def _from_scan_body(y_ref, o_ref, a_ref):
    bsz = o_ref.shape[0]

    def dims(i, carry):
        for u in range(SCAN_UNROLL):
            d = i * SCAN_UNROLL + u
            zt = y_ref[:, d, :].T
            for b in range(bsz):
                a_ref[b, pl.ds(d, B_HEADS, stride=B_HEAD_DIM), :] = zt[b * B_HEADS:(b + 1) * B_HEADS, :]
        return carry

    lax.fori_loop(0, B_HEAD_DIM // SCAN_UNROLL, dims, 0)
    for b in range(bsz):
        for blk in range(B_WIDTH // V7X_LANES):
            cols = slice(blk * V7X_LANES, (blk + 1) * V7X_LANES)
            o_ref[b, :, cols] = a_ref[b, cols, :].T


def _from_scan_layout(y, bsz, seq):
    tt = min(SCAN_TT, seq)
    return pl.pallas_call(
        _from_scan_body,
        grid=(seq // tt,),
        in_specs=[pl.BlockSpec((tt, B_HEAD_DIM, bsz * B_HEADS), lambda i: (i, 0, 0))],
        out_specs=pl.BlockSpec((bsz, tt, B_WIDTH), lambda i: (0, i, 0)),
        out_shape=jax.ShapeDtypeStruct((bsz, seq, B_WIDTH), F32),
        scratch_shapes=[pltpu.VMEM((bsz, B_WIDTH, tt), F32)],
        compiler_params=_params("arbitrary"),
        name="from_scan_layout",
    )(y).reshape(bsz * seq, B_WIDTH)


def _out_proj0_body(h_ref, ya_ref, ys_ref, g_ref, w_ref, o_ref):
    yb = ys_ref[...] * g_ref[...]
    o_ref[...] = h_ref[...] + (_dot(ya_ref[...], w_ref[:A_WIDTH, :]) + _dot(yb, w_ref[A_WIDTH:, :]))


def _out_proj0(h, ya, ys, g, w_out, tm=512):
    n = h.shape[0]
    return pl.pallas_call(
        _out_proj0_body,
        grid=(n // tm,),
        in_specs=[_rows_spec(tm, D_MODEL), _rows_spec(tm, A_WIDTH), _rows_spec(tm, B_WIDTH),
                  _rows_spec(tm, B_WIDTH), _const_spec(w_out.shape)],
        out_specs=_rows_spec(tm, D_MODEL),
        out_shape=jax.ShapeDtypeStruct((n, D_MODEL), F32),
        compiler_params=_params("parallel"),
        name="out_proj0",
    )(h, ya, ys, g, w_out)


def _out_proj1_body(h_ref, yc_ref, bcd_ref, prev_ref, cw_ref, w_ref, o_ref, *, tiles_per_seq):
    tm = h_ref.shape[0]
    first = (pl.program_id(0) % tiles_per_seq) == 0
    z = bcd_ref[:, D_WIDTH:2 * D_WIDTH] * bcd_ref[:, 2 * D_WIDTH:]
    zp = jnp.where(first, 0.0, prev_ref[:, D_WIDTH:2 * D_WIDTH] * prev_ref[:, 2 * D_WIDTH:])
    rowid = lax.broadcasted_iota(jnp.int32, (tm, 1), 0)
    z1 = jnp.where(rowid == 0, zp[7:8, :], pltpu.roll(z, 1, axis=0))
    z2 = pltpu.roll(z, 2, axis=0)
    z2 = jnp.where(rowid == 0, zp[6:7, :], jnp.where(rowid == 1, zp[7:8, :], z2))
    y = cw_ref[0:1, :] * z2 + cw_ref[1:2, :] * z1 + cw_ref[2:3, :] * z
    yd = bcd_ref[:, :D_WIDTH] * y
    o_ref[...] = h_ref[...] + (_dot(yc_ref[...], w_ref[:C_WIDTH, :]) + _dot(yd, w_ref[C_WIDTH:, :]))


def _out_proj1(h, yc, bcd, conv_w, w_out, seq, tm=512):
    n = h.shape[0]
    per8 = tm // V7X_SUBLANES
    return pl.pallas_call(
        functools.partial(_out_proj1_body, tiles_per_seq=seq // tm),
        grid=(n // tm,),
        in_specs=[_rows_spec(tm, D_MODEL), _rows_spec(tm, C_WIDTH), _rows_spec(tm, 3 * D_WIDTH),
                  pl.BlockSpec((V7X_SUBLANES, 3 * D_WIDTH), lambda i: (jnp.maximum(i * per8 - 1, 0), 0)),
                  _const_spec((CONV_W, D_WIDTH)), _const_spec(w_out.shape)],
        out_specs=_rows_spec(tm, D_MODEL),
        out_shape=jax.ShapeDtypeStruct((n, D_MODEL), F32),
        compiler_params=_params("parallel"),
        name="out_proj1",
    )(h, yc, bcd, bcd, conv_w, w_out)


def _ffn_body(h_ref, g_ref, wg_ref, wu_ref, wd_ref, o_ref):
    h = h_ref[...]
    hn = _rms(h, g_ref[...]).astype(BF16)
    gate = jnp.dot(hn, wg_ref[...], preferred_element_type=F32)
    up = jnp.dot(hn, wu_ref[...], preferred_element_type=F32)
    act = (gate * _sigmoid(gate)) * up
    o_ref[...] = h + _dot(act, wd_ref[...])


def _ffn(h, gain, wg, wu, wd, tm=256):
    n = h.shape[0]
    return pl.pallas_call(
        _ffn_body,
        grid=(n // tm,),
        in_specs=[_rows_spec(tm, D_MODEL), _const_spec((1, D_MODEL)), _const_spec(wg.shape),
                  _const_spec(wu.shape), _const_spec(wd.shape)],
        out_specs=_rows_spec(tm, D_MODEL),
        out_shape=jax.ShapeDtypeStruct((n, D_MODEL), F32),
        compiler_params=_params("parallel"),
        name="ffn",
    )(h, gain.reshape(1, -1), wg, wu, wd)


def _rope(x, c, s_up, s_dn):
    half = ROT_DIM // 2
    return x * c + pltpu.roll(x, half, axis=1) * s_up + pltpu.roll(x, V7X_LANES - half, axis=1) * s_dn


def _lane_blocks(x):
    return [x[:, i * V7X_LANES:(i + 1) * V7X_LANES] for i in range(x.shape[1] // V7X_LANES)]


def _in_proj1_body(x_ref, g_ref, c_ref, su_ref, sd_ref, kb_ref, one_ref, wq_ref, wkc_ref, wkv_ref,
                   wgl_ref, wbcd_ref, qn_o, qr_o, kc_o, ks_o, vs_o, kw_o, vw_o, gl_o, bcd_o):
    xn = _rms(x_ref[...], g_ref[...]).astype(BF16)
    d = functools.partial(jnp.dot, preferred_element_type=F32)
    c, su, sd = c_ref[...], su_ref[...], sd_ref[...]
    rope = lambda z: _rope(z, c, su, sd)
    q = d(xn, wq_ref[...]) * (C_HEAD_DIM ** -0.5)
    qn_o[...] = q.astype(BF16)
    qr_o[...] = jnp.concatenate([rope(z) for z in _lane_blocks(q)], axis=1).astype(BF16)
    kc_o[...] = d(xn, wkc_ref[...])
    kv = _lane_blocks(d(xn, wkv_ref[...]))
    hk = C_KV_HEADS
    ks_o[...] = jnp.concatenate([rope(z) + kb_ref[...] for z in kv[:hk]], axis=1).astype(BF16)
    vs_o[...] = jnp.concatenate([z + one_ref[...] for z in kv[hk:2 * hk]], axis=1).astype(BF16)
    kw_o[...] = jnp.concatenate([rope(z) for z in kv[2 * hk:3 * hk]], axis=1).astype(BF16)
    vw_o[...] = jnp.concatenate([z + one_ref[...] for z in kv[3 * hk:]], axis=1).astype(BF16)
    gl_o[...] = d(xn, wgl_ref[...])
    bcd_o[...] = d(xn, wbcd_ref[...])


def _head_tables(seq):
    half = ROT_DIM // 2
    pos = jnp.arange(seq, dtype=F32)
    inv_freq = ROPE_THETA ** (-jnp.arange(0, ROT_DIM, 2, dtype=F32) / ROT_DIM)
    ang = pos[:, None] * inv_freq[None, :]
    cos, sin = jnp.cos(ang), jnp.sin(ang)
    pad = jnp.zeros((seq, V7X_LANES - ROT_DIM), F32)
    zeros = jnp.zeros((seq, half), F32)
    c = jnp.concatenate([cos, cos, pad + 1.0], axis=1)
    s_up = jnp.concatenate([zeros, sin, pad], axis=1)
    s_dn = jnp.concatenate([-sin, zeros, pad], axis=1)
    lane = jnp.arange(V7X_LANES)[None, :]
    blk = (jnp.arange(seq) // SLC_BLK)[:, None]
    k_bias = jnp.where(lane == C_HEAD_DIM + blk, NEG_INF, 0.0).astype(F32)
    ones = (lane == C_HEAD_DIM).astype(F32)
    return c, s_up, s_dn, k_bias, ones


def _pad_heads(w, n_heads):
    width = w.shape[1] // n_heads
    w = w.reshape(w.shape[0], n_heads, width)
    return jnp.pad(w, ((0, 0), (0, 0), (0, V7X_LANES - width))).reshape(w.shape[0], n_heads * V7X_LANES)


def _in_proj1(x, gain, w_in, seq, tm=256):
    n, d = x.shape
    o = np.cumsum([0, C_WIDTH] + [KV_WIDTH] * 6 + [C_HEADS * N_BRANCH] + [D_WIDTH] * 3)
    wq = _pad_heads(w_in[:, o[0]:o[1]], C_HEADS).astype(BF16)
    wkc = w_in[:, o[1]:o[3]].astype(BF16)
    wkv = _pad_heads(w_in[:, o[3]:o[7]], 4 * C_KV_HEADS).astype(BF16)
    wgl = _pad_heads(w_in[:, o[7]:o[8]], C_KV_HEADS).astype(BF16)
    wbcd = w_in[:, o[8]:o[11]].astype(BF16)
    ws = [wq, wkc, wkv, wgl, wbcd]
    kvw = C_KV_HEADS * V7X_LANES
    widths = [C_HEADS * V7X_LANES] * 2 + [2 * KV_WIDTH] + [kvw] * 4 + [kvw, 3 * D_WIDTH]
    dts = [BF16, BF16, F32, BF16, BF16, BF16, BF16, F32, F32]
    tps = seq // tm
    tab_spec = pl.BlockSpec((tm, V7X_LANES), lambda i: (i % tps, 0))
    c, s_up, s_dn, k_bias, ones = _head_tables(seq)
    return pl.pallas_call(
        _in_proj1_body,
        grid=(n // tm,),
        in_specs=[_rows_spec(tm, d), _const_spec((1, d)), tab_spec, tab_spec, tab_spec, tab_spec,
                  _const_spec((1, V7X_LANES))] + [_const_spec(w.shape) for w in ws],
        out_specs=[_rows_spec(tm, wd) for wd in widths],
        out_shape=[jax.ShapeDtypeStruct((n, wd), dt) for wd, dt in zip(widths, dts)],
        compiler_params=_params("parallel"),
        name="in_proj1",
    )(x, gain.reshape(1, d), c, s_up, s_dn, k_bias, ones, *ws)


def _compress_body(r_ref, p_ref, w1_ref, w2_ref, o_ref):
    half = CMP_STRIDE * C_HEAD_DIM
    r = r_ref[0, 0]
    lo = _dot(r + p_ref[:, :half], w1_ref[:half, :])
    hi = _dot(r + p_ref[:, half:], w1_ref[half:, :])
    n_rows = r.shape[0]
    hidden = _gelu(lo + pltpu.roll(hi, n_rows - 1, axis=0))
    o_ref[0, 0] = _dot(hidden, w2_ref[...]).astype(o_ref.dtype)


def _compress(z, pos, w1, w2, bsz, seq):
    n_rows = seq // CMP_STRIDE
    r = z.reshape(bsz, n_rows, CMP_STRIDE, C_KV_HEADS, C_HEAD_DIM).transpose(0, 3, 1, 2, 4).reshape(
        bsz, C_KV_HEADS, n_rows, CMP_STRIDE * C_HEAD_DIM)
    w2p = _pad_heads(w2, 1).astype(BF16)
    blk = lambda w: pl.BlockSpec((1, 1, n_rows, w), lambda b, h: (b, h, 0, 0))
    return pl.pallas_call(
        _compress_body,
        grid=(bsz, C_KV_HEADS),
        in_specs=[blk(CMP_STRIDE * C_HEAD_DIM), _const_spec((1, CMP_LEN * C_HEAD_DIM)),
                  _const_spec(w1.shape), _const_spec(w2p.shape)],
        out_specs=blk(V7X_LANES),
        out_shape=jax.ShapeDtypeStruct((bsz, C_KV_HEADS, n_rows, V7X_LANES), BF16),
        compiler_params=_params("parallel", "parallel"),
        name="nsa_compress",
    )(r, pos.reshape(1, -1), w1.astype(BF16), w2p)


SEL_CHUNK = 512


def _stack_heads(blk):
    return jnp.concatenate(_lane_blocks(blk), axis=0)


def _rows4(x):
    return jnp.concatenate([x] * C_GROUP, axis=0)


def _nsa_body(qn_ref, qr_ref, gl_ref, kc_ref, vc_ref, ks_ref, vs_ref, kw_ref, vw_ref, ovl_ref, place_ref,
              o_ref):
    n_cmp = kc_ref.shape[2]
    n_slc = ovl_ref.shape[0]
    hd = C_HEAD_DIM
    qb = pl.program_id(2)
    s0 = qb * NSA_TQ
    t_pos = s0 + lax.broadcasted_iota(jnp.int32, (NSA_TQ, 1), 0)
    qn4 = _stack_heads(qn_ref[0])
    qr4 = _stack_heads(qr_ref[0])

    cmp_end = lax.broadcasted_iota(jnp.int32, (1, n_cmp), 1) * CMP_STRIDE + (CMP_LEN - 1)
    pc = _masked_softmax(_dot_nt(qn4, kc_ref[0, 0]), _rows4(cmp_end <= t_pos))
    o_c = _dot(pc, vc_ref[0, 0])
    pc_sum = (pc[:NSA_TQ] + pc[NSA_TQ:2 * NSA_TQ]) + (pc[2 * NSA_TQ:3 * NSA_TQ] + pc[3 * NSA_TQ:])

    p_hi, p_lo = _split_bf16(pc_sum)
    ovl = ovl_ref[...]
    imp = _dot_nt(ovl, p_hi) + _dot_nt(ovl, p_lo)
    jb = lax.broadcasted_iota(jnp.int32, (n_slc, 1), 0)
    t_row = s0 + lax.broadcasted_iota(jnp.int32, (1, NSA_TQ), 1)
    cur = t_row // SLC_BLK
    forced = (jb == 0) | (jb == cur) | (jb == cur - 1)
    imp = jnp.where(jb * SLC_BLK <= t_row, imp + jnp.where(forced, FORCE_BONUS, 0.0), NEG_INF)
    rank = jnp.zeros((n_slc, NSA_TQ), F32)
    for k in range(n_slc):
        ck = imp[k:k + 1, :]
        beats = (ck > imp) | ((ck == imp) & (jb > k))
        rank = rank + jnp.where(beats, 1.0, 0.0)
    not_sel = jnp.where(rank < float(min(SEL_TOPK, n_slc)), 0.0, 1.0).astype(BF16)
    not_sel_q = lax.dot_general(not_sel, place_ref[...], (((0,), (0,)), ((), ())),
                                preferred_element_type=F32)
    q_sel = qr4 + _rows4(not_sel_q.astype(BF16))

    def sel_chunk(c, m, acc, causal):
        k0 = pl.multiple_of(c * SEL_CHUNK, SEL_CHUNK)
        s = _dot_nt(q_sel, ks_ref[0, pl.ds(k0, SEL_CHUNK), :])
        if causal:
            key = k0 + lax.broadcasted_iota(jnp.int32, (1, SEL_CHUNK), 1)
            s = s + _rows4(jnp.where(key <= t_pos, 0.0, NEG_INF))
        m_new = jnp.maximum(m, jnp.max(s, axis=-1, keepdims=True))
        p = jnp.exp(s - m_new)
        acc = jnp.exp(m - m_new) * acc + _dot(p, vs_ref[0, pl.ds(k0, SEL_CHUNK), :])
        return m_new, acc

    rows = C_GROUP * NSA_TQ
    last = qb // (SEL_CHUNK // NSA_TQ)
    m, acc = lax.fori_loop(0, last, lambda c, ma: sel_chunk(c, ma[0], ma[1], False),
                           (jnp.full((rows, 1), NEG_INF, F32), jnp.zeros((rows, V7X_LANES), F32)))
    _, acc = sel_chunk(last, m, acc, True)
    o_s = acc[:, :hd] / jnp.maximum(acc[:, hd:hd + 1], 1e-30)

    band = NSA_TQ + WIN
    w0 = pl.multiple_of(jnp.maximum(s0 - WIN, 0), NSA_TQ)
    diff = t_pos - (w0 + lax.broadcasted_iota(jnp.int32, (1, band), 1))
    s = _dot_nt(qr4, kw_ref[0, pl.ds(w0, band), :]) + _rows4(
        jnp.where((diff >= 0) & (diff < WIN), 0.0, NEG_INF))
    p = jnp.exp(s - jnp.max(s, axis=-1, keepdims=True))
    acc = _dot(p, vw_ref[0, pl.ds(w0, band), :])
    o_w = acc[:, :hd] / jnp.maximum(acc[:, hd:hd + 1], 1e-30)

    gate = _sigmoid(gl_ref[0])
    outs = []
    for g in range(C_GROUP):
        r = slice(g * NSA_TQ, (g + 1) * NSA_TQ)
        gc = g * N_BRANCH
        outs.append(gate[:, gc:gc + 1] * o_c[r, :hd] + gate[:, gc + 1:gc + 2] * o_s[r]
                    + gate[:, gc + 2:gc + 3] * o_w[r])
    o_ref[0] = jnp.concatenate(outs, axis=1).astype(o_ref.dtype)


def _nsa(qn, qr, gl, k_cmp, v_cmp, ks, vs, kw, vw, bsz, seq):
    n_cmp = seq // CMP_STRIDE
    n_slc = seq // SLC_BLK
    assert C_HEAD_DIM + n_slc <= V7X_LANES and seq % SEL_CHUNK == 0
    ci = np.arange(n_cmp)[None, :] * CMP_STRIDE
    sj = np.arange(n_slc)[:, None] * SLC_BLK
    overlap_t = jnp.asarray(((ci < sj + SLC_BLK) & (ci + CMP_LEN > sj)).astype(np.float32)).astype(BF16)
    place = jnp.asarray(np.eye(n_slc, V7X_LANES, k=C_HEAD_DIM, dtype=np.float32)).astype(BF16)
    gw = C_GROUP * V7X_LANES
    q_spec = pl.BlockSpec((1, NSA_TQ, gw), lambda b, h, i: (b, i, h))
    gl_spec = pl.BlockSpec((1, NSA_TQ, V7X_LANES), lambda b, h, i: (b, i, h))
    cmp_spec = pl.BlockSpec((1, 1, n_cmp, V7X_LANES), lambda b, h, i: (b, h, 0, 0))
    kv_spec = pl.BlockSpec((1, seq, V7X_LANES), lambda b, h, i: (b, 0, h))
    as3 = lambda z: z.reshape(bsz, seq, z.shape[-1])
    return pl.pallas_call(
        _nsa_body,
        grid=(bsz, C_KV_HEADS, seq // NSA_TQ),
        in_specs=[q_spec, q_spec, gl_spec, cmp_spec, cmp_spec, kv_spec, kv_spec, kv_spec, kv_spec,
                  _const_spec((n_slc, n_cmp)), _const_spec((n_slc, V7X_LANES))],
        out_specs=pl.BlockSpec((1, NSA_TQ, C_GROUP * C_HEAD_DIM), lambda b, h, i: (b, i, h)),
        out_shape=jax.ShapeDtypeStruct((bsz, seq, C_WIDTH), BF16),
        compiler_params=_params("parallel", "parallel", "arbitrary"),
        name="nsa_attention",
    )(as3(qn), as3(qr), as3(gl), k_cmp, v_cmp, as3(ks), as3(vs), as3(kw), as3(vw), overlap_t, place)


ROW_CHUNKS = D_MODEL // V7X_LANES
assert ROW_CHUNKS == V7X_SUBLANES
DMA_UNROLL = 8
PAD_SPANS = N_EXPERTS + 1


def _tile_rows(n_rows):
    return (n_rows * ROW_CHUNKS, V7X_LANES)


def _row_tile(ref, r):
    start = r * ROW_CHUNKS
    if not isinstance(start, int):
        start = pl.multiple_of(start, ROW_CHUNKS)
    return ref.at[pl.ds(start, ROW_CHUNKS), :]


def _store_row_tiles(ref, x):
    for c in range(ROW_CHUNKS):
        ref[pl.ds(c, x.shape[0], stride=ROW_CHUNKS), :] = x[:, c * V7X_LANES:(c + 1) * V7X_LANES]


def _load_row_tiles(ref):
    rows = ref.shape[0] // ROW_CHUNKS
    return jnp.concatenate([ref[pl.ds(c, rows, stride=ROW_CHUNKS), :] for c in range(ROW_CHUNKS)], axis=1)


def _router_body(h_ref, g_ref, wr_ref, br_ref, tri_ref, hn_o, idx_o, gate_o, cnt_o, cnt_scr):
    @pl.when(pl.program_id(0) == 0)
    def _():
        cnt_scr[...] = jnp.zeros_like(cnt_scr)

    hn = _rms(h_ref[...], g_ref[...])
    _store_row_tiles(hn_o, hn)
    logits = _dot_f32(hn, wr_ref[...]) + br_ref[...]
    lane = lax.broadcasted_iota(jnp.int32, logits.shape, 1)
    m1 = jnp.max(logits, axis=-1, keepdims=True)
    i1 = jnp.min(jnp.where(logits == m1, lane, V7X_LANES), axis=-1, keepdims=True)
    rest = jnp.where(lane == i1, NEG_INF, logits)
    m2 = jnp.max(rest, axis=-1, keepdims=True)
    i2 = jnp.min(jnp.where(rest == m2, lane, V7X_LANES), axis=-1, keepdims=True)
    e2 = jnp.exp(m2 - m1)
    den = 1.0 + e2
    hit1, hit2 = lane == i1, lane == i2
    hits = jnp.where(hit1 | hit2, 1.0, 0.0)
    before = jnp.dot(tri_ref[...], hits.astype(BF16), preferred_element_type=F32) + cnt_scr[...]
    r1 = jnp.sum(jnp.where(hit1, before, 0.0), axis=-1, keepdims=True).astype(jnp.int32)
    r2 = jnp.sum(jnp.where(hit2, before, 0.0), axis=-1, keepdims=True).astype(jnp.int32)
    cnt_scr[...] = cnt_scr[...] + jnp.sum(hits, axis=0, keepdims=True)
    cnt_o[...] = cnt_scr[...]
    idx_o[...] = jnp.where(lane == 0, i1, jnp.where(lane == 1, i2, jnp.where(lane == 2, r1,
                           jnp.where(lane == 3, r2, 0))))
    gate_o[...] = jnp.where(lane == 0, 1.0 / den, jnp.where(lane == 1, e2 / den, 0.0))


def _router(h, gain, w_router, b_router, tm=512):
    n = h.shape[0]
    wr = jnp.pad(w_router, ((0, 0), (0, V7X_LANES - N_EXPERTS)))
    br = jnp.pad(b_router.reshape(1, -1), ((0, 0), (0, V7X_LANES - N_EXPERTS)), constant_values=NEG_INF)
    tri = jnp.asarray(np.tril(np.ones((tm, tm), np.float32), k=-1)).astype(BF16)
    return pl.pallas_call(
        _router_body,
        grid=(n // tm,),
        in_specs=[_rows_spec(tm, D_MODEL), _const_spec((1, D_MODEL)), _const_spec(wr.shape),
                  _const_spec(br.shape), _const_spec(tri.shape)],
        out_specs=[pl.BlockSpec(_tile_rows(tm), lambda i: (i, 0)), _rows_spec(tm, V7X_LANES),
                   _rows_spec(tm, V7X_LANES), _const_spec((1, V7X_LANES))],
        out_shape=[jax.ShapeDtypeStruct(_tile_rows(n), F32), jax.ShapeDtypeStruct((n, V7X_LANES), jnp.int32),
                   jax.ShapeDtypeStruct((n, V7X_LANES), F32), jax.ShapeDtypeStruct((1, V7X_LANES), F32)],
        scratch_shapes=[pltpu.VMEM((1, V7X_LANES), F32)],
        compiler_params=_params("arbitrary"),
        name="moe_router",
    )(h, gain.reshape(1, -1), wr, br, tri)


def _row_copy(src, dst, sem):
    return pltpu.make_async_copy(src, dst, sem)


def _dispatch_body(pad_ref, dest_ref, hn_ref, xs_ref, zero_scr, sem):
    @pl.when(pl.program_id(0) == 0)
    def _():
        zero_scr[...] = jnp.zeros_like(zero_scr)
        for e in range(PAD_SPANS):
            start, count = pad_ref[e], pad_ref[PAD_SPANS + e]

            def fill(r, c):
                _row_copy(zero_scr, _row_tile(xs_ref, start + r), sem).start()
                return c

            def filled(r, c):
                _row_copy(zero_scr, _row_tile(xs_ref, 0), sem).wait()
                return c

            lax.fori_loop(0, count, fill, 0)
            lax.fori_loop(0, count, filled, 0)

    def issue(blk, c):
        for u in range(DMA_UNROLL):
            r = blk * DMA_UNROLL + u
            for k in range(TOP_K):
                _row_copy(_row_tile(hn_ref, r), _row_tile(xs_ref, dest_ref[TOP_K * r + k]),
                          sem).start(priority=k)
        return c

    lax.fori_loop(0, DMA_ROWS // DMA_UNROLL, issue, 0)

    def drain(r, c):
        for k in range(TOP_K):
            _row_copy(_row_tile(hn_ref, 0), _row_tile(xs_ref, 0), sem).wait()
        return c

    lax.fori_loop(0, DMA_ROWS, drain, 0)


def _dispatch(pad_info, dest, hn_rows, p_rows):
    n = hn_rows.shape[0] // ROW_CHUNKS
    return pl.pallas_call(
        _dispatch_body,
        grid_spec=pltpu.PrefetchScalarGridSpec(
            num_scalar_prefetch=1,
            grid=(n // DMA_ROWS,),
            in_specs=[pl.BlockSpec((TOP_K * DMA_ROWS,), lambda i, pad: (i,), memory_space=pltpu.SMEM),
                      pl.BlockSpec(_tile_rows(DMA_ROWS), lambda i, pad: (i, 0))],
            out_specs=pl.BlockSpec(memory_space=pl.ANY),
            scratch_shapes=[pltpu.VMEM(_tile_rows(1), F32), pltpu.SemaphoreType.DMA(())]),
        out_shape=jax.ShapeDtypeStruct(_tile_rows(p_rows), F32),
        compiler_params=_params("arbitrary"),
        name="moe_dispatch",
    )(pad_info, dest, hn_rows)


def _experts_body(blk_e_ref, n_used_ref, x_ref, wg_ref, wu_ref, wd_ref, o_ref):
    i = pl.program_id(0)

    @pl.when(i < n_used_ref[0])
    def _():
        x = _load_row_tiles(x_ref).astype(BF16)
        gate = jnp.dot(x, wg_ref[0], preferred_element_type=F32)
        up = jnp.dot(x, wu_ref[0], preferred_element_type=F32)
        act = (gate * _sigmoid(gate)) * up
        _store_row_tiles(o_ref, _dot(act, wd_ref[0]))

    @pl.when(i >= n_used_ref[0])
    def _():
        o_ref[...] = jnp.zeros_like(o_ref)


def _experts(blk_e, n_used, xs, wg, wu, wd):
    p_rows = xs.shape[0] // ROW_CHUNKS
    x_spec = pl.BlockSpec(_tile_rows(MOE_ROWS), lambda i, be, nu: (jnp.minimum(i, nu[0] - 1), 0))
    o_spec = pl.BlockSpec(_tile_rows(MOE_ROWS), lambda i, be, nu: (i, 0))
    w_spec = lambda w: pl.BlockSpec((1,) + w.shape[1:], lambda i, be, nu: (be[i], 0, 0))
    return pl.pallas_call(
        _experts_body,
        grid_spec=pltpu.PrefetchScalarGridSpec(
            num_scalar_prefetch=2,
            grid=(p_rows // MOE_ROWS,),
            in_specs=[x_spec, w_spec(wg), w_spec(wu), w_spec(wd)],
            out_specs=o_spec),
        out_shape=jax.ShapeDtypeStruct(_tile_rows(p_rows), F32),
        compiler_params=_params("arbitrary"),
        name="moe_experts",
    )(blk_e, n_used, xs, wg, wu, wd)


def _combine_body(dest_ref, h_ref, gate_ref, g_ref, yb_ref, o_ref, y_scr, sem):
    tm = h_ref.shape[0]

    def issue(blk, c):
        for u in range(DMA_UNROLL):
            t = blk * DMA_UNROLL + u
            for k in range(TOP_K):
                _row_copy(_row_tile(yb_ref, dest_ref[TOP_K * t + k]), _row_tile(y_scr.at[k], t),
                          sem).start(priority=k)
        return c

    lax.fori_loop(0, tm // DMA_UNROLL, issue, 0)

    def drain(t, c):
        for k in range(TOP_K):
            _row_copy(_row_tile(yb_ref, 0), _row_tile(y_scr.at[0], 0), sem).wait()
        return c

    lax.fori_loop(0, tm, drain, 0)
    gate = gate_ref[...]
    moe = gate[:, 0:1] * _load_row_tiles(y_scr.at[0]) + gate[:, 1:2] * _load_row_tiles(y_scr.at[1])
    o_ref[...] = _rms(h_ref[...] + moe, g_ref[...])


def _combine(dest, h, gates, gain, yb_rows, tm=256):
    n = h.shape[0]
    return pl.pallas_call(
        _combine_body,
        grid=(n // tm,),
        in_specs=[pl.BlockSpec((TOP_K * tm,), lambda i: (i,), memory_space=pltpu.SMEM),
                  _rows_spec(tm, D_MODEL), _rows_spec(tm, V7X_LANES), _const_spec((1, D_MODEL)),
                  pl.BlockSpec(memory_space=pl.ANY)],
        out_specs=_rows_spec(tm, D_MODEL),
        out_shape=jax.ShapeDtypeStruct((n, D_MODEL), F32),
        scratch_shapes=[pltpu.VMEM((TOP_K,) + _tile_rows(tm), F32), pltpu.SemaphoreType.DMA(())],
        compiler_params=_params("arbitrary"),
        name="moe_combine_norm",
    )(dest, h, gates, gain.reshape(1, -1), yb_rows)


def _moe_layout(idx, counts):
    n = idx.shape[0]
    nk = n * TOP_K
    counts = counts[0, :N_EXPERTS].astype(jnp.int32)
    padded = ((counts + MOE_ROWS - 1) // MOE_ROWS) * MOE_ROWS
    p_end = jnp.cumsum(padded)
    p_start = p_end - padded
    experts = jnp.arange(N_EXPERTS, dtype=jnp.int32)[None, None, :]
    first = jnp.sum(jnp.where(idx[:, :TOP_K, None] == experts, p_start[None, None, :], 0), axis=-1)
    dest = (first + idx[:, TOP_K:2 * TOP_K]).reshape(nk)
    n_blk = (nk + MOE_ROWS - 1) // MOE_ROWS + N_EXPERTS
    blk_e = jnp.minimum(jnp.searchsorted(p_end, jnp.arange(n_blk) * MOE_ROWS, side="right"),
                        N_EXPERTS - 1).astype(jnp.int32)
    n_used = (p_end[-1:] // MOE_ROWS).astype(jnp.int32)
    p_rows = n_blk * MOE_ROWS
    pad_info = jnp.concatenate([p_start + counts, p_end[-1:], padded - counts,
                                p_rows - p_end[-1:]]).astype(jnp.int32)
    return dest.astype(jnp.int32), blk_e, n_used, pad_info, p_rows


def _moe_final(h, norm_g, w_router, b_router, wg, wu, wd, final_g):
    hn, idx, gates, counts = _router(h, norm_g, w_router, b_router)
    dest, blk_e, n_used, pad_info, p_rows = _moe_layout(idx, counts)
    xs = _dispatch(pad_info, dest, hn, p_rows)
    yb = _experts(blk_e, n_used, xs, wg.astype(BF16), wu.astype(BF16), wd.astype(BF16))
    return _combine(dest, h, gates, final_g, yb)


def kernel(x, e_norm_mix, e_w_in, sgu_ln_g, sgu_ln_b, sgu_w, sgu_b, rwkv_mu, rwkv_w0, rwkv_w2,
           rwkv_a0, rwkv_a2, rwkv_g2, rwkv_k_k, rwkv_k_a, rwkv_r_k, rwkv_gn_g, rwkv_gn_b, e_w_out,
           e_norm_ffn, ffn_w_gate, ffn_w_up, ffn_w_down, o_norm_mix, o_w_in, nsa_cmp_pos_k,
           nsa_cmp_pos_v, nsa_cmp_k_w1, nsa_cmp_k_w2, nsa_cmp_v_w1, nsa_cmp_v_w2, conv_w, o_w_out,
           o_norm_ffn, moe_router, moe_router_b, moe_w_gate, moe_w_up, moe_w_down, final_norm):
    bsz, seq, d = x.shape
    n = bsz * seq
    h = x.reshape(n, d)

    w_in = e_w_in[0].astype(BF16)
    p_uv, p_b = _norm_matmul(h, e_norm_mix[0], [w_in[:, :2 * A_WIDTH], w_in[:, 2 * A_WIDTH:]], [F32, F32])
    ya = _sgu(p_uv, sgu_ln_g[0], sgu_ln_b[0], sgu_w[0], sgu_b[0])
    scan_ops, g = _rwkv_prep(p_b, seq, rwkv_mu[0], rwkv_w0[0], rwkv_w2[0], rwkv_a0[0],
                             rwkv_a2[0], rwkv_g2[0])
    ys = _rwkv_scan(_to_scan_layout(scan_ops, bsz, seq), rwkv_k_k[0], rwkv_k_a[0], rwkv_r_k[0],
                    rwkv_gn_g[0], rwkv_gn_b[0], bsz, seq)
    h = _out_proj0(h, ya, _from_scan_layout(ys, bsz, seq), g, e_w_out[0].astype(BF16))
    h = _ffn(h, e_norm_ffn[0], ffn_w_gate[0].astype(BF16), ffn_w_up[0].astype(BF16),
             ffn_w_down[0].astype(BF16))

    qn, qr, kcvc, ks, vs, kw, vw, gl, bcd = _in_proj1(h, o_norm_mix[0], o_w_in[0], seq)
    k_cmp = _compress(kcvc[:, :KV_WIDTH], nsa_cmp_pos_k[0], nsa_cmp_k_w1[0], nsa_cmp_k_w2[0], bsz, seq)
    v_cmp = _compress(kcvc[:, KV_WIDTH:], nsa_cmp_pos_v[0], nsa_cmp_v_w1[0], nsa_cmp_v_w2[0], bsz, seq)
    yc = _nsa(qn, qr, gl, k_cmp, v_cmp, ks, vs, kw, vw, bsz, seq).reshape(n, C_WIDTH)
    h = _out_proj1(h, yc, bcd, conv_w[0], o_w_out[0].astype(BF16), seq)
    out = _moe_final(h, o_norm_ffn[0], moe_router[0], moe_router_b[0], moe_w_gate[0], moe_w_up[0],
                     moe_w_down[0], final_norm)
    return out.reshape(bsz, seq, d)
```

```python
import functools

import jax
import jax.numpy as jnp
import numpy as np
from jax import lax
from jax.experimental import pallas as pl
from jax.experimental.pallas import tpu as pltpu

F32 = jnp.float32
BF16 = jnp.bfloat16

D_MODEL = 1024
A_GROUPS = 4
A_GROUP_DIM = 128
A_WIDTH = A_GROUPS * A_GROUP_DIM
CHUNK = 128
SGU_LN_EPS = 1e-5
B_HEADS = 8
B_HEAD_DIM = 64
B_WIDTH = B_HEADS * B_HEAD_DIM
DECAY_LORA = 64
ICLR_LORA = 64
GATE_LORA = 128
B_IN = 3 * B_WIDTH + DECAY_LORA + ICLR_LORA + GATE_LORA
RWKV_GN_EPS = 64e-5
C_HEADS = 8
C_KV_HEADS = 2
C_GROUP = C_HEADS // C_KV_HEADS
C_HEAD_DIM = 64
C_WIDTH = C_HEADS * C_HEAD_DIM
KV_WIDTH = C_KV_HEADS * C_HEAD_DIM
N_BRANCH = 3
CMP_LEN = 32
CMP_STRIDE = 16
CMP_HIDDEN = 256
SLC_BLK = 64
SEL_TOPK = 8
WIN = 512
NSA_TQ = 256
ROT_DIM = C_HEAD_DIM // 4
ROPE_THETA = 500000.0
D_WIDTH = 512
CONV_W = 3
FFN_DIM = 2816
N_EXPERTS = 8
TOP_K = 2
EXPERT_DIM = 1408
NORM_EPS = 1e-6
NEG_INF = -1e30
FORCE_BONUS = 1e6

V7X_LANES = 128
V7X_SUBLANES = 8
V7X_VMEM_LIMIT = 56 * 1024 * 1024

MOE_ROWS = 512
DMA_ROWS = 512


def _params(*sem):
    return pltpu.CompilerParams(dimension_semantics=sem, vmem_limit_bytes=V7X_VMEM_LIMIT)


def _const_spec(shape):
    zeros = (0,) * len(shape)
    return pl.BlockSpec(shape, lambda *_: zeros)


def _rows_spec(tm, width):
    return pl.BlockSpec((tm, width), lambda i: (i, 0))


def _rms(x, g):
    return x * lax.rsqrt(jnp.mean(x * x, axis=-1, keepdims=True) + NORM_EPS) * g


def _gelu(x):
    return x * (0.5 * (1.0 + jnp.tanh(0.7978845608028654 * (x + 0.044715 * (x * x * x)))))


def _sigmoid(x):
    return 1.0 / (1.0 + jnp.exp(-x))


def _dot(a, b):
    return jnp.dot(a.astype(BF16), b.astype(BF16), preferred_element_type=F32)


def _dot_nt(a, b):
    return lax.dot_general(a.astype(BF16), b.astype(BF16), (((1,), (1,)), ((), ())),
                           preferred_element_type=F32)


def _split_bf16(a):
    hi = a.astype(BF16)
    lo = (a - hi.astype(F32)).astype(BF16)
    return hi, lo


def _dot_f32(a, b):
    ah, al = _split_bf16(a)
    bh, bl = _split_bf16(b)
    d = functools.partial(jnp.dot, preferred_element_type=F32)
    return d(ah, bh) + (d(al, bh) + d(ah, bl))


def _masked_softmax(s, mask):
    s = jnp.where(mask, s, NEG_INF)
    m = jnp.max(s, axis=-1, keepdims=True)
    p = jnp.where(mask, jnp.exp(s - m), 0.0)
    return p / jnp.maximum(jnp.sum(p, axis=-1, keepdims=True), 1e-30)


def _sgu_chunk(p_uv, lng_ref, lnb_ref, w_ref, b_ref):
    row = lax.broadcasted_iota(jnp.int32, (CHUNK, CHUNK), 0)
    col = lax.broadcasted_iota(jnp.int32, (CHUNK, CHUNK), 1)
    causal = col <= row
    u = _gelu(p_uv[:, :A_WIDTH])
    v = _gelu(p_uv[:, A_WIDTH:])
    outs = []
    for g in range(A_GROUPS):
        cols = slice(g * A_GROUP_DIM, (g + 1) * A_GROUP_DIM)
        vg = v[:, cols]
        mu = jnp.mean(vg, axis=-1, keepdims=True)
        dv = vg - mu
        var = jnp.mean(dv * dv, axis=-1, keepdims=True)
        vn = dv * lax.rsqrt(var + SGU_LN_EPS) * lng_ref[:, cols] + lnb_ref[:, cols]
        wm = jnp.where(causal, w_ref[g], 0.0)
        mixed = _dot(wm, vn) + b_ref[:, g:g + 1]
        outs.append(u[:, cols] * mixed)
    return jnp.concatenate(outs, axis=1)


def _softplus(x):
    return jnp.maximum(x, 0.0) + jnp.log(1.0 + jnp.exp(-jnp.abs(x)))


SCAN_OPERANDS = 5


def _layer0_front_body(x_ref, gain_ref, win_ref, lng_ref, lnb_ref, sw_ref, sb_ref, mu_ref, w0_ref, w2_ref,
                       a0_ref, a2_ref, g2_ref, ya_o, s_o, g_o, prev_scr, *, tiles_per_seq):
    tm = x_ref.shape[0]

    @pl.when(pl.program_id(0) == 0)
    def _():
        prev_scr[...] = jnp.zeros_like(prev_scr)

    p = jnp.dot(_rms(x_ref[...], gain_ref[...]).astype(BF16), win_ref[...], preferred_element_type=F32)
    for c in range(tm // CHUNK):
        rows = slice(c * CHUNK, (c + 1) * CHUNK)
        ya_o[rows, :] = _sgu_chunk(p[rows, :2 * A_WIDTH], lng_ref, lnb_ref, sw_ref, sb_ref).astype(ya_o.dtype)
    x = p[:, 2 * A_WIDTH:]
    first = (pl.program_id(0) % tiles_per_seq) == 0
    prev_row = jnp.where(first, 0.0, prev_scr[...])
    rowid = lax.broadcasted_iota(jnp.int32, (tm, 1), 0)
    shifted = jnp.where(rowid == 0, prev_row, pltpu.roll(x, 1, axis=0))
    prev_scr[...] = x[tm - 1:tm, :]
    xm = x + (shifted - x) * mu_ref[...]
    o = 3 * B_WIDTH
    wl = xm[:, o:o + DECAY_LORA]
    al = xm[:, o + DECAY_LORA:o + DECAY_LORA + ICLR_LORA]
    gl = xm[:, o + DECAY_LORA + ICLR_LORA:]
    w = -_softplus(-(w0_ref[...] + _dot(jnp.tanh(wl), w2_ref[...]))) - 0.5
    s_o[0] = xm[:, :B_WIDTH]
    s_o[1] = jnp.exp(-jnp.exp(w))
    s_o[2] = xm[:, B_WIDTH:2 * B_WIDTH]
    s_o[3] = xm[:, 2 * B_WIDTH:3 * B_WIDTH]
    s_o[4] = _sigmoid(a0_ref[...] + _dot(al, a2_ref[...]))
    g_o[...] = _dot(_sigmoid(gl), g2_ref[...])


def _layer0_front(h, seq, gain, w_in, ln_g, ln_b, w_s, b_s, mu, w0, w2, a0, a2, g2, tm=256):
    n, d = h.shape
    row = lambda v: v.reshape(1, -1)
    consts = [row(gain), w_in, row(ln_g), row(ln_b), w_s, b_s.T, row(mu), row(w0), w2, row(a0), a2, g2]
    outs = [jax.ShapeDtypeStruct((n, A_WIDTH), BF16), jax.ShapeDtypeStruct((SCAN_OPERANDS, n, B_WIDTH), F32),
            jax.ShapeDtypeStruct((n, B_WIDTH), F32)]
    return pl.pallas_call(
        functools.partial(_layer0_front_body, tiles_per_seq=seq // tm),
        grid=(n // tm,),
        in_specs=[_rows_spec(tm, d)] + [_const_spec(c.shape) for c in consts],
        out_specs=[_rows_spec(tm, A_WIDTH), pl.BlockSpec((SCAN_OPERANDS, tm, B_WIDTH), lambda i: (0, i, 0)),
                   _rows_spec(tm, B_WIDTH)],
        out_shape=outs,
        scratch_shapes=[pltpu.VMEM((1, B_IN), F32)],
        compiler_params=_params("arbitrary"),
        name="layer0_front",
    )(h, *consts)


def _rwkv_scan_body(r_ref, w_ref, k0_ref, v_ref, a_ref, kkp_ref, kap_ref, rk_ref, gng_ref, gnb_ref,
                    y_ref, s_ref, kkn_ref, ka_ref, km_ref):
    tt, n = r_ref.shape[0], r_ref.shape[1]

    @pl.when(pl.program_id(0) == 0)
    def _():
        s_ref[...] = jnp.zeros_like(s_ref)

    k0 = k0_ref[...]
    a = a_ref[...]
    kk = k0 * kkp_ref[...][None]
    kkn = kk / jnp.maximum(jnp.sqrt(jnp.sum(kk * kk, axis=1, keepdims=True)), 1e-12)
    kkn_ref[...] = kkn
    ka_ref[...] = kkn * a
    km_ref[...] = k0 * (1.0 + (a - 1.0) * kap_ref[...][None])

    zero = jnp.zeros((n, r_ref.shape[2]), F32)

    def sa_init(j, acc):
        return acc + s_ref[j] * kkn_ref[0, pl.ds(j, 1), :]

    sa0 = lax.fori_loop(0, n, sa_init, zero)

    def step(t, sa):
        v_t = v_ref[t]
        tn = jnp.minimum(t + 1, tt - 1)

        def jbody(j, carry):
            y, san = carry
            row = pl.ds(j, 1)
            sn = s_ref[j] * w_ref[t, row, :] + (v_t * km_ref[t, row, :] - sa * ka_ref[t, row, :])
            s_ref[j] = sn
            return y + sn * r_ref[t, row, :], san + sn * kkn_ref[tn, row, :]

        y, san = lax.fori_loop(0, n, jbody, (zero, zero), unroll=8)
        y_ref[t] = y
        return san

    lax.fori_loop(0, tt, step, sa0)

    y = y_ref[...]
    ym = jnp.mean(y, axis=1, keepdims=True)
    dy = y - ym
    yv = jnp.mean(dy * dy, axis=1, keepdims=True)
    yn = dy * lax.rsqrt(yv + RWKV_GN_EPS) * gng_ref[...][None] + gnb_ref[...][None]
    bonus = jnp.sum(r_ref[...] * km_ref[...] * rk_ref[...][None], axis=1, keepdims=True) * v_ref[...]
    y_ref[...] = yn + bonus


def _rwkv_scan(ops, k_k, k_a, r_k, gn_g, gn_b, bsz, seq, tt=32):
    n = B_HEAD_DIM
    lanes = bsz * B_HEADS

    def lane_param(p):
        return jnp.tile(p.reshape(B_HEADS, n).T, (1, bsz))

    op_spec = lambda a: pl.BlockSpec((None, tt, n, lanes), lambda c: (a, c, 0, 0))
    par_spec = _const_spec((n, lanes))
    return pl.pallas_call(
        _rwkv_scan_body,
        grid=(seq // tt,),
        in_specs=[op_spec(a) for a in range(SCAN_OPERANDS)] + [par_spec] * 5,
        out_specs=pl.BlockSpec((tt, n, lanes), lambda c: (c, 0, 0)),
        out_shape=jax.ShapeDtypeStruct((seq, n, lanes), F32),
        scratch_shapes=[pltpu.VMEM((n, n, lanes), F32)] + [pltpu.VMEM((tt, n, lanes), F32)] * 3,
        compiler_params=_params("arbitrary"),
        name="rwkv_scan",
    )(*([ops] * SCAN_OPERANDS), lane_param(k_k), lane_param(k_a), lane_param(r_k.reshape(-1)),
      lane_param(gn_g), lane_param(gn_b))


SCAN_TT = 128
SCAN_UNROLL = 8


def _to_scan_body(x_ref, o_ref, a_ref):
    bsz = x_ref.shape[0]
    for b in range(bsz):
        for blk in range(B_WIDTH // V7X_LANES):
            cols = slice(blk * V7X_LANES, (blk + 1) * V7X_LANES)
            a_ref[b, cols, :] = x_ref[b, :, cols].T

    def dims(i, carry):
        for u in range(SCAN_UNROLL):
            d = i * SCAN_UNROLL + u
            z = jnp.concatenate([a_ref[b, pl.ds(d, B_HEADS, stride=B_HEAD_DIM), :] for b in range(bsz)],
                                axis=0)
            o_ref[:, d, :] = z.T
        return carry

    lax.fori_loop(0, B_HEAD_DIM // SCAN_UNROLL, dims, 0)


def _to_scan_layout(z, bsz, seq):
    na = z.shape[0]
    tt = min(SCAN_TT, seq)
    return pl.pallas_call(
        _to_scan_body,
        grid=(na, seq // tt),
        in_specs=[pl.BlockSpec((None, bsz, tt, B_WIDTH), lambda a, i: (a, 0, i, 0))],
        out_specs=pl.BlockSpec((None, tt, B_HEAD_DIM, bsz * B_HEADS), lambda a, i: (a, i, 0, 0)),
        out_shape=jax.ShapeDtypeStruct((na, seq, B_HEAD_DIM, bsz * B_HEADS), F32),
        scratch_shapes=[pltpu.VMEM((bsz, B_WIDTH, tt), F32)],
        compiler_params=_params("parallel", "arbitrary"),
        name="to_scan_layout",
    )(z.reshape(na, bsz, seq, B_WIDTH))


def _from_scan_body(y_ref, o_ref, a_ref):
    bsz = o_ref.shape[0]

    def dims(i, carry):
        for u in range(SCAN_UNROLL):
            d = i * SCAN_UNROLL + u
            zt = y_ref[:, d, :].T
            for b in range(bsz):
                a_ref[b, pl.ds(d, B_HEADS, stride=B_HEAD_DIM), :] = zt[b * B_HEADS:(b + 1) * B_HEADS, :]
        return carry

    lax.fori_loop(0, B_HEAD_DIM // SCAN_UNROLL, dims, 0)
    for b in range(bsz):
        for blk in range(B_WIDTH // V7X_LANES):
            cols = slice(blk * V7X_LANES, (blk + 1) * V7X_LANES)
            o_ref[b, :, cols] = a_ref[b, cols, :].T


def _from_scan_layout(y, bsz, seq):
    tt = min(SCAN_TT, seq)
    return pl.pallas_call(
        _from_scan_body,
        grid=(seq // tt,),
        in_specs=[pl.BlockSpec((tt, B_HEAD_DIM, bsz * B_HEADS), lambda i: (i, 0, 0))],
        out_specs=pl.BlockSpec((bsz, tt, B_WIDTH), lambda i: (0, i, 0)),
        out_shape=jax.ShapeDtypeStruct((bsz, seq, B_WIDTH), F32),
        scratch_shapes=[pltpu.VMEM((bsz, B_WIDTH, tt), F32)],
        compiler_params=_params("arbitrary"),
        name="from_scan_layout",
    )(y).reshape(bsz * seq, B_WIDTH)


def _out_proj1_body(h_ref, yc_ref, bcd_ref, prev_ref, cw_ref, w_ref, o_ref, *, tiles_per_seq):
    tm = h_ref.shape[0]
    first = (pl.program_id(0) % tiles_per_seq) == 0
    z = bcd_ref[:, D_WIDTH:2 * D_WIDTH] * bcd_ref[:, 2 * D_WIDTH:]
    zp = jnp.where(first, 0.0, prev_ref[:, D_WIDTH:2 * D_WIDTH] * prev_ref[:, 2 * D_WIDTH:])
    rowid = lax.broadcasted_iota(jnp.int32, (tm, 1), 0)
    z1 = jnp.where(rowid == 0, zp[7:8, :], pltpu.roll(z, 1, axis=0))
    z2 = pltpu.roll(z, 2, axis=0)
    z2 = jnp.where(rowid == 0, zp[6:7, :], jnp.where(rowid == 1, zp[7:8, :], z2))
    y = cw_ref[0:1, :] * z2 + cw_ref[1:2, :] * z1 + cw_ref[2:3, :] * z
    yd = bcd_ref[:, :D_WIDTH] * y
    o_ref[...] = h_ref[...] + (_dot(yc_ref[...], w_ref[:C_WIDTH, :]) + _dot(yd, w_ref[C_WIDTH:, :]))


def _out_proj1(h, yc, bcd, conv_w, w_out, seq, tm=512):
    n = h.shape[0]
    per8 = tm // V7X_SUBLANES
    return pl.pallas_call(
        functools.partial(_out_proj1_body, tiles_per_seq=seq // tm),
        grid=(n // tm,),
        in_specs=[_rows_spec(tm, D_MODEL), _rows_spec(tm, C_WIDTH), _rows_spec(tm, 3 * D_WIDTH),
                  pl.BlockSpec((V7X_SUBLANES, 3 * D_WIDTH), lambda i: (jnp.maximum(i * per8 - 1, 0), 0)),
                  _const_spec((CONV_W, D_WIDTH)), _const_spec(w_out.shape)],
        out_specs=_rows_spec(tm, D_MODEL),
        out_shape=jax.ShapeDtypeStruct((n, D_MODEL), F32),
        compiler_params=_params("parallel"),
        name="out_proj1",
    )(h, yc, bcd, bcd, conv_w, w_out)


def _layer0_back_body(h_ref, ya_ref, ys_ref, gm_ref, wo_ref, g_ref, wg_ref, wu_ref, wd_ref, o_ref):
    yb = ys_ref[...] * gm_ref[...]
    h = h_ref[...] + (_dot(ya_ref[...], wo_ref[:A_WIDTH, :]) + _dot(yb, wo_ref[A_WIDTH:, :]))
    hn = _rms(h, g_ref[...]).astype(BF16)
    gate = jnp.dot(hn, wg_ref[...], preferred_element_type=F32)
    up = jnp.dot(hn, wu_ref[...], preferred_element_type=F32)
    act = (gate * _sigmoid(gate)) * up
    o_ref[...] = h + _dot(act, wd_ref[...])


def _layer0_back(h, ya, ys, gm, w_out, gain, wg, wu, wd, tm=256):
    n = h.shape[0]
    return pl.pallas_call(
        _layer0_back_body,
        grid=(n // tm,),
        in_specs=[_rows_spec(tm, D_MODEL), _rows_spec(tm, A_WIDTH), _rows_spec(tm, B_WIDTH),
                  _rows_spec(tm, B_WIDTH), _const_spec(w_out.shape), _const_spec((1, D_MODEL)),
                  _const_spec(wg.shape), _const_spec(wu.shape), _const_spec(wd.shape)],
        out_specs=_rows_spec(tm, D_MODEL),
        out_shape=jax.ShapeDtypeStruct((n, D_MODEL), F32),
        compiler_params=_params("parallel"),
        name="layer0_back",
    )(h, ya, ys, gm, w_out, gain.reshape(1, -1), wg, wu, wd)


def _rope(x, c, s_up, s_dn):
    half = ROT_DIM // 2
    return x * c + pltpu.roll(x, half, axis=1) * s_up + pltpu.roll(x, V7X_LANES - half, axis=1) * s_dn


def _lane_blocks(x):
    return [x[:, i * V7X_LANES:(i + 1) * V7X_LANES] for i in range(x.shape[1] // V7X_LANES)]


def _in_proj1_body(x_ref, g_ref, c_ref, su_ref, sd_ref, kb_ref, one_ref, wq_ref, wkc_ref, wkv_ref,
                   wgl_ref, wbcd_ref, qn_o, qr_o, kc_o, ks_o, vs_o, kw_o, vw_o, gl_o, bcd_o):
    xn = _rms(x_ref[...], g_ref[...]).astype(BF16)
    d = functools.partial(jnp.dot, preferred_element_type=F32)
    c, su, sd = c_ref[...], su_ref[...], sd_ref[...]
    rope = lambda z: _rope(z, c, su, sd)
    q = d(xn, wq_ref[...]) * (C_HEAD_DIM ** -0.5)
    qn_o[...] = q.astype(BF16)
    qr_o[...] = jnp.concatenate([rope(z) for z in _lane_blocks(q)], axis=1).astype(BF16)
    kc_o[...] = d(xn, wkc_ref[...])
    kv = _lane_blocks(d(xn, wkv_ref[...]))
    hk = C_KV_HEADS
    ks_o[...] = jnp.concatenate([rope(z) + kb_ref[...] for z in kv[:hk]], axis=1).astype(BF16)
    vs_o[...] = jnp.concatenate([z + one_ref[...] for z in kv[hk:2 * hk]], axis=1).astype(BF16)
    kw_o[...] = jnp.concatenate([rope(z) for z in kv[2 * hk:3 * hk]], axis=1).astype(BF16)
    vw_o[...] = jnp.concatenate([z + one_ref[...] for z in kv[3 * hk:]], axis=1).astype(BF16)
    gl_o[...] = d(xn, wgl_ref[...])
    bcd_o[...] = d(xn, wbcd_ref[...])


def _head_tables(seq):
    half = ROT_DIM // 2
    pos = jnp.arange(seq, dtype=F32)
    inv_freq = ROPE_THETA ** (-jnp.arange(0, ROT_DIM, 2, dtype=F32) / ROT_DIM)
    ang = pos[:, None] * inv_freq[None, :]
    cos, sin = jnp.cos(ang), jnp.sin(ang)
    pad = jnp.zeros((seq, V7X_LANES - ROT_DIM), F32)
    zeros = jnp.zeros((seq, half), F32)
    c = jnp.concatenate([cos, cos, pad + 1.0], axis=1)
    s_up = jnp.concatenate([zeros, sin, pad], axis=1)
    s_dn = jnp.concatenate([-sin, zeros, pad], axis=1)
    lane = jnp.arange(V7X_LANES)[None, :]
    blk = (jnp.arange(seq) // SLC_BLK)[:, None]
    k_bias = jnp.where(lane == C_HEAD_DIM + blk, NEG_INF, 0.0).astype(F32)
    ones = (lane == C_HEAD_DIM).astype(F32)
    return c, s_up, s_dn, k_bias, ones


def _pad_heads(w, n_heads):
    width = w.shape[1] // n_heads
    w = w.reshape(w.shape[0], n_heads, width)
    return jnp.pad(w, ((0, 0), (0, 0), (0, V7X_LANES - width))).reshape(w.shape[0], n_heads * V7X_LANES)


def _in_proj1(x, gain, w_in, seq, tm=256):
    n, d = x.shape
    o = np.cumsum([0, C_WIDTH] + [KV_WIDTH] * 6 + [C_HEADS * N_BRANCH] + [D_WIDTH] * 3)
    wq = _pad_heads(w_in[:, o[0]:o[1]], C_HEADS).astype(BF16)
    wkc = w_in[:, o[1]:o[3]].astype(BF16)
    wkv = _pad_heads(w_in[:, o[3]:o[7]], 4 * C_KV_HEADS).astype(BF16)
    wgl = _pad_heads(w_in[:, o[7]:o[8]], C_KV_HEADS).astype(BF16)
    wbcd = w_in[:, o[8]:o[11]].astype(BF16)
    ws = [wq, wkc, wkv, wgl, wbcd]
    kvw = C_KV_HEADS * V7X_LANES
    widths = [C_HEADS * V7X_LANES] * 2 + [2 * KV_WIDTH] + [kvw] * 4 + [kvw, 3 * D_WIDTH]
    dts = [BF16, BF16, F32, BF16, BF16, BF16, BF16, F32, F32]
    tps = seq // tm
    tab_spec = pl.BlockSpec((tm, V7X_LANES), lambda i: (i % tps, 0))
    c, s_up, s_dn, k_bias, ones = _head_tables(seq)
    return pl.pallas_call(
        _in_proj1_body,
        grid=(n // tm,),
        in_specs=[_rows_spec(tm, d), _const_spec((1, d)), tab_spec, tab_spec, tab_spec, tab_spec,
                  _const_spec((1, V7X_LANES))] + [_const_spec(w.shape) for w in ws],
        out_specs=[_rows_spec(tm, wd) for wd in widths],
        out_shape=[jax.ShapeDtypeStruct((n, wd), dt) for wd, dt in zip(widths, dts)],
        compiler_params=_params("parallel"),
        name="in_proj1",
    )(x, gain.reshape(1, d), c, s_up, s_dn, k_bias, ones, *ws)


def _compress_body(r_ref, p_ref, w1_ref, w2_ref, o_ref):
    half = CMP_STRIDE * C_HEAD_DIM
    r = r_ref[0, 0]
    lo = _dot(r + p_ref[:, :half], w1_ref[:half, :])
    hi = _dot(r + p_ref[:, half:], w1_ref[half:, :])
    n_rows = r.shape[0]
    hidden = _gelu(lo + pltpu.roll(hi, n_rows - 1, axis=0))
    o_ref[0, 0] = _dot(hidden, w2_ref[...]).astype(o_ref.dtype)


def _compress(z, pos, w1, w2, bsz, seq):
    n_rows = seq // CMP_STRIDE
    r = z.reshape(bsz, n_rows, CMP_STRIDE, C_KV_HEADS, C_HEAD_DIM).transpose(0, 3, 1, 2, 4).reshape(
        bsz, C_KV_HEADS, n_rows, CMP_STRIDE * C_HEAD_DIM)
    w2p = _pad_heads(w2, 1).astype(BF16)
    blk = lambda w: pl.BlockSpec((1, 1, n_rows, w), lambda b, h: (b, h, 0, 0))
    return pl.pallas_call(
        _compress_body,
        grid=(bsz, C_KV_HEADS),
        in_specs=[blk(CMP_STRIDE * C_HEAD_DIM), _const_spec((1, CMP_LEN * C_HEAD_DIM)),
                  _const_spec(w1.shape), _const_spec(w2p.shape)],
        out_specs=blk(V7X_LANES),
        out_shape=jax.ShapeDtypeStruct((bsz, C_KV_HEADS, n_rows, V7X_LANES), BF16),
        compiler_params=_params("parallel", "parallel"),
        name="nsa_compress",
    )(r, pos.reshape(1, -1), w1.astype(BF16), w2p)


SEL_CHUNK = 512


def _stack_heads(blk):
    return jnp.concatenate(_lane_blocks(blk), axis=0)


def _rows4(x):
    return jnp.concatenate([x] * C_GROUP, axis=0)


def _nsa_body(qn_ref, qr_ref, gl_ref, kc_ref, vc_ref, ks_ref, vs_ref, kw_ref, vw_ref, ovl_ref, place_ref,
              o_ref):
    n_cmp = kc_ref.shape[2]
    n_slc = ovl_ref.shape[0]
    hd = C_HEAD_DIM
    qb = pl.program_id(2)
    s0 = qb * NSA_TQ
    t_pos = s0 + lax.broadcasted_iota(jnp.int32, (NSA_TQ, 1), 0)
    qn4 = _stack_heads(qn_ref[0])
    qr4 = _stack_heads(qr_ref[0])

    cmp_end = lax.broadcasted_iota(jnp.int32, (1, n_cmp), 1) * CMP_STRIDE + (CMP_LEN - 1)
    pc = _masked_softmax(_dot_nt(qn4, kc_ref[0, 0]), _rows4(cmp_end <= t_pos))
    o_c = _dot(pc, vc_ref[0, 0])
    pc_sum = (pc[:NSA_TQ] + pc[NSA_TQ:2 * NSA_TQ]) + (pc[2 * NSA_TQ:3 * NSA_TQ] + pc[3 * NSA_TQ:])

    p_hi, p_lo = _split_bf16(pc_sum)
    ovl = ovl_ref[...]
    imp = _dot_nt(ovl, p_hi) + _dot_nt(ovl, p_lo)
    jb = lax.broadcasted_iota(jnp.int32, (n_slc, 1), 0)
    t_row = s0 + lax.broadcasted_iota(jnp.int32, (1, NSA_TQ), 1)
    cur = t_row // SLC_BLK
    forced = (jb == 0) | (jb == cur) | (jb == cur - 1)
    imp = jnp.where(jb * SLC_BLK <= t_row, imp + jnp.where(forced, FORCE_BONUS, 0.0), NEG_INF)
    rank = jnp.zeros((n_slc, NSA_TQ), F32)
    for k in range(n_slc):
        ck = imp[k:k + 1, :]
        beats = (ck > imp) | ((ck == imp) & (jb > k))
        rank = rank + jnp.where(beats, 1.0, 0.0)
    not_sel = jnp.where(rank < float(min(SEL_TOPK, n_slc)), 0.0, 1.0).astype(BF16)
    not_sel_q = lax.dot_general(not_sel, place_ref[...], (((0,), (0,)), ((), ())),
                                preferred_element_type=F32)
    q_sel = qr4 + _rows4(not_sel_q.astype(BF16))

    def sel_chunk(c, m, acc, causal):
        k0 = pl.multiple_of(c * SEL_CHUNK, SEL_CHUNK)
        s = _dot_nt(q_sel, ks_ref[0, pl.ds(k0, SEL_CHUNK), :])
        if causal:
            key = k0 + lax.broadcasted_iota(jnp.int32, (1, SEL_CHUNK), 1)
            s = s + _rows4(jnp.where(key <= t_pos, 0.0, NEG_INF))
        m_new = jnp.maximum(m, jnp.max(s, axis=-1, keepdims=True))
        p = jnp.exp(s - m_new)
        acc = jnp.exp(m - m_new) * acc + _dot(p, vs_ref[0, pl.ds(k0, SEL_CHUNK), :])
        return m_new, acc

    rows = C_GROUP * NSA_TQ
    last = qb // (SEL_CHUNK // NSA_TQ)
    m, acc = lax.fori_loop(0, last, lambda c, ma: sel_chunk(c, ma[0], ma[1], False),
                           (jnp.full((rows, 1), NEG_INF, F32), jnp.zeros((rows, V7X_LANES), F32)))
    _, acc = sel_chunk(last, m, acc, True)
    o_s = acc[:, :hd] / jnp.maximum(acc[:, hd:hd + 1], 1e-30)

    band = NSA_TQ + WIN
    w0 = pl.multiple_of(jnp.maximum(s0 - WIN, 0), NSA_TQ)
    diff = t_pos - (w0 + lax.broadcasted_iota(jnp.int32, (1, band), 1))
    s = _dot_nt(qr4, kw_ref[0, pl.ds(w0, band), :]) + _rows4(
        jnp.where((diff >= 0) & (diff < WIN), 0.0, NEG_INF))
    p = jnp.exp(s - jnp.max(s, axis=-1, keepdims=True))
    acc = _dot(p, vw_ref[0, pl.ds(w0, band), :])
    o_w = acc[:, :hd] / jnp.maximum(acc[:, hd:hd + 1], 1e-30)

    gate = _sigmoid(gl_ref[0])
    outs = []
    for g in range(C_GROUP):
        r = slice(g * NSA_TQ, (g + 1) * NSA_TQ)
        gc = g * N_BRANCH
        outs.append(gate[:, gc:gc + 1] * o_c[r, :hd] + gate[:, gc + 1:gc + 2] * o_s[r]
                    + gate[:, gc + 2:gc + 3] * o_w[r])
    o_ref[0] = jnp.concatenate(outs, axis=1).astype(o_ref.dtype)


def _nsa(qn, qr, gl, k_cmp, v_cmp, ks, vs, kw, vw, bsz, seq):
    n_cmp = seq // CMP_STRIDE
    n_slc = seq // SLC_BLK
    assert C_HEAD_DIM + n_slc <= V7X_LANES and seq % SEL_CHUNK == 0
    ci = np.arange(n_cmp)[None, :] * CMP_STRIDE
    sj = np.arange(n_slc)[:, None] * SLC_BLK
    overlap_t = jnp.asarray(((ci < sj + SLC_BLK) & (ci + CMP_LEN > sj)).astype(np.float32)).astype(BF16)
    place = jnp.asarray(np.eye(n_slc, V7X_LANES, k=C_HEAD_DIM, dtype=np.float32)).astype(BF16)
    gw = C_GROUP * V7X_LANES
    q_spec = pl.BlockSpec((1, NSA_TQ, gw), lambda b, h, i: (b, i, h))
    gl_spec = pl.BlockSpec((1, NSA_TQ, V7X_LANES), lambda b, h, i: (b, i, h))
    cmp_spec = pl.BlockSpec((1, 1, n_cmp, V7X_LANES), lambda b, h, i: (b, h, 0, 0))
    kv_spec = pl.BlockSpec((1, seq, V7X_LANES), lambda b, h, i: (b, 0, h))
    as3 = lambda z: z.reshape(bsz, seq, z.shape[-1])
    return pl.pallas_call(
        _nsa_body,
        grid=(bsz, C_KV_HEADS, seq // NSA_TQ),
        in_specs=[q_spec, q_spec, gl_spec, cmp_spec, cmp_spec, kv_spec, kv_spec, kv_spec, kv_spec,
                  _const_spec((n_slc, n_cmp)), _const_spec((n_slc, V7X_LANES))],
        out_specs=pl.BlockSpec((1, NSA_TQ, C_GROUP * C_HEAD_DIM), lambda b, h, i: (b, i, h)),
        out_shape=jax.ShapeDtypeStruct((bsz, seq, C_WIDTH), BF16),
        compiler_params=_params("parallel", "parallel", "arbitrary"),
        name="nsa_attention",
    )(as3(qn), as3(qr), as3(gl), k_cmp, v_cmp, as3(ks), as3(vs), as3(kw), as3(vw), overlap_t, place)


ROW_CHUNKS = D_MODEL // V7X_LANES
assert ROW_CHUNKS == V7X_SUBLANES
DMA_UNROLL = 8
PAD_SPANS = N_EXPERTS + 1


def _tile_rows(n_rows):
    return (n_rows * ROW_CHUNKS, V7X_LANES)


def _row_tile(ref, r):
    start = r * ROW_CHUNKS
    if not isinstance(start, int):
        start = pl.multiple_of(start, ROW_CHUNKS)
    return ref.at[pl.ds(start, ROW_CHUNKS), :]


def _store_row_tiles(ref, x):
    for c in range(ROW_CHUNKS):
        ref[pl.ds(c, x.shape[0], stride=ROW_CHUNKS), :] = x[:, c * V7X_LANES:(c + 1) * V7X_LANES]


def _load_row_tiles(ref):
    rows = ref.shape[0] // ROW_CHUNKS
    return jnp.concatenate([ref[pl.ds(c, rows, stride=ROW_CHUNKS), :] for c in range(ROW_CHUNKS)], axis=1)


def _router_body(h_ref, g_ref, wr_ref, br_ref, tri_ref, hn_o, idx_o, gate_o, cnt_o, cnt_scr):
    @pl.when(pl.program_id(0) == 0)
    def _():
        cnt_scr[...] = jnp.zeros_like(cnt_scr)

    hn = _rms(h_ref[...], g_ref[...])
    _store_row_tiles(hn_o, hn)
    logits = _dot_f32(hn, wr_ref[...]) + br_ref[...]
    lane = lax.broadcasted_iota(jnp.int32, logits.shape, 1)
    m1 = jnp.max(logits, axis=-1, keepdims=True)
    i1 = jnp.min(jnp.where(logits == m1, lane, V7X_LANES), axis=-1, keepdims=True)
    rest = jnp.where(lane == i1, NEG_INF, logits)
    m2 = jnp.max(rest, axis=-1, keepdims=True)
    i2 = jnp.min(jnp.where(rest == m2, lane, V7X_LANES), axis=-1, keepdims=True)
    e2 = jnp.exp(m2 - m1)
    den = 1.0 + e2
    hit1, hit2 = lane == i1, lane == i2
    hits = jnp.where(hit1 | hit2, 1.0, 0.0)
    before = jnp.dot(tri_ref[...], hits.astype(BF16), preferred_element_type=F32) + cnt_scr[...]
    r1 = jnp.sum(jnp.where(hit1, before, 0.0), axis=-1, keepdims=True).astype(jnp.int32)
    r2 = jnp.sum(jnp.where(hit2, before, 0.0), axis=-1, keepdims=True).astype(jnp.int32)
    cnt_scr[...] = cnt_scr[...] + jnp.sum(hits, axis=0, keepdims=True)
    cnt_o[...] = cnt_scr[...]
    idx_o[...] = jnp.where(lane == 0, i1, jnp.where(lane == 1, i2, jnp.where(lane == 2, r1,
                           jnp.where(lane == 3, r2, 0))))
    gate_o[...] = jnp.where(lane == 0, 1.0 / den, jnp.where(lane == 1, e2 / den, 0.0))


def _router(h, gain, w_router, b_router, tm=512):
    n = h.shape[0]
    wr = jnp.pad(w_router, ((0, 0), (0, V7X_LANES - N_EXPERTS)))
    br = jnp.pad(b_router.reshape(1, -1), ((0, 0), (0, V7X_LANES - N_EXPERTS)), constant_values=NEG_INF)
    tri = jnp.asarray(np.tril(np.ones((tm, tm), np.float32), k=-1)).astype(BF16)
    return pl.pallas_call(
        _router_body,
        grid=(n // tm,),
        in_specs=[_rows_spec(tm, D_MODEL), _const_spec((1, D_MODEL)), _const_spec(wr.shape),
                  _const_spec(br.shape), _const_spec(tri.shape)],
        out_specs=[pl.BlockSpec(_tile_rows(tm), lambda i: (i, 0)), _rows_spec(tm, V7X_LANES),
                   _rows_spec(tm, V7X_LANES), _const_spec((1, V7X_LANES))],
        out_shape=[jax.ShapeDtypeStruct(_tile_rows(n), F32), jax.ShapeDtypeStruct((n, V7X_LANES), jnp.int32),
                   jax.ShapeDtypeStruct((n, V7X_LANES), F32), jax.ShapeDtypeStruct((1, V7X_LANES), F32)],
        scratch_shapes=[pltpu.VMEM((1, V7X_LANES), F32)],
        compiler_params=_params("arbitrary"),
        name="moe_router",
    )(h, gain.reshape(1, -1), wr, br, tri)


def _row_copy(src, dst, sem):
    return pltpu.make_async_copy(src, dst, sem)


def _dispatch_body(pad_ref, dest_ref, hn_ref, xs_ref, zero_scr, sem):
    @pl.when(pl.program_id(0) == 0)
    def _():
        zero_scr[...] = jnp.zeros_like(zero_scr)
        for e in range(PAD_SPANS):
            start, count = pad_ref[e], pad_ref[PAD_SPANS + e]

            def fill(r, c):
                _row_copy(zero_scr, _row_tile(xs_ref, start + r), sem).start()
                return c

            def filled(r, c):
                _row_copy(zero_scr, _row_tile(xs_ref, 0), sem).wait()
                return c

            lax.fori_loop(0, count, fill, 0)
            lax.fori_loop(0, count, filled, 0)

    def issue(blk, c):
        for u in range(DMA_UNROLL):
            r = blk * DMA_UNROLL + u
            for k in range(TOP_K):
                _row_copy(_row_tile(hn_ref, r), _row_tile(xs_ref, dest_ref[TOP_K * r + k]),
                          sem).start(priority=k)
        return c

    lax.fori_loop(0, DMA_ROWS // DMA_UNROLL, issue, 0)

    def drain(r, c):
        for k in range(TOP_K):
            _row_copy(_row_tile(hn_ref, 0), _row_tile(xs_ref, 0), sem).wait()
        return c

    lax.fori_loop(0, DMA_ROWS, drain, 0)


def _dispatch(pad_info, dest, hn_rows, p_rows):
    n = hn_rows.shape[0] // ROW_CHUNKS
    return pl.pallas_call(
        _dispatch_body,
        grid_spec=pltpu.PrefetchScalarGridSpec(
            num_scalar_prefetch=1,
            grid=(n // DMA_ROWS,),
            in_specs=[pl.BlockSpec((TOP_K * DMA_ROWS,), lambda i, pad: (i,), memory_space=pltpu.SMEM),
                      pl.BlockSpec(_tile_rows(DMA_ROWS), lambda i, pad: (i, 0))],
            out_specs=pl.BlockSpec(memory_space=pl.ANY),
            scratch_shapes=[pltpu.VMEM(_tile_rows(1), F32), pltpu.SemaphoreType.DMA(())]),
        out_shape=jax.ShapeDtypeStruct(_tile_rows(p_rows), F32),
        compiler_params=_params("arbitrary"),
        name="moe_dispatch",
    )(pad_info, dest, hn_rows)


def _experts_body(blk_e_ref, n_used_ref, x_ref, wg_ref, wu_ref, wd_ref, o_ref):
    i = pl.program_id(0)

    @pl.when(i < n_used_ref[0])
    def _():
        x = _load_row_tiles(x_ref).astype(BF16)
        gate = jnp.dot(x, wg_ref[0], preferred_element_type=F32)
        up = jnp.dot(x, wu_ref[0], preferred_element_type=F32)
        act = (gate * _sigmoid(gate)) * up
        _store_row_tiles(o_ref, _dot(act, wd_ref[0]))

    @pl.when(i >= n_used_ref[0])
    def _():
        o_ref[...] = jnp.zeros_like(o_ref)


def _experts(blk_e, n_used, xs, wg, wu, wd):
    p_rows = xs.shape[0] // ROW_CHUNKS
    x_spec = pl.BlockSpec(_tile_rows(MOE_ROWS), lambda i, be, nu: (jnp.minimum(i, nu[0] - 1), 0))
    o_spec = pl.BlockSpec(_tile_rows(MOE_ROWS), lambda i, be, nu: (i, 0))
    w_spec = lambda w: pl.BlockSpec((1,) + w.shape[1:], lambda i, be, nu: (be[i], 0, 0))
    return pl.pallas_call(
        _experts_body,
        grid_spec=pltpu.PrefetchScalarGridSpec(
            num_scalar_prefetch=2,
            grid=(p_rows // MOE_ROWS,),
            in_specs=[x_spec, w_spec(wg), w_spec(wu), w_spec(wd)],
            out_specs=o_spec),
        out_shape=jax.ShapeDtypeStruct(_tile_rows(p_rows), F32),
        compiler_params=_params("arbitrary"),
        name="moe_experts",
    )(blk_e, n_used, xs, wg, wu, wd)


def _combine_body(dest_ref, h_ref, gate_ref, g_ref, yb_ref, o_ref, y_scr, sem):
    tm = h_ref.shape[0]

    def issue(blk, c):
        for u in range(DMA_UNROLL):
            t = blk * DMA_UNROLL + u
            for k in range(TOP_K):
                _row_copy(_row_tile(yb_ref, dest_ref[TOP_K * t + k]), _row_tile(y_scr.at[k], t),
                          sem).start(priority=k)
        return c

    lax.fori_loop(0, tm // DMA_UNROLL, issue, 0)

    def drain(t, c):
        for k in range(TOP_K):
            _row_copy(_row_tile(yb_ref, 0), _row_tile(y_scr.at[0], 0), sem).wait()
        return c

    lax.fori_loop(0, tm, drain, 0)
    gate = gate_ref[...]
    moe = gate[:, 0:1] * _load_row_tiles(y_scr.at[0]) + gate[:, 1:2] * _load_row_tiles(y_scr.at[1])
    o_ref[...] = _rms(h_ref[...] + moe, g_ref[...])


def _combine(dest, h, gates, gain, yb_rows, tm=256):
    n = h.shape[0]
    return pl.pallas_call(
        _combine_body,
        grid=(n // tm,),
        in_specs=[pl.BlockSpec((TOP_K * tm,), lambda i: (i,), memory_space=pltpu.SMEM),
                  _rows_spec(tm, D_MODEL), _rows_spec(tm, V7X_LANES), _const_spec((1, D_MODEL)),
                  pl.BlockSpec(memory_space=pl.ANY)],
        out_specs=_rows_spec(tm, D_MODEL),
        out_shape=jax.ShapeDtypeStruct((n, D_MODEL), F32),
        scratch_shapes=[pltpu.VMEM((TOP_K,) + _tile_rows(tm), F32), pltpu.SemaphoreType.DMA(())],
        compiler_params=_params("arbitrary"),
        name="moe_combine_norm",
    )(dest, h, gates, gain.reshape(1, -1), yb_rows)


def _moe_layout(idx, counts):
    n = idx.shape[0]
    nk = n * TOP_K
    counts = counts[0, :N_EXPERTS].astype(jnp.int32)
    padded = ((counts + MOE_ROWS - 1) // MOE_ROWS) * MOE_ROWS
    p_end = jnp.cumsum(padded)
    p_start = p_end - padded
    experts = jnp.arange(N_EXPERTS, dtype=jnp.int32)[None, None, :]
    first = jnp.sum(jnp.where(idx[:, :TOP_K, None] == experts, p_start[None, None, :], 0), axis=-1)
    dest = (first + idx[:, TOP_K:2 * TOP_K]).reshape(nk)
    n_blk = (nk + MOE_ROWS - 1) // MOE_ROWS + N_EXPERTS
    blk_e = jnp.minimum(jnp.searchsorted(p_end, jnp.arange(n_blk) * MOE_ROWS, side="right"),
                        N_EXPERTS - 1).astype(jnp.int32)
    n_used = (p_end[-1:] // MOE_ROWS).astype(jnp.int32)
    p_rows = n_blk * MOE_ROWS
    pad_info = jnp.concatenate([p_start + counts, p_end[-1:], padded - counts,
                                p_rows - p_end[-1:]]).astype(jnp.int32)
    return dest.astype(jnp.int32), blk_e, n_used, pad_info, p_rows


def _moe_final(h, norm_g, w_router, b_router, wg, wu, wd, final_g):
    hn, idx, gates, counts = _router(h, norm_g, w_router, b_router)
    dest, blk_e, n_used, pad_info, p_rows = _moe_layout(idx, counts)
    xs = _dispatch(pad_info, dest, hn, p_rows)
    yb = _experts(blk_e, n_used, xs, wg.astype(BF16), wu.astype(BF16), wd.astype(BF16))
    return _combine(dest, h, gates, final_g, yb)


def kernel(x, e_norm_mix, e_w_in, sgu_ln_g, sgu_ln_b, sgu_w, sgu_b, rwkv_mu, rwkv_w0, rwkv_w2,
           rwkv_a0, rwkv_a2, rwkv_g2, rwkv_k_k, rwkv_k_a, rwkv_r_k, rwkv_gn_g, rwkv_gn_b, e_w_out,
           e_norm_ffn, ffn_w_gate, ffn_w_up, ffn_w_down, o_norm_mix, o_w_in, nsa_cmp_pos_k,
           nsa_cmp_pos_v, nsa_cmp_k_w1, nsa_cmp_k_w2, nsa_cmp_v_w1, nsa_cmp_v_w2, conv_w, o_w_out,
           o_norm_ffn, moe_router, moe_router_b, moe_w_gate, moe_w_up, moe_w_down, final_norm):
    bsz, seq, d = x.shape
    n = bsz * seq
    h = x.reshape(n, d)

    ya, scan_ops, g = _layer0_front(h, seq, e_norm_mix[0], e_w_in[0].astype(BF16), sgu_ln_g[0], sgu_ln_b[0],
                                    sgu_w[0], sgu_b[0], rwkv_mu[0], rwkv_w0[0], rwkv_w2[0], rwkv_a0[0],
                                    rwkv_a2[0], rwkv_g2[0])
    ys = _rwkv_scan(_to_scan_layout(scan_ops, bsz, seq), rwkv_k_k[0], rwkv_k_a[0], rwkv_r_k[0],
                    rwkv_gn_g[0], rwkv_gn_b[0], bsz, seq)
    h = _layer0_back(h, ya, _from_scan_layout(ys, bsz, seq), g, e_w_out[0].astype(BF16), e_norm_ffn[0],
                     ffn_w_gate[0].astype(BF16), ffn_w_up[0].astype(BF16), ffn_w_down[0].astype(BF16))

    qn, qr, kcvc, ks, vs, kw, vw, gl, bcd = _in_proj1(h, o_norm_mix[0], o_w_in[0], seq)
    k_cmp = _compress(kcvc[:, :KV_WIDTH], nsa_cmp_pos_k[0], nsa_cmp_k_w1[0], nsa_cmp_k_w2[0], bsz, seq)
    v_cmp = _compress(kcvc[:, KV_WIDTH:], nsa_cmp_pos_v[0], nsa_cmp_v_w1[0], nsa_cmp_v_w2[0], bsz, seq)
    yc = _nsa(qn, qr, gl, k_cmp, v_cmp, ks, vs, kw, vw, bsz, seq).reshape(n, C_WIDTH)
    h = _out_proj1(h, yc, bcd, conv_w[0], o_w_out[0].astype(BF16), seq)
    out = _moe_final(h, o_norm_ffn[0], moe_router[0], moe_router_b[0], moe_w_gate[0], moe_w_up[0],
                     moe_w_down[0], final_norm)
    return out.reshape(bsz, seq, d)
```

```python
import functools

import jax
import jax.numpy as jnp
import numpy as np
from jax import lax
from jax.experimental import pallas as pl
from jax.experimental.pallas import tpu as pltpu

F32 = jnp.float32
BF16 = jnp.bfloat16

D_MODEL = 1024
A_GROUPS = 4
A_GROUP_DIM = 128
A_WIDTH = A_GROUPS * A_GROUP_DIM
CHUNK = 128
SGU_LN_EPS = 1e-5
B_HEADS = 8
B_HEAD_DIM = 64
B_WIDTH = B_HEADS * B_HEAD_DIM
DECAY_LORA = 64
ICLR_LORA = 64
GATE_LORA = 128
B_IN = 3 * B_WIDTH + DECAY_LORA + ICLR_LORA + GATE_LORA
RWKV_GN_EPS = 64e-5
C_HEADS = 8
C_KV_HEADS = 2
C_GROUP = C_HEADS // C_KV_HEADS
C_HEAD_DIM = 64
C_WIDTH = C_HEADS * C_HEAD_DIM
KV_WIDTH = C_KV_HEADS * C_HEAD_DIM
N_BRANCH = 3
CMP_LEN = 32
CMP_STRIDE = 16
CMP_HIDDEN = 256
SLC_BLK = 64
SEL_TOPK = 8
WIN = 512
NSA_TQ = 256
ROT_DIM = C_HEAD_DIM // 4
ROPE_THETA = 500000.0
D_WIDTH = 512
CONV_W = 3
FFN_DIM = 2816
N_EXPERTS = 8
TOP_K = 2
EXPERT_DIM = 1408
NORM_EPS = 1e-6
NEG_INF = -1e30
FORCE_BONUS = 1e6

V7X_LANES = 128
V7X_SUBLANES = 8
V7X_VMEM_LIMIT = 56 * 1024 * 1024

MOE_ROWS = 512
DMA_ROWS = 512


def _params(*sem):
    return pltpu.CompilerParams(dimension_semantics=sem, vmem_limit_bytes=V7X_VMEM_LIMIT)


def _const_spec(shape):
    zeros = (0,) * len(shape)
    return pl.BlockSpec(shape, lambda *_: zeros)


def _rows_spec(tm, width):
    return pl.BlockSpec((tm, width), lambda i: (i, 0))


def _rms(x, g):
    return x * lax.rsqrt(jnp.mean(x * x, axis=-1, keepdims=True) + NORM_EPS) * g


def _gelu(x):
    return x * (0.5 * (1.0 + jnp.tanh(0.7978845608028654 * (x + 0.044715 * (x * x * x)))))


def _sigmoid(x):
    return 1.0 / (1.0 + jnp.exp(-x))


def _dot(a, b):
    return jnp.dot(a.astype(BF16), b.astype(BF16), preferred_element_type=F32)


def _dot_nt(a, b):
    return lax.dot_general(a.astype(BF16), b.astype(BF16), (((1,), (1,)), ((), ())),
                           preferred_element_type=F32)


def _split_bf16(a):
    hi = a.astype(BF16)
    lo = (a - hi.astype(F32)).astype(BF16)
    return hi, lo


def _dot_f32(a, b):
    ah, al = _split_bf16(a)
    bh, bl = _split_bf16(b)
    d = functools.partial(jnp.dot, preferred_element_type=F32)
    return d(ah, bh) + (d(al, bh) + d(ah, bl))


def _masked_softmax(s, mask):
    s = jnp.where(mask, s, NEG_INF)
    m = jnp.max(s, axis=-1, keepdims=True)
    p = jnp.where(mask, jnp.exp(s - m), 0.0)
    return p / jnp.maximum(jnp.sum(p, axis=-1, keepdims=True), 1e-30)


def _sgu_chunk(p_uv, lng_ref, lnb_ref, w_ref, b_ref):
    row = lax.broadcasted_iota(jnp.int32, (CHUNK, CHUNK), 0)
    col = lax.broadcasted_iota(jnp.int32, (CHUNK, CHUNK), 1)
    causal = col <= row
    u = _gelu(p_uv[:, :A_WIDTH])
    v = _gelu(p_uv[:, A_WIDTH:])
    outs = []
    for g in range(A_GROUPS):
        cols = slice(g * A_GROUP_DIM, (g + 1) * A_GROUP_DIM)
        vg = v[:, cols]
        mu = jnp.mean(vg, axis=-1, keepdims=True)
        dv = vg - mu
        var = jnp.mean(dv * dv, axis=-1, keepdims=True)
        vn = dv * lax.rsqrt(var + SGU_LN_EPS) * lng_ref[:, cols] + lnb_ref[:, cols]
        wm = jnp.where(causal, w_ref[g], 0.0)
        mixed = _dot(wm, vn) + b_ref[:, g:g + 1]
        outs.append(u[:, cols] * mixed)
    return jnp.concatenate(outs, axis=1)


def _softplus(x):
    return jnp.maximum(x, 0.0) + jnp.log(1.0 + jnp.exp(-jnp.abs(x)))


SCAN_OPERANDS = 5


def _layer0_front_body(x_ref, gain_ref, win_ref, lng_ref, lnb_ref, sw_ref, sb_ref, mu_ref, w0_ref, w2_ref,
                       a0_ref, a2_ref, g2_ref, ya_o, s_o, g_o, prev_scr, *, tiles_per_seq):
    tm = x_ref.shape[0]

    @pl.when(pl.program_id(0) == 0)
    def _():
        prev_scr[...] = jnp.zeros_like(prev_scr)

    p = jnp.dot(_rms(x_ref[...], gain_ref[...]).astype(BF16), win_ref[...], preferred_element_type=F32)
    for c in range(tm // CHUNK):
        rows = slice(c * CHUNK, (c + 1) * CHUNK)
        ya_o[rows, :] = _sgu_chunk(p[rows, :2 * A_WIDTH], lng_ref, lnb_ref, sw_ref, sb_ref).astype(ya_o.dtype)
    x = p[:, 2 * A_WIDTH:]
    first = (pl.program_id(0) % tiles_per_seq) == 0
    prev_row = jnp.where(first, 0.0, prev_scr[...])
    rowid = lax.broadcasted_iota(jnp.int32, (tm, 1), 0)
    shifted = jnp.where(rowid == 0, prev_row, pltpu.roll(x, 1, axis=0))
    prev_scr[...] = x[tm - 1:tm, :]
    xm = x + (shifted - x) * mu_ref[...]
    o = 3 * B_WIDTH
    wl = xm[:, o:o + DECAY_LORA]
    al = xm[:, o + DECAY_LORA:o + DECAY_LORA + ICLR_LORA]
    gl = xm[:, o + DECAY_LORA + ICLR_LORA:]
    w = -_softplus(-(w0_ref[...] + _dot(jnp.tanh(wl), w2_ref[...]))) - 0.5
    s_o[0] = xm[:, :B_WIDTH]
    s_o[1] = jnp.exp(-jnp.exp(w))
    s_o[2] = xm[:, B_WIDTH:2 * B_WIDTH]
    s_o[3] = _sigmoid(a0_ref[...] + _dot(al, a2_ref[...]))
    s_o[4] = xm[:, 2 * B_WIDTH:3 * B_WIDTH]
    g_o[...] = _dot(_sigmoid(gl), g2_ref[...])


def _layer0_front(h, seq, gain, w_in, ln_g, ln_b, w_s, b_s, mu, w0, w2, a0, a2, g2, tm=256):
    n, d = h.shape
    perm = np.arange(B_WIDTH).reshape(B_HEADS, B_HEAD_DIM).T.reshape(-1)
    r0, k0 = 2 * A_WIDTH, 2 * A_WIDTH + B_WIDTH
    cols = np.concatenate([np.arange(r0), r0 + perm, k0 + perm, np.arange(k0 + B_WIDTH, w_in.shape[1])])
    w_in = w_in[:, cols]
    mu = mu[cols[r0:] - r0]
    w0, w2, a0, a2 = w0[perm], w2[:, perm], a0[perm], a2[:, perm]
    row = lambda v: v.reshape(1, -1)
    consts = [row(gain), w_in, row(ln_g), row(ln_b), w_s, b_s.T, row(mu), row(w0), w2, row(a0), a2, g2]
    outs = [jax.ShapeDtypeStruct((n, A_WIDTH), BF16), jax.ShapeDtypeStruct((SCAN_OPERANDS, n, B_WIDTH), F32),
            jax.ShapeDtypeStruct((n, B_WIDTH), F32)]
    return pl.pallas_call(
        functools.partial(_layer0_front_body, tiles_per_seq=seq // tm),
        grid=(n // tm,),
        in_specs=[_rows_spec(tm, d)] + [_const_spec(c.shape) for c in consts],
        out_specs=[_rows_spec(tm, A_WIDTH), pl.BlockSpec((SCAN_OPERANDS, tm, B_WIDTH), lambda i: (0, i, 0)),
                   _rows_spec(tm, B_WIDTH)],
        out_shape=outs,
        scratch_shapes=[pltpu.VMEM((1, B_IN), F32)],
        compiler_params=_params("arbitrary"),
        name="layer0_front",
    )(h, *consts)


def _rwkv_scan_body(r_ref, w_ref, k0_ref, a_ref, v_ref, kkp_ref, kap_ref, rk_ref, gng_ref, gnb_ref,
                    y_ref, s_ref, kkn_ref, ka_ref, km_ref):
    n, tt = r_ref.shape[0], r_ref.shape[1]

    @pl.when(pl.program_id(0) == 0)
    def _():
        s_ref[...] = jnp.zeros_like(s_ref)

    per_dim = lambda p_ref: p_ref[...][:, None, :]
    k0 = k0_ref[...]
    a = a_ref[...]
    kk = k0 * per_dim(kkp_ref)
    kkn = kk / jnp.maximum(jnp.sqrt(jnp.sum(kk * kk, axis=0, keepdims=True)), 1e-12)
    km = k0 * (1.0 + (a - 1.0) * per_dim(kap_ref))
    kkn_ref[...] = kkn
    ka_ref[...] = kkn * a
    km_ref[...] = km

    zero = jnp.zeros((n, r_ref.shape[2]), F32)

    def sa_init(j, acc):
        return acc + s_ref[j] * kkn_ref[j, pl.ds(0, 1), :]

    sa0 = lax.fori_loop(0, n, sa_init, zero)

    def step(t, sa):
        v_t = v_ref[t]
        now = pl.ds(t, 1)
        nxt = pl.ds(jnp.minimum(t + 1, tt - 1), 1)

        def jbody(j, carry):
            y, san = carry
            sn = s_ref[j] * w_ref[j, now, :] + (v_t * km_ref[j, now, :] - sa * ka_ref[j, now, :])
            s_ref[j] = sn
            return y + sn * r_ref[j, now, :], san + sn * kkn_ref[j, nxt, :]

        y, san = lax.fori_loop(0, n, jbody, (zero, zero), unroll=8)
        y_ref[t] = y
        return san

    lax.fori_loop(0, tt, step, sa0)

    y = y_ref[...]
    ym = jnp.mean(y, axis=1, keepdims=True)
    dy = y - ym
    yv = jnp.mean(dy * dy, axis=1, keepdims=True)
    yn = dy * lax.rsqrt(yv + RWKV_GN_EPS) * gng_ref[...][None] + gnb_ref[...][None]
    bonus = jnp.sum(r_ref[...] * km * per_dim(rk_ref), axis=0)
    y_ref[...] = yn + bonus[:, None, :] * v_ref[...]


def _rwkv_scan(rwka, v, k_k, k_a, r_k, gn_g, gn_b, bsz, seq, tt=32):
    n = B_HEAD_DIM
    lanes = bsz * B_HEADS

    def lane_param(p):
        return jnp.tile(p.reshape(B_HEADS, n).T, (1, bsz))

    op_spec = lambda a: pl.BlockSpec((None, n, tt, lanes), lambda c: (a, 0, c, 0))
    slab_spec = pl.BlockSpec((tt, n, lanes), lambda c: (c, 0, 0))
    par_spec = _const_spec((n, lanes))
    return pl.pallas_call(
        _rwkv_scan_body,
        grid=(seq // tt,),
        in_specs=[op_spec(a) for a in range(SCAN_OPERANDS - 1)] + [slab_spec] + [par_spec] * 5,
        out_specs=slab_spec,
        out_shape=jax.ShapeDtypeStruct((seq, n, lanes), F32),
        scratch_shapes=[pltpu.VMEM((n, n, lanes), F32)] + [pltpu.VMEM((n, tt, lanes), F32)] * 3,
        compiler_params=_params("arbitrary"),
        name="rwkv_scan",
    )(*([rwka] * (SCAN_OPERANDS - 1)), v, lane_param(k_k), lane_param(k_a), lane_param(r_k.reshape(-1)),
      lane_param(gn_g), lane_param(gn_b))


SCAN_TT = 128
SCAN_UNROLL = 8


def _time_to_lanes(x_ref, a_ref):
    for b in range(x_ref.shape[0]):
        for blk in range(B_WIDTH // V7X_LANES):
            cols = slice(blk * V7X_LANES, (blk + 1) * V7X_LANES)
            a_ref[b, cols, :] = x_ref[b, :, cols].T


def _to_scan_rows_body(x_ref, o_ref, a_ref):
    bsz = x_ref.shape[0]
    _time_to_lanes(x_ref, a_ref)

    def dims(i, carry):
        for u in range(SCAN_UNROLL):
            d = i * SCAN_UNROLL + u
            rows = pl.ds(pl.multiple_of(d * B_HEADS, B_HEADS), B_HEADS)
            z = jnp.concatenate([a_ref[b, rows, :] for b in range(bsz)], axis=0)
            o_ref[d] = z.T
        return carry

    lax.fori_loop(0, B_HEAD_DIM // SCAN_UNROLL, dims, 0)


def _to_scan_slab_body(x_ref, o_ref, a_ref):
    bsz = x_ref.shape[0]
    _time_to_lanes(x_ref, a_ref)

    def dims(i, carry):
        for u in range(SCAN_UNROLL):
            d = i * SCAN_UNROLL + u
            z = jnp.concatenate([a_ref[b, pl.ds(d, B_HEADS, stride=B_HEAD_DIM), :] for b in range(bsz)],
                                axis=0)
            o_ref[:, d, :] = z.T
        return carry

    lax.fori_loop(0, B_HEAD_DIM // SCAN_UNROLL, dims, 0)


def _to_scan_layout(ops, bsz, seq):
    tt = min(SCAN_TT, seq)
    lanes = bsz * B_HEADS
    ops4 = ops.reshape(SCAN_OPERANDS, bsz, seq, B_WIDTH)
    scratch = [pltpu.VMEM((bsz, B_WIDTH, tt), F32)]
    rwka = pl.pallas_call(
        _to_scan_rows_body,
        grid=(SCAN_OPERANDS - 1, seq // tt),
        in_specs=[pl.BlockSpec((None, bsz, tt, B_WIDTH), lambda a, i: (a, 0, i, 0))],
        out_specs=pl.BlockSpec((None, B_HEAD_DIM, tt, lanes), lambda a, i: (a, 0, i, 0)),
        out_shape=jax.ShapeDtypeStruct((SCAN_OPERANDS - 1, B_HEAD_DIM, seq, lanes), F32),
        scratch_shapes=scratch,
        compiler_params=_params("parallel", "arbitrary"),
        name="to_scan_rows",
    )(ops4)
    v = pl.pallas_call(
        _to_scan_slab_body,
        grid=(seq // tt,),
        in_specs=[pl.BlockSpec((None, bsz, tt, B_WIDTH), lambda i: (SCAN_OPERANDS - 1, 0, i, 0))],
        out_specs=pl.BlockSpec((tt, B_HEAD_DIM, lanes), lambda i: (i, 0, 0)),
        out_shape=jax.ShapeDtypeStruct((seq, B_HEAD_DIM, lanes), F32),
        scratch_shapes=scratch,
        compiler_params=_params("arbitrary"),
        name="to_scan_slab",
    )(ops4)
    return rwka, v


def _from_scan_body(y_ref, o_ref, a_ref):
    bsz = o_ref.shape[0]

    def dims(i, carry):
        for u in range(SCAN_UNROLL):
            d = i * SCAN_UNROLL + u
            zt = y_ref[:, d, :].T
            for b in range(bsz):
                a_ref[b, pl.ds(d, B_HEADS, stride=B_HEAD_DIM), :] = zt[b * B_HEADS:(b + 1) * B_HEADS, :]
        return carry

    lax.fori_loop(0, B_HEAD_DIM // SCAN_UNROLL, dims, 0)
    for b in range(bsz):
        for blk in range(B_WIDTH // V7X_LANES):
            cols = slice(blk * V7X_LANES, (blk + 1) * V7X_LANES)
            o_ref[b, :, cols] = a_ref[b, cols, :].T


def _from_scan_layout(y, bsz, seq):
    tt = min(SCAN_TT, seq)
    return pl.pallas_call(
        _from_scan_body,
        grid=(seq // tt,),
        in_specs=[pl.BlockSpec((tt, B_HEAD_DIM, bsz * B_HEADS), lambda i: (i, 0, 0))],
        out_specs=pl.BlockSpec((bsz, tt, B_WIDTH), lambda i: (0, i, 0)),
        out_shape=jax.ShapeDtypeStruct((bsz, seq, B_WIDTH), F32),
        scratch_shapes=[pltpu.VMEM((bsz, B_WIDTH, tt), F32)],
        compiler_params=_params("arbitrary"),
        name="from_scan_layout",
    )(y).reshape(bsz * seq, B_WIDTH)


def _out_proj1_body(h_ref, yc_ref, bcd_ref, prev_ref, cw_ref, w_ref, o_ref, *, tiles_per_seq):
    tm = h_ref.shape[0]
    first = (pl.program_id(0) % tiles_per_seq) == 0
    z = bcd_ref[:, D_WIDTH:2 * D_WIDTH] * bcd_ref[:, 2 * D_WIDTH:]
    zp = jnp.where(first, 0.0, prev_ref[:, D_WIDTH:2 * D_WIDTH] * prev_ref[:, 2 * D_WIDTH:])
    rowid = lax.broadcasted_iota(jnp.int32, (tm, 1), 0)
    z1 = jnp.where(rowid == 0, zp[7:8, :], pltpu.roll(z, 1, axis=0))
    z2 = pltpu.roll(z, 2, axis=0)
    z2 = jnp.where(rowid == 0, zp[6:7, :], jnp.where(rowid == 1, zp[7:8, :], z2))
    y = cw_ref[0:1, :] * z2 + cw_ref[1:2, :] * z1 + cw_ref[2:3, :] * z
    yd = bcd_ref[:, :D_WIDTH] * y
    o_ref[...] = h_ref[...] + (_dot(yc_ref[...], w_ref[:C_WIDTH, :]) + _dot(yd, w_ref[C_WIDTH:, :]))


def _out_proj1(h, yc, bcd, conv_w, w_out, seq, tm=512):
    n = h.shape[0]
    per8 = tm // V7X_SUBLANES
    return pl.pallas_call(
        functools.partial(_out_proj1_body, tiles_per_seq=seq // tm),
        grid=(n // tm,),
        in_specs=[_rows_spec(tm, D_MODEL), _rows_spec(tm, C_WIDTH), _rows_spec(tm, 3 * D_WIDTH),
                  pl.BlockSpec((V7X_SUBLANES, 3 * D_WIDTH), lambda i: (jnp.maximum(i * per8 - 1, 0), 0)),
                  _const_spec((CONV_W, D_WIDTH)), _const_spec(w_out.shape)],
        out_specs=_rows_spec(tm, D_MODEL),
        out_shape=jax.ShapeDtypeStruct((n, D_MODEL), F32),
        compiler_params=_params("parallel"),
        name="out_proj1",
    )(h, yc, bcd, bcd, conv_w, w_out)


def _layer0_back_body(h_ref, ya_ref, ys_ref, gm_ref, wo_ref, g_ref, wg_ref, wu_ref, wd_ref, o_ref):
    yb = ys_ref[...] * gm_ref[...]
    h = h_ref[...] + (_dot(ya_ref[...], wo_ref[:A_WIDTH, :]) + _dot(yb, wo_ref[A_WIDTH:, :]))
    hn = _rms(h, g_ref[...]).astype(BF16)
    gate = jnp.dot(hn, wg_ref[...], preferred_element_type=F32)
    up = jnp.dot(hn, wu_ref[...], preferred_element_type=F32)
    act = (gate * _sigmoid(gate)) * up
    o_ref[...] = h + _dot(act, wd_ref[...])


def _layer0_back(h, ya, ys, gm, w_out, gain, wg, wu, wd, tm=256):
    n = h.shape[0]
    return pl.pallas_call(
        _layer0_back_body,
        grid=(n // tm,),
        in_specs=[_rows_spec(tm, D_MODEL), _rows_spec(tm, A_WIDTH), _rows_spec(tm, B_WIDTH),
                  _rows_spec(tm, B_WIDTH), _const_spec(w_out.shape), _const_spec((1, D_MODEL)),
                  _const_spec(wg.shape), _const_spec(wu.shape), _const_spec(wd.shape)],
        out_specs=_rows_spec(tm, D_MODEL),
        out_shape=jax.ShapeDtypeStruct((n, D_MODEL), F32),
        compiler_params=_params("parallel"),
        name="layer0_back",
    )(h, ya, ys, gm, w_out, gain.reshape(1, -1), wg, wu, wd)


def _rope(x, c, s_up, s_dn):
    half = ROT_DIM // 2
    return x * c + pltpu.roll(x, half, axis=1) * s_up + pltpu.roll(x, V7X_LANES - half, axis=1) * s_dn


def _lane_blocks(x):
    return [x[:, i * V7X_LANES:(i + 1) * V7X_LANES] for i in range(x.shape[1] // V7X_LANES)]


def _in_proj1_body(x_ref, g_ref, c_ref, su_ref, sd_ref, kb_ref, one_ref, wq_ref, wkc_ref, wkv_ref,
                   wgl_ref, wbcd_ref, qn_o, qr_o, kc_o, ks_o, vs_o, kw_o, vw_o, gl_o, bcd_o):
    xn = _rms(x_ref[...], g_ref[...]).astype(BF16)
    d = functools.partial(jnp.dot, preferred_element_type=F32)
    c, su, sd = c_ref[...], su_ref[...], sd_ref[...]
    rope = lambda z: _rope(z, c, su, sd)
    q = d(xn, wq_ref[...]) * (C_HEAD_DIM ** -0.5)
    qn_o[...] = q.astype(BF16)
    qr_o[...] = jnp.concatenate([rope(z) for z in _lane_blocks(q)], axis=1).astype(BF16)
    kc_o[...] = d(xn, wkc_ref[...])
    kv = _lane_blocks(d(xn, wkv_ref[...]))
    hk = C_KV_HEADS
    ks_o[...] = jnp.concatenate([rope(z) + kb_ref[...] for z in kv[:hk]], axis=1).astype(BF16)
    vs_o[...] = jnp.concatenate([z + one_ref[...] for z in kv[hk:2 * hk]], axis=1).astype(BF16)
    kw_o[...] = jnp.concatenate([rope(z) for z in kv[2 * hk:3 * hk]], axis=1).astype(BF16)
    vw_o[...] = jnp.concatenate([z + one_ref[...] for z in kv[3 * hk:]], axis=1).astype(BF16)
    gl_o[...] = d(xn, wgl_ref[...])
    bcd_o[...] = d(xn, wbcd_ref[...])


def _head_tables(seq):
    half = ROT_DIM // 2
    pos = jnp.arange(seq, dtype=F32)
    inv_freq = ROPE_THETA ** (-jnp.arange(0, ROT_DIM, 2, dtype=F32) / ROT_DIM)
    ang = pos[:, None] * inv_freq[None, :]
    cos, sin = jnp.cos(ang), jnp.sin(ang)
    pad = jnp.zeros((seq, V7X_LANES - ROT_DIM), F32)
    zeros = jnp.zeros((seq, half), F32)
    c = jnp.concatenate([cos, cos, pad + 1.0], axis=1)
    s_up = jnp.concatenate([zeros, sin, pad], axis=1)
    s_dn = jnp.concatenate([-sin, zeros, pad], axis=1)
    lane = jnp.arange(V7X_LANES)[None, :]
    blk = (jnp.arange(seq) // SLC_BLK)[:, None]
    k_bias = jnp.where(lane == C_HEAD_DIM + blk, NEG_INF, 0.0).astype(F32)
    ones = (lane == C_HEAD_DIM).astype(F32)
    return c, s_up, s_dn, k_bias, ones


def _pad_heads(w, n_heads):
    width = w.shape[1] // n_heads
    w = w.reshape(w.shape[0], n_heads, width)
    return jnp.pad(w, ((0, 0), (0, 0), (0, V7X_LANES - width))).reshape(w.shape[0], n_heads * V7X_LANES)


def _in_proj1(x, gain, w_in, seq, tm=256):
    n, d = x.shape
    o = np.cumsum([0, C_WIDTH] + [KV_WIDTH] * 6 + [C_HEADS * N_BRANCH] + [D_WIDTH] * 3)
    wq = _pad_heads(w_in[:, o[0]:o[1]], C_HEADS).astype(BF16)
    wkc = w_in[:, o[1]:o[3]].astype(BF16)
    wkv = _pad_heads(w_in[:, o[3]:o[7]], 4 * C_KV_HEADS).astype(BF16)
    wgl = _pad_heads(w_in[:, o[7]:o[8]], C_KV_HEADS).astype(BF16)
    wbcd = w_in[:, o[8]:o[11]].astype(BF16)
    ws = [wq, wkc, wkv, wgl, wbcd]
    kvw = C_KV_HEADS * V7X_LANES
    widths = [C_HEADS * V7X_LANES] * 2 + [2 * KV_WIDTH] + [kvw] * 4 + [kvw, 3 * D_WIDTH]
    dts = [BF16, BF16, F32, BF16, BF16, BF16, BF16, F32, F32]
    tps = seq // tm
    tab_spec = pl.BlockSpec((tm, V7X_LANES), lambda i: (i % tps, 0))
    c, s_up, s_dn, k_bias, ones = _head_tables(seq)
    return pl.pallas_call(
        _in_proj1_body,
        grid=(n // tm,),
        in_specs=[_rows_spec(tm, d), _const_spec((1, d)), tab_spec, tab_spec, tab_spec, tab_spec,
                  _const_spec((1, V7X_LANES))] + [_const_spec(w.shape) for w in ws],
        out_specs=[_rows_spec(tm, wd) for wd in widths],
        out_shape=[jax.ShapeDtypeStruct((n, wd), dt) for wd, dt in zip(widths, dts)],
        compiler_params=_params("parallel"),
        name="in_proj1",
    )(x, gain.reshape(1, d), c, s_up, s_dn, k_bias, ones, *ws)


def _compress_body(z_ref, pk_ref, pv_ref, w1k_ref, w1v_ref, w2k_ref, w2v_ref, ko_ref, vo_ref, z_scr):
    half = CMP_STRIDE * C_HEAD_DIM
    n_rows = z_ref.shape[1] // CMP_STRIDE
    streams = ((pk_ref, w1k_ref, w2k_ref, ko_ref), (pv_ref, w1v_ref, w2v_ref, vo_ref))
    for s, (p_ref, w1_ref, w2_ref, o_ref) in enumerate(streams):
        z_scr[...] = z_ref[0, :, s * KV_WIDTH:(s + 1) * KV_WIDTH]
        every16 = [z_scr[pl.ds(l, n_rows, stride=CMP_STRIDE), :] for l in range(CMP_STRIDE)]
        for hk in range(C_KV_HEADS):
            cols = slice(hk * C_HEAD_DIM, (hk + 1) * C_HEAD_DIM)
            r = jnp.concatenate([z[:, cols] for z in every16], axis=1)
            lo = _dot(r + p_ref[:, :half], w1_ref[:half, :])
            hi = _dot(r + p_ref[:, half:], w1_ref[half:, :])
            hidden = _gelu(lo + pltpu.roll(hi, n_rows - 1, axis=0))
            o_ref[0, hk] = _dot(hidden, w2_ref[...]).astype(o_ref.dtype)


def _compress(kcvc, pos_k, pos_v, w1k, w1v, w2k, w2v, bsz, seq):
    n_rows = seq // CMP_STRIDE
    width = kcvc.shape[1]
    consts = [pos_k.reshape(1, -1), pos_v.reshape(1, -1), w1k.astype(BF16), w1v.astype(BF16),
              _pad_heads(w2k, 1).astype(BF16), _pad_heads(w2v, 1).astype(BF16)]
    o_spec = pl.BlockSpec((1, C_KV_HEADS, n_rows, V7X_LANES), lambda b: (b, 0, 0, 0))
    o_shape = jax.ShapeDtypeStruct((bsz, C_KV_HEADS, n_rows, V7X_LANES), BF16)
    return pl.pallas_call(
        _compress_body,
        grid=(bsz,),
        in_specs=[pl.BlockSpec((1, seq, width), lambda b: (b, 0, 0))] + [_const_spec(c.shape) for c in consts],
        out_specs=[o_spec, o_spec],
        out_shape=[o_shape, o_shape],
        scratch_shapes=[pltpu.VMEM((seq, KV_WIDTH), F32)],
        compiler_params=_params("parallel"),
        name="nsa_compress",
    )(kcvc.reshape(bsz, seq, width), *consts)


SEL_CHUNK = 512


def _stack_heads(blk):
    return jnp.concatenate(_lane_blocks(blk), axis=0)


def _rows4(x):
    return jnp.concatenate([x] * C_GROUP, axis=0)


def _nsa_body(qn_ref, qr_ref, gl_ref, kc_ref, vc_ref, ks_ref, vs_ref, kw_ref, vw_ref, ovl_ref, place_ref,
              o_ref):
    n_cmp = kc_ref.shape[2]
    n_slc = ovl_ref.shape[0]
    hd = C_HEAD_DIM
    qb = pl.program_id(2)
    s0 = qb * NSA_TQ
    t_pos = s0 + lax.broadcasted_iota(jnp.int32, (NSA_TQ, 1), 0)
    qn4 = _stack_heads(qn_ref[0])
    qr4 = _stack_heads(qr_ref[0])

    cmp_end = lax.broadcasted_iota(jnp.int32, (1, n_cmp), 1) * CMP_STRIDE + (CMP_LEN - 1)
    pc = _masked_softmax(_dot_nt(qn4, kc_ref[0, 0]), _rows4(cmp_end <= t_pos))
    o_c = _dot(pc, vc_ref[0, 0])
    pc_sum = (pc[:NSA_TQ] + pc[NSA_TQ:2 * NSA_TQ]) + (pc[2 * NSA_TQ:3 * NSA_TQ] + pc[3 * NSA_TQ:])

    p_hi, p_lo = _split_bf16(pc_sum)
    ovl = ovl_ref[...]
    imp = _dot_nt(ovl, p_hi) + _dot_nt(ovl, p_lo)
    jb = lax.broadcasted_iota(jnp.int32, (n_slc, 1), 0)
    t_row = s0 + lax.broadcasted_iota(jnp.int32, (1, NSA_TQ), 1)
    cur = t_row // SLC_BLK
    forced = (jb == 0) | (jb == cur) | (jb == cur - 1)
    imp = jnp.where(jb * SLC_BLK <= t_row, imp + jnp.where(forced, FORCE_BONUS, 0.0), NEG_INF)
    rank = jnp.zeros((n_slc, NSA_TQ), F32)
    for k in range(n_slc):
        ck = imp[k:k + 1, :]
        beats = (ck > imp) | ((ck == imp) & (jb > k))
        rank = rank + jnp.where(beats, 1.0, 0.0)
    not_sel = jnp.where(rank < float(min(SEL_TOPK, n_slc)), 0.0, 1.0).astype(BF16)
    not_sel_q = lax.dot_general(not_sel, place_ref[...], (((0,), (0,)), ((), ())),
                                preferred_element_type=F32)
    q_sel = qr4 + _rows4(not_sel_q.astype(BF16))

    def sel_chunk(c, m, acc, causal):
        k0 = pl.multiple_of(c * SEL_CHUNK, SEL_CHUNK)
        s = _dot_nt(q_sel, ks_ref[0, pl.ds(k0, SEL_CHUNK), :])
        if causal:
            key = k0 + lax.broadcasted_iota(jnp.int32, (1, SEL_CHUNK), 1)
            s = s + _rows4(jnp.where(key <= t_pos, 0.0, NEG_INF))
        m_new = jnp.maximum(m, jnp.max(s, axis=-1, keepdims=True))
        p = jnp.exp(s - m_new)
        acc = jnp.exp(m - m_new) * acc + _dot(p, vs_ref[0, pl.ds(k0, SEL_CHUNK), :])
        return m_new, acc

    rows = C_GROUP * NSA_TQ
    last = qb // (SEL_CHUNK // NSA_TQ)
    m, acc = lax.fori_loop(0, last, lambda c, ma: sel_chunk(c, ma[0], ma[1], False),
                           (jnp.full((rows, 1), NEG_INF, F32), jnp.zeros((rows, V7X_LANES), F32)))
    _, acc = sel_chunk(last, m, acc, True)
    o_s = acc[:, :hd] / jnp.maximum(acc[:, hd:hd + 1], 1e-30)

    band = NSA_TQ + WIN
    w0 = pl.multiple_of(jnp.maximum(s0 - WIN, 0), NSA_TQ)
    diff = t_pos - (w0 + lax.broadcasted_iota(jnp.int32, (1, band), 1))
    s = _dot_nt(qr4, kw_ref[0, pl.ds(w0, band), :]) + _rows4(
        jnp.where((diff >= 0) & (diff < WIN), 0.0, NEG_INF))
    p = jnp.exp(s - jnp.max(s, axis=-1, keepdims=True))
    acc = _dot(p, vw_ref[0, pl.ds(w0, band), :])
    o_w = acc[:, :hd] / jnp.maximum(acc[:, hd:hd + 1], 1e-30)

    gate = _sigmoid(gl_ref[0])
    outs = []
    for g in range(C_GROUP):
        r = slice(g * NSA_TQ, (g + 1) * NSA_TQ)
        gc = g * N_BRANCH
        outs.append(gate[:, gc:gc + 1] * o_c[r, :hd] + gate[:, gc + 1:gc + 2] * o_s[r]
                    + gate[:, gc + 2:gc + 3] * o_w[r])
    o_ref[0] = jnp.concatenate(outs, axis=1).astype(o_ref.dtype)


def _nsa(qn, qr, gl, k_cmp, v_cmp, ks, vs, kw, vw, bsz, seq):
    n_cmp = seq // CMP_STRIDE
    n_slc = seq // SLC_BLK
    assert C_HEAD_DIM + n_slc <= V7X_LANES and seq % SEL_CHUNK == 0
    ci = np.arange(n_cmp)[None, :] * CMP_STRIDE
    sj = np.arange(n_slc)[:, None] * SLC_BLK
    overlap_t = jnp.asarray(((ci < sj + SLC_BLK) & (ci + CMP_LEN > sj)).astype(np.float32)).astype(BF16)
    place = jnp.asarray(np.eye(n_slc, V7X_LANES, k=C_HEAD_DIM, dtype=np.float32)).astype(BF16)
    gw = C_GROUP * V7X_LANES
    q_spec = pl.BlockSpec((1, NSA_TQ, gw), lambda b, h, i: (b, i, h))
    gl_spec = pl.BlockSpec((1, NSA_TQ, V7X_LANES), lambda b, h, i: (b, i, h))
    cmp_spec = pl.BlockSpec((1, 1, n_cmp, V7X_LANES), lambda b, h, i: (b, h, 0, 0))
    kv_spec = pl.BlockSpec((1, seq, V7X_LANES), lambda b, h, i: (b, 0, h))
    as3 = lambda z: z.reshape(bsz, seq, z.shape[-1])
    return pl.pallas_call(
        _nsa_body,
        grid=(bsz, C_KV_HEADS, seq // NSA_TQ),
        in_specs=[q_spec, q_spec, gl_spec, cmp_spec, cmp_spec, kv_spec, kv_spec, kv_spec, kv_spec,
                  _const_spec((n_slc, n_cmp)), _const_spec((n_slc, V7X_LANES))],
        out_specs=pl.BlockSpec((1, NSA_TQ, C_GROUP * C_HEAD_DIM), lambda b, h, i: (b, i, h)),
        out_shape=jax.ShapeDtypeStruct((bsz, seq, C_WIDTH), BF16),
        compiler_params=_params("parallel", "parallel", "arbitrary"),
        name="nsa_attention",
    )(as3(qn), as3(qr), as3(gl), k_cmp, v_cmp, as3(ks), as3(vs), as3(kw), as3(vw), overlap_t, place)


ROW_CHUNKS = D_MODEL // V7X_LANES
assert ROW_CHUNKS == V7X_SUBLANES
DMA_UNROLL = 8
PAD_SPANS = N_EXPERTS + 1


def _tile_rows(n_rows):
    return (n_rows * ROW_CHUNKS, V7X_LANES)


def _row_tile(ref, r):
    start = r * ROW_CHUNKS
    if not isinstance(start, int):
        start = pl.multiple_of(start, ROW_CHUNKS)
    return ref.at[pl.ds(start, ROW_CHUNKS), :]


def _store_row_tiles(ref, x):
    for c in range(ROW_CHUNKS):
        ref[pl.ds(c, x.shape[0], stride=ROW_CHUNKS), :] = x[:, c * V7X_LANES:(c + 1) * V7X_LANES]


def _load_row_tiles(ref):
    rows = ref.shape[0] // ROW_CHUNKS
    return jnp.concatenate([ref[pl.ds(c, rows, stride=ROW_CHUNKS), :] for c in range(ROW_CHUNKS)], axis=1)


def _router_body(h_ref, g_ref, wr_ref, br_ref, tri_ref, hn_o, idx_o, gate_o, cnt_o, cnt_scr):
    @pl.when(pl.program_id(0) == 0)
    def _():
        cnt_scr[...] = jnp.zeros_like(cnt_scr)

    hn = _rms(h_ref[...], g_ref[...])
    _store_row_tiles(hn_o, hn)
    logits = _dot_f32(hn, wr_ref[...]) + br_ref[...]
    lane = lax.broadcasted_iota(jnp.int32, logits.shape, 1)
    m1 = jnp.max(logits, axis=-1, keepdims=True)
    i1 = jnp.min(jnp.where(logits == m1, lane, V7X_LANES), axis=-1, keepdims=True)
    rest = jnp.where(lane == i1, NEG_INF, logits)
    m2 = jnp.max(rest, axis=-1, keepdims=True)
    i2 = jnp.min(jnp.where(rest == m2, lane, V7X_LANES), axis=-1, keepdims=True)
    e2 = jnp.exp(m2 - m1)
    den = 1.0 + e2
    hit1, hit2 = lane == i1, lane == i2
    hits = jnp.where(hit1 | hit2, 1.0, 0.0)
    before = jnp.dot(tri_ref[...], hits.astype(BF16), preferred_element_type=F32) + cnt_scr[...]
    r1 = jnp.sum(jnp.where(hit1, before, 0.0), axis=-1, keepdims=True).astype(jnp.int32)
    r2 = jnp.sum(jnp.where(hit2, before, 0.0), axis=-1, keepdims=True).astype(jnp.int32)
    cnt_scr[...] = cnt_scr[...] + jnp.sum(hits, axis=0, keepdims=True)
    cnt_o[...] = cnt_scr[...]
    idx_o[...] = jnp.where(lane == 0, i1, jnp.where(lane == 1, i2, jnp.where(lane == 2, r1,
                           jnp.where(lane == 3, r2, 0))))
    gate_o[...] = jnp.where(lane == 0, 1.0 / den, jnp.where(lane == 1, e2 / den, 0.0))


def _router(h, gain, w_router, b_router, tm=512):
    n = h.shape[0]
    wr = jnp.pad(w_router, ((0, 0), (0, V7X_LANES - N_EXPERTS)))
    br = jnp.pad(b_router.reshape(1, -1), ((0, 0), (0, V7X_LANES - N_EXPERTS)), constant_values=NEG_INF)
    tri = jnp.asarray(np.tril(np.ones((tm, tm), np.float32), k=-1)).astype(BF16)
    return pl.pallas_call(
        _router_body,
        grid=(n // tm,),
        in_specs=[_rows_spec(tm, D_MODEL), _const_spec((1, D_MODEL)), _const_spec(wr.shape),
                  _const_spec(br.shape), _const_spec(tri.shape)],
        out_specs=[pl.BlockSpec(_tile_rows(tm), lambda i: (i, 0)), _rows_spec(tm, V7X_LANES),
                   _rows_spec(tm, V7X_LANES), _const_spec((1, V7X_LANES))],
        out_shape=[jax.ShapeDtypeStruct(_tile_rows(n), F32), jax.ShapeDtypeStruct((n, V7X_LANES), jnp.int32),
                   jax.ShapeDtypeStruct((n, V7X_LANES), F32), jax.ShapeDtypeStruct((1, V7X_LANES), F32)],
        scratch_shapes=[pltpu.VMEM((1, V7X_LANES), F32)],
        compiler_params=_params("arbitrary"),
        name="moe_router",
    )(h, gain.reshape(1, -1), wr, br, tri)


def _row_copy(src, dst, sem):
    return pltpu.make_async_copy(src, dst, sem)


def _dispatch_body(pad_ref, dest_ref, hn_ref, xs_ref, zero_scr, sem):
    @pl.when(pl.program_id(0) == 0)
    def _():
        zero_scr[...] = jnp.zeros_like(zero_scr)
        for e in range(PAD_SPANS):
            start, count = pad_ref[e], pad_ref[PAD_SPANS + e]

            def fill(r, c):
                _row_copy(zero_scr, _row_tile(xs_ref, start + r), sem).start()
                return c

            def filled(r, c):
                _row_copy(zero_scr, _row_tile(xs_ref, 0), sem).wait()
                return c

            lax.fori_loop(0, count, fill, 0)
            lax.fori_loop(0, count, filled, 0)

    def issue(blk, c):
        for u in range(DMA_UNROLL):
            r = blk * DMA_UNROLL + u
            for k in range(TOP_K):
                _row_copy(_row_tile(hn_ref, r), _row_tile(xs_ref, dest_ref[TOP_K * r + k]),
                          sem).start(priority=k)
        return c

    lax.fori_loop(0, DMA_ROWS // DMA_UNROLL, issue, 0)

    def drain(r, c):
        for k in range(TOP_K):
            _row_copy(_row_tile(hn_ref, 0), _row_tile(xs_ref, 0), sem).wait()
        return c

    lax.fori_loop(0, DMA_ROWS, drain, 0)


def _dispatch(pad_info, dest, hn_rows, p_rows):
    n = hn_rows.shape[0] // ROW_CHUNKS
    return pl.pallas_call(
        _dispatch_body,
        grid_spec=pltpu.PrefetchScalarGridSpec(
            num_scalar_prefetch=1,
            grid=(n // DMA_ROWS,),
            in_specs=[pl.BlockSpec((TOP_K * DMA_ROWS,), lambda i, pad: (i,), memory_space=pltpu.SMEM),
                      pl.BlockSpec(_tile_rows(DMA_ROWS), lambda i, pad: (i, 0))],
            out_specs=pl.BlockSpec(memory_space=pl.ANY),
            scratch_shapes=[pltpu.VMEM(_tile_rows(1), F32), pltpu.SemaphoreType.DMA(())]),
        out_shape=jax.ShapeDtypeStruct(_tile_rows(p_rows), F32),
        compiler_params=_params("arbitrary"),
        name="moe_dispatch",
    )(pad_info, dest, hn_rows)


def _experts_body(blk_e_ref, n_used_ref, x_ref, wg_ref, wu_ref, wd_ref, o_ref):
    i = pl.program_id(0)

    @pl.when(i < n_used_ref[0])
    def _():
        x = _load_row_tiles(x_ref).astype(BF16)
        gate = jnp.dot(x, wg_ref[0], preferred_element_type=F32)
        up = jnp.dot(x, wu_ref[0], preferred_element_type=F32)
        act = (gate * _sigmoid(gate)) * up
        _store_row_tiles(o_ref, _dot(act, wd_ref[0]))

    @pl.when(i >= n_used_ref[0])
    def _():
        o_ref[...] = jnp.zeros_like(o_ref)


def _experts(blk_e, n_used, xs, wg, wu, wd):
    p_rows = xs.shape[0] // ROW_CHUNKS
    x_spec = pl.BlockSpec(_tile_rows(MOE_ROWS), lambda i, be, nu: (jnp.minimum(i, nu[0] - 1), 0))
    o_spec = pl.BlockSpec(_tile_rows(MOE_ROWS), lambda i, be, nu: (i, 0))
    w_spec = lambda w: pl.BlockSpec((1,) + w.shape[1:], lambda i, be, nu: (be[i], 0, 0))
    return pl.pallas_call(
        _experts_body,
        grid_spec=pltpu.PrefetchScalarGridSpec(
            num_scalar_prefetch=2,
            grid=(p_rows // MOE_ROWS,),
            in_specs=[x_spec, w_spec(wg), w_spec(wu), w_spec(wd)],
            out_specs=o_spec),
        out_shape=jax.ShapeDtypeStruct(_tile_rows(p_rows), F32),
        compiler_params=_params("arbitrary"),
        name="moe_experts",
    )(blk_e, n_used, xs, wg, wu, wd)


def _combine_body(dest_ref, h_ref, gate_ref, g_ref, yb_ref, o_ref, y_scr, sem):
    tm = h_ref.shape[0]

    def issue(blk, c):
        for u in range(DMA_UNROLL):
            t = blk * DMA_UNROLL + u
            for k in range(TOP_K):
                _row_copy(_row_tile(yb_ref, dest_ref[TOP_K * t + k]), _row_tile(y_scr.at[k], t),
                          sem).start(priority=k)
        return c

    lax.fori_loop(0, tm // DMA_UNROLL, issue, 0)

    def drain(t, c):
        for k in range(TOP_K):
            _row_copy(_row_tile(yb_ref, 0), _row_tile(y_scr.at[0], 0), sem).wait()
        return c

    lax.fori_loop(0, tm, drain, 0)
    gate = gate_ref[...]
    moe = gate[:, 0:1] * _load_row_tiles(y_scr.at[0]) + gate[:, 1:2] * _load_row_tiles(y_scr.at[1])
    o_ref[...] = _rms(h_ref[...] + moe, g_ref[...])


def _combine(dest, h, gates, gain, yb_rows, tm=256):
    n = h.shape[0]
    return pl.pallas_call(
        _combine_body,
        grid=(n // tm,),
        in_specs=[pl.BlockSpec((TOP_K * tm,), lambda i: (i,), memory_space=pltpu.SMEM),
                  _rows_spec(tm, D_MODEL), _rows_spec(tm, V7X_LANES), _const_spec((1, D_MODEL)),
                  pl.BlockSpec(memory_space=pl.ANY)],
        out_specs=_rows_spec(tm, D_MODEL),
        out_shape=jax.ShapeDtypeStruct((n, D_MODEL), F32),
        scratch_shapes=[pltpu.VMEM((TOP_K,) + _tile_rows(tm), F32), pltpu.SemaphoreType.DMA(())],
        compiler_params=_params("arbitrary"),
        name="moe_combine_norm",
    )(dest, h, gates, gain.reshape(1, -1), yb_rows)


def _moe_layout(idx, counts):
    n = idx.shape[0]
    nk = n * TOP_K
    counts = counts[0, :N_EXPERTS].astype(jnp.int32)
    padded = ((counts + MOE_ROWS - 1) // MOE_ROWS) * MOE_ROWS
    p_end = jnp.cumsum(padded)
    p_start = p_end - padded
    experts = jnp.arange(N_EXPERTS, dtype=jnp.int32)[None, None, :]
    first = jnp.sum(jnp.where(idx[:, :TOP_K, None] == experts, p_start[None, None, :], 0), axis=-1)
    dest = (first + idx[:, TOP_K:2 * TOP_K]).reshape(nk)
    n_blk = (nk + MOE_ROWS - 1) // MOE_ROWS + N_EXPERTS
    blk_e = jnp.minimum(jnp.searchsorted(p_end, jnp.arange(n_blk) * MOE_ROWS, side="right"),
                        N_EXPERTS - 1).astype(jnp.int32)
    n_used = (p_end[-1:] // MOE_ROWS).astype(jnp.int32)
    p_rows = n_blk * MOE_ROWS
    pad_info = jnp.concatenate([p_start + counts, p_end[-1:], padded - counts,
                                p_rows - p_end[-1:]]).astype(jnp.int32)
    return dest.astype(jnp.int32), blk_e, n_used, pad_info, p_rows


def _moe_final(h, norm_g, w_router, b_router, wg, wu, wd, final_g):
    hn, idx, gates, counts = _router(h, norm_g, w_router, b_router)
    dest, blk_e, n_used, pad_info, p_rows = _moe_layout(idx, counts)
    xs = _dispatch(pad_info, dest, hn, p_rows)
    yb = _experts(blk_e, n_used, xs, wg.astype(BF16), wu.astype(BF16), wd.astype(BF16))
    return _combine(dest, h, gates, final_g, yb)


def kernel(x, e_norm_mix, e_w_in, sgu_ln_g, sgu_ln_b, sgu_w, sgu_b, rwkv_mu, rwkv_w0, rwkv_w2,
           rwkv_a0, rwkv_a2, rwkv_g2, rwkv_k_k, rwkv_k_a, rwkv_r_k, rwkv_gn_g, rwkv_gn_b, e_w_out,
           e_norm_ffn, ffn_w_gate, ffn_w_up, ffn_w_down, o_norm_mix, o_w_in, nsa_cmp_pos_k,
           nsa_cmp_pos_v, nsa_cmp_k_w1, nsa_cmp_k_w2, nsa_cmp_v_w1, nsa_cmp_v_w2, conv_w, o_w_out,
           o_norm_ffn, moe_router, moe_router_b, moe_w_gate, moe_w_up, moe_w_down, final_norm):
    bsz, seq, d = x.shape
    n = bsz * seq
    h = x.reshape(n, d)

    ya, scan_ops, g = _layer0_front(h, seq, e_norm_mix[0], e_w_in[0].astype(BF16), sgu_ln_g[0], sgu_ln_b[0],
                                    sgu_w[0], sgu_b[0], rwkv_mu[0], rwkv_w0[0], rwkv_w2[0], rwkv_a0[0],
                                    rwkv_a2[0], rwkv_g2[0])
    rwka, v = _to_scan_layout(scan_ops, bsz, seq)
    ys = _rwkv_scan(rwka, v, rwkv_k_k[0], rwkv_k_a[0], rwkv_r_k[0], rwkv_gn_g[0], rwkv_gn_b[0], bsz, seq)
    h = _layer0_back(h, ya, _from_scan_layout(ys, bsz, seq), g, e_w_out[0].astype(BF16), e_norm_ffn[0],
                     ffn_w_gate[0].astype(BF16), ffn_w_up[0].astype(BF16), ffn_w_down[0].astype(BF16))

    qn, qr, kcvc, ks, vs, kw, vw, gl, bcd = _in_proj1(h, o_norm_mix[0], o_w_in[0], seq)
    k_cmp, v_cmp = _compress(kcvc, nsa_cmp_pos_k[0], nsa_cmp_pos_v[0], nsa_cmp_k_w1[0], nsa_cmp_v_w1[0],
                             nsa_cmp_k_w2[0], nsa_cmp_v_w2[0], bsz, seq)
    yc = _nsa(qn, qr, gl, k_cmp, v_cmp, ks, vs, kw, vw, bsz, seq).reshape(n, C_WIDTH)
    h = _out_proj1(h, yc, bcd, conv_w[0], o_w_out[0].astype(BF16), seq)
    out = _moe_final(h, o_norm_ffn[0], moe_router[0], moe_router_b[0], moe_w_gate[0], moe_w_up[0],
                     moe_w_down[0], final_norm)
    return out.reshape(bsz, seq, d)
```

```python
import functools

import jax
import jax.numpy as jnp
import numpy as np
from jax import lax
from jax.experimental import pallas as pl
from jax.experimental.pallas import tpu as pltpu

F32 = jnp.float32
BF16 = jnp.bfloat16

D_MODEL = 1024
A_GROUPS = 4
A_GROUP_DIM = 128
A_WIDTH = A_GROUPS * A_GROUP_DIM
CHUNK = 128
SGU_LN_EPS = 1e-5
B_HEADS = 8
B_HEAD_DIM = 64
B_WIDTH = B_HEADS * B_HEAD_DIM
DECAY_LORA = 64
ICLR_LORA = 64
GATE_LORA = 128
B_IN = 3 * B_WIDTH + DECAY_LORA + ICLR_LORA + GATE_LORA
RWKV_GN_EPS = 64e-5
C_HEADS = 8
C_KV_HEADS = 2
C_GROUP = C_HEADS // C_KV_HEADS
C_HEAD_DIM = 64
C_WIDTH = C_HEADS * C_HEAD_DIM
KV_WIDTH = C_KV_HEADS * C_HEAD_DIM
N_BRANCH = 3
CMP_LEN = 32
CMP_STRIDE = 16
CMP_HIDDEN = 256
SLC_BLK = 64
SEL_TOPK = 8
WIN = 512
NSA_TQ = 256
ROT_DIM = C_HEAD_DIM // 4
ROPE_THETA = 500000.0
D_WIDTH = 512
CONV_W = 3
FFN_DIM = 2816
N_EXPERTS = 8
TOP_K = 2
EXPERT_DIM = 1408
NORM_EPS = 1e-6
NEG_INF = -1e30
FORCE_BONUS = 1e6

V7X_LANES = 128
V7X_SUBLANES = 8
V7X_VMEM_LIMIT = 56 * 1024 * 1024

MOE_ROWS = 512


def _params(*sem):
    return pltpu.CompilerParams(dimension_semantics=sem, vmem_limit_bytes=V7X_VMEM_LIMIT)


def _const_spec(shape):
    zeros = (0,) * len(shape)
    return pl.BlockSpec(shape, lambda *_: zeros)


def _rows_spec(tm, width):
    return pl.BlockSpec((tm, width), lambda i: (i, 0))


def _rms(x, g):
    return x * lax.rsqrt(jnp.mean(x * x, axis=-1, keepdims=True) + NORM_EPS) * g


def _gelu(x):
    return x * (0.5 * (1.0 + jnp.tanh(0.7978845608028654 * (x + 0.044715 * (x * x * x)))))


def _sigmoid(x):
    return 1.0 / (1.0 + jnp.exp(-x))


def _dot(a, b):
    return jnp.dot(a.astype(BF16), b.astype(BF16), preferred_element_type=F32)


def _dot_nt(a, b):
    return lax.dot_general(a.astype(BF16), b.astype(BF16), (((1,), (1,)), ((), ())),
                           preferred_element_type=F32)


def _split_bf16(a):
    hi = a.astype(BF16)
    lo = (a - hi.astype(F32)).astype(BF16)
    return hi, lo


def _dot_f32(a, b):
    ah, al = _split_bf16(a)
    bh, bl = _split_bf16(b)
    d = functools.partial(jnp.dot, preferred_element_type=F32)
    return d(ah, bh) + (d(al, bh) + d(ah, bl))


def _masked_softmax(s, mask):
    s = jnp.where(mask, s, NEG_INF)
    m = jnp.max(s, axis=-1, keepdims=True)
    p = jnp.where(mask, jnp.exp(s - m), 0.0)
    return p / jnp.maximum(jnp.sum(p, axis=-1, keepdims=True), 1e-30)


def _sgu_chunk(p_uv, lng_ref, lnb_ref, w_ref, b_ref):
    row = lax.broadcasted_iota(jnp.int32, (CHUNK, CHUNK), 0)
    col = lax.broadcasted_iota(jnp.int32, (CHUNK, CHUNK), 1)
    causal = col <= row
    u = _gelu(p_uv[:, :A_WIDTH])
    v = _gelu(p_uv[:, A_WIDTH:])
    outs = []
    for g in range(A_GROUPS):
        cols = slice(g * A_GROUP_DIM, (g + 1) * A_GROUP_DIM)
        vg = v[:, cols]
        mu = jnp.mean(vg, axis=-1, keepdims=True)
        dv = vg - mu
        var = jnp.mean(dv * dv, axis=-1, keepdims=True)
        vn = dv * lax.rsqrt(var + SGU_LN_EPS) * lng_ref[:, cols] + lnb_ref[:, cols]
        wm = jnp.where(causal, w_ref[g], 0.0)
        mixed = _dot(wm, vn) + b_ref[:, g:g + 1]
        outs.append(u[:, cols] * mixed)
    return jnp.concatenate(outs, axis=1)


def _softplus(x):
    return jnp.maximum(x, 0.0) + jnp.log(1.0 + jnp.exp(-jnp.abs(x)))


SCAN_OPERANDS = 5


def _layer0_front_body(x_ref, gain_ref, win_ref, lng_ref, lnb_ref, sw_ref, sb_ref, mu_ref, w0_ref, w2_ref,
                       a0_ref, a2_ref, g2_ref, ya_o, s_o, g_o, prev_scr, *, tiles_per_seq):
    tm = x_ref.shape[0]

    @pl.when(pl.program_id(0) == 0)
    def _():
        prev_scr[...] = jnp.zeros_like(prev_scr)

    p = jnp.dot(_rms(x_ref[...], gain_ref[...]).astype(BF16), win_ref[...], preferred_element_type=F32)
    for c in range(tm // CHUNK):
        rows = slice(c * CHUNK, (c + 1) * CHUNK)
        ya_o[rows, :] = _sgu_chunk(p[rows, :2 * A_WIDTH], lng_ref, lnb_ref, sw_ref, sb_ref).astype(ya_o.dtype)
    x = p[:, 2 * A_WIDTH:]
    first = (pl.program_id(0) % tiles_per_seq) == 0
    prev_row = jnp.where(first, 0.0, prev_scr[...])
    rowid = lax.broadcasted_iota(jnp.int32, (tm, 1), 0)
    shifted = jnp.where(rowid == 0, prev_row, pltpu.roll(x, 1, axis=0))
    prev_scr[...] = x[tm - 1:tm, :]
    xm = x + (shifted - x) * mu_ref[...]
    o = 3 * B_WIDTH
    wl = xm[:, o:o + DECAY_LORA]
    al = xm[:, o + DECAY_LORA:o + DECAY_LORA + ICLR_LORA]
    gl = xm[:, o + DECAY_LORA + ICLR_LORA:]
    w = -_softplus(-(w0_ref[...] + _dot(jnp.tanh(wl), w2_ref[...]))) - 0.5
    s_o[0] = xm[:, :B_WIDTH]
    s_o[1] = jnp.exp(-jnp.exp(w))
    s_o[2] = xm[:, B_WIDTH:2 * B_WIDTH]
    s_o[3] = _sigmoid(a0_ref[...] + _dot(al, a2_ref[...]))
    s_o[4] = xm[:, 2 * B_WIDTH:3 * B_WIDTH]
    g_o[...] = _dot(_sigmoid(gl), g2_ref[...])


def _layer0_front(h, seq, gain, w_in, ln_g, ln_b, w_s, b_s, mu, w0, w2, a0, a2, g2, tm=256):
    n, d = h.shape
    perm = np.arange(B_WIDTH).reshape(B_HEADS, B_HEAD_DIM).T.reshape(-1)
    r0, k0 = 2 * A_WIDTH, 2 * A_WIDTH + B_WIDTH
    cols = np.concatenate([np.arange(r0), r0 + perm, k0 + perm, np.arange(k0 + B_WIDTH, w_in.shape[1])])
    w_in = w_in[:, cols]
    mu = mu[cols[r0:] - r0]
    w0, w2, a0, a2 = w0[perm], w2[:, perm], a0[perm], a2[:, perm]
    row = lambda v: v.reshape(1, -1)
    consts = [row(gain), w_in, row(ln_g), row(ln_b), w_s, b_s.T, row(mu), row(w0), w2, row(a0), a2, g2]
    outs = [jax.ShapeDtypeStruct((n, A_WIDTH), BF16), jax.ShapeDtypeStruct((SCAN_OPERANDS, n, B_WIDTH), F32),
            jax.ShapeDtypeStruct((n, B_WIDTH), F32)]
    return pl.pallas_call(
        functools.partial(_layer0_front_body, tiles_per_seq=seq // tm),
        grid=(n // tm,),
        in_specs=[_rows_spec(tm, d)] + [_const_spec(c.shape) for c in consts],
        out_specs=[_rows_spec(tm, A_WIDTH), pl.BlockSpec((SCAN_OPERANDS, tm, B_WIDTH), lambda i: (0, i, 0)),
                   _rows_spec(tm, B_WIDTH)],
        out_shape=outs,
        scratch_shapes=[pltpu.VMEM((1, B_IN), F32)],
        compiler_params=_params("arbitrary"),
        name="layer0_front",
    )(h, *consts)


def _rwkv_scan_body(r_ref, w_ref, k0_ref, a_ref, v_ref, kkp_ref, kap_ref, rk_ref, gng_ref, gnb_ref,
                    y_ref, s_ref, kkn_ref, ka_ref, km_ref):
    n, tt = r_ref.shape[0], r_ref.shape[1]

    @pl.when(pl.program_id(0) == 0)
    def _():
        s_ref[...] = jnp.zeros_like(s_ref)

    per_dim = lambda p_ref: p_ref[...][:, None, :]
    k0 = k0_ref[...]
    a = a_ref[...]
    kk = k0 * per_dim(kkp_ref)
    kkn = kk / jnp.maximum(jnp.sqrt(jnp.sum(kk * kk, axis=0, keepdims=True)), 1e-12)
    km = k0 * (1.0 + (a - 1.0) * per_dim(kap_ref))
    kkn_ref[...] = kkn
    ka_ref[...] = kkn * a
    km_ref[...] = km

    zero = jnp.zeros((n, r_ref.shape[2]), F32)

    def sa_init(j, acc):
        return acc + s_ref[j] * kkn_ref[j, pl.ds(0, 1), :]

    sa0 = lax.fori_loop(0, n, sa_init, zero)

    def step(t, sa):
        v_t = v_ref[t]
        now = pl.ds(t, 1)
        nxt = pl.ds(jnp.minimum(t + 1, tt - 1), 1)

        def jbody(j, carry):
            y, san = carry
            sn = s_ref[j] * w_ref[j, now, :] + (v_t * km_ref[j, now, :] - sa * ka_ref[j, now, :])
            s_ref[j] = sn
            return y + sn * r_ref[j, now, :], san + sn * kkn_ref[j, nxt, :]

        y, san = lax.fori_loop(0, n, jbody, (zero, zero), unroll=8)
        y_ref[t] = y
        return san

    lax.fori_loop(0, tt, step, sa0)

    y = y_ref[...]
    ym = jnp.mean(y, axis=1, keepdims=True)
    dy = y - ym
    yv = jnp.mean(dy * dy, axis=1, keepdims=True)
    yn = dy * lax.rsqrt(yv + RWKV_GN_EPS) * gng_ref[...][None] + gnb_ref[...][None]
    bonus = jnp.sum(r_ref[...] * km * per_dim(rk_ref), axis=0)
    y_ref[...] = yn + bonus[:, None, :] * v_ref[...]


def _rwkv_scan(rwka, v, k_k, k_a, r_k, gn_g, gn_b, bsz, seq, tt=32):
    n = B_HEAD_DIM
    lanes = bsz * B_HEADS

    def lane_param(p):
        return jnp.tile(p.reshape(B_HEADS, n).T, (1, bsz))

    op_spec = lambda a: pl.BlockSpec((None, n, tt, lanes), lambda c: (a, 0, c, 0))
    slab_spec = pl.BlockSpec((tt, n, lanes), lambda c: (c, 0, 0))
    par_spec = _const_spec((n, lanes))
    return pl.pallas_call(
        _rwkv_scan_body,
        grid=(seq // tt,),
        in_specs=[op_spec(a) for a in range(SCAN_OPERANDS - 1)] + [slab_spec] + [par_spec] * 5,
        out_specs=slab_spec,
        out_shape=jax.ShapeDtypeStruct((seq, n, lanes), F32),
        scratch_shapes=[pltpu.VMEM((n, n, lanes), F32)] + [pltpu.VMEM((n, tt, lanes), F32)] * 3,
        compiler_params=_params("arbitrary"),
        name="rwkv_scan",
    )(*([rwka] * (SCAN_OPERANDS - 1)), v, lane_param(k_k), lane_param(k_a), lane_param(r_k.reshape(-1)),
      lane_param(gn_g), lane_param(gn_b))


SCAN_TT = 128
SCAN_UNROLL = 8


def _time_to_lanes(x_ref, a_ref):
    for b in range(x_ref.shape[0]):
        for blk in range(B_WIDTH // V7X_LANES):
            cols = slice(blk * V7X_LANES, (blk + 1) * V7X_LANES)
            a_ref[b, cols, :] = x_ref[b, :, cols].T


def _to_scan_rows_body(x_ref, o_ref, a_ref):
    bsz = x_ref.shape[0]
    _time_to_lanes(x_ref, a_ref)

    def dims(i, carry):
        for u in range(SCAN_UNROLL):
            d = i * SCAN_UNROLL + u
            rows = pl.ds(pl.multiple_of(d * B_HEADS, B_HEADS), B_HEADS)
            z = jnp.concatenate([a_ref[b, rows, :] for b in range(bsz)], axis=0)
            o_ref[d] = z.T
        return carry

    lax.fori_loop(0, B_HEAD_DIM // SCAN_UNROLL, dims, 0)


def _to_scan_slab_body(x_ref, o_ref, a_ref):
    bsz = x_ref.shape[0]
    _time_to_lanes(x_ref, a_ref)

    def dims(i, carry):
        for u in range(SCAN_UNROLL):
            d = i * SCAN_UNROLL + u
            z = jnp.concatenate([a_ref[b, pl.ds(d, B_HEADS, stride=B_HEAD_DIM), :] for b in range(bsz)],
                                axis=0)
            o_ref[:, d, :] = z.T
        return carry

    lax.fori_loop(0, B_HEAD_DIM // SCAN_UNROLL, dims, 0)


def _to_scan_layout(ops, bsz, seq):
    tt = min(SCAN_TT, seq)
    lanes = bsz * B_HEADS
    ops4 = ops.reshape(SCAN_OPERANDS, bsz, seq, B_WIDTH)
    scratch = [pltpu.VMEM((bsz, B_WIDTH, tt), F32)]
    rwka = pl.pallas_call(
        _to_scan_rows_body,
        grid=(SCAN_OPERANDS - 1, seq // tt),
        in_specs=[pl.BlockSpec((None, bsz, tt, B_WIDTH), lambda a, i: (a, 0, i, 0))],
        out_specs=pl.BlockSpec((None, B_HEAD_DIM, tt, lanes), lambda a, i: (a, 0, i, 0)),
        out_shape=jax.ShapeDtypeStruct((SCAN_OPERANDS - 1, B_HEAD_DIM, seq, lanes), F32),
        scratch_shapes=scratch,
        compiler_params=_params("parallel", "arbitrary"),
        name="to_scan_rows",
    )(ops4)
    v = pl.pallas_call(
        _to_scan_slab_body,
        grid=(seq // tt,),
        in_specs=[pl.BlockSpec((None, bsz, tt, B_WIDTH), lambda i: (SCAN_OPERANDS - 1, 0, i, 0))],
        out_specs=pl.BlockSpec((tt, B_HEAD_DIM, lanes), lambda i: (i, 0, 0)),
        out_shape=jax.ShapeDtypeStruct((seq, B_HEAD_DIM, lanes), F32),
        scratch_shapes=scratch,
        compiler_params=_params("arbitrary"),
        name="to_scan_slab",
    )(ops4)
    return rwka, v


def _from_scan_body(y_ref, o_ref, a_ref):
    bsz = o_ref.shape[0]

    def dims(i, carry):
        for u in range(SCAN_UNROLL):
            d = i * SCAN_UNROLL + u
            zt = y_ref[:, d, :].T
            for b in range(bsz):
                a_ref[b, pl.ds(d, B_HEADS, stride=B_HEAD_DIM), :] = zt[b * B_HEADS:(b + 1) * B_HEADS, :]
        return carry

    lax.fori_loop(0, B_HEAD_DIM // SCAN_UNROLL, dims, 0)
    for b in range(bsz):
        for blk in range(B_WIDTH // V7X_LANES):
            cols = slice(blk * V7X_LANES, (blk + 1) * V7X_LANES)
            o_ref[b, :, cols] = a_ref[b, cols, :].T


def _from_scan_layout(y, bsz, seq):
    tt = min(SCAN_TT, seq)
    return pl.pallas_call(
        _from_scan_body,
        grid=(seq // tt,),
        in_specs=[pl.BlockSpec((tt, B_HEAD_DIM, bsz * B_HEADS), lambda i: (i, 0, 0))],
        out_specs=pl.BlockSpec((bsz, tt, B_WIDTH), lambda i: (0, i, 0)),
        out_shape=jax.ShapeDtypeStruct((bsz, seq, B_WIDTH), F32),
        scratch_shapes=[pltpu.VMEM((bsz, B_WIDTH, tt), F32)],
        compiler_params=_params("arbitrary"),
        name="from_scan_layout",
    )(y).reshape(bsz * seq, B_WIDTH)


def _out_proj1_body(h_ref, yc_ref, bcd_ref, prev_ref, cw_ref, w_ref, o_ref, *, tiles_per_seq):
    tm = h_ref.shape[0]
    first = (pl.program_id(0) % tiles_per_seq) == 0
    z = bcd_ref[:, D_WIDTH:2 * D_WIDTH] * bcd_ref[:, 2 * D_WIDTH:]
    zp = jnp.where(first, 0.0, prev_ref[:, D_WIDTH:2 * D_WIDTH] * prev_ref[:, 2 * D_WIDTH:])
    rowid = lax.broadcasted_iota(jnp.int32, (tm, 1), 0)
    z1 = jnp.where(rowid == 0, zp[7:8, :], pltpu.roll(z, 1, axis=0))
    z2 = pltpu.roll(z, 2, axis=0)
    z2 = jnp.where(rowid == 0, zp[6:7, :], jnp.where(rowid == 1, zp[7:8, :], z2))
    y = cw_ref[0:1, :] * z2 + cw_ref[1:2, :] * z1 + cw_ref[2:3, :] * z
    yd = bcd_ref[:, :D_WIDTH] * y
    o_ref[...] = h_ref[...] + (_dot(yc_ref[...], w_ref[:C_WIDTH, :]) + _dot(yd, w_ref[C_WIDTH:, :]))


def _out_proj1(h, yc, bcd, conv_w, w_out, seq, tm=512):
    n = h.shape[0]
    per8 = tm // V7X_SUBLANES
    return pl.pallas_call(
        functools.partial(_out_proj1_body, tiles_per_seq=seq // tm),
        grid=(n // tm,),
        in_specs=[_rows_spec(tm, D_MODEL), _rows_spec(tm, C_WIDTH), _rows_spec(tm, 3 * D_WIDTH),
                  pl.BlockSpec((V7X_SUBLANES, 3 * D_WIDTH), lambda i: (jnp.maximum(i * per8 - 1, 0), 0)),
                  _const_spec((CONV_W, D_WIDTH)), _const_spec(w_out.shape)],
        out_specs=_rows_spec(tm, D_MODEL),
        out_shape=jax.ShapeDtypeStruct((n, D_MODEL), F32),
        compiler_params=_params("parallel"),
        name="out_proj1",
    )(h, yc, bcd, bcd, conv_w, w_out)


def _layer0_back_body(h_ref, ya_ref, ys_ref, gm_ref, wo_ref, g_ref, wg_ref, wu_ref, wd_ref, o_ref):
    yb = ys_ref[...] * gm_ref[...]
    h = h_ref[...] + (_dot(ya_ref[...], wo_ref[:A_WIDTH, :]) + _dot(yb, wo_ref[A_WIDTH:, :]))
    hn = _rms(h, g_ref[...]).astype(BF16)
    gate = jnp.dot(hn, wg_ref[...], preferred_element_type=F32)
    up = jnp.dot(hn, wu_ref[...], preferred_element_type=F32)
    act = (gate * _sigmoid(gate)) * up
    o_ref[...] = h + _dot(act, wd_ref[...])


def _layer0_back(h, ya, ys, gm, w_out, gain, wg, wu, wd, tm=256):
    n = h.shape[0]
    return pl.pallas_call(
        _layer0_back_body,
        grid=(n // tm,),
        in_specs=[_rows_spec(tm, D_MODEL), _rows_spec(tm, A_WIDTH), _rows_spec(tm, B_WIDTH),
                  _rows_spec(tm, B_WIDTH), _const_spec(w_out.shape), _const_spec((1, D_MODEL)),
                  _const_spec(wg.shape), _const_spec(wu.shape), _const_spec(wd.shape)],
        out_specs=_rows_spec(tm, D_MODEL),
        out_shape=jax.ShapeDtypeStruct((n, D_MODEL), F32),
        compiler_params=_params("parallel"),
        name="layer0_back",
    )(h, ya, ys, gm, w_out, gain.reshape(1, -1), wg, wu, wd)


def _rope(x, c, s_up, s_dn):
    half = ROT_DIM // 2
    return x * c + pltpu.roll(x, half, axis=1) * s_up + pltpu.roll(x, V7X_LANES - half, axis=1) * s_dn


def _lane_blocks(x):
    return [x[:, i * V7X_LANES:(i + 1) * V7X_LANES] for i in range(x.shape[1] // V7X_LANES)]


def _in_proj1_body(x_ref, g_ref, c_ref, su_ref, sd_ref, kb_ref, one_ref, wq_ref, wkc_ref, wkv_ref,
                   wgl_ref, wbcd_ref, qn_o, qr_o, kc_o, ks_o, vs_o, kw_o, vw_o, gl_o, bcd_o):
    xn = _rms(x_ref[...], g_ref[...]).astype(BF16)
    d = functools.partial(jnp.dot, preferred_element_type=F32)
    c, su, sd = c_ref[...], su_ref[...], sd_ref[...]
    rope = lambda z: _rope(z, c, su, sd)
    q = d(xn, wq_ref[...]) * (C_HEAD_DIM ** -0.5)
    qn_o[...] = q.astype(BF16)
    qr_o[...] = jnp.concatenate([rope(z) for z in _lane_blocks(q)], axis=1).astype(BF16)
    kc_o[...] = d(xn, wkc_ref[...])
    kv = _lane_blocks(d(xn, wkv_ref[...]))
    hk = C_KV_HEADS
    ks_o[...] = jnp.concatenate([rope(z) + kb_ref[...] for z in kv[:hk]], axis=1).astype(BF16)
    vs_o[...] = jnp.concatenate([z + one_ref[...] for z in kv[hk:2 * hk]], axis=1).astype(BF16)
    kw_o[...] = jnp.concatenate([rope(z) for z in kv[2 * hk:3 * hk]], axis=1).astype(BF16)
    vw_o[...] = jnp.concatenate([z + one_ref[...] for z in kv[3 * hk:]], axis=1).astype(BF16)
    gl_o[...] = d(xn, wgl_ref[...])
    bcd_o[...] = d(xn, wbcd_ref[...])


def _head_tables(seq):
    half = ROT_DIM // 2
    pos = jnp.arange(seq, dtype=F32)
    inv_freq = ROPE_THETA ** (-jnp.arange(0, ROT_DIM, 2, dtype=F32) / ROT_DIM)
    ang = pos[:, None] * inv_freq[None, :]
    cos, sin = jnp.cos(ang), jnp.sin(ang)
    pad = jnp.zeros((seq, V7X_LANES - ROT_DIM), F32)
    zeros = jnp.zeros((seq, half), F32)
    c = jnp.concatenate([cos, cos, pad + 1.0], axis=1)
    s_up = jnp.concatenate([zeros, sin, pad], axis=1)
    s_dn = jnp.concatenate([-sin, zeros, pad], axis=1)
    lane = jnp.arange(V7X_LANES)[None, :]
    blk = (jnp.arange(seq) // SLC_BLK)[:, None]
    k_bias = jnp.where(lane == C_HEAD_DIM + blk, NEG_INF, 0.0).astype(F32)
    ones = (lane == C_HEAD_DIM).astype(F32)
    return c, s_up, s_dn, k_bias, ones


def _pad_heads(w, n_heads):
    width = w.shape[1] // n_heads
    w = w.reshape(w.shape[0], n_heads, width)
    return jnp.pad(w, ((0, 0), (0, 0), (0, V7X_LANES - width))).reshape(w.shape[0], n_heads * V7X_LANES)


def _in_proj1(x, gain, w_in, seq, tm=256):
    n, d = x.shape
    o = np.cumsum([0, C_WIDTH] + [KV_WIDTH] * 6 + [C_HEADS * N_BRANCH] + [D_WIDTH] * 3)
    wq = _pad_heads(w_in[:, o[0]:o[1]], C_HEADS).astype(BF16)
    wkc = w_in[:, o[1]:o[3]].astype(BF16)
    wkv = _pad_heads(w_in[:, o[3]:o[7]], 4 * C_KV_HEADS).astype(BF16)
    wgl = _pad_heads(w_in[:, o[7]:o[8]], C_KV_HEADS).astype(BF16)
    wbcd = w_in[:, o[8]:o[11]].astype(BF16)
    ws = [wq, wkc, wkv, wgl, wbcd]
    kvw = C_KV_HEADS * V7X_LANES
    widths = [C_HEADS * V7X_LANES] * 2 + [2 * KV_WIDTH] + [kvw] * 4 + [kvw, 3 * D_WIDTH]
    dts = [BF16, BF16, F32, BF16, BF16, BF16, BF16, F32, F32]
    tps = seq // tm
    tab_spec = pl.BlockSpec((tm, V7X_LANES), lambda i: (i % tps, 0))
    c, s_up, s_dn, k_bias, ones = _head_tables(seq)
    return pl.pallas_call(
        _in_proj1_body,
        grid=(n // tm,),
        in_specs=[_rows_spec(tm, d), _const_spec((1, d)), tab_spec, tab_spec, tab_spec, tab_spec,
                  _const_spec((1, V7X_LANES))] + [_const_spec(w.shape) for w in ws],
        out_specs=[_rows_spec(tm, wd) for wd in widths],
        out_shape=[jax.ShapeDtypeStruct((n, wd), dt) for wd, dt in zip(widths, dts)],
        compiler_params=_params("parallel"),
        name="in_proj1",
    )(x, gain.reshape(1, d), c, s_up, s_dn, k_bias, ones, *ws)


def _compress_body(z_ref, pk_ref, pv_ref, w1k_ref, w1v_ref, w2k_ref, w2v_ref, ko_ref, vo_ref, z_scr):
    half = CMP_STRIDE * C_HEAD_DIM
    n_rows = z_ref.shape[1] // CMP_STRIDE
    streams = ((pk_ref, w1k_ref, w2k_ref, ko_ref), (pv_ref, w1v_ref, w2v_ref, vo_ref))
    for s, (p_ref, w1_ref, w2_ref, o_ref) in enumerate(streams):
        z_scr[...] = z_ref[0, :, s * KV_WIDTH:(s + 1) * KV_WIDTH]
        every16 = [z_scr[pl.ds(l, n_rows, stride=CMP_STRIDE), :] for l in range(CMP_STRIDE)]
        for hk in range(C_KV_HEADS):
            cols = slice(hk * C_HEAD_DIM, (hk + 1) * C_HEAD_DIM)
            r = jnp.concatenate([z[:, cols] for z in every16], axis=1)
            lo = _dot(r + p_ref[:, :half], w1_ref[:half, :])
            hi = _dot(r + p_ref[:, half:], w1_ref[half:, :])
            hidden = _gelu(lo + pltpu.roll(hi, n_rows - 1, axis=0))
            o_ref[0, hk] = _dot(hidden, w2_ref[...]).astype(o_ref.dtype)


def _compress(kcvc, pos_k, pos_v, w1k, w1v, w2k, w2v, bsz, seq):
    n_rows = seq // CMP_STRIDE
    width = kcvc.shape[1]
    consts = [pos_k.reshape(1, -1), pos_v.reshape(1, -1), w1k.astype(BF16), w1v.astype(BF16),
              _pad_heads(w2k, 1).astype(BF16), _pad_heads(w2v, 1).astype(BF16)]
    o_spec = pl.BlockSpec((1, C_KV_HEADS, n_rows, V7X_LANES), lambda b: (b, 0, 0, 0))
    o_shape = jax.ShapeDtypeStruct((bsz, C_KV_HEADS, n_rows, V7X_LANES), BF16)
    return pl.pallas_call(
        _compress_body,
        grid=(bsz,),
        in_specs=[pl.BlockSpec((1, seq, width), lambda b: (b, 0, 0))] + [_const_spec(c.shape) for c in consts],
        out_specs=[o_spec, o_spec],
        out_shape=[o_shape, o_shape],
        scratch_shapes=[pltpu.VMEM((seq, KV_WIDTH), F32)],
        compiler_params=_params("parallel"),
        name="nsa_compress",
    )(kcvc.reshape(bsz, seq, width), *consts)


SEL_CHUNK = 512


def _stack_heads(blk):
    return jnp.concatenate(_lane_blocks(blk), axis=0)


def _rows4(x):
    return jnp.concatenate([x] * C_GROUP, axis=0)


def _nsa_body(qn_ref, qr_ref, gl_ref, kc_ref, vc_ref, ks_ref, vs_ref, kw_ref, vw_ref, ovl_ref, place_ref,
              o_ref):
    n_cmp = kc_ref.shape[2]
    n_slc = ovl_ref.shape[0]
    hd = C_HEAD_DIM
    qb = pl.program_id(2)
    s0 = qb * NSA_TQ
    t_pos = s0 + lax.broadcasted_iota(jnp.int32, (NSA_TQ, 1), 0)
    qn4 = _stack_heads(qn_ref[0])
    qr4 = _stack_heads(qr_ref[0])

    cmp_end = lax.broadcasted_iota(jnp.int32, (1, n_cmp), 1) * CMP_STRIDE + (CMP_LEN - 1)
    pc = _masked_softmax(_dot_nt(qn4, kc_ref[0, 0]), _rows4(cmp_end <= t_pos))
    o_c = _dot(pc, vc_ref[0, 0])
    pc_sum = (pc[:NSA_TQ] + pc[NSA_TQ:2 * NSA_TQ]) + (pc[2 * NSA_TQ:3 * NSA_TQ] + pc[3 * NSA_TQ:])

    p_hi, p_lo = _split_bf16(pc_sum)
    ovl = ovl_ref[...]
    imp = _dot_nt(ovl, p_hi) + _dot_nt(ovl, p_lo)
    jb = lax.broadcasted_iota(jnp.int32, (n_slc, 1), 0)
    t_row = s0 + lax.broadcasted_iota(jnp.int32, (1, NSA_TQ), 1)
    cur = t_row // SLC_BLK
    forced = (jb == 0) | (jb == cur) | (jb == cur - 1)
    imp = jnp.where(jb * SLC_BLK <= t_row, imp + jnp.where(forced, FORCE_BONUS, 0.0), NEG_INF)
    rank = jnp.zeros((n_slc, NSA_TQ), F32)
    for k in range(n_slc):
        ck = imp[k:k + 1, :]
        beats = (ck > imp) | ((ck == imp) & (jb > k))
        rank = rank + jnp.where(beats, 1.0, 0.0)
    not_sel = jnp.where(rank < float(min(SEL_TOPK, n_slc)), 0.0, 1.0).astype(BF16)
    not_sel_q = lax.dot_general(not_sel, place_ref[...], (((0,), (0,)), ((), ())),
                                preferred_element_type=F32)
    q_sel = qr4 + _rows4(not_sel_q.astype(BF16))

    def sel_chunk(c, m, acc, causal):
        k0 = pl.multiple_of(c * SEL_CHUNK, SEL_CHUNK)
        s = _dot_nt(q_sel, ks_ref[0, pl.ds(k0, SEL_CHUNK), :])
        if causal:
            key = k0 + lax.broadcasted_iota(jnp.int32, (1, SEL_CHUNK), 1)
            s = s + _rows4(jnp.where(key <= t_pos, 0.0, NEG_INF))
        m_new = jnp.maximum(m, jnp.max(s, axis=-1, keepdims=True))
        p = jnp.exp(s - m_new)
        acc = jnp.exp(m - m_new) * acc + _dot(p, vs_ref[0, pl.ds(k0, SEL_CHUNK), :])
        return m_new, acc

    rows = C_GROUP * NSA_TQ
    last = qb // (SEL_CHUNK // NSA_TQ)
    m, acc = lax.fori_loop(0, last, lambda c, ma: sel_chunk(c, ma[0], ma[1], False),
                           (jnp.full((rows, 1), NEG_INF, F32), jnp.zeros((rows, V7X_LANES), F32)))
    _, acc = sel_chunk(last, m, acc, True)
    o_s = acc[:, :hd] / jnp.maximum(acc[:, hd:hd + 1], 1e-30)

    band = NSA_TQ + WIN
    w0 = pl.multiple_of(jnp.maximum(s0 - WIN, 0), NSA_TQ)
    diff = t_pos - (w0 + lax.broadcasted_iota(jnp.int32, (1, band), 1))
    s = _dot_nt(qr4, kw_ref[0, pl.ds(w0, band), :]) + _rows4(
        jnp.where((diff >= 0) & (diff < WIN), 0.0, NEG_INF))
    p = jnp.exp(s - jnp.max(s, axis=-1, keepdims=True))
    acc = _dot(p, vw_ref[0, pl.ds(w0, band), :])
    o_w = acc[:, :hd] / jnp.maximum(acc[:, hd:hd + 1], 1e-30)

    gate = _sigmoid(gl_ref[0])
    outs = []
    for g in range(C_GROUP):
        r = slice(g * NSA_TQ, (g + 1) * NSA_TQ)
        gc = g * N_BRANCH
        outs.append(gate[:, gc:gc + 1] * o_c[r, :hd] + gate[:, gc + 1:gc + 2] * o_s[r]
                    + gate[:, gc + 2:gc + 3] * o_w[r])
    o_ref[0] = jnp.concatenate(outs, axis=1).astype(o_ref.dtype)


def _nsa(qn, qr, gl, k_cmp, v_cmp, ks, vs, kw, vw, bsz, seq):
    n_cmp = seq // CMP_STRIDE
    n_slc = seq // SLC_BLK
    assert C_HEAD_DIM + n_slc <= V7X_LANES and seq % SEL_CHUNK == 0
    ci = np.arange(n_cmp)[None, :] * CMP_STRIDE
    sj = np.arange(n_slc)[:, None] * SLC_BLK
    overlap_t = jnp.asarray(((ci < sj + SLC_BLK) & (ci + CMP_LEN > sj)).astype(np.float32)).astype(BF16)
    place = jnp.asarray(np.eye(n_slc, V7X_LANES, k=C_HEAD_DIM, dtype=np.float32)).astype(BF16)
    gw = C_GROUP * V7X_LANES
    q_spec = pl.BlockSpec((1, NSA_TQ, gw), lambda b, h, i: (b, i, h))
    gl_spec = pl.BlockSpec((1, NSA_TQ, V7X_LANES), lambda b, h, i: (b, i, h))
    cmp_spec = pl.BlockSpec((1, 1, n_cmp, V7X_LANES), lambda b, h, i: (b, h, 0, 0))
    kv_spec = pl.BlockSpec((1, seq, V7X_LANES), lambda b, h, i: (b, 0, h))
    as3 = lambda z: z.reshape(bsz, seq, z.shape[-1])
    return pl.pallas_call(
        _nsa_body,
        grid=(bsz, C_KV_HEADS, seq // NSA_TQ),
        in_specs=[q_spec, q_spec, gl_spec, cmp_spec, cmp_spec, kv_spec, kv_spec, kv_spec, kv_spec,
                  _const_spec((n_slc, n_cmp)), _const_spec((n_slc, V7X_LANES))],
        out_specs=pl.BlockSpec((1, NSA_TQ, C_GROUP * C_HEAD_DIM), lambda b, h, i: (b, i, h)),
        out_shape=jax.ShapeDtypeStruct((bsz, seq, C_WIDTH), BF16),
        compiler_params=_params("parallel", "parallel", "arbitrary"),
        name="nsa_attention",
    )(as3(qn), as3(qr), as3(gl), k_cmp, v_cmp, as3(ks), as3(vs), as3(kw), as3(vw), overlap_t, place)


ROW_CHUNKS = D_MODEL // V7X_LANES
assert ROW_CHUNKS == V7X_SUBLANES
DMA_UNROLL = 8


def _tile_rows(n_rows):
    return (n_rows * ROW_CHUNKS, V7X_LANES)


def _row_tile(ref, r):
    start = r * ROW_CHUNKS
    if not isinstance(start, int):
        start = pl.multiple_of(start, ROW_CHUNKS)
    return ref.at[pl.ds(start, ROW_CHUNKS), :]


def _store_row_tiles(ref, x):
    for c in range(ROW_CHUNKS):
        ref[pl.ds(c, x.shape[0], stride=ROW_CHUNKS), :] = x[:, c * V7X_LANES:(c + 1) * V7X_LANES]


def _load_row_tiles(ref):
    rows = ref.shape[0] // ROW_CHUNKS
    return jnp.concatenate([ref[pl.ds(c, rows, stride=ROW_CHUNKS), :] for c in range(ROW_CHUNKS)], axis=1)


def _router_body(h_ref, g_ref, wr_ref, br_ref, tri_ref, hn_o, idx_o, gate_o, cnt_o, cnt_scr):
    @pl.when(pl.program_id(0) == 0)
    def _():
        cnt_scr[...] = jnp.zeros_like(cnt_scr)

    hn = _rms(h_ref[...], g_ref[...])
    _store_row_tiles(hn_o, hn)
    logits = _dot_f32(hn, wr_ref[...]) + br_ref[...]
    lane = lax.broadcasted_iota(jnp.int32, logits.shape, 1)
    m1 = jnp.max(logits, axis=-1, keepdims=True)
    i1 = jnp.min(jnp.where(logits == m1, lane, V7X_LANES), axis=-1, keepdims=True)
    rest = jnp.where(lane == i1, NEG_INF, logits)
    m2 = jnp.max(rest, axis=-1, keepdims=True)
    i2 = jnp.min(jnp.where(rest == m2, lane, V7X_LANES), axis=-1, keepdims=True)
    e2 = jnp.exp(m2 - m1)
    den = 1.0 + e2
    hit1, hit2 = lane == i1, lane == i2
    hits = jnp.where(hit1 | hit2, 1.0, 0.0)
    before = jnp.dot(tri_ref[...], hits.astype(BF16), preferred_element_type=F32) + cnt_scr[...]
    r1 = jnp.sum(jnp.where(hit1, before, 0.0), axis=-1, keepdims=True).astype(jnp.int32)
    r2 = jnp.sum(jnp.where(hit2, before, 0.0), axis=-1, keepdims=True).astype(jnp.int32)
    cnt_scr[...] = cnt_scr[...] + jnp.sum(hits, axis=0, keepdims=True)
    cnt_o[...] = cnt_scr[...]
    idx_o[...] = jnp.where(lane == 0, i1, jnp.where(lane == 1, i2, jnp.where(lane == 2, r1,
                           jnp.where(lane == 3, r2, 0))))
    gate_o[...] = jnp.where(lane == 0, 1.0 / den, jnp.where(lane == 1, e2 / den, 0.0))


def _router(h, gain, w_router, b_router, tm=512):
    n = h.shape[0]
    wr = jnp.pad(w_router, ((0, 0), (0, V7X_LANES - N_EXPERTS)))
    br = jnp.pad(b_router.reshape(1, -1), ((0, 0), (0, V7X_LANES - N_EXPERTS)), constant_values=NEG_INF)
    tri = jnp.asarray(np.tril(np.ones((tm, tm), np.float32), k=-1)).astype(BF16)
    return pl.pallas_call(
        _router_body,
        grid=(n // tm,),
        in_specs=[_rows_spec(tm, D_MODEL), _const_spec((1, D_MODEL)), _const_spec(wr.shape),
                  _const_spec(br.shape), _const_spec(tri.shape)],
        out_specs=[pl.BlockSpec(_tile_rows(tm), lambda i: (i, 0)), _rows_spec(tm, V7X_LANES),
                   _rows_spec(tm, V7X_LANES), _const_spec((1, V7X_LANES))],
        out_shape=[jax.ShapeDtypeStruct(_tile_rows(n), F32), jax.ShapeDtypeStruct((n, V7X_LANES), jnp.int32),
                   jax.ShapeDtypeStruct((n, V7X_LANES), F32), jax.ShapeDtypeStruct((1, V7X_LANES), F32)],
        scratch_shapes=[pltpu.VMEM((1, V7X_LANES), F32)],
        compiler_params=_params("arbitrary"),
        name="moe_router",
    )(h, gain.reshape(1, -1), wr, br, tri)


def _row_copy(src, dst, sem):
    return pltpu.make_async_copy(src, dst, sem)


def _experts_body(blk_e_ref, n_used_ref, first_ref, next_ref, hn_ref, wg_ref, wu_ref, wd_ref, o_ref,
                  x_scr, sems):
    i = pl.program_id(0)
    last = pl.num_programs(0) - 1
    n_used = n_used_ref[0]
    slot = i % 2

    def gather(rows_ref, s):
        for r in range(MOE_ROWS):
            _row_copy(_row_tile(hn_ref, rows_ref[r]), _row_tile(x_scr.at[s], r),
                      sems.at[s]).start(priority=r % 2)

    def drain(s):
        _row_copy(hn_ref.at[pl.ds(0, MOE_ROWS * ROW_CHUNKS), :], x_scr.at[s], sems.at[s]).wait()

    @pl.when(i == 0)
    def _():
        gather(first_ref, 0)

    @pl.when(i <= n_used)
    def _():
        drain(slot)

    @pl.when(i < n_used)
    def _():
        x = _load_row_tiles(x_scr.at[slot]).astype(BF16)
        gather(next_ref, 1 - slot)
        gate = jnp.dot(x, wg_ref[0], preferred_element_type=F32)
        up = jnp.dot(x, wu_ref[0], preferred_element_type=F32)
        act = (gate * _sigmoid(gate)) * up
        _store_row_tiles(o_ref, _dot(act, wd_ref[0]))

    @pl.when(i >= n_used)
    def _():
        o_ref[...] = jnp.zeros_like(o_ref)

    @pl.when((i == last) & (i < n_used))
    def _():
        drain(1 - slot)


def _experts(blk_e, n_used, row_src, hn_rows, wg, wu, wd):
    p_rows = row_src.shape[0]
    n_steps = p_rows // MOE_ROWS
    idx_spec = lambda f: pl.BlockSpec((MOE_ROWS,), f, memory_space=pltpu.SMEM)
    w_spec = lambda w: pl.BlockSpec((1,) + w.shape[1:], lambda i, be, nu: (be[i], 0, 0))
    return pl.pallas_call(
        _experts_body,
        grid_spec=pltpu.PrefetchScalarGridSpec(
            num_scalar_prefetch=2,
            grid=(n_steps,),
            in_specs=[idx_spec(lambda i, be, nu: (0,)),
                      idx_spec(lambda i, be, nu: (jnp.minimum(i + 1, n_steps - 1),)),
                      pl.BlockSpec(memory_space=pl.ANY), w_spec(wg), w_spec(wu), w_spec(wd)],
            out_specs=pl.BlockSpec(_tile_rows(MOE_ROWS), lambda i, be, nu: (i, 0)),
            scratch_shapes=[pltpu.VMEM((2,) + _tile_rows(MOE_ROWS), F32), pltpu.SemaphoreType.DMA((2,))]),
        out_shape=jax.ShapeDtypeStruct(_tile_rows(p_rows), F32),
        compiler_params=_params("arbitrary"),
        name="moe_experts",
    )(blk_e, n_used, row_src, row_src, hn_rows, wg, wu, wd)


def _combine_body(dest_ref, h_ref, gate_ref, g_ref, yb_ref, o_ref, y_scr, sem):
    tm = h_ref.shape[0]

    def issue(blk, c):
        for u in range(DMA_UNROLL):
            t = blk * DMA_UNROLL + u
            for k in range(TOP_K):
                _row_copy(_row_tile(yb_ref, dest_ref[TOP_K * t + k]), _row_tile(y_scr.at[k], t),
                          sem).start(priority=k)
        return c

    lax.fori_loop(0, tm // DMA_UNROLL, issue, 0)

    def drain(t, c):
        for k in range(TOP_K):
            _row_copy(_row_tile(yb_ref, 0), _row_tile(y_scr.at[0], 0), sem).wait()
        return c

    lax.fori_loop(0, tm, drain, 0)
    gate = gate_ref[...]
    moe = gate[:, 0:1] * _load_row_tiles(y_scr.at[0]) + gate[:, 1:2] * _load_row_tiles(y_scr.at[1])
    o_ref[...] = _rms(h_ref[...] + moe, g_ref[...])


def _combine(dest, h, gates, gain, yb_rows, tm=256):
    n = h.shape[0]
    return pl.pallas_call(
        _combine_body,
        grid=(n // tm,),
        in_specs=[pl.BlockSpec((TOP_K * tm,), lambda i: (i,), memory_space=pltpu.SMEM),
                  _rows_spec(tm, D_MODEL), _rows_spec(tm, V7X_LANES), _const_spec((1, D_MODEL)),
                  pl.BlockSpec(memory_space=pl.ANY)],
        out_specs=_rows_spec(tm, D_MODEL),
        out_shape=jax.ShapeDtypeStruct((n, D_MODEL), F32),
        scratch_shapes=[pltpu.VMEM((TOP_K,) + _tile_rows(tm), F32), pltpu.SemaphoreType.DMA(())],
        compiler_params=_params("arbitrary"),
        name="moe_combine_norm",
    )(dest, h, gates, gain.reshape(1, -1), yb_rows)


def _moe_layout(idx, counts):
    n = idx.shape[0]
    nk = n * TOP_K
    counts = counts[0, :N_EXPERTS].astype(jnp.int32)
    padded = ((counts + MOE_ROWS - 1) // MOE_ROWS) * MOE_ROWS
    p_end = jnp.cumsum(padded)
    p_start = p_end - padded
    experts = jnp.arange(N_EXPERTS, dtype=jnp.int32)[None, None, :]
    first = jnp.sum(jnp.where(idx[:, :TOP_K, None] == experts, p_start[None, None, :], 0), axis=-1)
    dest = (first + idx[:, TOP_K:2 * TOP_K]).reshape(nk)
    n_blk = (nk + MOE_ROWS - 1) // MOE_ROWS + N_EXPERTS
    blk_e = jnp.minimum(jnp.searchsorted(p_end, jnp.arange(n_blk) * MOE_ROWS, side="right"),
                        N_EXPERTS - 1).astype(jnp.int32)
    n_used = (p_end[-1:] // MOE_ROWS).astype(jnp.int32)
    tok_by_row = (jnp.argsort(dest) // TOP_K).astype(jnp.int32)
    rows = jnp.arange(n_blk * MOE_ROWS, dtype=jnp.int32)
    row_e = jnp.repeat(blk_e, MOE_ROWS)
    offset = rows - p_start[row_e]
    occupied = (offset < counts[row_e]) & (rows < p_end[-1])
    rank_all = jnp.cumsum(counts)[row_e] - counts[row_e] + offset
    row_src = jnp.where(occupied, tok_by_row[jnp.clip(rank_all, 0, nk - 1)], 0)
    return dest.astype(jnp.int32), blk_e, n_used, row_src.astype(jnp.int32)


def _moe_final(h, norm_g, w_router, b_router, wg, wu, wd, final_g):
    hn, idx, gates, counts = _router(h, norm_g, w_router, b_router)
    dest, blk_e, n_used, row_src = _moe_layout(idx, counts)
    yb = _experts(blk_e, n_used, row_src, hn, wg.astype(BF16), wu.astype(BF16), wd.astype(BF16))
    return _combine(dest, h, gates, final_g, yb)


def kernel(x, e_norm_mix, e_w_in, sgu_ln_g, sgu_ln_b, sgu_w, sgu_b, rwkv_mu, rwkv_w0, rwkv_w2,
           rwkv_a0, rwkv_a2, rwkv_g2, rwkv_k_k, rwkv_k_a, rwkv_r_k, rwkv_gn_g, rwkv_gn_b, e_w_out,
           e_norm_ffn, ffn_w_gate, ffn_w_up, ffn_w_down, o_norm_mix, o_w_in, nsa_cmp_pos_k,
           nsa_cmp_pos_v, nsa_cmp_k_w1, nsa_cmp_k_w2, nsa_cmp_v_w1, nsa_cmp_v_w2, conv_w, o_w_out,
           o_norm_ffn, moe_router, moe_router_b, moe_w_gate, moe_w_up, moe_w_down, final_norm):
    bsz, seq, d = x.shape
    n = bsz * seq
    h = x.reshape(n, d)

    ya, scan_ops, g = _layer0_front(h, seq, e_norm_mix[0], e_w_in[0].astype(BF16), sgu_ln_g[0], sgu_ln_b[0],
                                    sgu_w[0], sgu_b[0], rwkv_mu[0], rwkv_w0[0], rwkv_w2[0], rwkv_a0[0],
                                    rwkv_a2[0], rwkv_g2[0])
    rwka, v = _to_scan_layout(scan_ops, bsz, seq)
    ys = _rwkv_scan(rwka, v, rwkv_k_k[0], rwkv_k_a[0], rwkv_r_k[0], rwkv_gn_g[0], rwkv_gn_b[0], bsz, seq)
    h = _layer0_back(h, ya, _from_scan_layout(ys, bsz, seq), g, e_w_out[0].astype(BF16), e_norm_ffn[0],
                     ffn_w_gate[0].astype(BF16), ffn_w_up[0].astype(BF16), ffn_w_down[0].astype(BF16))

    qn, qr, kcvc, ks, vs, kw, vw, gl, bcd = _in_proj1(h, o_norm_mix[0], o_w_in[0], seq)
    k_cmp, v_cmp = _compress(kcvc, nsa_cmp_pos_k[0], nsa_cmp_pos_v[0], nsa_cmp_k_w1[0], nsa_cmp_v_w1[0],
                             nsa_cmp_k_w2[0], nsa_cmp_v_w2[0], bsz, seq)
    yc = _nsa(qn, qr, gl, k_cmp, v_cmp, ks, vs, kw, vw, bsz, seq).reshape(n, C_WIDTH)
    h = _out_proj1(h, yc, bcd, conv_w[0], o_w_out[0].astype(BF16), seq)
    out = _moe_final(h, o_norm_ffn[0], moe_router[0], moe_router_b[0], moe_w_gate[0], moe_w_up[0],
                     moe_w_down[0], final_norm)
    return out.reshape(bsz, seq, d)
```

```python
import functools

import jax
import jax.numpy as jnp
import numpy as np
from jax import lax
from jax.experimental import pallas as pl
from jax.experimental.pallas import tpu as pltpu

F32 = jnp.float32
BF16 = jnp.bfloat16

D_MODEL = 1024
A_GROUPS = 4
A_GROUP_DIM = 128
A_WIDTH = A_GROUPS * A_GROUP_DIM
CHUNK = 128
SGU_LN_EPS = 1e-5
B_HEADS = 8
B_HEAD_DIM = 64
B_WIDTH = B_HEADS * B_HEAD_DIM
DECAY_LORA = 64
ICLR_LORA = 64
GATE_LORA = 128
B_IN = 3 * B_WIDTH + DECAY_LORA + ICLR_LORA + GATE_LORA
RWKV_GN_EPS = 64e-5
C_HEADS = 8
C_KV_HEADS = 2
C_GROUP = C_HEADS // C_KV_HEADS
C_HEAD_DIM = 64
C_WIDTH = C_HEADS * C_HEAD_DIM
KV_WIDTH = C_KV_HEADS * C_HEAD_DIM
N_BRANCH = 3
CMP_LEN = 32
CMP_STRIDE = 16
CMP_HIDDEN = 256
SLC_BLK = 64
SEL_TOPK = 8
WIN = 512
NSA_TQ = 512
ROT_DIM = C_HEAD_DIM // 4
ROPE_THETA = 500000.0
D_WIDTH = 512
CONV_W = 3
FFN_DIM = 2816
N_EXPERTS = 8
TOP_K = 2
EXPERT_DIM = 1408
NORM_EPS = 1e-6
NEG_INF = -1e30
FORCE_BONUS = 1e6

V7X_LANES = 128
V7X_SUBLANES = 8
V7X_VMEM_LIMIT = 56 * 1024 * 1024

MOE_ROWS = 512
DMA_ROWS = 512


def _params(*sem):
    return pltpu.CompilerParams(dimension_semantics=sem, vmem_limit_bytes=V7X_VMEM_LIMIT)


def _const_spec(shape):
    zeros = (0,) * len(shape)
    return pl.BlockSpec(shape, lambda *_: zeros)


def _rows_spec(tm, width):
    return pl.BlockSpec((tm, width), lambda i: (i, 0))


def _rms(x, g):
    return x * lax.rsqrt(jnp.mean(x * x, axis=-1, keepdims=True) + NORM_EPS) * g


def _gelu(x):
    return x * (0.5 * (1.0 + jnp.tanh(0.7978845608028654 * (x + 0.044715 * (x * x * x)))))


def _sigmoid(x):
    return 1.0 / (1.0 + jnp.exp(-x))


def _dot(a, b):
    return jnp.dot(a.astype(BF16), b.astype(BF16), preferred_element_type=F32)


def _dot_nt(a, b):
    return lax.dot_general(a.astype(BF16), b.astype(BF16), (((1,), (1,)), ((), ())),
                           preferred_element_type=F32)


def _split_bf16(a):
    hi = a.astype(BF16)
    lo = (a - hi.astype(F32)).astype(BF16)
    return hi, lo


def _dot_f32(a, b):
    ah, al = _split_bf16(a)
    bh, bl = _split_bf16(b)
    d = functools.partial(jnp.dot, preferred_element_type=F32)
    return d(ah, bh) + (d(al, bh) + d(ah, bl))


def _masked_softmax(s, mask):
    s = jnp.where(mask, s, NEG_INF)
    m = jnp.max(s, axis=-1, keepdims=True)
    p = jnp.where(mask, jnp.exp(s - m), 0.0)
    return p / jnp.maximum(jnp.sum(p, axis=-1, keepdims=True), 1e-30)


def _sgu_chunk(p_uv, lng_ref, lnb_ref, w_ref, b_ref):
    row = lax.broadcasted_iota(jnp.int32, (CHUNK, CHUNK), 0)
    col = lax.broadcasted_iota(jnp.int32, (CHUNK, CHUNK), 1)
    causal = col <= row
    u = _gelu(p_uv[:, :A_WIDTH])
    v = _gelu(p_uv[:, A_WIDTH:])
    outs = []
    for g in range(A_GROUPS):
        cols = slice(g * A_GROUP_DIM, (g + 1) * A_GROUP_DIM)
        vg = v[:, cols]
        mu = jnp.mean(vg, axis=-1, keepdims=True)
        dv = vg - mu
        var = jnp.mean(dv * dv, axis=-1, keepdims=True)
        vn = dv * lax.rsqrt(var + SGU_LN_EPS) * lng_ref[:, cols] + lnb_ref[:, cols]
        wm = jnp.where(causal, w_ref[g], 0.0)
        mixed = _dot(wm, vn) + b_ref[:, g:g + 1]
        outs.append(u[:, cols] * mixed)
    return jnp.concatenate(outs, axis=1)


def _softplus(x):
    return jnp.maximum(x, 0.0) + jnp.log(1.0 + jnp.exp(-jnp.abs(x)))


SCAN_OPERANDS = 5


def _layer0_front_body(x_ref, gain_ref, win_ref, lng_ref, lnb_ref, sw_ref, sb_ref, mu_ref, w0_ref, w2_ref,
                       a0_ref, a2_ref, g2_ref, ya_o, s_o, g_o, prev_scr, *, tiles_per_seq):
    tm = x_ref.shape[0]

    @pl.when(pl.program_id(0) == 0)
    def _():
        prev_scr[...] = jnp.zeros_like(prev_scr)

    p = jnp.dot(_rms(x_ref[...], gain_ref[...]).astype(BF16), win_ref[...], preferred_element_type=F32)
    for c in range(tm // CHUNK):
        rows = slice(c * CHUNK, (c + 1) * CHUNK)
        ya_o[rows, :] = _sgu_chunk(p[rows, :2 * A_WIDTH], lng_ref, lnb_ref, sw_ref, sb_ref).astype(ya_o.dtype)
    x = p[:, 2 * A_WIDTH:]
    first = (pl.program_id(0) % tiles_per_seq) == 0
    prev_row = jnp.where(first, 0.0, prev_scr[...])
    rowid = lax.broadcasted_iota(jnp.int32, (tm, 1), 0)
    shifted = jnp.where(rowid == 0, prev_row, pltpu.roll(x, 1, axis=0))
    prev_scr[...] = x[tm - 1:tm, :]
    xm = x + (shifted - x) * mu_ref[...]
    o = 3 * B_WIDTH
    wl = xm[:, o:o + DECAY_LORA]
    al = xm[:, o + DECAY_LORA:o + DECAY_LORA + ICLR_LORA]
    gl = xm[:, o + DECAY_LORA + ICLR_LORA:]
    w = -_softplus(-(w0_ref[...] + _dot(jnp.tanh(wl), w2_ref[...]))) - 0.5
    s_o[0] = xm[:, :B_WIDTH]
    s_o[1] = jnp.exp(-jnp.exp(w))
    s_o[2] = xm[:, B_WIDTH:2 * B_WIDTH]
    s_o[3] = _sigmoid(a0_ref[...] + _dot(al, a2_ref[...]))
    s_o[4] = xm[:, 2 * B_WIDTH:3 * B_WIDTH]
    g_o[...] = _dot(_sigmoid(gl), g2_ref[...])


def _layer0_front(h, seq, gain, w_in, ln_g, ln_b, w_s, b_s, mu, w0, w2, a0, a2, g2, tm=512):
    n, d = h.shape
    perm = np.arange(B_WIDTH).reshape(B_HEADS, B_HEAD_DIM).T.reshape(-1)
    r0, k0 = 2 * A_WIDTH, 2 * A_WIDTH + B_WIDTH
    cols = np.concatenate([np.arange(r0), r0 + perm, k0 + perm, np.arange(k0 + B_WIDTH, w_in.shape[1])])
    w_in = w_in[:, cols]
    mu = mu[cols[r0:] - r0]
    w0, w2, a0, a2 = w0[perm], w2[:, perm], a0[perm], a2[:, perm]
    row = lambda v: v.reshape(1, -1)
    consts = [row(gain), w_in, row(ln_g), row(ln_b), w_s, b_s.T, row(mu), row(w0), w2, row(a0), a2, g2]
    outs = [jax.ShapeDtypeStruct((n, A_WIDTH), BF16), jax.ShapeDtypeStruct((SCAN_OPERANDS, n, B_WIDTH), F32),
            jax.ShapeDtypeStruct((n, B_WIDTH), F32)]
    return pl.pallas_call(
        functools.partial(_layer0_front_body, tiles_per_seq=seq // tm),
        grid=(n // tm,),
        in_specs=[_rows_spec(tm, d)] + [_const_spec(c.shape) for c in consts],
        out_specs=[_rows_spec(tm, A_WIDTH), pl.BlockSpec((SCAN_OPERANDS, tm, B_WIDTH), lambda i: (0, i, 0)),
                   _rows_spec(tm, B_WIDTH)],
        out_shape=outs,
        scratch_shapes=[pltpu.VMEM((1, B_IN), F32)],
        compiler_params=_params("arbitrary"),
        name="layer0_front",
    )(h, *consts)


def _rwkv_scan_body(r_ref, w_ref, k0_ref, a_ref, v_ref, kkp_ref, kap_ref, rk_ref, gng_ref, gnb_ref,
                    y_ref, s_ref, kkn_ref, ka_ref, km_ref):
    n, tt = r_ref.shape[0], r_ref.shape[1]

    @pl.when(pl.program_id(0) == 0)
    def _():
        s_ref[...] = jnp.zeros_like(s_ref)

    per_dim = lambda p_ref: p_ref[...][:, None, :]
    k0 = k0_ref[...]
    a = a_ref[...]
    kk = k0 * per_dim(kkp_ref)
    kkn = kk / jnp.maximum(jnp.sqrt(jnp.sum(kk * kk, axis=0, keepdims=True)), 1e-12)
    km = k0 * (1.0 + (a - 1.0) * per_dim(kap_ref))
    kkn_ref[...] = kkn
    ka_ref[...] = kkn * a
    km_ref[...] = km

    zero = jnp.zeros((n, r_ref.shape[2]), F32)

    def sa_init(j, acc):
        return acc + s_ref[j] * kkn_ref[j, pl.ds(0, 1), :]

    sa0 = lax.fori_loop(0, n, sa_init, zero)

    def step(t, sa):
        v_t = v_ref[t]
        now = pl.ds(t, 1)
        nxt = pl.ds(jnp.minimum(t + 1, tt - 1), 1)

        def jbody(j, carry):
            y, san = carry
            sn = s_ref[j] * w_ref[j, now, :] + (v_t * km_ref[j, now, :] - sa * ka_ref[j, now, :])
            s_ref[j] = sn
            return y + sn * r_ref[j, now, :], san + sn * kkn_ref[j, nxt, :]

        y, san = lax.fori_loop(0, n, jbody, (zero, zero), unroll=8)
        y_ref[t] = y
        return san

    lax.fori_loop(0, tt, step, sa0)

    y = y_ref[...]
    ym = jnp.mean(y, axis=1, keepdims=True)
    dy = y - ym
    yv = jnp.mean(dy * dy, axis=1, keepdims=True)
    yn = dy * lax.rsqrt(yv + RWKV_GN_EPS) * gng_ref[...][None] + gnb_ref[...][None]
    bonus = jnp.sum(r_ref[...] * km * per_dim(rk_ref), axis=0)
    y_ref[...] = yn + bonus[:, None, :] * v_ref[...]


def _rwkv_scan(rwka, v, k_k, k_a, r_k, gn_g, gn_b, bsz, seq, tt=32):
    n = B_HEAD_DIM
    lanes = bsz * B_HEADS

    def lane_param(p):
        return jnp.tile(p.reshape(B_HEADS, n).T, (1, bsz))

    op_spec = lambda a: pl.BlockSpec((None, n, tt, lanes), lambda c: (a, 0, c, 0))
    slab_spec = pl.BlockSpec((tt, n, lanes), lambda c: (c, 0, 0))
    par_spec = _const_spec((n, lanes))
    return pl.pallas_call(
        _rwkv_scan_body,
        grid=(seq // tt,),
        in_specs=[op_spec(a) for a in range(SCAN_OPERANDS - 1)] + [slab_spec] + [par_spec] * 5,
        out_specs=slab_spec,
        out_shape=jax.ShapeDtypeStruct((seq, n, lanes), F32),
        scratch_shapes=[pltpu.VMEM((n, n, lanes), F32)] + [pltpu.VMEM((n, tt, lanes), F32)] * 3,
        compiler_params=_params("arbitrary"),
        name="rwkv_scan",
    )(*([rwka] * (SCAN_OPERANDS - 1)), v, lane_param(k_k), lane_param(k_a), lane_param(r_k.reshape(-1)),
      lane_param(gn_g), lane_param(gn_b))


SCAN_TT = 128
SCAN_UNROLL = 8


def _time_to_lanes(x_ref, a_ref):
    for b in range(x_ref.shape[0]):
        for blk in range(B_WIDTH // V7X_LANES):
            cols = slice(blk * V7X_LANES, (blk + 1) * V7X_LANES)
            a_ref[b, cols, :] = x_ref[b, :, cols].T


def _to_scan_rows_body(x_ref, o_ref, a_ref):
    bsz = x_ref.shape[0]
    _time_to_lanes(x_ref, a_ref)

    def dims(i, carry):
        for u in range(SCAN_UNROLL):
            d = i * SCAN_UNROLL + u
            rows = pl.ds(pl.multiple_of(d * B_HEADS, B_HEADS), B_HEADS)
            z = jnp.concatenate([a_ref[b, rows, :] for b in range(bsz)], axis=0)
            o_ref[d] = z.T
        return carry

    lax.fori_loop(0, B_HEAD_DIM // SCAN_UNROLL, dims, 0)


def _to_scan_slab_body(x_ref, o_ref, a_ref):
    bsz = x_ref.shape[0]
    _time_to_lanes(x_ref, a_ref)

    def dims(i, carry):
        for u in range(SCAN_UNROLL):
            d = i * SCAN_UNROLL + u
            z = jnp.concatenate([a_ref[b, pl.ds(d, B_HEADS, stride=B_HEAD_DIM), :] for b in range(bsz)],
                                axis=0)
            o_ref[:, d, :] = z.T
        return carry

    lax.fori_loop(0, B_HEAD_DIM // SCAN_UNROLL, dims, 0)


def _to_scan_layout(ops, bsz, seq):
    tt = min(SCAN_TT, seq)
    lanes = bsz * B_HEADS
    ops4 = ops.reshape(SCAN_OPERANDS, bsz, seq, B_WIDTH)
    scratch = [pltpu.VMEM((bsz, B_WIDTH, tt), F32)]
    rwka = pl.pallas_call(
        _to_scan_rows_body,
        grid=(SCAN_OPERANDS - 1, seq // tt),
        in_specs=[pl.BlockSpec((None, bsz, tt, B_WIDTH), lambda a, i: (a, 0, i, 0))],
        out_specs=pl.BlockSpec((None, B_HEAD_DIM, tt, lanes), lambda a, i: (a, 0, i, 0)),
        out_shape=jax.ShapeDtypeStruct((SCAN_OPERANDS - 1, B_HEAD_DIM, seq, lanes), F32),
        scratch_shapes=scratch,
        compiler_params=_params("parallel", "arbitrary"),
        name="to_scan_rows",
    )(ops4)
    v = pl.pallas_call(
        _to_scan_slab_body,
        grid=(seq // tt,),
        in_specs=[pl.BlockSpec((None, bsz, tt, B_WIDTH), lambda i: (SCAN_OPERANDS - 1, 0, i, 0))],
        out_specs=pl.BlockSpec((tt, B_HEAD_DIM, lanes), lambda i: (i, 0, 0)),
        out_shape=jax.ShapeDtypeStruct((seq, B_HEAD_DIM, lanes), F32),
        scratch_shapes=scratch,
        compiler_params=_params("arbitrary"),
        name="to_scan_slab",
    )(ops4)
    return rwka, v


def _from_scan_body(y_ref, o_ref, a_ref):
    bsz = o_ref.shape[0]

    def dims(i, carry):
        for u in range(SCAN_UNROLL):
            d = i * SCAN_UNROLL + u
            zt = y_ref[:, d, :].T
            for b in range(bsz):
                a_ref[b, pl.ds(d, B_HEADS, stride=B_HEAD_DIM), :] = zt[b * B_HEADS:(b + 1) * B_HEADS, :]
        return carry

    lax.fori_loop(0, B_HEAD_DIM // SCAN_UNROLL, dims, 0)
    for b in range(bsz):
        for blk in range(B_WIDTH // V7X_LANES):
            cols = slice(blk * V7X_LANES, (blk + 1) * V7X_LANES)
            o_ref[b, :, cols] = a_ref[b, cols, :].T


def _from_scan_layout(y, bsz, seq):
    tt = min(SCAN_TT, seq)
    return pl.pallas_call(
        _from_scan_body,
        grid=(seq // tt,),
        in_specs=[pl.BlockSpec((tt, B_HEAD_DIM, bsz * B_HEADS), lambda i: (i, 0, 0))],
        out_specs=pl.BlockSpec((bsz, tt, B_WIDTH), lambda i: (0, i, 0)),
        out_shape=jax.ShapeDtypeStruct((bsz, seq, B_WIDTH), F32),
        scratch_shapes=[pltpu.VMEM((bsz, B_WIDTH, tt), F32)],
        compiler_params=_params("arbitrary"),
        name="from_scan_layout",
    )(y).reshape(bsz * seq, B_WIDTH)


def _layer1_mid_body(h_ref, yc_ref, bcd_ref, prev_ref, cw_ref, w_ref, g_ref, wr_ref, br_ref, tri_ref,
                     o_ref, idx_o, gate_o, cnt_o, cnt_scr, *, tiles_per_seq):
    tm = h_ref.shape[0]

    @pl.when(pl.program_id(0) == 0)
    def _():
        cnt_scr[...] = jnp.zeros_like(cnt_scr)

    first = (pl.program_id(0) % tiles_per_seq) == 0
    z = bcd_ref[:, D_WIDTH:2 * D_WIDTH] * bcd_ref[:, 2 * D_WIDTH:]
    zp = jnp.where(first, 0.0, prev_ref[:, D_WIDTH:2 * D_WIDTH] * prev_ref[:, 2 * D_WIDTH:])
    rowid = lax.broadcasted_iota(jnp.int32, (tm, 1), 0)
    z1 = jnp.where(rowid == 0, zp[7:8, :], pltpu.roll(z, 1, axis=0))
    z2 = pltpu.roll(z, 2, axis=0)
    z2 = jnp.where(rowid == 0, zp[6:7, :], jnp.where(rowid == 1, zp[7:8, :], z2))
    y = cw_ref[0:1, :] * z2 + cw_ref[1:2, :] * z1 + cw_ref[2:3, :] * z
    yd = bcd_ref[:, :D_WIDTH] * y
    h_new = h_ref[...] + (_dot(yc_ref[...], w_ref[:C_WIDTH, :]) + _dot(yd, w_ref[C_WIDTH:, :]))
    o_ref[...] = h_new
    idx_o[...], gate_o[...] = _route(_rms(h_new, g_ref[...]), wr_ref, br_ref, tri_ref, cnt_scr)
    cnt_o[...] = cnt_scr[...]


def _layer1_mid(h, yc, bcd, conv_w, w_out, seq, gain, w_router, b_router, tm=512):
    n = h.shape[0]
    per8 = tm // V7X_SUBLANES
    wr = jnp.pad(w_router, ((0, 0), (0, V7X_LANES - N_EXPERTS)))
    br = jnp.pad(b_router.reshape(1, -1), ((0, 0), (0, V7X_LANES - N_EXPERTS)), constant_values=NEG_INF)
    tri = jnp.asarray(np.tril(np.ones((tm, tm), np.float32), k=-1)).astype(BF16)
    return pl.pallas_call(
        functools.partial(_layer1_mid_body, tiles_per_seq=seq // tm),
        grid=(n // tm,),
        in_specs=[_rows_spec(tm, D_MODEL), _rows_spec(tm, C_WIDTH), _rows_spec(tm, 3 * D_WIDTH),
                  pl.BlockSpec((V7X_SUBLANES, 3 * D_WIDTH), lambda i: (jnp.maximum(i * per8 - 1, 0), 0)),
                  _const_spec((CONV_W, D_WIDTH)), _const_spec(w_out.shape), _const_spec((1, D_MODEL)),
                  _const_spec(wr.shape), _const_spec(br.shape), _const_spec(tri.shape)],
        out_specs=[_rows_spec(tm, D_MODEL), _rows_spec(tm, V7X_LANES), _rows_spec(tm, V7X_LANES),
                   _const_spec((1, V7X_LANES))],
        out_shape=[jax.ShapeDtypeStruct((n, D_MODEL), F32), jax.ShapeDtypeStruct((n, V7X_LANES), jnp.int32),
                   jax.ShapeDtypeStruct((n, V7X_LANES), F32), jax.ShapeDtypeStruct((1, V7X_LANES), F32)],
        scratch_shapes=[pltpu.VMEM((1, V7X_LANES), F32)],
        compiler_params=_params("arbitrary"),
        name="layer1_mid",
    )(h, yc, bcd, bcd, conv_w, w_out, gain.reshape(1, -1), wr, br, tri)


def _layer0_back_body(h_ref, ya_ref, ys_ref, gm_ref, wo_ref, g_ref, wg_ref, wu_ref, wd_ref, o_ref):
    yb = ys_ref[...] * gm_ref[...]
    h = h_ref[...] + (_dot(ya_ref[...], wo_ref[:A_WIDTH, :]) + _dot(yb, wo_ref[A_WIDTH:, :]))
    hn = _rms(h, g_ref[...]).astype(BF16)
    gate = jnp.dot(hn, wg_ref[...], preferred_element_type=F32)
    up = jnp.dot(hn, wu_ref[...], preferred_element_type=F32)
    act = (gate * _sigmoid(gate)) * up
    o_ref[...] = h + _dot(act, wd_ref[...])


def _layer0_back(h, ya, ys, gm, w_out, gain, wg, wu, wd, tm=512):
    n = h.shape[0]
    return pl.pallas_call(
        _layer0_back_body,
        grid=(n // tm,),
        in_specs=[_rows_spec(tm, D_MODEL), _rows_spec(tm, A_WIDTH), _rows_spec(tm, B_WIDTH),
                  _rows_spec(tm, B_WIDTH), _const_spec(w_out.shape), _const_spec((1, D_MODEL)),
                  _const_spec(wg.shape), _const_spec(wu.shape), _const_spec(wd.shape)],
        out_specs=_rows_spec(tm, D_MODEL),
        out_shape=jax.ShapeDtypeStruct((n, D_MODEL), F32),
        compiler_params=_params("parallel"),
        name="layer0_back",
    )(h, ya, ys, gm, w_out, gain.reshape(1, -1), wg, wu, wd)


def _rope(x, c, s_up, s_dn):
    half = ROT_DIM // 2
    return x * c + pltpu.roll(x, half, axis=1) * s_up + pltpu.roll(x, V7X_LANES - half, axis=1) * s_dn


def _lane_blocks(x):
    return [x[:, i * V7X_LANES:(i + 1) * V7X_LANES] for i in range(x.shape[1] // V7X_LANES)]


def _in_proj1_body(x_ref, g_ref, c_ref, su_ref, sd_ref, kb_ref, one_ref, wq_ref, wkc_ref, wkv_ref,
                   wgl_ref, wbcd_ref, qn_o, qr_o, kc_o, ks_o, vs_o, kw_o, vw_o, gl_o, bcd_o):
    xn = _rms(x_ref[...], g_ref[...]).astype(BF16)
    d = functools.partial(jnp.dot, preferred_element_type=F32)
    c, su, sd = c_ref[...], su_ref[...], sd_ref[...]
    rope = lambda z: _rope(z, c, su, sd)
    q = d(xn, wq_ref[...]) * (C_HEAD_DIM ** -0.5)
    qn_o[...] = q.astype(BF16)
    qr_o[...] = jnp.concatenate([rope(z) for z in _lane_blocks(q)], axis=1).astype(BF16)
    kc_o[...] = d(xn, wkc_ref[...])
    kv = _lane_blocks(d(xn, wkv_ref[...]))
    hk = C_KV_HEADS
    ks_o[...] = jnp.concatenate([rope(z) + kb_ref[...] for z in kv[:hk]], axis=1).astype(BF16)
    vs_o[...] = jnp.concatenate([z + one_ref[...] for z in kv[hk:2 * hk]], axis=1).astype(BF16)
    kw_o[...] = jnp.concatenate([rope(z) for z in kv[2 * hk:3 * hk]], axis=1).astype(BF16)
    vw_o[...] = jnp.concatenate([z + one_ref[...] for z in kv[3 * hk:]], axis=1).astype(BF16)
    gl_o[...] = d(xn, wgl_ref[...])
    bcd_o[...] = d(xn, wbcd_ref[...])


def _head_tables(seq):
    half = ROT_DIM // 2
    pos = jnp.arange(seq, dtype=F32)
    inv_freq = ROPE_THETA ** (-jnp.arange(0, ROT_DIM, 2, dtype=F32) / ROT_DIM)
    ang = pos[:, None] * inv_freq[None, :]
    cos, sin = jnp.cos(ang), jnp.sin(ang)
    pad = jnp.zeros((seq, V7X_LANES - ROT_DIM), F32)
    zeros = jnp.zeros((seq, half), F32)
    c = jnp.concatenate([cos, cos, pad + 1.0], axis=1)
    s_up = jnp.concatenate([zeros, sin, pad], axis=1)
    s_dn = jnp.concatenate([-sin, zeros, pad], axis=1)
    lane = jnp.arange(V7X_LANES)[None, :]
    blk = (jnp.arange(seq) // SLC_BLK)[:, None]
    k_bias = jnp.where(lane == C_HEAD_DIM + blk, NEG_INF, 0.0).astype(F32)
    ones = (lane == C_HEAD_DIM).astype(F32)
    return c, s_up, s_dn, k_bias, ones


def _pad_heads(w, n_heads):
    width = w.shape[1] // n_heads
    w = w.reshape(w.shape[0], n_heads, width)
    return jnp.pad(w, ((0, 0), (0, 0), (0, V7X_LANES - width))).reshape(w.shape[0], n_heads * V7X_LANES)


def _in_proj1(x, gain, w_in, seq, tm=256):
    n, d = x.shape
    o = np.cumsum([0, C_WIDTH] + [KV_WIDTH] * 6 + [C_HEADS * N_BRANCH] + [D_WIDTH] * 3)
    wq = _pad_heads(w_in[:, o[0]:o[1]], C_HEADS).astype(BF16)
    wkc = w_in[:, o[1]:o[3]].astype(BF16)
    wkv = _pad_heads(w_in[:, o[3]:o[7]], 4 * C_KV_HEADS).astype(BF16)
    wgl = _pad_heads(w_in[:, o[7]:o[8]], C_KV_HEADS).astype(BF16)
    wbcd = w_in[:, o[8]:o[11]].astype(BF16)
    ws = [wq, wkc, wkv, wgl, wbcd]
    kvw = C_KV_HEADS * V7X_LANES
    widths = [C_HEADS * V7X_LANES] * 2 + [2 * KV_WIDTH] + [kvw] * 4 + [kvw, 3 * D_WIDTH]
    dts = [BF16, BF16, F32, BF16, BF16, BF16, BF16, F32, F32]
    tps = seq // tm
    tab_spec = pl.BlockSpec((tm, V7X_LANES), lambda i: (i % tps, 0))
    c, s_up, s_dn, k_bias, ones = _head_tables(seq)
    return pl.pallas_call(
        _in_proj1_body,
        grid=(n // tm,),
        in_specs=[_rows_spec(tm, d), _const_spec((1, d)), tab_spec, tab_spec, tab_spec, tab_spec,
                  _const_spec((1, V7X_LANES))] + [_const_spec(w.shape) for w in ws],
        out_specs=[_rows_spec(tm, wd) for wd in widths],
        out_shape=[jax.ShapeDtypeStruct((n, wd), dt) for wd, dt in zip(widths, dts)],
        compiler_params=_params("parallel"),
        name="in_proj1",
    )(x, gain.reshape(1, d), c, s_up, s_dn, k_bias, ones, *ws)


def _compress_body(z_ref, pk_ref, pv_ref, w1k_ref, w1v_ref, w2k_ref, w2v_ref, ko_ref, vo_ref, z_scr):
    half = CMP_STRIDE * C_HEAD_DIM
    n_rows = z_ref.shape[1] // CMP_STRIDE
    streams = ((pk_ref, w1k_ref, w2k_ref, ko_ref), (pv_ref, w1v_ref, w2v_ref, vo_ref))
    for s, (p_ref, w1_ref, w2_ref, o_ref) in enumerate(streams):
        z_scr[...] = z_ref[0, :, s * KV_WIDTH:(s + 1) * KV_WIDTH]
        every16 = [z_scr[pl.ds(l, n_rows, stride=CMP_STRIDE), :] for l in range(CMP_STRIDE)]
        for hk in range(C_KV_HEADS):
            cols = slice(hk * C_HEAD_DIM, (hk + 1) * C_HEAD_DIM)
            r = jnp.concatenate([z[:, cols] for z in every16], axis=1)
            lo = _dot(r + p_ref[:, :half], w1_ref[:half, :])
            hi = _dot(r + p_ref[:, half:], w1_ref[half:, :])
            hidden = _gelu(lo + pltpu.roll(hi, n_rows - 1, axis=0))
            o_ref[0, hk] = _dot(hidden, w2_ref[...]).astype(o_ref.dtype)


def _compress(kcvc, pos_k, pos_v, w1k, w1v, w2k, w2v, bsz, seq):
    n_rows = seq // CMP_STRIDE
    width = kcvc.shape[1]
    consts = [pos_k.reshape(1, -1), pos_v.reshape(1, -1), w1k.astype(BF16), w1v.astype(BF16),
              _pad_heads(w2k, 1).astype(BF16), _pad_heads(w2v, 1).astype(BF16)]
    o_spec = pl.BlockSpec((1, C_KV_HEADS, n_rows, V7X_LANES), lambda b: (b, 0, 0, 0))
    o_shape = jax.ShapeDtypeStruct((bsz, C_KV_HEADS, n_rows, V7X_LANES), BF16)
    return pl.pallas_call(
        _compress_body,
        grid=(bsz,),
        in_specs=[pl.BlockSpec((1, seq, width), lambda b: (b, 0, 0))] + [_const_spec(c.shape) for c in consts],
        out_specs=[o_spec, o_spec],
        out_shape=[o_shape, o_shape],
        scratch_shapes=[pltpu.VMEM((seq, KV_WIDTH), F32)],
        compiler_params=_params("parallel"),
        name="nsa_compress",
    )(kcvc.reshape(bsz, seq, width), *consts)


SEL_CHUNK = 512


def _stack_heads(blk):
    return jnp.concatenate(_lane_blocks(blk), axis=0)


def _rows4(x):
    return jnp.concatenate([x] * C_GROUP, axis=0)


def _nsa_body(qn_ref, qr_ref, gl_ref, kc_ref, vc_ref, ks_ref, vs_ref, kw_ref, vw_ref, ovl_ref, place_ref,
              o_ref):
    n_cmp = kc_ref.shape[2]
    n_slc = ovl_ref.shape[0]
    hd = C_HEAD_DIM
    qb = pl.program_id(2)
    s0 = qb * NSA_TQ
    t_pos = s0 + lax.broadcasted_iota(jnp.int32, (NSA_TQ, 1), 0)
    qn4 = _stack_heads(qn_ref[0])
    qr4 = _stack_heads(qr_ref[0])

    cmp_end = lax.broadcasted_iota(jnp.int32, (1, n_cmp), 1) * CMP_STRIDE + (CMP_LEN - 1)
    pc = _masked_softmax(_dot_nt(qn4, kc_ref[0, 0]), _rows4(cmp_end <= t_pos))
    o_c = _dot(pc, vc_ref[0, 0])
    pc_sum = (pc[:NSA_TQ] + pc[NSA_TQ:2 * NSA_TQ]) + (pc[2 * NSA_TQ:3 * NSA_TQ] + pc[3 * NSA_TQ:])

    p_hi, p_lo = _split_bf16(pc_sum)
    ovl = ovl_ref[...]
    imp = _dot_nt(ovl, p_hi) + _dot_nt(ovl, p_lo)
    jb = lax.broadcasted_iota(jnp.int32, (n_slc, 1), 0)
    t_row = s0 + lax.broadcasted_iota(jnp.int32, (1, NSA_TQ), 1)
    cur = t_row // SLC_BLK
    forced = (jb == 0) | (jb == cur) | (jb == cur - 1)
    imp = jnp.where(jb * SLC_BLK <= t_row, imp + jnp.where(forced, FORCE_BONUS, 0.0), NEG_INF)
    rank = jnp.zeros((n_slc, NSA_TQ), F32)
    for k in range(n_slc):
        ck = imp[k:k + 1, :]
        beats = (ck > imp) | ((ck == imp) & (jb > k))
        rank = rank + jnp.where(beats, 1.0, 0.0)
    not_sel = jnp.where(rank < float(min(SEL_TOPK, n_slc)), 0.0, 1.0).astype(BF16)
    not_sel_q = lax.dot_general(not_sel, place_ref[...], (((0,), (0,)), ((), ())),
                                preferred_element_type=F32)
    q_sel = qr4 + _rows4(not_sel_q.astype(BF16))

    def sel_chunk(c, m, acc, causal):
        k0 = pl.multiple_of(c * SEL_CHUNK, SEL_CHUNK)
        s = _dot_nt(q_sel, ks_ref[0, pl.ds(k0, SEL_CHUNK), :])
        if causal:
            key = k0 + lax.broadcasted_iota(jnp.int32, (1, SEL_CHUNK), 1)
            s = s + _rows4(jnp.where(key <= t_pos, 0.0, NEG_INF))
        m_new = jnp.maximum(m, jnp.max(s, axis=-1, keepdims=True))
        p = jnp.exp(s - m_new)
        acc = jnp.exp(m - m_new) * acc + _dot(p, vs_ref[0, pl.ds(k0, SEL_CHUNK), :])
        return m_new, acc

    rows = C_GROUP * NSA_TQ
    last = qb // (SEL_CHUNK // NSA_TQ)
    m, acc = lax.fori_loop(0, last, lambda c, ma: sel_chunk(c, ma[0], ma[1], False),
                           (jnp.full((rows, 1), NEG_INF, F32), jnp.zeros((rows, V7X_LANES), F32)))
    _, acc = sel_chunk(last, m, acc, True)
    o_s = acc[:, :hd] / jnp.maximum(acc[:, hd:hd + 1], 1e-30)

    band = NSA_TQ + WIN
    w0 = pl.multiple_of(jnp.maximum(s0 - WIN, 0), NSA_TQ)
    diff = t_pos - (w0 + lax.broadcasted_iota(jnp.int32, (1, band), 1))
    s = _dot_nt(qr4, kw_ref[0, pl.ds(w0, band), :]) + _rows4(
        jnp.where((diff >= 0) & (diff < WIN), 0.0, NEG_INF))
    p = jnp.exp(s - jnp.max(s, axis=-1, keepdims=True))
    acc = _dot(p, vw_ref[0, pl.ds(w0, band), :])
    o_w = acc[:, :hd] / jnp.maximum(acc[:, hd:hd + 1], 1e-30)

    gate = _sigmoid(gl_ref[0])
    outs = []
    for g in range(C_GROUP):
        r = slice(g * NSA_TQ, (g + 1) * NSA_TQ)
        gc = g * N_BRANCH
        outs.append(gate[:, gc:gc + 1] * o_c[r, :hd] + gate[:, gc + 1:gc + 2] * o_s[r]
                    + gate[:, gc + 2:gc + 3] * o_w[r])
    o_ref[0] = jnp.concatenate(outs, axis=1).astype(o_ref.dtype)


def _nsa(qn, qr, gl, k_cmp, v_cmp, ks, vs, kw, vw, bsz, seq):
    n_cmp = seq // CMP_STRIDE
    n_slc = seq // SLC_BLK
    assert C_HEAD_DIM + n_slc <= V7X_LANES and seq % SEL_CHUNK == 0
    ci = np.arange(n_cmp)[None, :] * CMP_STRIDE
    sj = np.arange(n_slc)[:, None] * SLC_BLK
    overlap_t = jnp.asarray(((ci < sj + SLC_BLK) & (ci + CMP_LEN > sj)).astype(np.float32)).astype(BF16)
    place = jnp.asarray(np.eye(n_slc, V7X_LANES, k=C_HEAD_DIM, dtype=np.float32)).astype(BF16)
    gw = C_GROUP * V7X_LANES
    q_spec = pl.BlockSpec((1, NSA_TQ, gw), lambda b, h, i: (b, i, h))
    gl_spec = pl.BlockSpec((1, NSA_TQ, V7X_LANES), lambda b, h, i: (b, i, h))
    cmp_spec = pl.BlockSpec((1, 1, n_cmp, V7X_LANES), lambda b, h, i: (b, h, 0, 0))
    kv_spec = pl.BlockSpec((1, seq, V7X_LANES), lambda b, h, i: (b, 0, h))
    as3 = lambda z: z.reshape(bsz, seq, z.shape[-1])
    return pl.pallas_call(
        _nsa_body,
        grid=(bsz, C_KV_HEADS, seq // NSA_TQ),
        in_specs=[q_spec, q_spec, gl_spec, cmp_spec, cmp_spec, kv_spec, kv_spec, kv_spec, kv_spec,
                  _const_spec((n_slc, n_cmp)), _const_spec((n_slc, V7X_LANES))],
        out_specs=pl.BlockSpec((1, NSA_TQ, C_GROUP * C_HEAD_DIM), lambda b, h, i: (b, i, h)),
        out_shape=jax.ShapeDtypeStruct((bsz, seq, C_WIDTH), BF16),
        compiler_params=_params("parallel", "parallel", "arbitrary"),
        name="nsa_attention",
    )(as3(qn), as3(qr), as3(gl), k_cmp, v_cmp, as3(ks), as3(vs), as3(kw), as3(vw), overlap_t, place)


ROW_CHUNKS = D_MODEL // V7X_LANES
assert ROW_CHUNKS == V7X_SUBLANES
DMA_UNROLL = 8
PAD_SPANS = N_EXPERTS + 1


def _tile_rows(n_rows):
    return (n_rows * ROW_CHUNKS, V7X_LANES)


def _row_tile(ref, r):
    start = r * ROW_CHUNKS
    if not isinstance(start, int):
        start = pl.multiple_of(start, ROW_CHUNKS)
    return ref.at[pl.ds(start, ROW_CHUNKS), :]


def _store_row_tiles(ref, x):
    for c in range(ROW_CHUNKS):
        ref[pl.ds(c, x.shape[0], stride=ROW_CHUNKS), :] = x[:, c * V7X_LANES:(c + 1) * V7X_LANES]


def _load_row_tiles(ref):
    rows = ref.shape[0] // ROW_CHUNKS
    return jnp.concatenate([ref[pl.ds(c, rows, stride=ROW_CHUNKS), :] for c in range(ROW_CHUNKS)], axis=1)


def _route(hn, wr_ref, br_ref, tri_ref, cnt_scr):
    logits = _dot_f32(hn, wr_ref[...]) + br_ref[...]
    lane = lax.broadcasted_iota(jnp.int32, logits.shape, 1)
    m1 = jnp.max(logits, axis=-1, keepdims=True)
    i1 = jnp.min(jnp.where(logits == m1, lane, V7X_LANES), axis=-1, keepdims=True)
    rest = jnp.where(lane == i1, NEG_INF, logits)
    m2 = jnp.max(rest, axis=-1, keepdims=True)
    i2 = jnp.min(jnp.where(rest == m2, lane, V7X_LANES), axis=-1, keepdims=True)
    e2 = jnp.exp(m2 - m1)
    den = 1.0 + e2
    hit1, hit2 = lane == i1, lane == i2
    hits = jnp.where(hit1 | hit2, 1.0, 0.0)
    before = jnp.dot(tri_ref[...], hits.astype(BF16), preferred_element_type=F32) + cnt_scr[...]
    r1 = jnp.sum(jnp.where(hit1, before, 0.0), axis=-1, keepdims=True).astype(jnp.int32)
    r2 = jnp.sum(jnp.where(hit2, before, 0.0), axis=-1, keepdims=True).astype(jnp.int32)
    cnt_scr[...] = cnt_scr[...] + jnp.sum(hits, axis=0, keepdims=True)
    idx = jnp.where(lane == 0, i1, jnp.where(lane == 1, i2, jnp.where(lane == 2, r1,
                    jnp.where(lane == 3, r2, 0))))
    gate = jnp.where(lane == 0, 1.0 / den, jnp.where(lane == 1, e2 / den, 0.0))
    return idx, gate


def _row_copy(src, dst, sem):
    return pltpu.make_async_copy(src, dst, sem)


def _dispatch_body(pad_ref, dest_ref, h_ref, g_ref, xs_ref, hn_ref, zero_scr, sem):
    _store_row_tiles(hn_ref, _rms(h_ref[...], g_ref[...]))

    @pl.when(pl.program_id(0) == 0)
    def _():
        zero_scr[...] = jnp.zeros_like(zero_scr)
        for e in range(PAD_SPANS):
            start, count = pad_ref[e], pad_ref[PAD_SPANS + e]

            def fill(r, c):
                _row_copy(zero_scr, _row_tile(xs_ref, start + r), sem).start()
                return c

            def filled(r, c):
                _row_copy(zero_scr, _row_tile(xs_ref, 0), sem).wait()
                return c

            lax.fori_loop(0, count, fill, 0)
            lax.fori_loop(0, count, filled, 0)

    def issue(blk, c):
        for u in range(DMA_UNROLL):
            r = blk * DMA_UNROLL + u
            for k in range(TOP_K):
                _row_copy(_row_tile(hn_ref, r), _row_tile(xs_ref, dest_ref[TOP_K * r + k]),
                          sem).start(priority=k)
        return c

    lax.fori_loop(0, DMA_ROWS // DMA_UNROLL, issue, 0)

    def drain(r, c):
        for k in range(TOP_K):
            _row_copy(_row_tile(hn_ref, 0), _row_tile(xs_ref, 0), sem).wait()
        return c

    lax.fori_loop(0, DMA_ROWS, drain, 0)


def _dispatch(pad_info, dest, h, gain, p_rows):
    n = h.shape[0]
    return pl.pallas_call(
        _dispatch_body,
        grid_spec=pltpu.PrefetchScalarGridSpec(
            num_scalar_prefetch=1,
            grid=(n // DMA_ROWS,),
            in_specs=[pl.BlockSpec((TOP_K * DMA_ROWS,), lambda i, pad: (i,), memory_space=pltpu.SMEM),
                      pl.BlockSpec((DMA_ROWS, D_MODEL), lambda i, pad: (i, 0)),
                      pl.BlockSpec((1, D_MODEL), lambda i, pad: (0, 0))],
            out_specs=pl.BlockSpec(memory_space=pl.ANY),
            scratch_shapes=[pltpu.VMEM(_tile_rows(DMA_ROWS), F32), pltpu.VMEM(_tile_rows(1), F32),
                            pltpu.SemaphoreType.DMA(())]),
        out_shape=jax.ShapeDtypeStruct(_tile_rows(p_rows), F32),
        compiler_params=_params("arbitrary"),
        name="moe_dispatch",
    )(pad_info, dest, h, gain.reshape(1, -1))


def _experts_body(blk_e_ref, n_used_ref, x_ref, wg_ref, wu_ref, wd_ref, o_ref):
    i = pl.program_id(0)

    @pl.when(i < n_used_ref[0])
    def _():
        x = _load_row_tiles(x_ref).astype(BF16)
        gate = jnp.dot(x, wg_ref[0], preferred_element_type=F32)
        up = jnp.dot(x, wu_ref[0], preferred_element_type=F32)
        act = (gate * _sigmoid(gate)) * up
        _store_row_tiles(o_ref, _dot(act, wd_ref[0]))

    @pl.when(i >= n_used_ref[0])
    def _():
        o_ref[...] = jnp.zeros_like(o_ref)


def _experts(blk_e, n_used, xs, wg, wu, wd):
    p_rows = xs.shape[0] // ROW_CHUNKS
    x_spec = pl.BlockSpec(_tile_rows(MOE_ROWS), lambda i, be, nu: (jnp.minimum(i, nu[0] - 1), 0))
    o_spec = pl.BlockSpec(_tile_rows(MOE_ROWS), lambda i, be, nu: (i, 0))
    w_spec = lambda w: pl.BlockSpec((1,) + w.shape[1:], lambda i, be, nu: (be[i], 0, 0))
    return pl.pallas_call(
        _experts_body,
        grid_spec=pltpu.PrefetchScalarGridSpec(
            num_scalar_prefetch=2,
            grid=(p_rows // MOE_ROWS,),
            in_specs=[x_spec, w_spec(wg), w_spec(wu), w_spec(wd)],
            out_specs=o_spec),
        out_shape=jax.ShapeDtypeStruct(_tile_rows(p_rows), F32),
        compiler_params=_params("arbitrary"),
        name="moe_experts",
    )(blk_e, n_used, xs, wg, wu, wd)


def _combine_body(dest_ref, h_ref, gate_ref, g_ref, yb_ref, o_ref, y_scr, sem):
    tm = h_ref.shape[0]

    def issue(blk, c):
        for u in range(DMA_UNROLL):
            t = blk * DMA_UNROLL + u
            for k in range(TOP_K):
                _row_copy(_row_tile(yb_ref, dest_ref[TOP_K * t + k]), _row_tile(y_scr.at[k], t),
                          sem).start(priority=k)
        return c

    lax.fori_loop(0, tm // DMA_UNROLL, issue, 0)

    def drain(t, c):
        for k in range(TOP_K):
            _row_copy(_row_tile(yb_ref, 0), _row_tile(y_scr.at[0], 0), sem).wait()
        return c

    lax.fori_loop(0, tm, drain, 0)
    gate = gate_ref[...]
    moe = gate[:, 0:1] * _load_row_tiles(y_scr.at[0]) + gate[:, 1:2] * _load_row_tiles(y_scr.at[1])
    o_ref[...] = _rms(h_ref[...] + moe, g_ref[...])


def _combine(dest, h, gates, gain, yb_rows, tm=256):
    n = h.shape[0]
    return pl.pallas_call(
        _combine_body,
        grid=(n // tm,),
        in_specs=[pl.BlockSpec((TOP_K * tm,), lambda i: (i,), memory_space=pltpu.SMEM),
                  _rows_spec(tm, D_MODEL), _rows_spec(tm, V7X_LANES), _const_spec((1, D_MODEL)),
                  pl.BlockSpec(memory_space=pl.ANY)],
        out_specs=_rows_spec(tm, D_MODEL),
        out_shape=jax.ShapeDtypeStruct((n, D_MODEL), F32),
        scratch_shapes=[pltpu.VMEM((TOP_K,) + _tile_rows(tm), F32), pltpu.SemaphoreType.DMA(())],
        compiler_params=_params("arbitrary"),
        name="moe_combine_norm",
    )(dest, h, gates, gain.reshape(1, -1), yb_rows)


def _moe_layout(idx, counts):
    n = idx.shape[0]
    nk = n * TOP_K
    counts = counts[0, :N_EXPERTS].astype(jnp.int32)
    padded = ((counts + MOE_ROWS - 1) // MOE_ROWS) * MOE_ROWS
    p_end = jnp.cumsum(padded)
    p_start = p_end - padded
    experts = jnp.arange(N_EXPERTS, dtype=jnp.int32)[None, None, :]
    first = jnp.sum(jnp.where(idx[:, :TOP_K, None] == experts, p_start[None, None, :], 0), axis=-1)
    dest = (first + idx[:, TOP_K:2 * TOP_K]).reshape(nk)
    n_blk = (nk + MOE_ROWS - 1) // MOE_ROWS + N_EXPERTS
    blk_e = jnp.minimum(jnp.searchsorted(p_end, jnp.arange(n_blk) * MOE_ROWS, side="right"),
                        N_EXPERTS - 1).astype(jnp.int32)
    n_used = (p_end[-1:] // MOE_ROWS).astype(jnp.int32)
    p_rows = n_blk * MOE_ROWS
    pad_info = jnp.concatenate([p_start + counts, p_end[-1:], padded - counts,
                                p_rows - p_end[-1:]]).astype(jnp.int32)
    return dest.astype(jnp.int32), blk_e, n_used, pad_info, p_rows


def _moe_final(h, idx, gates, counts, norm_g, wg, wu, wd, final_g):
    dest, blk_e, n_used, pad_info, p_rows = _moe_layout(idx, counts)
    xs = _dispatch(pad_info, dest, h, norm_g, p_rows)
    yb = _experts(blk_e, n_used, xs, wg.astype(BF16), wu.astype(BF16), wd.astype(BF16))
    return _combine(dest, h, gates, final_g, yb)


def kernel(x, e_norm_mix, e_w_in, sgu_ln_g, sgu_ln_b, sgu_w, sgu_b, rwkv_mu, rwkv_w0, rwkv_w2,
           rwkv_a0, rwkv_a2, rwkv_g2, rwkv_k_k, rwkv_k_a, rwkv_r_k, rwkv_gn_g, rwkv_gn_b, e_w_out,
           e_norm_ffn, ffn_w_gate, ffn_w_up, ffn_w_down, o_norm_mix, o_w_in, nsa_cmp_pos_k,
           nsa_cmp_pos_v, nsa_cmp_k_w1, nsa_cmp_k_w2, nsa_cmp_v_w1, nsa_cmp_v_w2, conv_w, o_w_out,
           o_norm_ffn, moe_router, moe_router_b, moe_w_gate, moe_w_up, moe_w_down, final_norm):
    bsz, seq, d = x.shape
    n = bsz * seq
    h = x.reshape(n, d)

    ya, scan_ops, g = _layer0_front(h, seq, e_norm_mix[0], e_w_in[0].astype(BF16), sgu_ln_g[0], sgu_ln_b[0],
                                    sgu_w[0], sgu_b[0], rwkv_mu[0], rwkv_w0[0], rwkv_w2[0], rwkv_a0[0],
                                    rwkv_a2[0], rwkv_g2[0])
    rwka, v = _to_scan_layout(scan_ops, bsz, seq)
    ys = _rwkv_scan(rwka, v, rwkv_k_k[0], rwkv_k_a[0], rwkv_r_k[0], rwkv_gn_g[0], rwkv_gn_b[0], bsz, seq)
    h = _layer0_back(h, ya, _from_scan_layout(ys, bsz, seq), g, e_w_out[0].astype(BF16), e_norm_ffn[0],
                     ffn_w_gate[0].astype(BF16), ffn_w_up[0].astype(BF16), ffn_w_down[0].astype(BF16))

    qn, qr, kcvc, ks, vs, kw, vw, gl, bcd = _in_proj1(h, o_norm_mix[0], o_w_in[0], seq)
    k_cmp, v_cmp = _compress(kcvc, nsa_cmp_pos_k[0], nsa_cmp_pos_v[0], nsa_cmp_k_w1[0], nsa_cmp_v_w1[0],
                             nsa_cmp_k_w2[0], nsa_cmp_v_w2[0], bsz, seq)
    yc = _nsa(qn, qr, gl, k_cmp, v_cmp, ks, vs, kw, vw, bsz, seq).reshape(n, C_WIDTH)
    h, idx, gates, counts = _layer1_mid(h, yc, bcd, conv_w[0], o_w_out[0].astype(BF16), seq, o_norm_ffn[0],
                                        moe_router[0], moe_router_b[0])
    out = _moe_final(h, idx, gates, counts, o_norm_ffn[0], moe_w_gate[0], moe_w_up[0], moe_w_down[0],
                     final_norm)
    return out.reshape(bsz, seq, d)
```

```python
import functools

import jax
import jax.numpy as jnp
import numpy as np
from jax import lax
from jax.experimental import pallas as pl
from jax.experimental.pallas import tpu as pltpu

F32 = jnp.float32
BF16 = jnp.bfloat16

D_MODEL = 1024
A_GROUPS = 4
A_GROUP_DIM = 128
A_WIDTH = A_GROUPS * A_GROUP_DIM
CHUNK = 128
SGU_LN_EPS = 1e-5
B_HEADS = 8
B_HEAD_DIM = 64
B_WIDTH = B_HEADS * B_HEAD_DIM
DECAY_LORA = 64
ICLR_LORA = 64
GATE_LORA = 128
B_IN = 3 * B_WIDTH + DECAY_LORA + ICLR_LORA + GATE_LORA
RWKV_GN_EPS = 64e-5
C_HEADS = 8
C_KV_HEADS = 2
C_GROUP = C_HEADS // C_KV_HEADS
C_HEAD_DIM = 64
C_WIDTH = C_HEADS * C_HEAD_DIM
KV_WIDTH = C_KV_HEADS * C_HEAD_DIM
N_BRANCH = 3
CMP_LEN = 32
CMP_STRIDE = 16
CMP_HIDDEN = 256
SLC_BLK = 64
SEL_TOPK = 8
WIN = 512
NSA_TQ = 512
ROT_DIM = C_HEAD_DIM // 4
ROPE_THETA = 500000.0
D_WIDTH = 512
CONV_W = 3
FFN_DIM = 2816
N_EXPERTS = 8
TOP_K = 2
EXPERT_DIM = 1408
NORM_EPS = 1e-6
NEG_INF = -1e30
FORCE_BONUS = 1e6

V7X_LANES = 128
V7X_SUBLANES = 8
V7X_VMEM_LIMIT = 56 * 1024 * 1024

MOE_ROWS = 512
DMA_ROWS = 512


def _params(*sem):
    return pltpu.CompilerParams(dimension_semantics=sem, vmem_limit_bytes=V7X_VMEM_LIMIT)


def _const_spec(shape):
    zeros = (0,) * len(shape)
    return pl.BlockSpec(shape, lambda *_: zeros)


def _rows_spec(tm, width):
    return pl.BlockSpec((tm, width), lambda i: (i, 0))


def _rms(x, g):
    return x * lax.rsqrt(jnp.mean(x * x, axis=-1, keepdims=True) + NORM_EPS) * g


def _gelu(x):
    return x * (0.5 * (1.0 + jnp.tanh(0.7978845608028654 * (x + 0.044715 * (x * x * x)))))


def _sigmoid(x):
    return 1.0 / (1.0 + jnp.exp(-x))


def _dot(a, b):
    return jnp.dot(a.astype(BF16), b.astype(BF16), preferred_element_type=F32)


def _dot_nt(a, b):
    return lax.dot_general(a.astype(BF16), b.astype(BF16), (((1,), (1,)), ((), ())),
                           preferred_element_type=F32)


def _split_bf16(a):
    hi = a.astype(BF16)
    lo = (a - hi.astype(F32)).astype(BF16)
    return hi, lo


def _dot_f32(a, b):
    ah, al = _split_bf16(a)
    bh, bl = _split_bf16(b)
    d = functools.partial(jnp.dot, preferred_element_type=F32)
    return d(ah, bh) + (d(al, bh) + d(ah, bl))


def _masked_softmax(s, mask):
    s = jnp.where(mask, s, NEG_INF)
    m = jnp.max(s, axis=-1, keepdims=True)
    p = jnp.where(mask, jnp.exp(s - m), 0.0)
    return p / jnp.maximum(jnp.sum(p, axis=-1, keepdims=True), 1e-30)


def _sgu_chunk(p_uv, lng_ref, lnb_ref, w_ref, b_ref):
    row = lax.broadcasted_iota(jnp.int32, (CHUNK, CHUNK), 0)
    col = lax.broadcasted_iota(jnp.int32, (CHUNK, CHUNK), 1)
    causal = col <= row
    u = _gelu(p_uv[:, :A_WIDTH])
    v = _gelu(p_uv[:, A_WIDTH:])
    outs = []
    for g in range(A_GROUPS):
        cols = slice(g * A_GROUP_DIM, (g + 1) * A_GROUP_DIM)
        vg = v[:, cols]
        mu = jnp.mean(vg, axis=-1, keepdims=True)
        dv = vg - mu
        var = jnp.mean(dv * dv, axis=-1, keepdims=True)
        vn = dv * lax.rsqrt(var + SGU_LN_EPS) * lng_ref[:, cols] + lnb_ref[:, cols]
        wm = jnp.where(causal, w_ref[g], 0.0)
        mixed = _dot(wm, vn) + b_ref[:, g:g + 1]
        outs.append(u[:, cols] * mixed)
    return jnp.concatenate(outs, axis=1)


def _softplus(x):
    return jnp.maximum(x, 0.0) + jnp.log(1.0 + jnp.exp(-jnp.abs(x)))


SCAN_OPERANDS = 5


def _layer0_front_body(x_ref, gain_ref, win_ref, lng_ref, lnb_ref, sw_ref, sb_ref, mu_ref, w0_ref, w2_ref,
                       a0_ref, a2_ref, g2_ref, ya_o, s_o, g_o, prev_scr, *, tiles_per_seq):
    tm = x_ref.shape[0]

    @pl.when(pl.program_id(0) == 0)
    def _():
        prev_scr[...] = jnp.zeros_like(prev_scr)

    p = jnp.dot(_rms(x_ref[...], gain_ref[...]).astype(BF16), win_ref[...], preferred_element_type=F32)
    for c in range(tm // CHUNK):
        rows = slice(c * CHUNK, (c + 1) * CHUNK)
        ya_o[rows, :] = _sgu_chunk(p[rows, :2 * A_WIDTH], lng_ref, lnb_ref, sw_ref, sb_ref).astype(ya_o.dtype)
    x = p[:, 2 * A_WIDTH:]
    first = (pl.program_id(0) % tiles_per_seq) == 0
    prev_row = jnp.where(first, 0.0, prev_scr[...])
    rowid = lax.broadcasted_iota(jnp.int32, (tm, 1), 0)
    shifted = jnp.where(rowid == 0, prev_row, pltpu.roll(x, 1, axis=0))
    prev_scr[...] = x[tm - 1:tm, :]
    xm = x + (shifted - x) * mu_ref[...]
    o = 3 * B_WIDTH
    wl = xm[:, o:o + DECAY_LORA]
    al = xm[:, o + DECAY_LORA:o + DECAY_LORA + ICLR_LORA]
    gl = xm[:, o + DECAY_LORA + ICLR_LORA:]
    w = -_softplus(-(w0_ref[...] + _dot(jnp.tanh(wl), w2_ref[...]))) - 0.5
    s_o[0] = xm[:, :B_WIDTH]
    s_o[1] = jnp.exp(-jnp.exp(w))
    s_o[2] = xm[:, B_WIDTH:2 * B_WIDTH]
    s_o[3] = _sigmoid(a0_ref[...] + _dot(al, a2_ref[...]))
    s_o[4] = xm[:, 2 * B_WIDTH:3 * B_WIDTH]
    g_o[...] = _dot(_sigmoid(gl), g2_ref[...])


def _layer0_front(h, seq, gain, w_in, ln_g, ln_b, w_s, b_s, mu, w0, w2, a0, a2, g2, tm=512):
    n, d = h.shape
    perm = np.arange(B_WIDTH).reshape(B_HEADS, B_HEAD_DIM).T.reshape(-1)
    r0, k0 = 2 * A_WIDTH, 2 * A_WIDTH + B_WIDTH
    cols = np.concatenate([np.arange(r0), r0 + perm, k0 + perm, np.arange(k0 + B_WIDTH, w_in.shape[1])])
    w_in = w_in[:, cols]
    mu = mu[cols[r0:] - r0]
    w0, w2, a0, a2 = w0[perm], w2[:, perm], a0[perm], a2[:, perm]
    row = lambda v: v.reshape(1, -1)
    consts = [row(gain), w_in, row(ln_g), row(ln_b), w_s, b_s.T, row(mu), row(w0), w2, row(a0), a2, g2]
    outs = [jax.ShapeDtypeStruct((n, A_WIDTH), BF16), jax.ShapeDtypeStruct((SCAN_OPERANDS, n, B_WIDTH), F32),
            jax.ShapeDtypeStruct((n, B_WIDTH), F32)]
    return pl.pallas_call(
        functools.partial(_layer0_front_body, tiles_per_seq=seq // tm),
        grid=(n // tm,),
        in_specs=[_rows_spec(tm, d)] + [_const_spec(c.shape) for c in consts],
        out_specs=[_rows_spec(tm, A_WIDTH), pl.BlockSpec((SCAN_OPERANDS, tm, B_WIDTH), lambda i: (0, i, 0)),
                   _rows_spec(tm, B_WIDTH)],
        out_shape=outs,
        scratch_shapes=[pltpu.VMEM((1, B_IN), F32)],
        compiler_params=_params("arbitrary"),
        name="layer0_front",
    )(h, *consts)


def _rwkv_scan_body(r_ref, w_ref, k0_ref, a_ref, v_ref, kkp_ref, kap_ref, rk_ref, gng_ref, gnb_ref,
                    y_ref, s_ref, kkn_ref, ka_ref, km_ref):
    n, tt = r_ref.shape[0], r_ref.shape[1]

    @pl.when(pl.program_id(0) == 0)
    def _():
        s_ref[...] = jnp.zeros_like(s_ref)

    per_dim = lambda p_ref: p_ref[...][:, None, :]
    k0 = k0_ref[...]
    a = a_ref[...]
    kk = k0 * per_dim(kkp_ref)
    kkn = kk / jnp.maximum(jnp.sqrt(jnp.sum(kk * kk, axis=0, keepdims=True)), 1e-12)
    km = k0 * (1.0 + (a - 1.0) * per_dim(kap_ref))
    kkn_ref[...] = kkn
    ka_ref[...] = kkn * a
    km_ref[...] = km

    zero = jnp.zeros((n, r_ref.shape[2]), F32)

    def sa_init(j, acc):
        return acc + s_ref[j] * kkn_ref[j, pl.ds(0, 1), :]

    sa0 = lax.fori_loop(0, n, sa_init, zero)

    def step(t, sa):
        v_t = v_ref[t]
        now = pl.ds(t, 1)
        nxt = pl.ds(jnp.minimum(t + 1, tt - 1), 1)

        def jbody(j, carry):
            y, san = carry
            sn = s_ref[j] * w_ref[j, now, :] + (v_t * km_ref[j, now, :] - sa * ka_ref[j, now, :])
            s_ref[j] = sn
            return y + sn * r_ref[j, now, :], san + sn * kkn_ref[j, nxt, :]

        y, san = lax.fori_loop(0, n, jbody, (zero, zero), unroll=8)
        y_ref[t] = y
        return san

    lax.fori_loop(0, tt, step, sa0)

    y = y_ref[...]
    ym = jnp.mean(y, axis=1, keepdims=True)
    dy = y - ym
    yv = jnp.mean(dy * dy, axis=1, keepdims=True)
    yn = dy * lax.rsqrt(yv + RWKV_GN_EPS) * gng_ref[...][None] + gnb_ref[...][None]
    bonus = jnp.sum(r_ref[...] * km * per_dim(rk_ref), axis=0)
    y_ref[...] = yn + bonus[:, None, :] * v_ref[...]


def _rwkv_scan(rwka, v, k_k, k_a, r_k, gn_g, gn_b, bsz, seq, tt=32):
    n = B_HEAD_DIM
    lanes = bsz * B_HEADS

    def lane_param(p):
        return jnp.tile(p.reshape(B_HEADS, n).T, (1, bsz))

    op_spec = lambda a: pl.BlockSpec((None, n, tt, lanes), lambda c: (a, 0, c, 0))
    slab_spec = pl.BlockSpec((tt, n, lanes), lambda c: (c, 0, 0))
    par_spec = _const_spec((n, lanes))
    return pl.pallas_call(
        _rwkv_scan_body,
        grid=(seq // tt,),
        in_specs=[op_spec(a) for a in range(SCAN_OPERANDS - 1)] + [slab_spec] + [par_spec] * 5,
        out_specs=slab_spec,
        out_shape=jax.ShapeDtypeStruct((seq, n, lanes), F32),
        scratch_shapes=[pltpu.VMEM((n, n, lanes), F32)] + [pltpu.VMEM((n, tt, lanes), F32)] * 3,
        compiler_params=_params("arbitrary"),
        name="rwkv_scan",
    )(*([rwka] * (SCAN_OPERANDS - 1)), v, lane_param(k_k), lane_param(k_a), lane_param(r_k.reshape(-1)),
      lane_param(gn_g), lane_param(gn_b))


SCAN_TT = 128
SCAN_UNROLL = 8


def _time_to_lanes(x_ref, a_ref):
    for b in range(x_ref.shape[0]):
        for blk in range(B_WIDTH // V7X_LANES):
            cols = slice(blk * V7X_LANES, (blk + 1) * V7X_LANES)
            a_ref[b, cols, :] = x_ref[b, :, cols].T


def _to_scan_rows_body(x_ref, o_ref, a_ref):
    bsz = x_ref.shape[0]
    _time_to_lanes(x_ref, a_ref)

    def dims(i, carry):
        for u in range(SCAN_UNROLL):
            d = i * SCAN_UNROLL + u
            rows = pl.ds(pl.multiple_of(d * B_HEADS, B_HEADS), B_HEADS)
            z = jnp.concatenate([a_ref[b, rows, :] for b in range(bsz)], axis=0)
            o_ref[d] = z.T
        return carry

    lax.fori_loop(0, B_HEAD_DIM // SCAN_UNROLL, dims, 0)


def _to_scan_slab_body(x_ref, o_ref, a_ref):
    bsz = x_ref.shape[0]
    _time_to_lanes(x_ref, a_ref)

    def dims(i, carry):
        for u in range(SCAN_UNROLL):
            d = i * SCAN_UNROLL + u
            z = jnp.concatenate([a_ref[b, pl.ds(d, B_HEADS, stride=B_HEAD_DIM), :] for b in range(bsz)],
                                axis=0)
            o_ref[:, d, :] = z.T
        return carry

    lax.fori_loop(0, B_HEAD_DIM // SCAN_UNROLL, dims, 0)


def _to_scan_layout(ops, bsz, seq):
    tt = min(SCAN_TT, seq)
    lanes = bsz * B_HEADS
    ops4 = ops.reshape(SCAN_OPERANDS, bsz, seq, B_WIDTH)
    scratch = [pltpu.VMEM((bsz, B_WIDTH, tt), F32)]
    rwka = pl.pallas_call(
        _to_scan_rows_body,
        grid=(SCAN_OPERANDS - 1, seq // tt),
        in_specs=[pl.BlockSpec((None, bsz, tt, B_WIDTH), lambda a, i: (a, 0, i, 0))],
        out_specs=pl.BlockSpec((None, B_HEAD_DIM, tt, lanes), lambda a, i: (a, 0, i, 0)),
        out_shape=jax.ShapeDtypeStruct((SCAN_OPERANDS - 1, B_HEAD_DIM, seq, lanes), F32),
        scratch_shapes=scratch,
        compiler_params=_params("parallel", "arbitrary"),
        name="to_scan_rows",
    )(ops4)
    v = pl.pallas_call(
        _to_scan_slab_body,
        grid=(seq // tt,),
        in_specs=[pl.BlockSpec((None, bsz, tt, B_WIDTH), lambda i: (SCAN_OPERANDS - 1, 0, i, 0))],
        out_specs=pl.BlockSpec((tt, B_HEAD_DIM, lanes), lambda i: (i, 0, 0)),
        out_shape=jax.ShapeDtypeStruct((seq, B_HEAD_DIM, lanes), F32),
        scratch_shapes=scratch,
        compiler_params=_params("arbitrary"),
        name="to_scan_slab",
    )(ops4)
    return rwka, v


def _from_scan_body(y_ref, o_ref, a_ref):
    bsz = o_ref.shape[0]

    def dims(i, carry):
        for u in range(SCAN_UNROLL):
            d = i * SCAN_UNROLL + u
            zt = y_ref[:, d, :].T
            for b in range(bsz):
                a_ref[b, pl.ds(d, B_HEADS, stride=B_HEAD_DIM), :] = zt[b * B_HEADS:(b + 1) * B_HEADS, :]
        return carry

    lax.fori_loop(0, B_HEAD_DIM // SCAN_UNROLL, dims, 0)
    for b in range(bsz):
        for blk in range(B_WIDTH // V7X_LANES):
            cols = slice(blk * V7X_LANES, (blk + 1) * V7X_LANES)
            o_ref[b, :, cols] = a_ref[b, cols, :].T


def _from_scan_layout(y, bsz, seq):
    tt = min(SCAN_TT, seq)
    return pl.pallas_call(
        _from_scan_body,
        grid=(seq // tt,),
        in_specs=[pl.BlockSpec((tt, B_HEAD_DIM, bsz * B_HEADS), lambda i: (i, 0, 0))],
        out_specs=pl.BlockSpec((bsz, tt, B_WIDTH), lambda i: (0, i, 0)),
        out_shape=jax.ShapeDtypeStruct((bsz, seq, B_WIDTH), F32),
        scratch_shapes=[pltpu.VMEM((bsz, B_WIDTH, tt), F32)],
        compiler_params=_params("arbitrary"),
        name="from_scan_layout",
    )(y).reshape(bsz * seq, B_WIDTH)


def _layer1_mid_body(h_ref, yc_ref, bcd_ref, prev_ref, cw_ref, w_ref, g_ref, wr_ref, br_ref, tri_ref,
                     o_ref, idx_o, gate_o, cnt_o, cnt_scr, *, tiles_per_seq):
    tm = h_ref.shape[0]

    @pl.when(pl.program_id(0) == 0)
    def _():
        cnt_scr[...] = jnp.zeros_like(cnt_scr)

    first = (pl.program_id(0) % tiles_per_seq) == 0
    z = bcd_ref[:, D_WIDTH:2 * D_WIDTH] * bcd_ref[:, 2 * D_WIDTH:]
    zp = jnp.where(first, 0.0, prev_ref[:, D_WIDTH:2 * D_WIDTH] * prev_ref[:, 2 * D_WIDTH:])
    rowid = lax.broadcasted_iota(jnp.int32, (tm, 1), 0)
    z1 = jnp.where(rowid == 0, zp[7:8, :], pltpu.roll(z, 1, axis=0))
    z2 = pltpu.roll(z, 2, axis=0)
    z2 = jnp.where(rowid == 0, zp[6:7, :], jnp.where(rowid == 1, zp[7:8, :], z2))
    y = cw_ref[0:1, :] * z2 + cw_ref[1:2, :] * z1 + cw_ref[2:3, :] * z
    yd = bcd_ref[:, :D_WIDTH] * y
    h_new = h_ref[...] + (_dot(yc_ref[...], w_ref[:C_WIDTH, :]) + _dot(yd, w_ref[C_WIDTH:, :]))
    o_ref[...] = h_new
    idx_o[...], gate_o[...] = _route(_rms(h_new, g_ref[...]), wr_ref, br_ref, tri_ref, cnt_scr)
    cnt_o[...] = cnt_scr[...]


def _layer1_mid(h, yc, bcd, conv_w, w_out, seq, gain, w_router, b_router, tm=512):
    n = h.shape[0]
    per8 = tm // V7X_SUBLANES
    wr = jnp.pad(w_router, ((0, 0), (0, V7X_LANES - N_EXPERTS)))
    br = jnp.pad(b_router.reshape(1, -1), ((0, 0), (0, V7X_LANES - N_EXPERTS)), constant_values=NEG_INF)
    tri = jnp.asarray(np.tril(np.ones((tm, tm), np.float32), k=-1)).astype(BF16)
    return pl.pallas_call(
        functools.partial(_layer1_mid_body, tiles_per_seq=seq // tm),
        grid=(n // tm,),
        in_specs=[_rows_spec(tm, D_MODEL), _rows_spec(tm, C_WIDTH), _rows_spec(tm, 3 * D_WIDTH),
                  pl.BlockSpec((V7X_SUBLANES, 3 * D_WIDTH), lambda i: (jnp.maximum(i * per8 - 1, 0), 0)),
                  _const_spec((CONV_W, D_WIDTH)), _const_spec(w_out.shape), _const_spec((1, D_MODEL)),
                  _const_spec(wr.shape), _const_spec(br.shape), _const_spec(tri.shape)],
        out_specs=[_rows_spec(tm, D_MODEL), _rows_spec(tm, V7X_LANES), _rows_spec(tm, V7X_LANES),
                   _const_spec((1, V7X_LANES))],
        out_shape=[jax.ShapeDtypeStruct((n, D_MODEL), F32), jax.ShapeDtypeStruct((n, V7X_LANES), jnp.int32),
                   jax.ShapeDtypeStruct((n, V7X_LANES), F32), jax.ShapeDtypeStruct((1, V7X_LANES), F32)],
        scratch_shapes=[pltpu.VMEM((1, V7X_LANES), F32)],
        compiler_params=_params("arbitrary"),
        name="layer1_mid",
    )(h, yc, bcd, bcd, conv_w, w_out, gain.reshape(1, -1), wr, br, tri)


def _layer0_back_body(h_ref, ya_ref, ys_ref, gm_ref, wo_ref, g_ref, wg_ref, wu_ref, wd_ref, o_ref):
    yb = ys_ref[...] * gm_ref[...]
    h = h_ref[...] + (_dot(ya_ref[...], wo_ref[:A_WIDTH, :]) + _dot(yb, wo_ref[A_WIDTH:, :]))
    hn = _rms(h, g_ref[...]).astype(BF16)
    gate = jnp.dot(hn, wg_ref[...], preferred_element_type=F32)
    up = jnp.dot(hn, wu_ref[...], preferred_element_type=F32)
    act = (gate * _sigmoid(gate)) * up
    o_ref[...] = h + _dot(act, wd_ref[...])


def _layer0_back(h, ya, ys, gm, w_out, gain, wg, wu, wd, tm=512):
    n = h.shape[0]
    return pl.pallas_call(
        _layer0_back_body,
        grid=(n // tm,),
        in_specs=[_rows_spec(tm, D_MODEL), _rows_spec(tm, A_WIDTH), _rows_spec(tm, B_WIDTH),
                  _rows_spec(tm, B_WIDTH), _const_spec(w_out.shape), _const_spec((1, D_MODEL)),
                  _const_spec(wg.shape), _const_spec(wu.shape), _const_spec(wd.shape)],
        out_specs=_rows_spec(tm, D_MODEL),
        out_shape=jax.ShapeDtypeStruct((n, D_MODEL), F32),
        compiler_params=_params("parallel"),
        name="layer0_back",
    )(h, ya, ys, gm, w_out, gain.reshape(1, -1), wg, wu, wd)


def _rope(x, c, s_up, s_dn):
    half = ROT_DIM // 2
    return x * c + pltpu.roll(x, half, axis=1) * s_up + pltpu.roll(x, V7X_LANES - half, axis=1) * s_dn


def _lane_blocks(x):
    return [x[:, i * V7X_LANES:(i + 1) * V7X_LANES] for i in range(x.shape[1] // V7X_LANES)]


def _in_proj1_body(x_ref, g_ref, c_ref, su_ref, sd_ref, kb_ref, one_ref, wq_ref, wkc_ref, wkv_ref,
                   wgl_ref, wbcd_ref, qn_o, qr_o, kc_o, ks_o, vs_o, kw_o, vw_o, gl_o, bcd_o):
    xn = _rms(x_ref[...], g_ref[...]).astype(BF16)
    d = functools.partial(jnp.dot, preferred_element_type=F32)
    c, su, sd = c_ref[...], su_ref[...], sd_ref[...]
    rope = lambda z: _rope(z, c, su, sd)
    q = d(xn, wq_ref[...]) * (C_HEAD_DIM ** -0.5)
    qn_o[...] = q.astype(BF16)
    qr_o[...] = jnp.concatenate([rope(z) for z in _lane_blocks(q)], axis=1).astype(BF16)
    kc_o[...] = d(xn, wkc_ref[...])
    kv = _lane_blocks(d(xn, wkv_ref[...]))
    hk = C_KV_HEADS
    ks_o[...] = jnp.concatenate([rope(z) + kb_ref[...] for z in kv[:hk]], axis=1).astype(BF16)
    vs_o[...] = jnp.concatenate([z + one_ref[...] for z in kv[hk:2 * hk]], axis=1).astype(BF16)
    kw_o[...] = jnp.concatenate([rope(z) for z in kv[2 * hk:3 * hk]], axis=1).astype(BF16)
    vw_o[...] = jnp.concatenate([z + one_ref[...] for z in kv[3 * hk:]], axis=1).astype(BF16)
    gl_o[...] = d(xn, wgl_ref[...])
    bcd_o[...] = d(xn, wbcd_ref[...])


def _head_tables(seq):
    half = ROT_DIM // 2
    pos = jnp.arange(seq, dtype=F32)
    inv_freq = ROPE_THETA ** (-jnp.arange(0, ROT_DIM, 2, dtype=F32) / ROT_DIM)
    ang = pos[:, None] * inv_freq[None, :]
    cos, sin = jnp.cos(ang), jnp.sin(ang)
    pad = jnp.zeros((seq, V7X_LANES - ROT_DIM), F32)
    zeros = jnp.zeros((seq, half), F32)
    c = jnp.concatenate([cos, cos, pad + 1.0], axis=1)
    s_up = jnp.concatenate([zeros, sin, pad], axis=1)
    s_dn = jnp.concatenate([-sin, zeros, pad], axis=1)
    lane = jnp.arange(V7X_LANES)[None, :]
    blk = (jnp.arange(seq) // SLC_BLK)[:, None]
    k_bias = jnp.where(lane == C_HEAD_DIM + blk, NEG_INF, 0.0).astype(F32)
    ones = (lane == C_HEAD_DIM).astype(F32)
    return c, s_up, s_dn, k_bias, ones


def _pad_heads(w, n_heads):
    width = w.shape[1] // n_heads
    w = w.reshape(w.shape[0], n_heads, width)
    return jnp.pad(w, ((0, 0), (0, 0), (0, V7X_LANES - width))).reshape(w.shape[0], n_heads * V7X_LANES)


def _in_proj1(x, gain, w_in, seq, tm=256):
    n, d = x.shape
    o = np.cumsum([0, C_WIDTH] + [KV_WIDTH] * 6 + [C_HEADS * N_BRANCH] + [D_WIDTH] * 3)
    wq = _pad_heads(w_in[:, o[0]:o[1]], C_HEADS).astype(BF16)
    wkc = w_in[:, o[1]:o[3]].astype(BF16)
    wkv = _pad_heads(w_in[:, o[3]:o[7]], 4 * C_KV_HEADS).astype(BF16)
    wgl = _pad_heads(w_in[:, o[7]:o[8]], C_KV_HEADS).astype(BF16)
    wbcd = w_in[:, o[8]:o[11]].astype(BF16)
    ws = [wq, wkc, wkv, wgl, wbcd]
    kvw = C_KV_HEADS * V7X_LANES
    widths = [C_HEADS * V7X_LANES] * 2 + [2 * KV_WIDTH] + [kvw] * 4 + [kvw, 3 * D_WIDTH]
    dts = [BF16, BF16, F32, BF16, BF16, BF16, BF16, F32, F32]
    tps = seq // tm
    tab_spec = pl.BlockSpec((tm, V7X_LANES), lambda i: (i % tps, 0))
    c, s_up, s_dn, k_bias, ones = _head_tables(seq)
    return pl.pallas_call(
        _in_proj1_body,
        grid=(n // tm,),
        in_specs=[_rows_spec(tm, d), _const_spec((1, d)), tab_spec, tab_spec, tab_spec, tab_spec,
                  _const_spec((1, V7X_LANES))] + [_const_spec(w.shape) for w in ws],
        out_specs=[_rows_spec(tm, wd) for wd in widths],
        out_shape=[jax.ShapeDtypeStruct((n, wd), dt) for wd, dt in zip(widths, dts)],
        compiler_params=_params("parallel"),
        name="in_proj1",
    )(x, gain.reshape(1, d), c, s_up, s_dn, k_bias, ones, *ws)


def _compress_body(z_ref, pk_ref, pv_ref, w1k_ref, w1v_ref, w2k_ref, w2v_ref, ko_ref, vo_ref, z_scr):
    half = CMP_STRIDE * C_HEAD_DIM
    n_rows = z_ref.shape[1] // CMP_STRIDE
    streams = ((pk_ref, w1k_ref, w2k_ref, ko_ref), (pv_ref, w1v_ref, w2v_ref, vo_ref))
    for s, (p_ref, w1_ref, w2_ref, o_ref) in enumerate(streams):
        z_scr[...] = z_ref[0, :, s * KV_WIDTH:(s + 1) * KV_WIDTH]
        every16 = [z_scr[pl.ds(l, n_rows, stride=CMP_STRIDE), :] for l in range(CMP_STRIDE)]
        for hk in range(C_KV_HEADS):
            cols = slice(hk * C_HEAD_DIM, (hk + 1) * C_HEAD_DIM)
            r = jnp.concatenate([z[:, cols] for z in every16], axis=1)
            lo = _dot(r + p_ref[:, :half], w1_ref[:half, :])
            hi = _dot(r + p_ref[:, half:], w1_ref[half:, :])
            hidden = _gelu(lo + pltpu.roll(hi, n_rows - 1, axis=0))
            o_ref[0, hk] = _dot(hidden, w2_ref[...]).astype(o_ref.dtype)


def _compress(kcvc, pos_k, pos_v, w1k, w1v, w2k, w2v, bsz, seq):
    n_rows = seq // CMP_STRIDE
    width = kcvc.shape[1]
    consts = [pos_k.reshape(1, -1), pos_v.reshape(1, -1), w1k.astype(BF16), w1v.astype(BF16),
              _pad_heads(w2k, 1).astype(BF16), _pad_heads(w2v, 1).astype(BF16)]
    o_spec = pl.BlockSpec((1, C_KV_HEADS, n_rows, V7X_LANES), lambda b: (b, 0, 0, 0))
    o_shape = jax.ShapeDtypeStruct((bsz, C_KV_HEADS, n_rows, V7X_LANES), BF16)
    return pl.pallas_call(
        _compress_body,
        grid=(bsz,),
        in_specs=[pl.BlockSpec((1, seq, width), lambda b: (b, 0, 0))] + [_const_spec(c.shape) for c in consts],
        out_specs=[o_spec, o_spec],
        out_shape=[o_shape, o_shape],
        scratch_shapes=[pltpu.VMEM((seq, KV_WIDTH), F32)],
        compiler_params=_params("parallel"),
        name="nsa_compress",
    )(kcvc.reshape(bsz, seq, width), *consts)


SEL_CHUNK = 512


def _stack_heads(blk):
    return jnp.concatenate(_lane_blocks(blk), axis=0)


def _rows4(x):
    return jnp.concatenate([x] * C_GROUP, axis=0)


def _nsa_body(qn_ref, qr_ref, gl_ref, kc_ref, vc_ref, ks_ref, vs_ref, kw_ref, vw_ref, ovl_ref, place_ref,
              o_ref):
    n_cmp = kc_ref.shape[2]
    n_slc = ovl_ref.shape[0]
    hd = C_HEAD_DIM
    qb = pl.program_id(2)
    s0 = qb * NSA_TQ
    t_pos = s0 + lax.broadcasted_iota(jnp.int32, (NSA_TQ, 1), 0)
    qn4 = _stack_heads(qn_ref[0])
    qr4 = _stack_heads(qr_ref[0])

    cmp_end = lax.broadcasted_iota(jnp.int32, (1, n_cmp), 1) * CMP_STRIDE + (CMP_LEN - 1)
    pc = _masked_softmax(_dot_nt(qn4, kc_ref[0, 0]), _rows4(cmp_end <= t_pos))
    o_c = _dot(pc, vc_ref[0, 0])
    pc_sum = (pc[:NSA_TQ] + pc[NSA_TQ:2 * NSA_TQ]) + (pc[2 * NSA_TQ:3 * NSA_TQ] + pc[3 * NSA_TQ:])

    p_hi, p_lo = _split_bf16(pc_sum)
    ovl = ovl_ref[...]
    imp = _dot_nt(ovl, p_hi) + _dot_nt(ovl, p_lo)
    jb = lax.broadcasted_iota(jnp.int32, (n_slc, 1), 0)
    t_row = s0 + lax.broadcasted_iota(jnp.int32, (1, NSA_TQ), 1)
    cur = t_row // SLC_BLK
    forced = (jb == 0) | (jb == cur) | (jb == cur - 1)
    imp = jnp.where(jb * SLC_BLK <= t_row, imp + jnp.where(forced, FORCE_BONUS, 0.0), NEG_INF)
    rank = jnp.zeros((n_slc, NSA_TQ), F32)
    for k in range(n_slc):
        ck = imp[k:k + 1, :]
        beats = (ck > imp) | ((ck == imp) & (jb > k))
        rank = rank + jnp.where(beats, 1.0, 0.0)
    not_sel = jnp.where(rank < float(min(SEL_TOPK, n_slc)), 0.0, 1.0).astype(BF16)
    not_sel_q = lax.dot_general(not_sel, place_ref[...], (((0,), (0,)), ((), ())),
                                preferred_element_type=F32)
    q_sel = qr4 + _rows4(not_sel_q.astype(BF16))

    def sel_chunk(c, m, acc, causal):
        k0 = pl.multiple_of(c * SEL_CHUNK, SEL_CHUNK)
        s = _dot_nt(q_sel, ks_ref[0, pl.ds(k0, SEL_CHUNK), :])
        if causal:
            key = k0 + lax.broadcasted_iota(jnp.int32, (1, SEL_CHUNK), 1)
            s = s + _rows4(jnp.where(key <= t_pos, 0.0, NEG_INF))
        m_new = jnp.maximum(m, jnp.max(s, axis=-1, keepdims=True))
        p = jnp.exp(s - m_new)
        acc = jnp.exp(m - m_new) * acc + _dot(p, vs_ref[0, pl.ds(k0, SEL_CHUNK), :])
        return m_new, acc

    rows = C_GROUP * NSA_TQ
    last = qb // (SEL_CHUNK // NSA_TQ)
    m, acc = lax.fori_loop(0, last, lambda c, ma: sel_chunk(c, ma[0], ma[1], False),
                           (jnp.full((rows, 1), NEG_INF, F32), jnp.zeros((rows, V7X_LANES), F32)))
    _, acc = sel_chunk(last, m, acc, True)
    o_s = acc[:, :hd] / jnp.maximum(acc[:, hd:hd + 1], 1e-30)

    band = NSA_TQ + WIN
    w0 = pl.multiple_of(jnp.maximum(s0 - WIN, 0), NSA_TQ)
    diff = t_pos - (w0 + lax.broadcasted_iota(jnp.int32, (1, band), 1))
    s = _dot_nt(qr4, kw_ref[0, pl.ds(w0, band), :]) + _rows4(
        jnp.where((diff >= 0) & (diff < WIN), 0.0, NEG_INF))
    p = jnp.exp(s - jnp.max(s, axis=-1, keepdims=True))
    acc = _dot(p, vw_ref[0, pl.ds(w0, band), :])
    o_w = acc[:, :hd] / jnp.maximum(acc[:, hd:hd + 1], 1e-30)

    gate = _sigmoid(gl_ref[0])
    outs = []
    for g in range(C_GROUP):
        r = slice(g * NSA_TQ, (g + 1) * NSA_TQ)
        gc = g * N_BRANCH
        outs.append(gate[:, gc:gc + 1] * o_c[r, :hd] + gate[:, gc + 1:gc + 2] * o_s[r]
                    + gate[:, gc + 2:gc + 3] * o_w[r])
    o_ref[0] = jnp.concatenate(outs, axis=1).astype(o_ref.dtype)


def _nsa(qn, qr, gl, k_cmp, v_cmp, ks, vs, kw, vw, bsz, seq):
    n_cmp = seq // CMP_STRIDE
    n_slc = seq // SLC_BLK
    assert C_HEAD_DIM + n_slc <= V7X_LANES and seq % SEL_CHUNK == 0
    ci = np.arange(n_cmp)[None, :] * CMP_STRIDE
    sj = np.arange(n_slc)[:, None] * SLC_BLK
    overlap_t = jnp.asarray(((ci < sj + SLC_BLK) & (ci + CMP_LEN > sj)).astype(np.float32)).astype(BF16)
    place = jnp.asarray(np.eye(n_slc, V7X_LANES, k=C_HEAD_DIM, dtype=np.float32)).astype(BF16)
    gw = C_GROUP * V7X_LANES
    q_spec = pl.BlockSpec((1, NSA_TQ, gw), lambda b, h, i: (b, i, h))
    gl_spec = pl.BlockSpec((1, NSA_TQ, V7X_LANES), lambda b, h, i: (b, i, h))
    cmp_spec = pl.BlockSpec((1, 1, n_cmp, V7X_LANES), lambda b, h, i: (b, h, 0, 0))
    kv_spec = pl.BlockSpec((1, seq, V7X_LANES), lambda b, h, i: (b, 0, h))
    as3 = lambda z: z.reshape(bsz, seq, z.shape[-1])
    return pl.pallas_call(
        _nsa_body,
        grid=(bsz, C_KV_HEADS, seq // NSA_TQ),
        in_specs=[q_spec, q_spec, gl_spec, cmp_spec, cmp_spec, kv_spec, kv_spec, kv_spec, kv_spec,
                  _const_spec((n_slc, n_cmp)), _const_spec((n_slc, V7X_LANES))],
        out_specs=pl.BlockSpec((1, NSA_TQ, C_GROUP * C_HEAD_DIM), lambda b, h, i: (b, i, h)),
        out_shape=jax.ShapeDtypeStruct((bsz, seq, C_WIDTH), BF16),
        compiler_params=_params("parallel", "parallel", "arbitrary"),
        name="nsa_attention",
    )(as3(qn), as3(qr), as3(gl), k_cmp, v_cmp, as3(ks), as3(vs), as3(kw), as3(vw), overlap_t, place)


ROW_CHUNKS = D_MODEL // V7X_LANES
assert ROW_CHUNKS == V7X_SUBLANES
DMA_UNROLL = 8
PAD_SPANS = N_EXPERTS + 1


def _tile_rows(n_rows):
    return (n_rows * ROW_CHUNKS, V7X_LANES)


def _row_tile(ref, r):
    start = r * ROW_CHUNKS
    if not isinstance(start, int):
        start = pl.multiple_of(start, ROW_CHUNKS)
    return ref.at[pl.ds(start, ROW_CHUNKS), :]


def _store_row_tiles(ref, x):
    for c in range(ROW_CHUNKS):
        ref[pl.ds(c, x.shape[0], stride=ROW_CHUNKS), :] = x[:, c * V7X_LANES:(c + 1) * V7X_LANES]


def _load_row_tiles(ref):
    rows = ref.shape[0] // ROW_CHUNKS
    return jnp.concatenate([ref[pl.ds(c, rows, stride=ROW_CHUNKS), :] for c in range(ROW_CHUNKS)], axis=1)


def _route(hn, wr_ref, br_ref, tri_ref, cnt_scr):
    logits = _dot_f32(hn, wr_ref[...]) + br_ref[...]
    lane = lax.broadcasted_iota(jnp.int32, logits.shape, 1)
    m1 = jnp.max(logits, axis=-1, keepdims=True)
    i1 = jnp.min(jnp.where(logits == m1, lane, V7X_LANES), axis=-1, keepdims=True)
    rest = jnp.where(lane == i1, NEG_INF, logits)
    m2 = jnp.max(rest, axis=-1, keepdims=True)
    i2 = jnp.min(jnp.where(rest == m2, lane, V7X_LANES), axis=-1, keepdims=True)
    e2 = jnp.exp(m2 - m1)
    den = 1.0 + e2
    hit1, hit2 = lane == i1, lane == i2
    hits = jnp.where(hit1 | hit2, 1.0, 0.0)
    before = jnp.dot(tri_ref[...], hits.astype(BF16), preferred_element_type=F32) + cnt_scr[...]
    r1 = jnp.sum(jnp.where(hit1, before, 0.0), axis=-1, keepdims=True).astype(jnp.int32)
    r2 = jnp.sum(jnp.where(hit2, before, 0.0), axis=-1, keepdims=True).astype(jnp.int32)
    cnt_scr[...] = cnt_scr[...] + jnp.sum(hits, axis=0, keepdims=True)
    idx = jnp.where(lane == 0, i1, jnp.where(lane == 1, i2, jnp.where(lane == 2, r1,
                    jnp.where(lane == 3, r2, 0))))
    gate = jnp.where(lane == 0, 1.0 / den, jnp.where(lane == 1, e2 / den, 0.0))
    return idx, gate


def _row_copy(src, dst, sem):
    return pltpu.make_async_copy(src, dst, sem)


def _dispatch_body(pad_ref, dest_ref, h_ref, g_ref, xs_ref, hn_ref, zero_scr, sem):
    _store_row_tiles(hn_ref, _rms(h_ref[...], g_ref[...]))

    @pl.when(pl.program_id(0) == 0)
    def _():
        zero_scr[...] = jnp.zeros_like(zero_scr)
        for e in range(PAD_SPANS):
            start, count = pad_ref[e], pad_ref[PAD_SPANS + e]

            def fill(r, c):
                _row_copy(zero_scr, _row_tile(xs_ref, start + r), sem).start()
                return c

            def filled(r, c):
                _row_copy(zero_scr, _row_tile(xs_ref, 0), sem).wait()
                return c

            lax.fori_loop(0, count, fill, 0)
            lax.fori_loop(0, count, filled, 0)

    def issue(blk, c):
        for u in range(DMA_UNROLL):
            r = blk * DMA_UNROLL + u
            for k in range(TOP_K):
                _row_copy(_row_tile(hn_ref, r), _row_tile(xs_ref, dest_ref[TOP_K * r + k]),
                          sem).start(priority=k)
        return c

    lax.fori_loop(0, DMA_ROWS // DMA_UNROLL, issue, 0)

    def drain(r, c):
        for k in range(TOP_K):
            _row_copy(_row_tile(hn_ref, 0), _row_tile(xs_ref, 0), sem).wait()
        return c

    lax.fori_loop(0, DMA_ROWS, drain, 0)


def _dispatch(pad_info, dest, h, gain, p_rows):
    n = h.shape[0]
    return pl.pallas_call(
        _dispatch_body,
        grid_spec=pltpu.PrefetchScalarGridSpec(
            num_scalar_prefetch=1,
            grid=(n // DMA_ROWS,),
            in_specs=[pl.BlockSpec((TOP_K * DMA_ROWS,), lambda i, pad: (i,), memory_space=pltpu.SMEM),
                      pl.BlockSpec((DMA_ROWS, D_MODEL), lambda i, pad: (i, 0)),
                      pl.BlockSpec((1, D_MODEL), lambda i, pad: (0, 0))],
            out_specs=pl.BlockSpec(memory_space=pl.ANY),
            scratch_shapes=[pltpu.VMEM(_tile_rows(DMA_ROWS), F32), pltpu.VMEM(_tile_rows(1), F32),
                            pltpu.SemaphoreType.DMA(())]),
        out_shape=jax.ShapeDtypeStruct(_tile_rows(p_rows), F32),
        compiler_params=_params("arbitrary"),
        name="moe_dispatch",
    )(pad_info, dest, h, gain.reshape(1, -1))


def _experts_body(blk_e_ref, n_used_ref, prev_ref, cur_ref, x_ref, wg_ref, wu_ref, wd_ref, g_ref,
                  o_scr, sems):
    i = pl.program_id(0)
    slot = i % 2

    def scatter(rows_ref, s):
        for r in range(MOE_ROWS):
            _row_copy(_row_tile(o_scr.at[s], r), _row_tile(g_ref, rows_ref[r]), sems.at[s]).start(priority=r % 2)

    def drain(s):
        _row_copy(o_scr.at[s], g_ref.at[pl.ds(0, MOE_ROWS * ROW_CHUNKS), :], sems.at[s]).wait()

    @pl.when(i == 0)
    def _():
        o_scr[...] = jnp.zeros_like(o_scr)

    @pl.when(i >= 1)
    def _():
        drain(slot)

    x = _load_row_tiles(x_ref).astype(BF16)
    scatter(prev_ref, 1 - slot)
    gate = jnp.dot(x, wg_ref[0], preferred_element_type=F32)
    up = jnp.dot(x, wu_ref[0], preferred_element_type=F32)
    act = (gate * _sigmoid(gate)) * up
    _store_row_tiles(o_scr.at[slot], _dot(act, wd_ref[0]))

    @pl.when(i == pl.num_programs(0) - 1)
    def _():
        scatter(cur_ref, slot)
        drain(1 - slot)
        drain(slot)


def _experts(blk_e, n_used, slots, xs, wg, wu, wd):
    p_rows = xs.shape[0] // ROW_CHUNKS
    out_rows = slots.shape[0]
    idx_spec = lambda off: pl.BlockSpec((MOE_ROWS,), lambda i, be, nu: (i + off,), memory_space=pltpu.SMEM)
    x_spec = pl.BlockSpec(_tile_rows(MOE_ROWS), lambda i, be, nu: (jnp.minimum(i, nu[0] - 1), 0))
    w_spec = lambda w: pl.BlockSpec((1,) + w.shape[1:], lambda i, be, nu: (be[i], 0, 0))
    return pl.pallas_call(
        _experts_body,
        grid_spec=pltpu.PrefetchScalarGridSpec(
            num_scalar_prefetch=2,
            grid=(p_rows // MOE_ROWS,),
            in_specs=[idx_spec(0), idx_spec(1), x_spec, w_spec(wg), w_spec(wu), w_spec(wd)],
            out_specs=pl.BlockSpec(memory_space=pl.ANY),
            scratch_shapes=[pltpu.VMEM((2,) + _tile_rows(MOE_ROWS), F32), pltpu.SemaphoreType.DMA((2,))]),
        out_shape=jax.ShapeDtypeStruct(_tile_rows(out_rows), F32),
        compiler_params=_params("arbitrary"),
        name="moe_experts",
    )(blk_e, n_used, slots, slots, xs, wg, wu, wd)


def _combine_body(h_ref, gate_ref, g_ref, y0_ref, y1_ref, o_ref):
    gate = gate_ref[...]
    moe = gate[:, 0:1] * _load_row_tiles(y0_ref) + gate[:, 1:2] * _load_row_tiles(y1_ref)
    o_ref[...] = _rms(h_ref[...] + moe, g_ref[...])


def _combine(h, gates, gain, y_rows, tm=512):
    n = h.shape[0]
    steps = n // tm
    return pl.pallas_call(
        _combine_body,
        grid=(steps,),
        in_specs=[_rows_spec(tm, D_MODEL), _rows_spec(tm, V7X_LANES), _const_spec((1, D_MODEL)),
                  pl.BlockSpec(_tile_rows(tm), lambda i: (i, 0)),
                  pl.BlockSpec(_tile_rows(tm), lambda i: (i + steps, 0))],
        out_specs=_rows_spec(tm, D_MODEL),
        out_shape=jax.ShapeDtypeStruct((n, D_MODEL), F32),
        compiler_params=_params("parallel"),
        name="moe_combine_norm",
    )(h, gates, gain.reshape(1, -1), y_rows, y_rows)


def _moe_layout(idx, counts):
    n = idx.shape[0]
    nk = n * TOP_K
    counts = counts[0, :N_EXPERTS].astype(jnp.int32)
    padded = ((counts + MOE_ROWS - 1) // MOE_ROWS) * MOE_ROWS
    p_end = jnp.cumsum(padded)
    p_start = p_end - padded
    experts = jnp.arange(N_EXPERTS, dtype=jnp.int32)[None, None, :]
    first = jnp.sum(jnp.where(idx[:, :TOP_K, None] == experts, p_start[None, None, :], 0), axis=-1)
    dest = (first + idx[:, TOP_K:2 * TOP_K]).reshape(nk)
    n_blk = (nk + MOE_ROWS - 1) // MOE_ROWS + N_EXPERTS
    blk_e = jnp.minimum(jnp.searchsorted(p_end, jnp.arange(n_blk) * MOE_ROWS, side="right"),
                        N_EXPERTS - 1).astype(jnp.int32)
    n_used = (p_end[-1:] // MOE_ROWS).astype(jnp.int32)
    p_rows = n_blk * MOE_ROWS
    pad_info = jnp.concatenate([p_start + counts, p_end[-1:], padded - counts,
                                p_rows - p_end[-1:]]).astype(jnp.int32)
    pair_by_row = jnp.argsort(dest).astype(jnp.int32)
    rows = jnp.arange(p_rows, dtype=jnp.int32)
    row_e = jnp.repeat(blk_e, MOE_ROWS)
    offset = rows - p_start[row_e]
    before = (jnp.cumsum(counts) - counts)[row_e]
    occupied = offset < counts[row_e]
    pair = pair_by_row[jnp.clip(before + offset, 0, nk - 1)]
    spare = nk + rows - (before + jnp.minimum(offset, counts[row_e]))
    slots = jnp.where(occupied, (pair % TOP_K) * n + pair // TOP_K, spare)
    slots = jnp.concatenate([p_rows + jnp.arange(MOE_ROWS, dtype=jnp.int32), slots]).astype(jnp.int32)
    return dest.astype(jnp.int32), blk_e, n_used, pad_info, p_rows, slots


def _moe_final(h, idx, gates, counts, norm_g, wg, wu, wd, final_g):
    dest, blk_e, n_used, pad_info, p_rows, slots = _moe_layout(idx, counts)
    xs = _dispatch(pad_info, dest, h, norm_g, p_rows)
    y = _experts(blk_e, n_used, slots, xs, wg.astype(BF16), wu.astype(BF16), wd.astype(BF16))
    return _combine(h, gates, final_g, y)


def kernel(x, e_norm_mix, e_w_in, sgu_ln_g, sgu_ln_b, sgu_w, sgu_b, rwkv_mu, rwkv_w0, rwkv_w2,
           rwkv_a0, rwkv_a2, rwkv_g2, rwkv_k_k, rwkv_k_a, rwkv_r_k, rwkv_gn_g, rwkv_gn_b, e_w_out,
           e_norm_ffn, ffn_w_gate, ffn_w_up, ffn_w_down, o_norm_mix, o_w_in, nsa_cmp_pos_k,
           nsa_cmp_pos_v, nsa_cmp_k_w1, nsa_cmp_k_w2, nsa_cmp_v_w1, nsa_cmp_v_w2, conv_w, o_w_out,
           o_norm_ffn, moe_router, moe_router_b, moe_w_gate, moe_w_up, moe_w_down, final_norm):
    bsz, seq, d = x.shape
    n = bsz * seq
    h = x.reshape(n, d)

    ya, scan_ops, g = _layer0_front(h, seq, e_norm_mix[0], e_w_in[0].astype(BF16), sgu_ln_g[0], sgu_ln_b[0],
                                    sgu_w[0], sgu_b[0], rwkv_mu[0], rwkv_w0[0], rwkv_w2[0], rwkv_a0[0],
                                    rwkv_a2[0], rwkv_g2[0])
    rwka, v = _to_scan_layout(scan_ops, bsz, seq)
    ys = _rwkv_scan(rwka, v, rwkv_k_k[0], rwkv_k_a[0], rwkv_r_k[0], rwkv_gn_g[0], rwkv_gn_b[0], bsz, seq)
    h = _layer0_back(h, ya, _from_scan_layout(ys, bsz, seq), g, e_w_out[0].astype(BF16), e_norm_ffn[0],
                     ffn_w_gate[0].astype(BF16), ffn_w_up[0].astype(BF16), ffn_w_down[0].astype(BF16))

    qn, qr, kcvc, ks, vs, kw, vw, gl, bcd = _in_proj1(h, o_norm_mix[0], o_w_in[0], seq)
    k_cmp, v_cmp = _compress(kcvc, nsa_cmp_pos_k[0], nsa_cmp_pos_v[0], nsa_cmp_k_w1[0], nsa_cmp_v_w1[0],
                             nsa_cmp_k_w2[0], nsa_cmp_v_w2[0], bsz, seq)
    yc = _nsa(qn, qr, gl, k_cmp, v_cmp, ks, vs, kw, vw, bsz, seq).reshape(n, C_WIDTH)
    h, idx, gates, counts = _layer1_mid(h, yc, bcd, conv_w[0], o_w_out[0].astype(BF16), seq, o_norm_ffn[0],
                                        moe_router[0], moe_router_b[0])
    out = _moe_final(h, idx, gates, counts, o_norm_ffn[0], moe_w_gate[0], moe_w_up[0], moe_w_down[0],
                     final_norm)
    return out.reshape(bsz, seq, d)
```

```python
import functools

import jax
import jax.numpy as jnp
import numpy as np
from jax import lax
from jax.experimental import pallas as pl
from jax.experimental.pallas import tpu as pltpu

F32 = jnp.float32
BF16 = jnp.bfloat16

D_MODEL = 1024
A_GROUPS = 4
A_GROUP_DIM = 128
A_WIDTH = A_GROUPS * A_GROUP_DIM
CHUNK = 128
SGU_LN_EPS = 1e-5
B_HEADS = 8
B_HEAD_DIM = 64
B_WIDTH = B_HEADS * B_HEAD_DIM
DECAY_LORA = 64
ICLR_LORA = 64
GATE_LORA = 128
B_IN = 3 * B_WIDTH + DECAY_LORA + ICLR_LORA + GATE_LORA
RWKV_GN_EPS = 64e-5
C_HEADS = 8
C_KV_HEADS = 2
C_GROUP = C_HEADS // C_KV_HEADS
C_HEAD_DIM = 64
C_WIDTH = C_HEADS * C_HEAD_DIM
KV_WIDTH = C_KV_HEADS * C_HEAD_DIM
N_BRANCH = 3
CMP_LEN = 32
CMP_STRIDE = 16
CMP_HIDDEN = 256
SLC_BLK = 64
SEL_TOPK = 8
WIN = 512
NSA_TQ = 512
ROT_DIM = C_HEAD_DIM // 4
ROPE_THETA = 500000.0
D_WIDTH = 512
CONV_W = 3
FFN_DIM = 2816
N_EXPERTS = 8
TOP_K = 2
EXPERT_DIM = 1408
NORM_EPS = 1e-6
NEG_INF = -1e30
FORCE_BONUS = 1e6

V7X_LANES = 128
V7X_SUBLANES = 8
V7X_VMEM_LIMIT = 56 * 1024 * 1024

MOE_ROWS = 512
DMA_ROWS = 512


def _params(*sem):
    return pltpu.CompilerParams(dimension_semantics=sem, vmem_limit_bytes=V7X_VMEM_LIMIT)


def _const_spec(shape):
    zeros = (0,) * len(shape)
    return pl.BlockSpec(shape, lambda *_: zeros)


def _rows_spec(tm, width):
    return pl.BlockSpec((tm, width), lambda i: (i, 0))


def _rms(x, g):
    return x * lax.rsqrt(jnp.mean(x * x, axis=-1, keepdims=True) + NORM_EPS) * g


def _gelu(x):
    return x * (0.5 * (1.0 + jnp.tanh(0.7978845608028654 * (x + 0.044715 * (x * x * x)))))


def _sigmoid(x):
    return 1.0 / (1.0 + jnp.exp(-x))


def _dot(a, b):
    return jnp.dot(a.astype(BF16), b.astype(BF16), preferred_element_type=F32)


def _dot_nt(a, b):
    return lax.dot_general(a.astype(BF16), b.astype(BF16), (((1,), (1,)), ((), ())),
                           preferred_element_type=F32)


def _split_bf16(a):
    hi = a.astype(BF16)
    lo = (a - hi.astype(F32)).astype(BF16)
    return hi, lo


def _dot_f32(a, b):
    ah, al = _split_bf16(a)
    bh, bl = _split_bf16(b)
    d = functools.partial(jnp.dot, preferred_element_type=F32)
    return d(ah, bh) + (d(al, bh) + d(ah, bl))


def _masked_softmax(s, mask):
    s = jnp.where(mask, s, NEG_INF)
    m = jnp.max(s, axis=-1, keepdims=True)
    p = jnp.where(mask, jnp.exp(s - m), 0.0)
    return p / jnp.maximum(jnp.sum(p, axis=-1, keepdims=True), 1e-30)


def _sgu_chunk(p_uv, lng_ref, lnb_ref, w_ref, b_ref):
    row = lax.broadcasted_iota(jnp.int32, (CHUNK, CHUNK), 0)
    col = lax.broadcasted_iota(jnp.int32, (CHUNK, CHUNK), 1)
    causal = col <= row
    u = _gelu(p_uv[:, :A_WIDTH])
    v = _gelu(p_uv[:, A_WIDTH:])
    outs = []
    for g in range(A_GROUPS):
        cols = slice(g * A_GROUP_DIM, (g + 1) * A_GROUP_DIM)
        vg = v[:, cols]
        mu = jnp.mean(vg, axis=-1, keepdims=True)
        dv = vg - mu
        var = jnp.mean(dv * dv, axis=-1, keepdims=True)
        vn = dv * lax.rsqrt(var + SGU_LN_EPS) * lng_ref[:, cols] + lnb_ref[:, cols]
        wm = jnp.where(causal, w_ref[g], 0.0)
        mixed = _dot(wm, vn) + b_ref[:, g:g + 1]
        outs.append(u[:, cols] * mixed)
    return jnp.concatenate(outs, axis=1)


def _softplus(x):
    return jnp.maximum(x, 0.0) + jnp.log(1.0 + jnp.exp(-jnp.abs(x)))


SCAN_OPERANDS = 5


def _layer0_front_body(x_ref, gain_ref, win_ref, lng_ref, lnb_ref, sw_ref, sb_ref, mu_ref, w0_ref, w2_ref,
                       a0_ref, a2_ref, g2_ref, ya_o, s_o, g_o, prev_scr, *, tiles_per_seq):
    tm = x_ref.shape[0]

    @pl.when(pl.program_id(0) == 0)
    def _():
        prev_scr[...] = jnp.zeros_like(prev_scr)

    p = jnp.dot(_rms(x_ref[...], gain_ref[...]).astype(BF16), win_ref[...], preferred_element_type=F32)
    for c in range(tm // CHUNK):
        rows = slice(c * CHUNK, (c + 1) * CHUNK)
        ya_o[rows, :] = _sgu_chunk(p[rows, :2 * A_WIDTH], lng_ref, lnb_ref, sw_ref, sb_ref).astype(ya_o.dtype)
    x = p[:, 2 * A_WIDTH:]
    first = (pl.program_id(0) % tiles_per_seq) == 0
    prev_row = jnp.where(first, 0.0, prev_scr[...])
    rowid = lax.broadcasted_iota(jnp.int32, (tm, 1), 0)
    shifted = jnp.where(rowid == 0, prev_row, pltpu.roll(x, 1, axis=0))
    prev_scr[...] = x[tm - 1:tm, :]
    xm = x + (shifted - x) * mu_ref[...]
    o = 3 * B_WIDTH
    wl = xm[:, o:o + DECAY_LORA]
    al = xm[:, o + DECAY_LORA:o + DECAY_LORA + ICLR_LORA]
    gl = xm[:, o + DECAY_LORA + ICLR_LORA:]
    w = -_softplus(-(w0_ref[...] + _dot(jnp.tanh(wl), w2_ref[...]))) - 0.5
    s_o[0] = xm[:, :B_WIDTH]
    s_o[1] = jnp.exp(-jnp.exp(w))
    s_o[2] = xm[:, B_WIDTH:2 * B_WIDTH]
    s_o[3] = _sigmoid(a0_ref[...] + _dot(al, a2_ref[...]))
    s_o[4] = xm[:, 2 * B_WIDTH:3 * B_WIDTH]
    g_o[...] = _dot(_sigmoid(gl), g2_ref[...])


def _layer0_front(h, seq, gain, w_in, ln_g, ln_b, w_s, b_s, mu, w0, w2, a0, a2, g2, tm=512):
    n, d = h.shape
    perm = np.arange(B_WIDTH).reshape(B_HEADS, B_HEAD_DIM).T.reshape(-1)
    r0, k0 = 2 * A_WIDTH, 2 * A_WIDTH + B_WIDTH
    cols = np.concatenate([np.arange(r0), r0 + perm, k0 + perm, np.arange(k0 + B_WIDTH, w_in.shape[1])])
    w_in = w_in[:, cols]
    mu = mu[cols[r0:] - r0]
    w0, w2, a0, a2 = w0[perm], w2[:, perm], a0[perm], a2[:, perm]
    row = lambda v: v.reshape(1, -1)
    consts = [row(gain), w_in, row(ln_g), row(ln_b), w_s, b_s.T, row(mu), row(w0), w2, row(a0), a2, g2]
    outs = [jax.ShapeDtypeStruct((n, A_WIDTH), BF16), jax.ShapeDtypeStruct((SCAN_OPERANDS, n, B_WIDTH), F32),
            jax.ShapeDtypeStruct((n, B_WIDTH), F32)]
    return pl.pallas_call(
        functools.partial(_layer0_front_body, tiles_per_seq=seq // tm),
        grid=(n // tm,),
        in_specs=[_rows_spec(tm, d)] + [_const_spec(c.shape) for c in consts],
        out_specs=[_rows_spec(tm, A_WIDTH), pl.BlockSpec((SCAN_OPERANDS, tm, B_WIDTH), lambda i: (0, i, 0)),
                   _rows_spec(tm, B_WIDTH)],
        out_shape=outs,
        scratch_shapes=[pltpu.VMEM((1, B_IN), F32)],
        compiler_params=_params("arbitrary"),
        name="layer0_front",
    )(h, *consts)


def _rwkv_scan_body(r_ref, w_ref, k0_ref, a_ref, v_ref, kkp_ref, kap_ref, rk_ref, gng_ref, gnb_ref,
                    y_ref, s_ref, kkn_ref, ka_ref, km_ref):
    n, tt = r_ref.shape[0], r_ref.shape[1]

    @pl.when(pl.program_id(0) == 0)
    def _():
        s_ref[...] = jnp.zeros_like(s_ref)

    per_dim = lambda p_ref: p_ref[...][:, None, :]
    k0 = k0_ref[...]
    a = a_ref[...]
    kk = k0 * per_dim(kkp_ref)
    kkn = kk / jnp.maximum(jnp.sqrt(jnp.sum(kk * kk, axis=0, keepdims=True)), 1e-12)
    km = k0 * (1.0 + (a - 1.0) * per_dim(kap_ref))
    kkn_ref[...] = kkn
    ka_ref[...] = kkn * a
    km_ref[...] = km

    zero = jnp.zeros((n, r_ref.shape[2]), F32)

    def sa_init(j, acc):
        return acc + s_ref[j] * kkn_ref[j, pl.ds(0, 1), :]

    sa0 = lax.fori_loop(0, n, sa_init, zero)

    def step(t, sa):
        v_t = v_ref[t]
        now = pl.ds(t, 1)
        nxt = pl.ds(jnp.minimum(t + 1, tt - 1), 1)

        def jbody(j, carry):
            y, san = carry
            sn = s_ref[j] * w_ref[j, now, :] + (v_t * km_ref[j, now, :] - sa * ka_ref[j, now, :])
            s_ref[j] = sn
            return y + sn * r_ref[j, now, :], san + sn * kkn_ref[j, nxt, :]

        y, san = lax.fori_loop(0, n, jbody, (zero, zero), unroll=8)
        y_ref[t] = y
        return san

    lax.fori_loop(0, tt, step, sa0)

    y = y_ref[...]
    ym = jnp.mean(y, axis=1, keepdims=True)
    dy = y - ym
    yv = jnp.mean(dy * dy, axis=1, keepdims=True)
    yn = dy * lax.rsqrt(yv + RWKV_GN_EPS) * gng_ref[...][None] + gnb_ref[...][None]
    bonus = jnp.sum(r_ref[...] * km * per_dim(rk_ref), axis=0)
    y_ref[...] = yn + bonus[:, None, :] * v_ref[...]


def _rwkv_scan(rwka, v, k_k, k_a, r_k, gn_g, gn_b, bsz, seq, tt=32):
    n = B_HEAD_DIM
    lanes = bsz * B_HEADS

    def lane_param(p):
        return jnp.tile(p.reshape(B_HEADS, n).T, (1, bsz))

    op_spec = lambda a: pl.BlockSpec((None, n, tt, lanes), lambda c: (a, 0, c, 0))
    slab_spec = pl.BlockSpec((tt, n, lanes), lambda c: (c, 0, 0))
    par_spec = _const_spec((n, lanes))
    return pl.pallas_call(
        _rwkv_scan_body,
        grid=(seq // tt,),
        in_specs=[op_spec(a) for a in range(SCAN_OPERANDS - 1)] + [slab_spec] + [par_spec] * 5,
        out_specs=slab_spec,
        out_shape=jax.ShapeDtypeStruct((seq, n, lanes), F32),
        scratch_shapes=[pltpu.VMEM((n, n, lanes), F32)] + [pltpu.VMEM((n, tt, lanes), F32)] * 3,
        compiler_params=_params("arbitrary"),
        name="rwkv_scan",
    )(*([rwka] * (SCAN_OPERANDS - 1)), v, lane_param(k_k), lane_param(k_a), lane_param(r_k.reshape(-1)),
      lane_param(gn_g), lane_param(gn_b))


SCAN_TT = 128
SCAN_UNROLL = 8


def _time_to_lanes(x_ref, a_ref):
    for b in range(x_ref.shape[0]):
        for blk in range(B_WIDTH // V7X_LANES):
            cols = slice(blk * V7X_LANES, (blk + 1) * V7X_LANES)
            a_ref[b, cols, :] = x_ref[b, :, cols].T


def _to_scan_rows_body(x_ref, o_ref, a_ref):
    bsz = x_ref.shape[0]
    _time_to_lanes(x_ref, a_ref)

    def dims(i, carry):
        for u in range(SCAN_UNROLL):
            d = i * SCAN_UNROLL + u
            rows = pl.ds(pl.multiple_of(d * B_HEADS, B_HEADS), B_HEADS)
            z = jnp.concatenate([a_ref[b, rows, :] for b in range(bsz)], axis=0)
            o_ref[d] = z.T
        return carry

    lax.fori_loop(0, B_HEAD_DIM // SCAN_UNROLL, dims, 0)


def _to_scan_slab_body(x_ref, o_ref, a_ref):
    bsz = x_ref.shape[0]
    _time_to_lanes(x_ref, a_ref)

    def dims(i, carry):
        for u in range(SCAN_UNROLL):
            d = i * SCAN_UNROLL + u
            z = jnp.concatenate([a_ref[b, pl.ds(d, B_HEADS, stride=B_HEAD_DIM), :] for b in range(bsz)],
                                axis=0)
            o_ref[:, d, :] = z.T
        return carry

    lax.fori_loop(0, B_HEAD_DIM // SCAN_UNROLL, dims, 0)


def _to_scan_layout(ops, bsz, seq):
    tt = min(SCAN_TT, seq)
    lanes = bsz * B_HEADS
    ops4 = ops.reshape(SCAN_OPERANDS, bsz, seq, B_WIDTH)
    scratch = [pltpu.VMEM((bsz, B_WIDTH, tt), F32)]
    rwka = pl.pallas_call(
        _to_scan_rows_body,
        grid=(SCAN_OPERANDS - 1, seq // tt),
        in_specs=[pl.BlockSpec((None, bsz, tt, B_WIDTH), lambda a, i: (a, 0, i, 0))],
        out_specs=pl.BlockSpec((None, B_HEAD_DIM, tt, lanes), lambda a, i: (a, 0, i, 0)),
        out_shape=jax.ShapeDtypeStruct((SCAN_OPERANDS - 1, B_HEAD_DIM, seq, lanes), F32),
        scratch_shapes=scratch,
        compiler_params=_params("parallel", "arbitrary"),
        name="to_scan_rows",
    )(ops4)
    v = pl.pallas_call(
        _to_scan_slab_body,
        grid=(seq // tt,),
        in_specs=[pl.BlockSpec((None, bsz, tt, B_WIDTH), lambda i: (SCAN_OPERANDS - 1, 0, i, 0))],
        out_specs=pl.BlockSpec((tt, B_HEAD_DIM, lanes), lambda i: (i, 0, 0)),
        out_shape=jax.ShapeDtypeStruct((seq, B_HEAD_DIM, lanes), F32),
        scratch_shapes=scratch,
        compiler_params=_params("arbitrary"),
        name="to_scan_slab",
    )(ops4)
    return rwka, v


def _from_scan_body(y_ref, o_ref, a_ref):
    bsz = o_ref.shape[0]

    def dims(i, carry):
        for u in range(SCAN_UNROLL):
            d = i * SCAN_UNROLL + u
            zt = y_ref[:, d, :].T
            for b in range(bsz):
                a_ref[b, pl.ds(d, B_HEADS, stride=B_HEAD_DIM), :] = zt[b * B_HEADS:(b + 1) * B_HEADS, :]
        return carry

    lax.fori_loop(0, B_HEAD_DIM // SCAN_UNROLL, dims, 0)
    for b in range(bsz):
        for blk in range(B_WIDTH // V7X_LANES):
            cols = slice(blk * V7X_LANES, (blk + 1) * V7X_LANES)
            o_ref[b, :, cols] = a_ref[b, cols, :].T


def _from_scan_layout(y, bsz, seq):
    tt = min(SCAN_TT, seq)
    return pl.pallas_call(
        _from_scan_body,
        grid=(seq // tt,),
        in_specs=[pl.BlockSpec((tt, B_HEAD_DIM, bsz * B_HEADS), lambda i: (i, 0, 0))],
        out_specs=pl.BlockSpec((bsz, tt, B_WIDTH), lambda i: (0, i, 0)),
        out_shape=jax.ShapeDtypeStruct((bsz, seq, B_WIDTH), F32),
        scratch_shapes=[pltpu.VMEM((bsz, B_WIDTH, tt), F32)],
        compiler_params=_params("arbitrary"),
        name="from_scan_layout",
    )(y).reshape(bsz * seq, B_WIDTH)


def _layer1_mid_body(h_ref, yc_ref, bcd_ref, prev_ref, cw_ref, w_ref, g_ref, wr_ref, br_ref, tri_ref,
                     o_ref, idx_o, gate_o, cnt_o, cnt_scr, *, tiles_per_seq):
    tm = h_ref.shape[0]

    @pl.when(pl.program_id(0) == 0)
    def _():
        cnt_scr[...] = jnp.zeros_like(cnt_scr)

    first = (pl.program_id(0) % tiles_per_seq) == 0
    z = bcd_ref[:, D_WIDTH:2 * D_WIDTH] * bcd_ref[:, 2 * D_WIDTH:]
    zp = jnp.where(first, 0.0, prev_ref[:, D_WIDTH:2 * D_WIDTH] * prev_ref[:, 2 * D_WIDTH:])
    rowid = lax.broadcasted_iota(jnp.int32, (tm, 1), 0)
    z1 = jnp.where(rowid == 0, zp[7:8, :], pltpu.roll(z, 1, axis=0))
    z2 = pltpu.roll(z, 2, axis=0)
    z2 = jnp.where(rowid == 0, zp[6:7, :], jnp.where(rowid == 1, zp[7:8, :], z2))
    y = cw_ref[0:1, :] * z2 + cw_ref[1:2, :] * z1 + cw_ref[2:3, :] * z
    yd = bcd_ref[:, :D_WIDTH] * y
    h_new = h_ref[...] + (_dot(yc_ref[...], w_ref[:C_WIDTH, :]) + _dot(yd, w_ref[C_WIDTH:, :]))
    o_ref[...] = h_new
    idx_o[...], gate_o[...] = _route(_rms(h_new, g_ref[...]), wr_ref, br_ref, tri_ref, cnt_scr)
    cnt_o[...] = cnt_scr[...]


def _layer1_mid(h, yc, bcd, conv_w, w_out, seq, gain, w_router, b_router, tm=512):
    n = h.shape[0]
    per8 = tm // V7X_SUBLANES
    wr = jnp.pad(w_router, ((0, 0), (0, V7X_LANES - N_EXPERTS)))
    br = jnp.pad(b_router.reshape(1, -1), ((0, 0), (0, V7X_LANES - N_EXPERTS)), constant_values=NEG_INF)
    tri = jnp.asarray(np.tril(np.ones((tm, tm), np.float32), k=-1)).astype(BF16)
    return pl.pallas_call(
        functools.partial(_layer1_mid_body, tiles_per_seq=seq // tm),
        grid=(n // tm,),
        in_specs=[_rows_spec(tm, D_MODEL), _rows_spec(tm, C_WIDTH), _rows_spec(tm, 3 * D_WIDTH),
                  pl.BlockSpec((V7X_SUBLANES, 3 * D_WIDTH), lambda i: (jnp.maximum(i * per8 - 1, 0), 0)),
                  _const_spec((CONV_W, D_WIDTH)), _const_spec(w_out.shape), _const_spec((1, D_MODEL)),
                  _const_spec(wr.shape), _const_spec(br.shape), _const_spec(tri.shape)],
        out_specs=[_rows_spec(tm, D_MODEL), _rows_spec(tm, V7X_LANES), _rows_spec(tm, V7X_LANES),
                   _const_spec((1, V7X_LANES))],
        out_shape=[jax.ShapeDtypeStruct((n, D_MODEL), F32), jax.ShapeDtypeStruct((n, V7X_LANES), jnp.int32),
                   jax.ShapeDtypeStruct((n, V7X_LANES), F32), jax.ShapeDtypeStruct((1, V7X_LANES), F32)],
        scratch_shapes=[pltpu.VMEM((1, V7X_LANES), F32)],
        compiler_params=_params("arbitrary"),
        name="layer1_mid",
    )(h, yc, bcd, bcd, conv_w, w_out, gain.reshape(1, -1), wr, br, tri)


def _layer0_back_body(h_ref, ya_ref, ys_ref, gm_ref, wo_ref, g_ref, wg_ref, wu_ref, wd_ref, o_ref):
    yb = ys_ref[...] * gm_ref[...]
    h = h_ref[...] + (_dot(ya_ref[...], wo_ref[:A_WIDTH, :]) + _dot(yb, wo_ref[A_WIDTH:, :]))
    hn = _rms(h, g_ref[...]).astype(BF16)
    gate = jnp.dot(hn, wg_ref[...], preferred_element_type=F32)
    up = jnp.dot(hn, wu_ref[...], preferred_element_type=F32)
    act = (gate * _sigmoid(gate)) * up
    o_ref[...] = h + _dot(act, wd_ref[...])


def _layer0_back(h, ya, ys, gm, w_out, gain, wg, wu, wd, tm=512):
    n = h.shape[0]
    return pl.pallas_call(
        _layer0_back_body,
        grid=(n // tm,),
        in_specs=[_rows_spec(tm, D_MODEL), _rows_spec(tm, A_WIDTH), _rows_spec(tm, B_WIDTH),
                  _rows_spec(tm, B_WIDTH), _const_spec(w_out.shape), _const_spec((1, D_MODEL)),
                  _const_spec(wg.shape), _const_spec(wu.shape), _const_spec(wd.shape)],
        out_specs=_rows_spec(tm, D_MODEL),
        out_shape=jax.ShapeDtypeStruct((n, D_MODEL), F32),
        compiler_params=_params("parallel"),
        name="layer0_back",
    )(h, ya, ys, gm, w_out, gain.reshape(1, -1), wg, wu, wd)


def _rope(x, c, s_up, s_dn):
    half = ROT_DIM // 2
    return x * c + pltpu.roll(x, half, axis=1) * s_up + pltpu.roll(x, V7X_LANES - half, axis=1) * s_dn


def _lane_blocks(x):
    return [x[:, i * V7X_LANES:(i + 1) * V7X_LANES] for i in range(x.shape[1] // V7X_LANES)]


def _head_blocks(x):
    zeros = jnp.zeros((x.shape[0], V7X_LANES - C_HEAD_DIM), x.dtype)
    return [jnp.concatenate([x[:, h * C_HEAD_DIM:(h + 1) * C_HEAD_DIM], zeros], axis=1)
            for h in range(x.shape[1] // C_HEAD_DIM)]


def _in_proj1_body(x_ref, g_ref, c_ref, su_ref, sd_ref, kb_ref, one_ref, wq_ref, wkc_ref, wkv_ref,
                   wgl_ref, wbcd_ref, qn_o, qr_o, kc_o, ks_o, vs_o, kw_o, vw_o, gl_o, bcd_o):
    xn = _rms(x_ref[...], g_ref[...]).astype(BF16)
    d = functools.partial(jnp.dot, preferred_element_type=F32)
    c, su, sd = c_ref[...], su_ref[...], sd_ref[...]
    rope = lambda z: _rope(z, c, su, sd)
    q = _head_blocks(d(xn, wq_ref[...]) * (C_HEAD_DIM ** -0.5))
    qn_o[...] = jnp.concatenate(q, axis=1).astype(BF16)
    qr_o[...] = jnp.concatenate([rope(z) for z in q], axis=1).astype(BF16)
    kc_o[...] = d(xn, wkc_ref[...])
    kv = _head_blocks(d(xn, wkv_ref[...]))
    hk = C_KV_HEADS
    ks_o[...] = jnp.concatenate([rope(z) + kb_ref[...] for z in kv[:hk]], axis=1).astype(BF16)
    vs_o[...] = jnp.concatenate([z + one_ref[...] for z in kv[hk:2 * hk]], axis=1).astype(BF16)
    kw_o[...] = jnp.concatenate([rope(z) for z in kv[2 * hk:3 * hk]], axis=1).astype(BF16)
    vw_o[...] = jnp.concatenate([z + one_ref[...] for z in kv[3 * hk:]], axis=1).astype(BF16)
    gl_o[...] = d(xn, wgl_ref[...])
    bcd_o[...] = d(xn, wbcd_ref[...])


def _head_tables(seq):
    half = ROT_DIM // 2
    pos = jnp.arange(seq, dtype=F32)
    inv_freq = ROPE_THETA ** (-jnp.arange(0, ROT_DIM, 2, dtype=F32) / ROT_DIM)
    ang = pos[:, None] * inv_freq[None, :]
    cos, sin = jnp.cos(ang), jnp.sin(ang)
    pad = jnp.zeros((seq, V7X_LANES - ROT_DIM), F32)
    zeros = jnp.zeros((seq, half), F32)
    c = jnp.concatenate([cos, cos, pad + 1.0], axis=1)
    s_up = jnp.concatenate([zeros, sin, pad], axis=1)
    s_dn = jnp.concatenate([-sin, zeros, pad], axis=1)
    lane = jnp.arange(V7X_LANES)[None, :]
    blk = (jnp.arange(seq) // SLC_BLK)[:, None]
    k_bias = jnp.where(lane == C_HEAD_DIM + blk, NEG_INF, 0.0).astype(F32)
    ones = (lane == C_HEAD_DIM).astype(F32)
    return c, s_up, s_dn, k_bias, ones


def _pad_heads(w, n_heads):
    width = w.shape[1] // n_heads
    w = w.reshape(w.shape[0], n_heads, width)
    return jnp.pad(w, ((0, 0), (0, 0), (0, V7X_LANES - width))).reshape(w.shape[0], n_heads * V7X_LANES)


def _in_proj1(x, gain, w_in, seq, tm=256):
    n, d = x.shape
    o = np.cumsum([0, C_WIDTH] + [KV_WIDTH] * 6 + [C_HEADS * N_BRANCH] + [D_WIDTH] * 3)
    wq = w_in[:, o[0]:o[1]].astype(BF16)
    wkc = w_in[:, o[1]:o[3]].astype(BF16)
    wkv = w_in[:, o[3]:o[7]].astype(BF16)
    wgl = _pad_heads(w_in[:, o[7]:o[8]], C_KV_HEADS).astype(BF16)
    wbcd = w_in[:, o[8]:o[11]].astype(BF16)
    ws = [wq, wkc, wkv, wgl, wbcd]
    kvw = C_KV_HEADS * V7X_LANES
    widths = [C_HEADS * V7X_LANES] * 2 + [2 * KV_WIDTH] + [kvw] * 4 + [kvw, 3 * D_WIDTH]
    dts = [BF16, BF16, F32, BF16, BF16, BF16, BF16, F32, F32]
    tps = seq // tm
    tab_spec = pl.BlockSpec((tm, V7X_LANES), lambda i: (i % tps, 0))
    c, s_up, s_dn, k_bias, ones = _head_tables(seq)
    return pl.pallas_call(
        _in_proj1_body,
        grid=(n // tm,),
        in_specs=[_rows_spec(tm, d), _const_spec((1, d)), tab_spec, tab_spec, tab_spec, tab_spec,
                  _const_spec((1, V7X_LANES))] + [_const_spec(w.shape) for w in ws],
        out_specs=[_rows_spec(tm, wd) for wd in widths],
        out_shape=[jax.ShapeDtypeStruct((n, wd), dt) for wd, dt in zip(widths, dts)],
        compiler_params=_params("parallel"),
        name="in_proj1",
    )(x, gain.reshape(1, d), c, s_up, s_dn, k_bias, ones, *ws)


def _compress_body(z_ref, pk_ref, pv_ref, w1k_ref, w1v_ref, w2k_ref, w2v_ref, ko_ref, vo_ref, z_scr):
    half = CMP_STRIDE * C_HEAD_DIM
    n_rows = z_ref.shape[1] // CMP_STRIDE
    streams = ((pk_ref, w1k_ref, w2k_ref, ko_ref), (pv_ref, w1v_ref, w2v_ref, vo_ref))
    for s, (p_ref, w1_ref, w2_ref, o_ref) in enumerate(streams):
        z_scr[...] = z_ref[0, :, s * KV_WIDTH:(s + 1) * KV_WIDTH]
        every16 = [z_scr[pl.ds(l, n_rows, stride=CMP_STRIDE), :] for l in range(CMP_STRIDE)]
        for hk in range(C_KV_HEADS):
            cols = slice(hk * C_HEAD_DIM, (hk + 1) * C_HEAD_DIM)
            r = jnp.concatenate([z[:, cols] for z in every16], axis=1)
            lo = _dot(r + p_ref[:, :half], w1_ref[:half, :])
            hi = _dot(r + p_ref[:, half:], w1_ref[half:, :])
            hidden = _gelu(lo + pltpu.roll(hi, n_rows - 1, axis=0))
            o_ref[0, hk] = _dot(hidden, w2_ref[...]).astype(o_ref.dtype)


def _compress(kcvc, pos_k, pos_v, w1k, w1v, w2k, w2v, bsz, seq):
    n_rows = seq // CMP_STRIDE
    width = kcvc.shape[1]
    consts = [pos_k.reshape(1, -1), pos_v.reshape(1, -1), w1k.astype(BF16), w1v.astype(BF16),
              _pad_heads(w2k, 1).astype(BF16), _pad_heads(w2v, 1).astype(BF16)]
    o_spec = pl.BlockSpec((1, C_KV_HEADS, n_rows, V7X_LANES), lambda b: (b, 0, 0, 0))
    o_shape = jax.ShapeDtypeStruct((bsz, C_KV_HEADS, n_rows, V7X_LANES), BF16)
    return pl.pallas_call(
        _compress_body,
        grid=(bsz,),
        in_specs=[pl.BlockSpec((1, seq, width), lambda b: (b, 0, 0))] + [_const_spec(c.shape) for c in consts],
        out_specs=[o_spec, o_spec],
        out_shape=[o_shape, o_shape],
        scratch_shapes=[pltpu.VMEM((seq, KV_WIDTH), F32)],
        compiler_params=_params("parallel"),
        name="nsa_compress",
    )(kcvc.reshape(bsz, seq, width), *consts)


SEL_CHUNK = 512


def _stack_heads(blk):
    return jnp.concatenate(_lane_blocks(blk), axis=0)


def _rows4(x):
    return jnp.concatenate([x] * C_GROUP, axis=0)


def _nsa_body(qn_ref, qr_ref, gl_ref, kc_ref, vc_ref, ks_ref, vs_ref, kw_ref, vw_ref, ovl_ref, place_ref,
              o_ref):
    n_cmp = kc_ref.shape[2]
    n_slc = ovl_ref.shape[0]
    hd = C_HEAD_DIM
    qb = pl.program_id(2)
    s0 = qb * NSA_TQ
    t_pos = s0 + lax.broadcasted_iota(jnp.int32, (NSA_TQ, 1), 0)
    qn4 = _stack_heads(qn_ref[0])
    qr4 = _stack_heads(qr_ref[0])

    cmp_end = lax.broadcasted_iota(jnp.int32, (1, n_cmp), 1) * CMP_STRIDE + (CMP_LEN - 1)
    pc = _masked_softmax(_dot_nt(qn4, kc_ref[0, 0]), _rows4(cmp_end <= t_pos))
    o_c = _dot(pc, vc_ref[0, 0])
    pc_sum = (pc[:NSA_TQ] + pc[NSA_TQ:2 * NSA_TQ]) + (pc[2 * NSA_TQ:3 * NSA_TQ] + pc[3 * NSA_TQ:])

    p_hi, p_lo = _split_bf16(pc_sum)
    ovl = ovl_ref[...]
    imp = _dot_nt(ovl, p_hi) + _dot_nt(ovl, p_lo)
    jb = lax.broadcasted_iota(jnp.int32, (n_slc, 1), 0)
    t_row = s0 + lax.broadcasted_iota(jnp.int32, (1, NSA_TQ), 1)
    cur = t_row // SLC_BLK
    forced = (jb == 0) | (jb == cur) | (jb == cur - 1)
    imp = jnp.where(jb * SLC_BLK <= t_row, imp + jnp.where(forced, FORCE_BONUS, 0.0), NEG_INF)
    rank = jnp.zeros((n_slc, NSA_TQ), F32)
    for k in range(n_slc):
        ck = imp[k:k + 1, :]
        beats = (ck > imp) | ((ck == imp) & (jb > k))
        rank = rank + jnp.where(beats, 1.0, 0.0)
    not_sel = jnp.where(rank < float(min(SEL_TOPK, n_slc)), 0.0, 1.0).astype(BF16)
    not_sel_q = lax.dot_general(not_sel, place_ref[...], (((0,), (0,)), ((), ())),
                                preferred_element_type=F32)
    q_sel = qr4 + _rows4(not_sel_q.astype(BF16))

    def sel_chunk(c, m, acc, causal):
        k0 = pl.multiple_of(c * SEL_CHUNK, SEL_CHUNK)
        s = _dot_nt(q_sel, ks_ref[0, pl.ds(k0, SEL_CHUNK), :])
        if causal:
            key = k0 + lax.broadcasted_iota(jnp.int32, (1, SEL_CHUNK), 1)
            s = s + _rows4(jnp.where(key <= t_pos, 0.0, NEG_INF))
        m_new = jnp.maximum(m, jnp.max(s, axis=-1, keepdims=True))
        p = jnp.exp(s - m_new)
        acc = jnp.exp(m - m_new) * acc + _dot(p, vs_ref[0, pl.ds(k0, SEL_CHUNK), :])
        return m_new, acc

    rows = C_GROUP * NSA_TQ
    last = qb // (SEL_CHUNK // NSA_TQ)
    m, acc = lax.fori_loop(0, last, lambda c, ma: sel_chunk(c, ma[0], ma[1], False),
                           (jnp.full((rows, 1), NEG_INF, F32), jnp.zeros((rows, V7X_LANES), F32)))
    _, acc = sel_chunk(last, m, acc, True)
    o_s = acc[:, :hd] / jnp.maximum(acc[:, hd:hd + 1], 1e-30)

    band = NSA_TQ + WIN
    w0 = pl.multiple_of(jnp.maximum(s0 - WIN, 0), NSA_TQ)
    diff = t_pos - (w0 + lax.broadcasted_iota(jnp.int32, (1, band), 1))
    s = _dot_nt(qr4, kw_ref[0, pl.ds(w0, band), :]) + _rows4(
        jnp.where((diff >= 0) & (diff < WIN), 0.0, NEG_INF))
    p = jnp.exp(s - jnp.max(s, axis=-1, keepdims=True))
    acc = _dot(p, vw_ref[0, pl.ds(w0, band), :])
    o_w = acc[:, :hd] / jnp.maximum(acc[:, hd:hd + 1], 1e-30)

    gate = _sigmoid(gl_ref[0])
    outs = []
    for g in range(C_GROUP):
        r = slice(g * NSA_TQ, (g + 1) * NSA_TQ)
        gc = g * N_BRANCH
        outs.append(gate[:, gc:gc + 1] * o_c[r, :hd] + gate[:, gc + 1:gc + 2] * o_s[r]
                    + gate[:, gc + 2:gc + 3] * o_w[r])
    o_ref[0] = jnp.concatenate(outs, axis=1).astype(o_ref.dtype)


def _nsa(qn, qr, gl, k_cmp, v_cmp, ks, vs, kw, vw, bsz, seq):
    n_cmp = seq // CMP_STRIDE
    n_slc = seq // SLC_BLK
    assert C_HEAD_DIM + n_slc <= V7X_LANES and seq % SEL_CHUNK == 0
    ci = np.arange(n_cmp)[None, :] * CMP_STRIDE
    sj = np.arange(n_slc)[:, None] * SLC_BLK
    overlap_t = jnp.asarray(((ci < sj + SLC_BLK) & (ci + CMP_LEN > sj)).astype(np.float32)).astype(BF16)
    place = jnp.asarray(np.eye(n_slc, V7X_LANES, k=C_HEAD_DIM, dtype=np.float32)).astype(BF16)
    gw = C_GROUP * V7X_LANES
    q_spec = pl.BlockSpec((1, NSA_TQ, gw), lambda b, h, i: (b, i, h))
    gl_spec = pl.BlockSpec((1, NSA_TQ, V7X_LANES), lambda b, h, i: (b, i, h))
    cmp_spec = pl.BlockSpec((1, 1, n_cmp, V7X_LANES), lambda b, h, i: (b, h, 0, 0))
    kv_spec = pl.BlockSpec((1, seq, V7X_LANES), lambda b, h, i: (b, 0, h))
    as3 = lambda z: z.reshape(bsz, seq, z.shape[-1])
    return pl.pallas_call(
        _nsa_body,
        grid=(bsz, C_KV_HEADS, seq // NSA_TQ),
        in_specs=[q_spec, q_spec, gl_spec, cmp_spec, cmp_spec, kv_spec, kv_spec, kv_spec, kv_spec,
                  _const_spec((n_slc, n_cmp)), _const_spec((n_slc, V7X_LANES))],
        out_specs=pl.BlockSpec((1, NSA_TQ, C_GROUP * C_HEAD_DIM), lambda b, h, i: (b, i, h)),
        out_shape=jax.ShapeDtypeStruct((bsz, seq, C_WIDTH), BF16),
        compiler_params=_params("parallel", "parallel", "arbitrary"),
        name="nsa_attention",
    )(as3(qn), as3(qr), as3(gl), k_cmp, v_cmp, as3(ks), as3(vs), as3(kw), as3(vw), overlap_t, place)


ROW_CHUNKS = D_MODEL // V7X_LANES
assert ROW_CHUNKS == V7X_SUBLANES
DMA_UNROLL = 8
PAD_SPANS = N_EXPERTS + 1


def _tile_rows(n_rows):
    return (n_rows * ROW_CHUNKS, V7X_LANES)


def _row_tile(ref, r):
    start = r * ROW_CHUNKS
    if not isinstance(start, int):
        start = pl.multiple_of(start, ROW_CHUNKS)
    return ref.at[pl.ds(start, ROW_CHUNKS), :]


def _store_row_tiles(ref, x):
    for c in range(ROW_CHUNKS):
        ref[pl.ds(c, x.shape[0], stride=ROW_CHUNKS), :] = x[:, c * V7X_LANES:(c + 1) * V7X_LANES]


def _load_row_tiles(ref):
    rows = ref.shape[0] // ROW_CHUNKS
    return jnp.concatenate([ref[pl.ds(c, rows, stride=ROW_CHUNKS), :] for c in range(ROW_CHUNKS)], axis=1)


def _route(hn, wr_ref, br_ref, tri_ref, cnt_scr):
    logits = _dot_f32(hn, wr_ref[...]) + br_ref[...]
    lane = lax.broadcasted_iota(jnp.int32, logits.shape, 1)
    m1 = jnp.max(logits, axis=-1, keepdims=True)
    i1 = jnp.min(jnp.where(logits == m1, lane, V7X_LANES), axis=-1, keepdims=True)
    rest = jnp.where(lane == i1, NEG_INF, logits)
    m2 = jnp.max(rest, axis=-1, keepdims=True)
    i2 = jnp.min(jnp.where(rest == m2, lane, V7X_LANES), axis=-1, keepdims=True)
    e2 = jnp.exp(m2 - m1)
    den = 1.0 + e2
    hit1, hit2 = lane == i1, lane == i2
    hits = jnp.where(hit1 | hit2, 1.0, 0.0)
    before = jnp.dot(tri_ref[...], hits.astype(BF16), preferred_element_type=F32) + cnt_scr[...]
    r1 = jnp.sum(jnp.where(hit1, before, 0.0), axis=-1, keepdims=True).astype(jnp.int32)
    r2 = jnp.sum(jnp.where(hit2, before, 0.0), axis=-1, keepdims=True).astype(jnp.int32)
    cnt_scr[...] = cnt_scr[...] + jnp.sum(hits, axis=0, keepdims=True)
    idx = jnp.where(lane == 0, i1, jnp.where(lane == 1, i2, jnp.where(lane == 2, r1,
                    jnp.where(lane == 3, r2, 0))))
    gate = jnp.where(lane == 0, 1.0 / den, jnp.where(lane == 1, e2 / den, 0.0))
    return idx, gate


def _row_copy(src, dst, sem):
    return pltpu.make_async_copy(src, dst, sem)


def _dispatch_body(pad_ref, dest_ref, h_ref, g_ref, xs_ref, hn_scr, zero_scr, sems, fill_sem):
    i = pl.program_id(0)
    slot = i % 2
    hn_ref = hn_scr.at[slot]
    _store_row_tiles(hn_ref, _rms(h_ref[...], g_ref[...]))

    @pl.when(i == 0)
    def _():
        zero_scr[...] = jnp.zeros_like(zero_scr)
        for e in range(PAD_SPANS):
            start, count = pad_ref[e], pad_ref[PAD_SPANS + e]

            def fill(r, c):
                _row_copy(zero_scr, _row_tile(xs_ref, start + r), fill_sem).start()
                return c

            def filled(r, c):
                _row_copy(zero_scr, _row_tile(xs_ref, 0), fill_sem).wait()
                return c

            lax.fori_loop(0, count, fill, 0)
            lax.fori_loop(0, count, filled, 0)

    def issue(blk, c):
        for u in range(DMA_UNROLL):
            r = blk * DMA_UNROLL + u
            for k in range(TOP_K):
                _row_copy(_row_tile(hn_ref, r), _row_tile(xs_ref, dest_ref[TOP_K * r + k]),
                          sems.at[slot]).start(priority=k)
        return c

    lax.fori_loop(0, DMA_ROWS // DMA_UNROLL, issue, 0)

    def drain(s):
        for _ in range(TOP_K):
            _row_copy(hn_scr.at[s], xs_ref.at[pl.ds(0, DMA_ROWS * ROW_CHUNKS), :], sems.at[s]).wait()

    @pl.when(i >= 1)
    def _():
        drain(1 - slot)

    @pl.when(i == pl.num_programs(0) - 1)
    def _():
        drain(slot)


def _dispatch(pad_info, dest, h, gain, p_rows):
    n = h.shape[0]
    return pl.pallas_call(
        _dispatch_body,
        grid_spec=pltpu.PrefetchScalarGridSpec(
            num_scalar_prefetch=1,
            grid=(n // DMA_ROWS,),
            in_specs=[pl.BlockSpec((TOP_K * DMA_ROWS,), lambda i, pad: (i,), memory_space=pltpu.SMEM),
                      pl.BlockSpec((DMA_ROWS, D_MODEL), lambda i, pad: (i, 0)),
                      pl.BlockSpec((1, D_MODEL), lambda i, pad: (0, 0))],
            out_specs=pl.BlockSpec(memory_space=pl.ANY),
            scratch_shapes=[pltpu.VMEM((2,) + _tile_rows(DMA_ROWS), F32), pltpu.VMEM(_tile_rows(1), F32),
                            pltpu.SemaphoreType.DMA((2,)), pltpu.SemaphoreType.DMA(())]),
        out_shape=jax.ShapeDtypeStruct(_tile_rows(p_rows), F32),
        compiler_params=_params("arbitrary"),
        name="moe_dispatch",
    )(pad_info, dest, h, gain.reshape(1, -1))


def _experts_body(blk_e_ref, n_used_ref, prev_ref, cur_ref, x_ref, wg_ref, wu_ref, wd_ref, g_ref,
                  o_scr, sems):
    i = pl.program_id(0)
    slot = i % 2

    def scatter(rows_ref, s):
        for r in range(MOE_ROWS):
            _row_copy(_row_tile(o_scr.at[s], r), _row_tile(g_ref, rows_ref[r]), sems.at[s]).start(priority=r % 2)

    def drain(s):
        _row_copy(o_scr.at[s], g_ref.at[pl.ds(0, MOE_ROWS * ROW_CHUNKS), :], sems.at[s]).wait()

    @pl.when(i == 0)
    def _():
        o_scr[...] = jnp.zeros_like(o_scr)

    @pl.when(i >= 1)
    def _():
        drain(slot)

    x = _load_row_tiles(x_ref).astype(BF16)
    scatter(prev_ref, 1 - slot)
    gate = jnp.dot(x, wg_ref[0], preferred_element_type=F32)
    up = jnp.dot(x, wu_ref[0], preferred_element_type=F32)
    act = (gate * _sigmoid(gate)) * up
    _store_row_tiles(o_scr.at[slot], _dot(act, wd_ref[0]))

    @pl.when(i == pl.num_programs(0) - 1)
    def _():
        scatter(cur_ref, slot)
        drain(1 - slot)
        drain(slot)


def _experts(blk_e, n_used, slots, xs, wg, wu, wd):
    p_rows = xs.shape[0] // ROW_CHUNKS
    out_rows = slots.shape[0]
    idx_spec = lambda off: pl.BlockSpec((MOE_ROWS,), lambda i, be, nu: (i + off,), memory_space=pltpu.SMEM)
    x_spec = pl.BlockSpec(_tile_rows(MOE_ROWS), lambda i, be, nu: (jnp.minimum(i, nu[0] - 1), 0))
    w_spec = lambda w: pl.BlockSpec((1,) + w.shape[1:], lambda i, be, nu: (be[i], 0, 0))
    return pl.pallas_call(
        _experts_body,
        grid_spec=pltpu.PrefetchScalarGridSpec(
            num_scalar_prefetch=2,
            grid=(p_rows // MOE_ROWS,),
            in_specs=[idx_spec(0), idx_spec(1), x_spec, w_spec(wg), w_spec(wu), w_spec(wd)],
            out_specs=pl.BlockSpec(memory_space=pl.ANY),
            scratch_shapes=[pltpu.VMEM((2,) + _tile_rows(MOE_ROWS), F32), pltpu.SemaphoreType.DMA((2,))]),
        out_shape=jax.ShapeDtypeStruct(_tile_rows(out_rows), F32),
        compiler_params=_params("arbitrary"),
        name="moe_experts",
    )(blk_e, n_used, slots, slots, xs, wg, wu, wd)


def _combine_body(h_ref, gate_ref, g_ref, y0_ref, y1_ref, o_ref):
    gate = gate_ref[...]
    moe = gate[:, 0:1] * _load_row_tiles(y0_ref) + gate[:, 1:2] * _load_row_tiles(y1_ref)
    o_ref[...] = _rms(h_ref[...] + moe, g_ref[...])


def _combine(h, gates, gain, y_rows, tm=512):
    n = h.shape[0]
    steps = n // tm
    return pl.pallas_call(
        _combine_body,
        grid=(steps,),
        in_specs=[_rows_spec(tm, D_MODEL), _rows_spec(tm, V7X_LANES), _const_spec((1, D_MODEL)),
                  pl.BlockSpec(_tile_rows(tm), lambda i: (i, 0)),
                  pl.BlockSpec(_tile_rows(tm), lambda i: (i + steps, 0))],
        out_specs=_rows_spec(tm, D_MODEL),
        out_shape=jax.ShapeDtypeStruct((n, D_MODEL), F32),
        compiler_params=_params("parallel"),
        name="moe_combine_norm",
    )(h, gates, gain.reshape(1, -1), y_rows, y_rows)


def _moe_layout(idx, counts):
    n = idx.shape[0]
    nk = n * TOP_K
    counts = counts[0, :N_EXPERTS].astype(jnp.int32)
    padded = ((counts + MOE_ROWS - 1) // MOE_ROWS) * MOE_ROWS
    p_end = jnp.cumsum(padded)
    p_start = p_end - padded
    experts = jnp.arange(N_EXPERTS, dtype=jnp.int32)[None, None, :]
    first = jnp.sum(jnp.where(idx[:, :TOP_K, None] == experts, p_start[None, None, :], 0), axis=-1)
    dest = (first + idx[:, TOP_K:2 * TOP_K]).reshape(nk)
    n_blk = (nk + MOE_ROWS - 1) // MOE_ROWS + N_EXPERTS
    blk_e = jnp.minimum(jnp.searchsorted(p_end, jnp.arange(n_blk) * MOE_ROWS, side="right"),
                        N_EXPERTS - 1).astype(jnp.int32)
    n_used = (p_end[-1:] // MOE_ROWS).astype(jnp.int32)
    p_rows = n_blk * MOE_ROWS
    pad_info = jnp.concatenate([p_start + counts, p_end[-1:], padded - counts,
                                p_rows - p_end[-1:]]).astype(jnp.int32)
    pair_by_row = jnp.argsort(dest).astype(jnp.int32)
    rows = jnp.arange(p_rows, dtype=jnp.int32)
    row_e = jnp.repeat(blk_e, MOE_ROWS)
    offset = rows - p_start[row_e]
    before = (jnp.cumsum(counts) - counts)[row_e]
    occupied = offset < counts[row_e]
    pair = pair_by_row[jnp.clip(before + offset, 0, nk - 1)]
    spare = nk + rows - (before + jnp.minimum(offset, counts[row_e]))
    slots = jnp.where(occupied, (pair % TOP_K) * n + pair // TOP_K, spare)
    slots = jnp.concatenate([p_rows + jnp.arange(MOE_ROWS, dtype=jnp.int32), slots]).astype(jnp.int32)
    return dest.astype(jnp.int32), blk_e, n_used, pad_info, p_rows, slots


def _moe_final(h, idx, gates, counts, norm_g, wg, wu, wd, final_g):
    dest, blk_e, n_used, pad_info, p_rows, slots = _moe_layout(idx, counts)
    xs = _dispatch(pad_info, dest, h, norm_g, p_rows)
    y = _experts(blk_e, n_used, slots, xs, wg.astype(BF16), wu.astype(BF16), wd.astype(BF16))
    return _combine(h, gates, final_g, y)


def kernel(x, e_norm_mix, e_w_in, sgu_ln_g, sgu_ln_b, sgu_w, sgu_b, rwkv_mu, rwkv_w0, rwkv_w2,
           rwkv_a0, rwkv_a2, rwkv_g2, rwkv_k_k, rwkv_k_a, rwkv_r_k, rwkv_gn_g, rwkv_gn_b, e_w_out,
           e_norm_ffn, ffn_w_gate, ffn_w_up, ffn_w_down, o_norm_mix, o_w_in, nsa_cmp_pos_k,
           nsa_cmp_pos_v, nsa_cmp_k_w1, nsa_cmp_k_w2, nsa_cmp_v_w1, nsa_cmp_v_w2, conv_w, o_w_out,
           o_norm_ffn, moe_router, moe_router_b, moe_w_gate, moe_w_up, moe_w_down, final_norm):
    bsz, seq, d = x.shape
    n = bsz * seq
    h = x.reshape(n, d)

    ya, scan_ops, g = _layer0_front(h, seq, e_norm_mix[0], e_w_in[0].astype(BF16), sgu_ln_g[0], sgu_ln_b[0],
                                    sgu_w[0], sgu_b[0], rwkv_mu[0], rwkv_w0[0], rwkv_w2[0], rwkv_a0[0],
                                    rwkv_a2[0], rwkv_g2[0])
    rwka, v = _to_scan_layout(scan_ops, bsz, seq)
    ys = _rwkv_scan(rwka, v, rwkv_k_k[0], rwkv_k_a[0], rwkv_r_k[0], rwkv_gn_g[0], rwkv_gn_b[0], bsz, seq)
    h = _layer0_back(h, ya, _from_scan_layout(ys, bsz, seq), g, e_w_out[0].astype(BF16), e_norm_ffn[0],
                     ffn_w_gate[0].astype(BF16), ffn_w_up[0].astype(BF16), ffn_w_down[0].astype(BF16))

    qn, qr, kcvc, ks, vs, kw, vw, gl, bcd = _in_proj1(h, o_norm_mix[0], o_w_in[0], seq)
    k_cmp, v_cmp = _compress(kcvc, nsa_cmp_pos_k[0], nsa_cmp_pos_v[0], nsa_cmp_k_w1[0], nsa_cmp_v_w1[0],
                             nsa_cmp_k_w2[0], nsa_cmp_v_w2[0], bsz, seq)
    yc = _nsa(qn, qr, gl, k_cmp, v_cmp, ks, vs, kw, vw, bsz, seq).reshape(n, C_WIDTH)
    h, idx, gates, counts = _layer1_mid(h, yc, bcd, conv_w[0], o_w_out[0].astype(BF16), seq, o_norm_ffn[0],
                                        moe_router[0], moe_router_b[0])
    out = _moe_final(h, idx, gates, counts, o_norm_ffn[0], moe_w_gate[0], moe_w_up[0], moe_w_down[0],
                     final_norm)
    return out.reshape(bsz, seq, d)
```

```python
import functools

import jax
import jax.numpy as jnp
import numpy as np
from jax import lax
from jax.experimental import pallas as pl
from jax.experimental.pallas import tpu as pltpu

F32 = jnp.float32
BF16 = jnp.bfloat16

D_MODEL = 1024
A_GROUPS = 4
A_GROUP_DIM = 128
A_WIDTH = A_GROUPS * A_GROUP_DIM
CHUNK = 128
SGU_LN_EPS = 1e-5
B_HEADS = 8
B_HEAD_DIM = 64
B_WIDTH = B_HEADS * B_HEAD_DIM
DECAY_LORA = 64
ICLR_LORA = 64
GATE_LORA = 128
B_IN = 3 * B_WIDTH + DECAY_LORA + ICLR_LORA + GATE_LORA
RWKV_GN_EPS = 64e-5
C_HEADS = 8
C_KV_HEADS = 2
C_GROUP = C_HEADS // C_KV_HEADS
C_HEAD_DIM = 64
C_WIDTH = C_HEADS * C_HEAD_DIM
KV_WIDTH = C_KV_HEADS * C_HEAD_DIM
N_BRANCH = 3
CMP_LEN = 32
CMP_STRIDE = 16
CMP_HIDDEN = 256
SLC_BLK = 64
SEL_TOPK = 8
WIN = 512
NSA_TQ = 512
ROT_DIM = C_HEAD_DIM // 4
ROPE_THETA = 500000.0
D_WIDTH = 512
CONV_W = 3
FFN_DIM = 2816
N_EXPERTS = 8
TOP_K = 2
EXPERT_DIM = 1408
NORM_EPS = 1e-6
NEG_INF = -1e30
FORCE_BONUS = 1e6

V7X_LANES = 128
V7X_SUBLANES = 8
V7X_VMEM_LIMIT = 56 * 1024 * 1024

MOE_ROWS = 512
DMA_ROWS = 512


def _params(*sem):
    return pltpu.CompilerParams(dimension_semantics=sem, vmem_limit_bytes=V7X_VMEM_LIMIT)


def _const_spec(shape):
    zeros = (0,) * len(shape)
    return pl.BlockSpec(shape, lambda *_: zeros)


def _rows_spec(tm, width):
    return pl.BlockSpec((tm, width), lambda i: (i, 0))


def _rms(x, g):
    return x * lax.rsqrt(jnp.mean(x * x, axis=-1, keepdims=True) + NORM_EPS) * g


def _gelu(x):
    return x * (0.5 * (1.0 + jnp.tanh(0.7978845608028654 * (x + 0.044715 * (x * x * x)))))


def _sigmoid(x):
    return 1.0 / (1.0 + jnp.exp(-x))


def _dot(a, b):
    return jnp.dot(a.astype(BF16), b.astype(BF16), preferred_element_type=F32)


def _dot_nt(a, b):
    return lax.dot_general(a.astype(BF16), b.astype(BF16), (((1,), (1,)), ((), ())),
                           preferred_element_type=F32)


def _split_bf16(a):
    hi = a.astype(BF16)
    lo = (a - hi.astype(F32)).astype(BF16)
    return hi, lo


def _dot_f32(a, b):
    ah, al = _split_bf16(a)
    bh, bl = _split_bf16(b)
    d = functools.partial(jnp.dot, preferred_element_type=F32)
    return d(ah, bh) + (d(al, bh) + d(ah, bl))


def _masked_softmax(s, mask):
    s = jnp.where(mask, s, NEG_INF)
    m = jnp.max(s, axis=-1, keepdims=True)
    p = jnp.where(mask, jnp.exp(s - m), 0.0)
    return p / jnp.maximum(jnp.sum(p, axis=-1, keepdims=True), 1e-30)


def _sgu_chunk(p_uv, lng_ref, lnb_ref, w_ref, b_ref):
    row = lax.broadcasted_iota(jnp.int32, (CHUNK, CHUNK), 0)
    col = lax.broadcasted_iota(jnp.int32, (CHUNK, CHUNK), 1)
    causal = col <= row
    u = _gelu(p_uv[:, :A_WIDTH])
    v = _gelu(p_uv[:, A_WIDTH:])
    outs = []
    for g in range(A_GROUPS):
        cols = slice(g * A_GROUP_DIM, (g + 1) * A_GROUP_DIM)
        vg = v[:, cols]
        mu = jnp.mean(vg, axis=-1, keepdims=True)
        dv = vg - mu
        var = jnp.mean(dv * dv, axis=-1, keepdims=True)
        vn = dv * lax.rsqrt(var + SGU_LN_EPS) * lng_ref[:, cols] + lnb_ref[:, cols]
        wm = jnp.where(causal, w_ref[g], 0.0)
        mixed = _dot(wm, vn) + b_ref[:, g:g + 1]
        outs.append(u[:, cols] * mixed)
    return jnp.concatenate(outs, axis=1)


def _softplus(x):
    return jnp.maximum(x, 0.0) + jnp.log(1.0 + jnp.exp(-jnp.abs(x)))


SCAN_OPERANDS = 5


def _layer0_front_body(x_ref, gain_ref, win_ref, lng_ref, lnb_ref, sw_ref, sb_ref, mu_ref, w0_ref, w2_ref,
                       a0_ref, a2_ref, g2_ref, ya_o, s_o, g_o, prev_scr, *, tiles_per_seq):
    tm = x_ref.shape[0]

    @pl.when(pl.program_id(0) == 0)
    def _():
        prev_scr[...] = jnp.zeros_like(prev_scr)

    p = jnp.dot(_rms(x_ref[...], gain_ref[...]).astype(BF16), win_ref[...], preferred_element_type=F32)
    for c in range(tm // CHUNK):
        rows = slice(c * CHUNK, (c + 1) * CHUNK)
        ya_o[rows, :] = _sgu_chunk(p[rows, :2 * A_WIDTH], lng_ref, lnb_ref, sw_ref, sb_ref).astype(ya_o.dtype)
    x = p[:, 2 * A_WIDTH:]
    first = (pl.program_id(0) % tiles_per_seq) == 0
    prev_row = jnp.where(first, 0.0, prev_scr[...])
    rowid = lax.broadcasted_iota(jnp.int32, (tm, 1), 0)
    shifted = jnp.where(rowid == 0, prev_row, pltpu.roll(x, 1, axis=0))
    prev_scr[...] = x[tm - 1:tm, :]
    xm = x + (shifted - x) * mu_ref[...]
    o = 3 * B_WIDTH
    wl = xm[:, o:o + DECAY_LORA]
    al = xm[:, o + DECAY_LORA:o + DECAY_LORA + ICLR_LORA]
    gl = xm[:, o + DECAY_LORA + ICLR_LORA:]
    w = -_softplus(-(w0_ref[...] + _dot(jnp.tanh(wl), w2_ref[...]))) - 0.5
    s_o[0] = xm[:, :B_WIDTH]
    s_o[1] = jnp.exp(-jnp.exp(w))
    s_o[2] = xm[:, B_WIDTH:2 * B_WIDTH]
    s_o[3] = _sigmoid(a0_ref[...] + _dot(al, a2_ref[...]))
    s_o[4] = xm[:, 2 * B_WIDTH:3 * B_WIDTH]
    g_o[...] = _dot(_sigmoid(gl), g2_ref[...])


def _layer0_front(h, seq, gain, w_in, ln_g, ln_b, w_s, b_s, mu, w0, w2, a0, a2, g2, tm=512):
    n, d = h.shape
    perm = np.arange(B_WIDTH).reshape(B_HEADS, B_HEAD_DIM).T.reshape(-1)
    r0, k0 = 2 * A_WIDTH, 2 * A_WIDTH + B_WIDTH
    cols = np.concatenate([np.arange(r0), r0 + perm, k0 + perm, np.arange(k0 + B_WIDTH, w_in.shape[1])])
    w_in = w_in[:, cols]
    mu = mu[cols[r0:] - r0]
    w0, w2, a0, a2 = w0[perm], w2[:, perm], a0[perm], a2[:, perm]
    g2 = g2[:, perm]
    row = lambda v: v.reshape(1, -1)
    consts = [row(gain), w_in, row(ln_g), row(ln_b), w_s, b_s.T, row(mu), row(w0), w2, row(a0), a2, g2]
    outs = [jax.ShapeDtypeStruct((n, A_WIDTH), BF16), jax.ShapeDtypeStruct((SCAN_OPERANDS, n, B_WIDTH), F32),
            jax.ShapeDtypeStruct((n, B_WIDTH), F32)]
    return pl.pallas_call(
        functools.partial(_layer0_front_body, tiles_per_seq=seq // tm),
        grid=(n // tm,),
        in_specs=[_rows_spec(tm, d)] + [_const_spec(c.shape) for c in consts],
        out_specs=[_rows_spec(tm, A_WIDTH), pl.BlockSpec((SCAN_OPERANDS, tm, B_WIDTH), lambda i: (0, i, 0)),
                   _rows_spec(tm, B_WIDTH)],
        out_shape=outs,
        scratch_shapes=[pltpu.VMEM((1, B_IN), F32)],
        compiler_params=_params("arbitrary"),
        name="layer0_front",
    )(h, *consts)


def _rwkv_scan_body(r_ref, w_ref, k0_ref, a_ref, v_ref, kkp_ref, kap_ref, rk_ref, gng_ref, gnb_ref,
                    y_ref, s_ref, kkn_ref, ka_ref, km_ref):
    n, tt = r_ref.shape[0], r_ref.shape[1]

    @pl.when(pl.program_id(0) == 0)
    def _():
        s_ref[...] = jnp.zeros_like(s_ref)

    per_dim = lambda p_ref: p_ref[...][:, None, :]
    k0 = k0_ref[...]
    a = a_ref[...]
    kk = k0 * per_dim(kkp_ref)
    kkn = kk / jnp.maximum(jnp.sqrt(jnp.sum(kk * kk, axis=0, keepdims=True)), 1e-12)
    km = k0 * (1.0 + (a - 1.0) * per_dim(kap_ref))
    kkn_ref[...] = kkn
    ka_ref[...] = kkn * a
    km_ref[...] = km

    zero = jnp.zeros((n, r_ref.shape[2]), F32)

    def sa_init(j, acc):
        return acc + s_ref[j] * kkn_ref[j, pl.ds(0, 1), :]

    sa0 = lax.fori_loop(0, n, sa_init, zero, unroll=8)

    def step(t, sa):
        v_t = v_ref[t]
        now = pl.ds(t, 1)
        nxt = pl.ds(jnp.minimum(t + 1, tt - 1), 1)

        def jbody(j, carry):
            y, san = carry
            sn = s_ref[j] * w_ref[j, now, :] + (v_t * km_ref[j, now, :] - sa * ka_ref[j, now, :])
            s_ref[j] = sn
            return y + sn * r_ref[j, now, :], san + sn * kkn_ref[j, nxt, :]

        y, san = lax.fori_loop(0, n, jbody, (zero, zero), unroll=8)
        y_ref[t] = y
        return san

    lax.fori_loop(0, tt, step, sa0)

    y = y_ref[...]
    ym = jnp.mean(y, axis=1, keepdims=True)
    dy = y - ym
    yv = jnp.mean(dy * dy, axis=1, keepdims=True)
    yn = dy * lax.rsqrt(yv + RWKV_GN_EPS) * gng_ref[...][None] + gnb_ref[...][None]
    bonus = jnp.sum(r_ref[...] * km * per_dim(rk_ref), axis=0)
    y_ref[...] = yn + bonus[:, None, :] * v_ref[...]


def _rwkv_scan(rwka, v, k_k, k_a, r_k, gn_g, gn_b, bsz, seq, tt=32):
    n = B_HEAD_DIM
    lanes = bsz * B_HEADS

    def lane_param(p):
        return jnp.tile(p.reshape(B_HEADS, n).T, (1, bsz))

    op_spec = lambda a: pl.BlockSpec((None, n, tt, lanes), lambda c: (a, 0, c, 0))
    slab_spec = pl.BlockSpec((tt, n, lanes), lambda c: (c, 0, 0))
    par_spec = _const_spec((n, lanes))
    return pl.pallas_call(
        _rwkv_scan_body,
        grid=(seq // tt,),
        in_specs=[op_spec(a) for a in range(SCAN_OPERANDS - 1)] + [slab_spec] + [par_spec] * 5,
        out_specs=slab_spec,
        out_shape=jax.ShapeDtypeStruct((seq, n, lanes), F32),
        scratch_shapes=[pltpu.VMEM((n, n, lanes), F32)] + [pltpu.VMEM((n, tt, lanes), F32)] * 3,
        compiler_params=_params("arbitrary"),
        name="rwkv_scan",
    )(*([rwka] * (SCAN_OPERANDS - 1)), v, lane_param(k_k), lane_param(k_a), lane_param(r_k.reshape(-1)),
      lane_param(gn_g), lane_param(gn_b))


SCAN_TT = 128
SCAN_UNROLL = 8


def _time_to_lanes(x_ref, a_ref):
    for b in range(x_ref.shape[0]):
        for blk in range(B_WIDTH // V7X_LANES):
            cols = slice(blk * V7X_LANES, (blk + 1) * V7X_LANES)
            a_ref[b, cols, :] = x_ref[b, :, cols].T


def _to_scan_rows_body(x_ref, o_ref, a_ref):
    bsz = x_ref.shape[0]
    _time_to_lanes(x_ref, a_ref)

    def dims(i, carry):
        for u in range(SCAN_UNROLL):
            d = i * SCAN_UNROLL + u
            rows = pl.ds(pl.multiple_of(d * B_HEADS, B_HEADS), B_HEADS)
            z = jnp.concatenate([a_ref[b, rows, :] for b in range(bsz)], axis=0)
            o_ref[d] = z.T
        return carry

    lax.fori_loop(0, B_HEAD_DIM // SCAN_UNROLL, dims, 0)


def _to_scan_slab_body(x_ref, o_ref, a_ref):
    bsz = x_ref.shape[0]
    _time_to_lanes(x_ref, a_ref)

    def dims(i, carry):
        for u in range(SCAN_UNROLL):
            d = i * SCAN_UNROLL + u
            z = jnp.concatenate([a_ref[b, pl.ds(d, B_HEADS, stride=B_HEAD_DIM), :] for b in range(bsz)],
                                axis=0)
            o_ref[:, d, :] = z.T
        return carry

    lax.fori_loop(0, B_HEAD_DIM // SCAN_UNROLL, dims, 0)


def _to_scan_layout(ops, bsz, seq):
    tt = min(SCAN_TT, seq)
    lanes = bsz * B_HEADS
    ops4 = ops.reshape(SCAN_OPERANDS, bsz, seq, B_WIDTH)
    scratch = [pltpu.VMEM((bsz, B_WIDTH, tt), F32)]
    rwka = pl.pallas_call(
        _to_scan_rows_body,
        grid=(SCAN_OPERANDS - 1, seq // tt),
        in_specs=[pl.BlockSpec((None, bsz, tt, B_WIDTH), lambda a, i: (a, 0, i, 0))],
        out_specs=pl.BlockSpec((None, B_HEAD_DIM, tt, lanes), lambda a, i: (a, 0, i, 0)),
        out_shape=jax.ShapeDtypeStruct((SCAN_OPERANDS - 1, B_HEAD_DIM, seq, lanes), F32),
        scratch_shapes=scratch,
        compiler_params=_params("parallel", "arbitrary"),
        name="to_scan_rows",
    )(ops4)
    v = pl.pallas_call(
        _to_scan_slab_body,
        grid=(seq // tt,),
        in_specs=[pl.BlockSpec((None, bsz, tt, B_WIDTH), lambda i: (SCAN_OPERANDS - 1, 0, i, 0))],
        out_specs=pl.BlockSpec((tt, B_HEAD_DIM, lanes), lambda i: (i, 0, 0)),
        out_shape=jax.ShapeDtypeStruct((seq, B_HEAD_DIM, lanes), F32),
        scratch_shapes=scratch,
        compiler_params=_params("arbitrary"),
        name="to_scan_slab",
    )(ops4)
    return rwka, v


def _from_scan_body(y_ref, o_ref, a_ref):
    bsz = o_ref.shape[0]

    def dims(i, carry):
        for u in range(SCAN_UNROLL):
            d = i * SCAN_UNROLL + u
            zt = y_ref[:, d, :].T
            rows = pl.ds(pl.multiple_of(d * B_HEADS, B_HEADS), B_HEADS)
            for b in range(bsz):
                a_ref[b, rows, :] = zt[b * B_HEADS:(b + 1) * B_HEADS, :]
        return carry

    lax.fori_loop(0, B_HEAD_DIM // SCAN_UNROLL, dims, 0)
    for b in range(bsz):
        for blk in range(B_WIDTH // V7X_LANES):
            cols = slice(blk * V7X_LANES, (blk + 1) * V7X_LANES)
            o_ref[b, :, cols] = a_ref[b, cols, :].T


def _from_scan_layout(y, bsz, seq):
    tt = min(SCAN_TT, seq)
    return pl.pallas_call(
        _from_scan_body,
        grid=(seq // tt,),
        in_specs=[pl.BlockSpec((tt, B_HEAD_DIM, bsz * B_HEADS), lambda i: (i, 0, 0))],
        out_specs=pl.BlockSpec((bsz, tt, B_WIDTH), lambda i: (0, i, 0)),
        out_shape=jax.ShapeDtypeStruct((bsz, seq, B_WIDTH), F32),
        scratch_shapes=[pltpu.VMEM((bsz, B_WIDTH, tt), F32)],
        compiler_params=_params("arbitrary"),
        name="from_scan_layout",
    )(y).reshape(bsz * seq, B_WIDTH)


def _layer1_mid_body(h_ref, yc_ref, bcd_ref, prev_ref, cw_ref, w_ref, g_ref, wr_ref, br_ref, tri_ref,
                     o_ref, idx_o, gate_o, cnt_o, cnt_scr, *, tiles_per_seq):
    tm = h_ref.shape[0]

    @pl.when(pl.program_id(0) == 0)
    def _():
        cnt_scr[...] = jnp.zeros_like(cnt_scr)

    first = (pl.program_id(0) % tiles_per_seq) == 0
    z = bcd_ref[:, D_WIDTH:2 * D_WIDTH] * bcd_ref[:, 2 * D_WIDTH:]
    zp = jnp.where(first, 0.0, prev_ref[:, D_WIDTH:2 * D_WIDTH] * prev_ref[:, 2 * D_WIDTH:])
    rowid = lax.broadcasted_iota(jnp.int32, (tm, 1), 0)
    z1 = jnp.where(rowid == 0, zp[7:8, :], pltpu.roll(z, 1, axis=0))
    z2 = pltpu.roll(z, 2, axis=0)
    z2 = jnp.where(rowid == 0, zp[6:7, :], jnp.where(rowid == 1, zp[7:8, :], z2))
    y = cw_ref[0:1, :] * z2 + cw_ref[1:2, :] * z1 + cw_ref[2:3, :] * z
    yd = bcd_ref[:, :D_WIDTH] * y
    h_new = h_ref[...] + (_dot(yc_ref[...], w_ref[:C_WIDTH, :]) + _dot(yd, w_ref[C_WIDTH:, :]))
    o_ref[...] = h_new
    idx_o[...], gate_o[...] = _route(_rms(h_new, g_ref[...]), wr_ref, br_ref, tri_ref, cnt_scr)
    cnt_o[...] = cnt_scr[...]


def _layer1_mid(h, yc, bcd, conv_w, w_out, seq, gain, w_router, b_router, tm=512):
    n = h.shape[0]
    per8 = tm // V7X_SUBLANES
    wr = jnp.pad(w_router, ((0, 0), (0, V7X_LANES - N_EXPERTS)))
    br = jnp.pad(b_router.reshape(1, -1), ((0, 0), (0, V7X_LANES - N_EXPERTS)), constant_values=NEG_INF)
    tri = jnp.asarray(np.tril(np.ones((tm, tm), np.float32), k=-1)).astype(BF16)
    return pl.pallas_call(
        functools.partial(_layer1_mid_body, tiles_per_seq=seq // tm),
        grid=(n // tm,),
        in_specs=[_rows_spec(tm, D_MODEL), _rows_spec(tm, C_WIDTH), _rows_spec(tm, 3 * D_WIDTH),
                  pl.BlockSpec((V7X_SUBLANES, 3 * D_WIDTH), lambda i: (jnp.maximum(i * per8 - 1, 0), 0)),
                  _const_spec((CONV_W, D_WIDTH)), _const_spec(w_out.shape), _const_spec((1, D_MODEL)),
                  _const_spec(wr.shape), _const_spec(br.shape), _const_spec(tri.shape)],
        out_specs=[_rows_spec(tm, D_MODEL), _rows_spec(tm, V7X_LANES), _rows_spec(tm, V7X_LANES),
                   _const_spec((1, V7X_LANES))],
        out_shape=[jax.ShapeDtypeStruct((n, D_MODEL), F32), jax.ShapeDtypeStruct((n, V7X_LANES), jnp.int32),
                   jax.ShapeDtypeStruct((n, V7X_LANES), F32), jax.ShapeDtypeStruct((1, V7X_LANES), F32)],
        scratch_shapes=[pltpu.VMEM((1, V7X_LANES), F32)],
        compiler_params=_params("arbitrary"),
        name="layer1_mid",
    )(h, yc, bcd, bcd, conv_w, w_out, gain.reshape(1, -1), wr, br, tri)


def _layer0_back_body(h_ref, ya_ref, ys_ref, gm_ref, wo_ref, g_ref, wg_ref, wu_ref, wd_ref, o_ref):
    yb = ys_ref[...] * gm_ref[...]
    h = h_ref[...] + (_dot(ya_ref[...], wo_ref[:A_WIDTH, :]) + _dot(yb, wo_ref[A_WIDTH:, :]))
    hn = _rms(h, g_ref[...]).astype(BF16)
    gate = jnp.dot(hn, wg_ref[...], preferred_element_type=F32)
    up = jnp.dot(hn, wu_ref[...], preferred_element_type=F32)
    act = (gate * _sigmoid(gate)) * up
    o_ref[...] = h + _dot(act, wd_ref[...])


def _layer0_back(h, ya, ys, gm, w_out, gain, wg, wu, wd, tm=512):
    n = h.shape[0]
    perm = np.arange(B_WIDTH).reshape(B_HEADS, B_HEAD_DIM).T.reshape(-1)
    w_out = jnp.concatenate([w_out[:A_WIDTH], w_out[A_WIDTH:][perm]], axis=0)
    return pl.pallas_call(
        _layer0_back_body,
        grid=(n // tm,),
        in_specs=[_rows_spec(tm, D_MODEL), _rows_spec(tm, A_WIDTH), _rows_spec(tm, B_WIDTH),
                  _rows_spec(tm, B_WIDTH), _const_spec(w_out.shape), _const_spec((1, D_MODEL)),
                  _const_spec(wg.shape), _const_spec(wu.shape), _const_spec(wd.shape)],
        out_specs=_rows_spec(tm, D_MODEL),
        out_shape=jax.ShapeDtypeStruct((n, D_MODEL), F32),
        compiler_params=_params("parallel"),
        name="layer0_back",
    )(h, ya, ys, gm, w_out, gain.reshape(1, -1), wg, wu, wd)


def _rope(x, c, s_up, s_dn):
    half = ROT_DIM // 2
    return x * c + pltpu.roll(x, half, axis=1) * s_up + pltpu.roll(x, V7X_LANES - half, axis=1) * s_dn


def _lane_blocks(x):
    return [x[:, i * V7X_LANES:(i + 1) * V7X_LANES] for i in range(x.shape[1] // V7X_LANES)]


def _head_blocks(x):
    zeros = jnp.zeros((x.shape[0], V7X_LANES - C_HEAD_DIM), x.dtype)
    return [jnp.concatenate([x[:, h * C_HEAD_DIM:(h + 1) * C_HEAD_DIM], zeros], axis=1)
            for h in range(x.shape[1] // C_HEAD_DIM)]


def _in_proj1_body(x_ref, g_ref, c_ref, su_ref, sd_ref, kb_ref, one_ref, wq_ref, wkc_ref, wkv_ref,
                   wgl_ref, wbcd_ref, qn_o, qr_o, kc_o, ks_o, vs_o, kw_o, vw_o, gl_o, bcd_o):
    xn = _rms(x_ref[...], g_ref[...]).astype(BF16)
    d = functools.partial(jnp.dot, preferred_element_type=F32)
    c, su, sd = c_ref[...], su_ref[...], sd_ref[...]
    rope = lambda z: _rope(z, c, su, sd)
    q = _head_blocks(d(xn, wq_ref[...]) * (C_HEAD_DIM ** -0.5))
    qn_o[...] = jnp.concatenate(q, axis=1).astype(BF16)
    qr_o[...] = jnp.concatenate([rope(z) for z in q], axis=1).astype(BF16)
    kc_o[...] = d(xn, wkc_ref[...])
    kv = _head_blocks(d(xn, wkv_ref[...]))
    hk = C_KV_HEADS
    ks_o[...] = jnp.concatenate([rope(z) + kb_ref[...] for z in kv[:hk]], axis=1).astype(BF16)
    vs_o[...] = jnp.concatenate([z + one_ref[...] for z in kv[hk:2 * hk]], axis=1).astype(BF16)
    kw_o[...] = jnp.concatenate([rope(z) for z in kv[2 * hk:3 * hk]], axis=1).astype(BF16)
    vw_o[...] = jnp.concatenate([z + one_ref[...] for z in kv[3 * hk:]], axis=1).astype(BF16)
    gl_o[...] = d(xn, wgl_ref[...])
    bcd_o[...] = d(xn, wbcd_ref[...])


def _head_tables(seq):
    half = ROT_DIM // 2
    pos = jnp.arange(seq, dtype=F32)
    inv_freq = ROPE_THETA ** (-jnp.arange(0, ROT_DIM, 2, dtype=F32) / ROT_DIM)
    ang = pos[:, None] * inv_freq[None, :]
    cos, sin = jnp.cos(ang), jnp.sin(ang)
    pad = jnp.zeros((seq, V7X_LANES - ROT_DIM), F32)
    zeros = jnp.zeros((seq, half), F32)
    c = jnp.concatenate([cos, cos, pad + 1.0], axis=1)
    s_up = jnp.concatenate([zeros, sin, pad], axis=1)
    s_dn = jnp.concatenate([-sin, zeros, pad], axis=1)
    lane = jnp.arange(V7X_LANES)[None, :]
    blk = (jnp.arange(seq) // SLC_BLK)[:, None]
    k_bias = jnp.where(lane == C_HEAD_DIM + blk, NEG_INF, 0.0).astype(F32)
    ones = (lane == C_HEAD_DIM).astype(F32)
    return c, s_up, s_dn, k_bias, ones


def _pad_heads(w, n_heads):
    width = w.shape[1] // n_heads
    w = w.reshape(w.shape[0], n_heads, width)
    return jnp.pad(w, ((0, 0), (0, 0), (0, V7X_LANES - width))).reshape(w.shape[0], n_heads * V7X_LANES)


def _in_proj1(x, gain, w_in, seq, tm=256):
    n, d = x.shape
    o = np.cumsum([0, C_WIDTH] + [KV_WIDTH] * 6 + [C_HEADS * N_BRANCH] + [D_WIDTH] * 3)
    wq = w_in[:, o[0]:o[1]].astype(BF16)
    wkc = w_in[:, o[1]:o[3]].astype(BF16)
    wkv = w_in[:, o[3]:o[7]].astype(BF16)
    wgl = _pad_heads(w_in[:, o[7]:o[8]], C_KV_HEADS).astype(BF16)
    wbcd = w_in[:, o[8]:o[11]].astype(BF16)
    ws = [wq, wkc, wkv, wgl, wbcd]
    kvw = C_KV_HEADS * V7X_LANES
    widths = [C_HEADS * V7X_LANES] * 2 + [2 * KV_WIDTH] + [kvw] * 4 + [kvw, 3 * D_WIDTH]
    dts = [BF16, BF16, F32, BF16, BF16, BF16, BF16, F32, F32]
    tps = seq // tm
    tab_spec = pl.BlockSpec((tm, V7X_LANES), lambda i: (i % tps, 0))
    c, s_up, s_dn, k_bias, ones = _head_tables(seq)
    return pl.pallas_call(
        _in_proj1_body,
        grid=(n // tm,),
        in_specs=[_rows_spec(tm, d), _const_spec((1, d)), tab_spec, tab_spec, tab_spec, tab_spec,
                  _const_spec((1, V7X_LANES))] + [_const_spec(w.shape) for w in ws],
        out_specs=[_rows_spec(tm, wd) for wd in widths],
        out_shape=[jax.ShapeDtypeStruct((n, wd), dt) for wd, dt in zip(widths, dts)],
        compiler_params=_params("parallel"),
        name="in_proj1",
    )(x, gain.reshape(1, d), c, s_up, s_dn, k_bias, ones, *ws)


def _compress_body(z_ref, pk_ref, pv_ref, w1k_ref, w1v_ref, w2k_ref, w2v_ref, ko_ref, vo_ref, z_scr):
    half = CMP_STRIDE * C_HEAD_DIM
    n_rows = z_ref.shape[1] // CMP_STRIDE
    streams = ((pk_ref, w1k_ref, w2k_ref, ko_ref), (pv_ref, w1v_ref, w2v_ref, vo_ref))
    for s, (p_ref, w1_ref, w2_ref, o_ref) in enumerate(streams):
        z_scr[...] = z_ref[0, :, s * KV_WIDTH:(s + 1) * KV_WIDTH]
        every16 = [z_scr[pl.ds(l, n_rows, stride=CMP_STRIDE), :] for l in range(CMP_STRIDE)]
        for hk in range(C_KV_HEADS):
            cols = slice(hk * C_HEAD_DIM, (hk + 1) * C_HEAD_DIM)
            r = jnp.concatenate([z[:, cols] for z in every16], axis=1)
            lo = _dot(r + p_ref[:, :half], w1_ref[:half, :])
            hi = _dot(r + p_ref[:, half:], w1_ref[half:, :])
            hidden = _gelu(lo + pltpu.roll(hi, n_rows - 1, axis=0))
            o_ref[0, hk] = _dot(hidden, w2_ref[...]).astype(o_ref.dtype)


def _compress(kcvc, pos_k, pos_v, w1k, w1v, w2k, w2v, bsz, seq):
    n_rows = seq // CMP_STRIDE
    width = kcvc.shape[1]
    consts = [pos_k.reshape(1, -1), pos_v.reshape(1, -1), w1k.astype(BF16), w1v.astype(BF16),
              _pad_heads(w2k, 1).astype(BF16), _pad_heads(w2v, 1).astype(BF16)]
    o_spec = pl.BlockSpec((1, C_KV_HEADS, n_rows, V7X_LANES), lambda b: (b, 0, 0, 0))
    o_shape = jax.ShapeDtypeStruct((bsz, C_KV_HEADS, n_rows, V7X_LANES), BF16)
    return pl.pallas_call(
        _compress_body,
        grid=(bsz,),
        in_specs=[pl.BlockSpec((1, seq, width), lambda b: (b, 0, 0))] + [_const_spec(c.shape) for c in consts],
        out_specs=[o_spec, o_spec],
        out_shape=[o_shape, o_shape],
        scratch_shapes=[pltpu.VMEM((seq, KV_WIDTH), F32)],
        compiler_params=_params("parallel"),
        name="nsa_compress",
    )(kcvc.reshape(bsz, seq, width), *consts)


SEL_CHUNK = 512


def _stack_heads(blk):
    return jnp.concatenate(_lane_blocks(blk), axis=0)


def _rows4(x):
    return jnp.concatenate([x] * C_GROUP, axis=0)


def _nsa_body(qn_ref, qr_ref, gl_ref, kc_ref, vc_ref, ks_ref, vs_ref, kw_ref, vw_ref, ovl_ref, place_ref,
              o_ref):
    n_cmp = kc_ref.shape[2]
    n_slc = ovl_ref.shape[0]
    hd = C_HEAD_DIM
    qb = pl.program_id(2)
    s0 = qb * NSA_TQ
    t_pos = s0 + lax.broadcasted_iota(jnp.int32, (NSA_TQ, 1), 0)
    qn4 = _stack_heads(qn_ref[0])
    qr4 = _stack_heads(qr_ref[0])

    cmp_end = lax.broadcasted_iota(jnp.int32, (1, n_cmp), 1) * CMP_STRIDE + (CMP_LEN - 1)
    pc = _masked_softmax(_dot_nt(qn4, kc_ref[0, 0]), _rows4(cmp_end <= t_pos))
    o_c = _dot(pc, vc_ref[0, 0])
    pc_sum = (pc[:NSA_TQ] + pc[NSA_TQ:2 * NSA_TQ]) + (pc[2 * NSA_TQ:3 * NSA_TQ] + pc[3 * NSA_TQ:])

    p_hi, p_lo = _split_bf16(pc_sum)
    ovl = ovl_ref[...]
    imp = _dot_nt(ovl, p_hi) + _dot_nt(ovl, p_lo)
    jb = lax.broadcasted_iota(jnp.int32, (n_slc, 1), 0)
    t_row = s0 + lax.broadcasted_iota(jnp.int32, (1, NSA_TQ), 1)
    cur = t_row // SLC_BLK
    forced = (jb == 0) | (jb == cur) | (jb == cur - 1)
    imp = jnp.where(jb * SLC_BLK <= t_row, imp + jnp.where(forced, FORCE_BONUS, 0.0), NEG_INF)
    rank = jnp.zeros((n_slc, NSA_TQ), F32)
    for k in range(n_slc):
        ck = imp[k:k + 1, :]
        beats = (ck > imp) | ((ck == imp) & (jb > k))
        rank = rank + jnp.where(beats, 1.0, 0.0)
    not_sel = jnp.where(rank < float(min(SEL_TOPK, n_slc)), 0.0, 1.0).astype(BF16)
    not_sel_q = lax.dot_general(not_sel, place_ref[...], (((0,), (0,)), ((), ())),
                                preferred_element_type=F32)
    q_sel = qr4 + _rows4(not_sel_q.astype(BF16))

    def sel_chunk(c, m, acc, causal):
        k0 = pl.multiple_of(c * SEL_CHUNK, SEL_CHUNK)
        s = _dot_nt(q_sel, ks_ref[0, pl.ds(k0, SEL_CHUNK), :])
        if causal:
            key = k0 + lax.broadcasted_iota(jnp.int32, (1, SEL_CHUNK), 1)
            s = s + _rows4(jnp.where(key <= t_pos, 0.0, NEG_INF))
        m_new = jnp.maximum(m, jnp.max(s, axis=-1, keepdims=True))
        p = jnp.exp(s - m_new)
        acc = jnp.exp(m - m_new) * acc + _dot(p, vs_ref[0, pl.ds(k0, SEL_CHUNK), :])
        return m_new, acc

    rows = C_GROUP * NSA_TQ
    last = qb // (SEL_CHUNK // NSA_TQ)
    m, acc = lax.fori_loop(0, last, lambda c, ma: sel_chunk(c, ma[0], ma[1], False),
                           (jnp.full((rows, 1), NEG_INF, F32), jnp.zeros((rows, V7X_LANES), F32)))
    _, acc = sel_chunk(last, m, acc, True)
    o_s = acc[:, :hd] / jnp.maximum(acc[:, hd:hd + 1], 1e-30)

    band = NSA_TQ + WIN
    w0 = pl.multiple_of(jnp.maximum(s0 - WIN, 0), NSA_TQ)
    diff = t_pos - (w0 + lax.broadcasted_iota(jnp.int32, (1, band), 1))
    s = _dot_nt(qr4, kw_ref[0, pl.ds(w0, band), :]) + _rows4(
        jnp.where((diff >= 0) & (diff < WIN), 0.0, NEG_INF))
    p = jnp.exp(s - jnp.max(s, axis=-1, keepdims=True))
    acc = _dot(p, vw_ref[0, pl.ds(w0, band), :])
    o_w = acc[:, :hd] / jnp.maximum(acc[:, hd:hd + 1], 1e-30)

    gate = _sigmoid(gl_ref[0])
    outs = []
    for g in range(C_GROUP):
        r = slice(g * NSA_TQ, (g + 1) * NSA_TQ)
        gc = g * N_BRANCH
        outs.append(gate[:, gc:gc + 1] * o_c[r, :hd] + gate[:, gc + 1:gc + 2] * o_s[r]
                    + gate[:, gc + 2:gc + 3] * o_w[r])
    o_ref[0] = jnp.concatenate(outs, axis=1).astype(o_ref.dtype)


def _nsa(qn, qr, gl, k_cmp, v_cmp, ks, vs, kw, vw, bsz, seq):
    n_cmp = seq // CMP_STRIDE
    n_slc = seq // SLC_BLK
    assert C_HEAD_DIM + n_slc <= V7X_LANES and seq % SEL_CHUNK == 0
    ci = np.arange(n_cmp)[None, :] * CMP_STRIDE
    sj = np.arange(n_slc)[:, None] * SLC_BLK
    overlap_t = jnp.asarray(((ci < sj + SLC_BLK) & (ci + CMP_LEN > sj)).astype(np.float32)).astype(BF16)
    place = jnp.asarray(np.eye(n_slc, V7X_LANES, k=C_HEAD_DIM, dtype=np.float32)).astype(BF16)
    gw = C_GROUP * V7X_LANES
    q_spec = pl.BlockSpec((1, NSA_TQ, gw), lambda b, h, i: (b, i, h))
    gl_spec = pl.BlockSpec((1, NSA_TQ, V7X_LANES), lambda b, h, i: (b, i, h))
    cmp_spec = pl.BlockSpec((1, 1, n_cmp, V7X_LANES), lambda b, h, i: (b, h, 0, 0))
    kv_spec = pl.BlockSpec((1, seq, V7X_LANES), lambda b, h, i: (b, 0, h))
    as3 = lambda z: z.reshape(bsz, seq, z.shape[-1])
    return pl.pallas_call(
        _nsa_body,
        grid=(bsz, C_KV_HEADS, seq // NSA_TQ),
        in_specs=[q_spec, q_spec, gl_spec, cmp_spec, cmp_spec, kv_spec, kv_spec, kv_spec, kv_spec,
                  _const_spec((n_slc, n_cmp)), _const_spec((n_slc, V7X_LANES))],
        out_specs=pl.BlockSpec((1, NSA_TQ, C_GROUP * C_HEAD_DIM), lambda b, h, i: (b, i, h)),
        out_shape=jax.ShapeDtypeStruct((bsz, seq, C_WIDTH), BF16),
        compiler_params=_params("parallel", "parallel", "arbitrary"),
        name="nsa_attention",
    )(as3(qn), as3(qr), as3(gl), k_cmp, v_cmp, as3(ks), as3(vs), as3(kw), as3(vw), overlap_t, place)


ROW_CHUNKS = D_MODEL // V7X_LANES
assert ROW_CHUNKS == V7X_SUBLANES
DMA_UNROLL = 8
PAD_SPANS = N_EXPERTS + 1


def _tile_rows(n_rows):
    return (n_rows * ROW_CHUNKS, V7X_LANES)


def _row_tile(ref, r):
    start = r * ROW_CHUNKS
    if not isinstance(start, int):
        start = pl.multiple_of(start, ROW_CHUNKS)
    return ref.at[pl.ds(start, ROW_CHUNKS), :]


def _store_row_tiles(ref, x):
    for c in range(ROW_CHUNKS):
        ref[pl.ds(c, x.shape[0], stride=ROW_CHUNKS), :] = x[:, c * V7X_LANES:(c + 1) * V7X_LANES]


def _load_row_tiles(ref):
    rows = ref.shape[0] // ROW_CHUNKS
    return jnp.concatenate([ref[pl.ds(c, rows, stride=ROW_CHUNKS), :] for c in range(ROW_CHUNKS)], axis=1)


def _route(hn, wr_ref, br_ref, tri_ref, cnt_scr):
    logits = _dot_f32(hn, wr_ref[...]) + br_ref[...]
    lane = lax.broadcasted_iota(jnp.int32, logits.shape, 1)
    m1 = jnp.max(logits, axis=-1, keepdims=True)
    i1 = jnp.min(jnp.where(logits == m1, lane, V7X_LANES), axis=-1, keepdims=True)
    rest = jnp.where(lane == i1, NEG_INF, logits)
    m2 = jnp.max(rest, axis=-1, keepdims=True)
    i2 = jnp.min(jnp.where(rest == m2, lane, V7X_LANES), axis=-1, keepdims=True)
    e2 = jnp.exp(m2 - m1)
    den = 1.0 + e2
    hit1, hit2 = lane == i1, lane == i2
    hits = jnp.where(hit1 | hit2, 1.0, 0.0)
    before = jnp.dot(tri_ref[...], hits.astype(BF16), preferred_element_type=F32) + cnt_scr[...]
    r1 = jnp.sum(jnp.where(hit1, before, 0.0), axis=-1, keepdims=True).astype(jnp.int32)
    r2 = jnp.sum(jnp.where(hit2, before, 0.0), axis=-1, keepdims=True).astype(jnp.int32)
    cnt_scr[...] = cnt_scr[...] + jnp.sum(hits, axis=0, keepdims=True)
    idx = jnp.where(lane == 0, i1, jnp.where(lane == 1, i2, jnp.where(lane == 2, r1,
                    jnp.where(lane == 3, r2, 0))))
    gate = jnp.where(lane == 0, 1.0 / den, jnp.where(lane == 1, e2 / den, 0.0))
    return idx, gate


def _row_copy(src, dst, sem):
    return pltpu.make_async_copy(src, dst, sem)


def _dispatch_body(pad_ref, dest_ref, h_ref, g_ref, xs_ref, hn_scr, zero_scr, sems, fill_sem):
    i = pl.program_id(0)
    slot = i % 2
    hn_ref = hn_scr.at[slot]
    _store_row_tiles(hn_ref, _rms(h_ref[...], g_ref[...]))

    @pl.when(i == 0)
    def _():
        zero_scr[...] = jnp.zeros_like(zero_scr)
        for e in range(PAD_SPANS):
            start, count = pad_ref[e], pad_ref[PAD_SPANS + e]

            def fill(r, c):
                _row_copy(zero_scr, _row_tile(xs_ref, start + r), fill_sem).start()
                return c

            def filled(r, c):
                _row_copy(zero_scr, _row_tile(xs_ref, 0), fill_sem).wait()
                return c

            lax.fori_loop(0, count, fill, 0)
            lax.fori_loop(0, count, filled, 0)

    def issue(blk, c):
        for u in range(DMA_UNROLL):
            r = blk * DMA_UNROLL + u
            for k in range(TOP_K):
                _row_copy(_row_tile(hn_ref, r), _row_tile(xs_ref, dest_ref[TOP_K * r + k]),
                          sems.at[slot]).start(priority=k)
        return c

    lax.fori_loop(0, DMA_ROWS // DMA_UNROLL, issue, 0)

    def drain(s):
        for _ in range(TOP_K):
            _row_copy(hn_scr.at[s], xs_ref.at[pl.ds(0, DMA_ROWS * ROW_CHUNKS), :], sems.at[s]).wait()

    @pl.when(i >= 1)
    def _():
        drain(1 - slot)

    @pl.when(i == pl.num_programs(0) - 1)
    def _():
        drain(slot)


def _dispatch(pad_info, dest, h, gain, p_rows):
    n = h.shape[0]
    return pl.pallas_call(
        _dispatch_body,
        grid_spec=pltpu.PrefetchScalarGridSpec(
            num_scalar_prefetch=1,
            grid=(n // DMA_ROWS,),
            in_specs=[pl.BlockSpec((TOP_K * DMA_ROWS,), lambda i, pad: (i,), memory_space=pltpu.SMEM),
                      pl.BlockSpec((DMA_ROWS, D_MODEL), lambda i, pad: (i, 0)),
                      pl.BlockSpec((1, D_MODEL), lambda i, pad: (0, 0))],
            out_specs=pl.BlockSpec(memory_space=pl.ANY),
            scratch_shapes=[pltpu.VMEM((2,) + _tile_rows(DMA_ROWS), F32), pltpu.VMEM(_tile_rows(1), F32),
                            pltpu.SemaphoreType.DMA((2,)), pltpu.SemaphoreType.DMA(())]),
        out_shape=jax.ShapeDtypeStruct(_tile_rows(p_rows), F32),
        compiler_params=_params("arbitrary"),
        name="moe_dispatch",
    )(pad_info, dest, h, gain.reshape(1, -1))


def _experts_body(blk_e_ref, n_used_ref, prev_ref, cur_ref, x_ref, wg_ref, wu_ref, wd_ref, g_ref,
                  o_scr, sems):
    i = pl.program_id(0)
    slot = i % 2

    def scatter(rows_ref, s):
        for r in range(MOE_ROWS):
            _row_copy(_row_tile(o_scr.at[s], r), _row_tile(g_ref, rows_ref[r]), sems.at[s]).start(priority=r % 2)

    def drain(s):
        _row_copy(o_scr.at[s], g_ref.at[pl.ds(0, MOE_ROWS * ROW_CHUNKS), :], sems.at[s]).wait()

    @pl.when(i == 0)
    def _():
        o_scr[...] = jnp.zeros_like(o_scr)

    @pl.when(i >= 1)
    def _():
        drain(slot)

    x = _load_row_tiles(x_ref).astype(BF16)
    scatter(prev_ref, 1 - slot)
    gate = jnp.dot(x, wg_ref[0], preferred_element_type=F32)
    up = jnp.dot(x, wu_ref[0], preferred_element_type=F32)
    act = (gate * _sigmoid(gate)) * up
    _store_row_tiles(o_scr.at[slot], _dot(act, wd_ref[0]))

    @pl.when(i == pl.num_programs(0) - 1)
    def _():
        scatter(cur_ref, slot)
        drain(1 - slot)
        drain(slot)


def _experts(blk_e, n_used, slots, xs, wg, wu, wd):
    p_rows = xs.shape[0] // ROW_CHUNKS
    out_rows = slots.shape[0]
    idx_spec = lambda off: pl.BlockSpec((MOE_ROWS,), lambda i, be, nu: (i + off,), memory_space=pltpu.SMEM)
    x_spec = pl.BlockSpec(_tile_rows(MOE_ROWS), lambda i, be, nu: (jnp.minimum(i, nu[0] - 1), 0))
    w_spec = lambda w: pl.BlockSpec((1,) + w.shape[1:], lambda i, be, nu: (be[i], 0, 0))
    return pl.pallas_call(
        _experts_body,
        grid_spec=pltpu.PrefetchScalarGridSpec(
            num_scalar_prefetch=2,
            grid=(p_rows // MOE_ROWS,),
            in_specs=[idx_spec(0), idx_spec(1), x_spec, w_spec(wg), w_spec(wu), w_spec(wd)],
            out_specs=pl.BlockSpec(memory_space=pl.ANY),
            scratch_shapes=[pltpu.VMEM((2,) + _tile_rows(MOE_ROWS), F32), pltpu.SemaphoreType.DMA((2,))]),
        out_shape=jax.ShapeDtypeStruct(_tile_rows(out_rows), F32),
        compiler_params=_params("arbitrary"),
        name="moe_experts",
    )(blk_e, n_used, slots, slots, xs, wg, wu, wd)


def _combine_body(h_ref, gate_ref, g_ref, y0_ref, y1_ref, o_ref):
    gate = gate_ref[...]
    moe = gate[:, 0:1] * _load_row_tiles(y0_ref) + gate[:, 1:2] * _load_row_tiles(y1_ref)
    o_ref[...] = _rms(h_ref[...] + moe, g_ref[...])


def _combine(h, gates, gain, y_rows, tm=512):
    n = h.shape[0]
    steps = n // tm
    return pl.pallas_call(
        _combine_body,
        grid=(steps,),
        in_specs=[_rows_spec(tm, D_MODEL), _rows_spec(tm, V7X_LANES), _const_spec((1, D_MODEL)),
                  pl.BlockSpec(_tile_rows(tm), lambda i: (i, 0)),
                  pl.BlockSpec(_tile_rows(tm), lambda i: (i + steps, 0))],
        out_specs=_rows_spec(tm, D_MODEL),
        out_shape=jax.ShapeDtypeStruct((n, D_MODEL), F32),
        compiler_params=_params("parallel"),
        name="moe_combine_norm",
    )(h, gates, gain.reshape(1, -1), y_rows, y_rows)


def _moe_layout(idx, counts):
    n = idx.shape[0]
    nk = n * TOP_K
    counts = counts[0, :N_EXPERTS].astype(jnp.int32)
    padded = ((counts + MOE_ROWS - 1) // MOE_ROWS) * MOE_ROWS
    p_end = jnp.cumsum(padded)
    p_start = p_end - padded
    experts = jnp.arange(N_EXPERTS, dtype=jnp.int32)[None, None, :]
    first = jnp.sum(jnp.where(idx[:, :TOP_K, None] == experts, p_start[None, None, :], 0), axis=-1)
    dest = (first + idx[:, TOP_K:2 * TOP_K]).reshape(nk)
    n_blk = (nk + MOE_ROWS - 1) // MOE_ROWS + N_EXPERTS
    blk_first = jnp.arange(n_blk, dtype=jnp.int32)[:, None] * MOE_ROWS
    blk_e = jnp.minimum(jnp.sum((p_end[None, :] <= blk_first).astype(jnp.int32), axis=1), N_EXPERTS - 1)
    n_used = (p_end[-1:] // MOE_ROWS).astype(jnp.int32)
    p_rows = n_blk * MOE_ROWS
    pad_info = jnp.concatenate([p_start + counts, p_end[-1:], padded - counts,
                                p_rows - p_end[-1:]]).astype(jnp.int32)
    pair_by_row = jnp.argsort(dest).astype(jnp.int32)
    rows = jnp.arange(p_rows, dtype=jnp.int32)
    row_e = jnp.repeat(blk_e, MOE_ROWS)
    offset = rows - p_start[row_e]
    before = (jnp.cumsum(counts) - counts)[row_e]
    occupied = offset < counts[row_e]
    pair = pair_by_row[jnp.clip(before + offset, 0, nk - 1)]
    spare = nk + rows - (before + jnp.minimum(offset, counts[row_e]))
    slots = jnp.where(occupied, (pair % TOP_K) * n + pair // TOP_K, spare)
    slots = jnp.concatenate([p_rows + jnp.arange(MOE_ROWS, dtype=jnp.int32), slots]).astype(jnp.int32)
    return dest.astype(jnp.int32), blk_e, n_used, pad_info, p_rows, slots


def _moe_final(h, idx, gates, counts, norm_g, wg, wu, wd, final_g):
    dest, blk_e, n_used, pad_info, p_rows, slots = _moe_layout(idx, counts)
    xs = _dispatch(pad_info, dest, h, norm_g, p_rows)
    y = _experts(blk_e, n_used, slots, xs, wg.astype(BF16), wu.astype(BF16), wd.astype(BF16))
    return _combine(h, gates, final_g, y)


def kernel(x, e_norm_mix, e_w_in, sgu_ln_g, sgu_ln_b, sgu_w, sgu_b, rwkv_mu, rwkv_w0, rwkv_w2,
           rwkv_a0, rwkv_a2, rwkv_g2, rwkv_k_k, rwkv_k_a, rwkv_r_k, rwkv_gn_g, rwkv_gn_b, e_w_out,
           e_norm_ffn, ffn_w_gate, ffn_w_up, ffn_w_down, o_norm_mix, o_w_in, nsa_cmp_pos_k,
           nsa_cmp_pos_v, nsa_cmp_k_w1, nsa_cmp_k_w2, nsa_cmp_v_w1, nsa_cmp_v_w2, conv_w, o_w_out,
           o_norm_ffn, moe_router, moe_router_b, moe_w_gate, moe_w_up, moe_w_down, final_norm):
    bsz, seq, d = x.shape
    n = bsz * seq
    h = x.reshape(n, d)

    ya, scan_ops, g = _layer0_front(h, seq, e_norm_mix[0], e_w_in[0].astype(BF16), sgu_ln_g[0], sgu_ln_b[0],
                                    sgu_w[0], sgu_b[0], rwkv_mu[0], rwkv_w0[0], rwkv_w2[0], rwkv_a0[0],
                                    rwkv_a2[0], rwkv_g2[0])
    rwka, v = _to_scan_layout(scan_ops, bsz, seq)
    ys = _rwkv_scan(rwka, v, rwkv_k_k[0], rwkv_k_a[0], rwkv_r_k[0], rwkv_gn_g[0], rwkv_gn_b[0], bsz, seq)
    h = _layer0_back(h, ya, _from_scan_layout(ys, bsz, seq), g, e_w_out[0].astype(BF16), e_norm_ffn[0],
                     ffn_w_gate[0].astype(BF16), ffn_w_up[0].astype(BF16), ffn_w_down[0].astype(BF16))

    qn, qr, kcvc, ks, vs, kw, vw, gl, bcd = _in_proj1(h, o_norm_mix[0], o_w_in[0], seq)
    k_cmp, v_cmp = _compress(kcvc, nsa_cmp_pos_k[0], nsa_cmp_pos_v[0], nsa_cmp_k_w1[0], nsa_cmp_v_w1[0],
                             nsa_cmp_k_w2[0], nsa_cmp_v_w2[0], bsz, seq)
    yc = _nsa(qn, qr, gl, k_cmp, v_cmp, ks, vs, kw, vw, bsz, seq).reshape(n, C_WIDTH)
    h, idx, gates, counts = _layer1_mid(h, yc, bcd, conv_w[0], o_w_out[0].astype(BF16), seq, o_norm_ffn[0],
                                        moe_router[0], moe_router_b[0])
    out = _moe_final(h, idx, gates, counts, o_norm_ffn[0], moe_w_gate[0], moe_w_up[0], moe_w_down[0],
                     final_norm)
    return out.reshape(bsz, seq, d)
```

```python
import functools

import jax
import jax.numpy as jnp
import numpy as np
from jax import lax
from jax.experimental import pallas as pl
from jax.experimental.pallas import tpu as pltpu

F32 = jnp.float32
BF16 = jnp.bfloat16

D_MODEL = 1024
A_GROUPS = 4
A_GROUP_DIM = 128
A_WIDTH = A_GROUPS * A_GROUP_DIM
CHUNK = 128
SGU_LN_EPS = 1e-5
B_HEADS = 8
B_HEAD_DIM = 64
B_WIDTH = B_HEADS * B_HEAD_DIM
DECAY_LORA = 64
ICLR_LORA = 64
GATE_LORA = 128
B_IN = 3 * B_WIDTH + DECAY_LORA + ICLR_LORA + GATE_LORA
RWKV_GN_EPS = 64e-5
C_HEADS = 8
C_KV_HEADS = 2
C_GROUP = C_HEADS // C_KV_HEADS
C_HEAD_DIM = 64
C_WIDTH = C_HEADS * C_HEAD_DIM
KV_WIDTH = C_KV_HEADS * C_HEAD_DIM
N_BRANCH = 3
CMP_LEN = 32
CMP_STRIDE = 16
CMP_HIDDEN = 256
SLC_BLK = 64
SEL_TOPK = 8
WIN = 512
NSA_TQ = 512
ROT_DIM = C_HEAD_DIM // 4
ROPE_THETA = 500000.0
D_WIDTH = 512
CONV_W = 3
FFN_DIM = 2816
N_EXPERTS = 8
TOP_K = 2
EXPERT_DIM = 1408
NORM_EPS = 1e-6
NEG_INF = -1e30
FORCE_BONUS = 1e6

V7X_LANES = 128
V7X_SUBLANES = 8
V7X_VMEM_LIMIT = 56 * 1024 * 1024

MOE_ROWS = 512
DMA_ROWS = 512


def _params(*sem):
    return pltpu.CompilerParams(dimension_semantics=sem, vmem_limit_bytes=V7X_VMEM_LIMIT)


def _const_spec(shape):
    zeros = (0,) * len(shape)
    return pl.BlockSpec(shape, lambda *_: zeros)


def _rows_spec(tm, width):
    return pl.BlockSpec((tm, width), lambda i: (i, 0))


def _rms(x, g):
    return x * lax.rsqrt(jnp.mean(x * x, axis=-1, keepdims=True) + NORM_EPS) * g


def _gelu(x):
    return x * (0.5 * (1.0 + jnp.tanh(0.7978845608028654 * (x + 0.044715 * (x * x * x)))))


def _sigmoid(x):
    return 1.0 / (1.0 + jnp.exp(-x))


def _dot(a, b):
    return jnp.dot(a.astype(BF16), b.astype(BF16), preferred_element_type=F32)


def _dot_nt(a, b):
    return lax.dot_general(a.astype(BF16), b.astype(BF16), (((1,), (1,)), ((), ())),
                           preferred_element_type=F32)


def _split_bf16(a):
    hi = a.astype(BF16)
    lo = (a - hi.astype(F32)).astype(BF16)
    return hi, lo


def _dot_f32(a, b):
    ah, al = _split_bf16(a)
    bh, bl = _split_bf16(b)
    d = functools.partial(jnp.dot, preferred_element_type=F32)
    return d(ah, bh) + (d(al, bh) + d(ah, bl))


def _masked_softmax(s, mask):
    s = jnp.where(mask, s, NEG_INF)
    m = jnp.max(s, axis=-1, keepdims=True)
    p = jnp.where(mask, jnp.exp(s - m), 0.0)
    return p / jnp.maximum(jnp.sum(p, axis=-1, keepdims=True), 1e-30)


def _sgu_chunk(p_uv, lng_ref, lnb_ref, w_ref, b_ref):
    row = lax.broadcasted_iota(jnp.int32, (CHUNK, CHUNK), 0)
    col = lax.broadcasted_iota(jnp.int32, (CHUNK, CHUNK), 1)
    causal = col <= row
    u = _gelu(p_uv[:, :A_WIDTH])
    v = _gelu(p_uv[:, A_WIDTH:])
    outs = []
    for g in range(A_GROUPS):
        cols = slice(g * A_GROUP_DIM, (g + 1) * A_GROUP_DIM)
        vg = v[:, cols]
        mu = jnp.mean(vg, axis=-1, keepdims=True)
        dv = vg - mu
        var = jnp.mean(dv * dv, axis=-1, keepdims=True)
        vn = dv * lax.rsqrt(var + SGU_LN_EPS) * lng_ref[:, cols] + lnb_ref[:, cols]
        wm = jnp.where(causal, w_ref[g], 0.0)
        mixed = _dot(wm, vn) + b_ref[:, g:g + 1]
        outs.append(u[:, cols] * mixed)
    return jnp.concatenate(outs, axis=1)


def _softplus(x):
    return jnp.maximum(x, 0.0) + jnp.log(1.0 + jnp.exp(-jnp.abs(x)))


SCAN_OPERANDS = 5


def _layer0_front_body(x_ref, gain_ref, win_ref, lng_ref, lnb_ref, sw_ref, sb_ref, mu_ref, w0_ref, w2_ref,
                       a0_ref, a2_ref, g2_ref, ya_o, s_o, g_o, prev_scr, *, tiles_per_seq):
    tm = x_ref.shape[0]

    @pl.when(pl.program_id(0) == 0)
    def _():
        prev_scr[...] = jnp.zeros_like(prev_scr)

    p = jnp.dot(_rms(x_ref[...], gain_ref[...]).astype(BF16), win_ref[...], preferred_element_type=F32)
    for c in range(tm // CHUNK):
        rows = slice(c * CHUNK, (c + 1) * CHUNK)
        ya_o[rows, :] = _sgu_chunk(p[rows, :2 * A_WIDTH], lng_ref, lnb_ref, sw_ref, sb_ref).astype(ya_o.dtype)
    x = p[:, 2 * A_WIDTH:]
    first = (pl.program_id(0) % tiles_per_seq) == 0
    prev_row = jnp.where(first, 0.0, prev_scr[...])
    rowid = lax.broadcasted_iota(jnp.int32, (tm, 1), 0)
    shifted = jnp.where(rowid == 0, prev_row, pltpu.roll(x, 1, axis=0))
    prev_scr[...] = x[tm - 1:tm, :]
    xm = x + (shifted - x) * mu_ref[...]
    o = 3 * B_WIDTH
    wl = xm[:, o:o + DECAY_LORA]
    al = xm[:, o + DECAY_LORA:o + DECAY_LORA + ICLR_LORA]
    gl = xm[:, o + DECAY_LORA + ICLR_LORA:]
    w = -_softplus(-(w0_ref[...] + _dot(jnp.tanh(wl), w2_ref[...]))) - 0.5
    s_o[0] = xm[:, :B_WIDTH]
    s_o[1] = jnp.exp(-jnp.exp(w))
    s_o[2] = xm[:, B_WIDTH:2 * B_WIDTH]
    s_o[3] = _sigmoid(a0_ref[...] + _dot(al, a2_ref[...]))
    s_o[4] = xm[:, 2 * B_WIDTH:3 * B_WIDTH]
    g_o[...] = _dot(_sigmoid(gl), g2_ref[...])


def _layer0_front(h, seq, gain, w_in, ln_g, ln_b, w_s, b_s, mu, w0, w2, a0, a2, g2, tm=512):
    n, d = h.shape
    perm = np.arange(B_WIDTH).reshape(B_HEADS, B_HEAD_DIM).T.reshape(-1)
    r0, k0 = 2 * A_WIDTH, 2 * A_WIDTH + B_WIDTH
    cols = np.concatenate([np.arange(r0), r0 + perm, k0 + perm, np.arange(k0 + B_WIDTH, w_in.shape[1])])
    w_in = w_in[:, cols]
    mu = mu[cols[r0:] - r0]
    w0, w2, a0, a2 = w0[perm], w2[:, perm], a0[perm], a2[:, perm]
    g2 = g2[:, perm]
    row = lambda v: v.reshape(1, -1)
    consts = [row(gain), w_in, row(ln_g), row(ln_b), w_s, b_s.T, row(mu), row(w0), w2, row(a0), a2, g2]
    outs = [jax.ShapeDtypeStruct((n, A_WIDTH), BF16), jax.ShapeDtypeStruct((SCAN_OPERANDS, n, B_WIDTH), F32),
            jax.ShapeDtypeStruct((n, B_WIDTH), F32)]
    return pl.pallas_call(
        functools.partial(_layer0_front_body, tiles_per_seq=seq // tm),
        grid=(n // tm,),
        in_specs=[_rows_spec(tm, d)] + [_const_spec(c.shape) for c in consts],
        out_specs=[_rows_spec(tm, A_WIDTH), pl.BlockSpec((SCAN_OPERANDS, tm, B_WIDTH), lambda i: (0, i, 0)),
                   _rows_spec(tm, B_WIDTH)],
        out_shape=outs,
        scratch_shapes=[pltpu.VMEM((1, B_IN), F32)],
        compiler_params=_params("arbitrary"),
        name="layer0_front",
    )(h, *consts)


def _rwkv_scan_body(r_ref, w_ref, k0_ref, a_ref, v_ref, kkp_ref, kap_ref, rk_ref, gng_ref, gnb_ref,
                    y_ref, s_ref, kkn_ref, ka_ref, km_ref):
    n, tt = r_ref.shape[0], r_ref.shape[1]

    @pl.when(pl.program_id(0) == 0)
    def _():
        s_ref[...] = jnp.zeros_like(s_ref)

    per_dim = lambda p_ref: p_ref[...][:, None, :]
    k0 = k0_ref[...]
    a = a_ref[...]
    kk = k0 * per_dim(kkp_ref)
    kkn = kk / jnp.maximum(jnp.sqrt(jnp.sum(kk * kk, axis=0, keepdims=True)), 1e-12)
    km = k0 * (1.0 + (a - 1.0) * per_dim(kap_ref))
    kkn_ref[...] = kkn
    ka_ref[...] = kkn * a
    km_ref[...] = km

    zero = jnp.zeros((n, r_ref.shape[2]), F32)

    def sa_init(j, acc):
        return acc + s_ref[j] * kkn_ref[j, pl.ds(0, 1), :]

    sa0 = lax.fori_loop(0, n, sa_init, zero, unroll=8)

    def step(t, sa):
        v_t = v_ref[t]
        now = pl.ds(t, 1)
        nxt = pl.ds(jnp.minimum(t + 1, tt - 1), 1)

        def jbody(j, carry):
            y, san = carry
            sn = s_ref[j] * w_ref[j, now, :] + (v_t * km_ref[j, now, :] - sa * ka_ref[j, now, :])
            s_ref[j] = sn
            return y + sn * r_ref[j, now, :], san + sn * kkn_ref[j, nxt, :]

        y, san = lax.fori_loop(0, n, jbody, (zero, zero), unroll=8)
        y_ref[t] = y
        return san

    lax.fori_loop(0, tt, step, sa0)

    y = y_ref[...]
    ym = jnp.mean(y, axis=1, keepdims=True)
    dy = y - ym
    yv = jnp.mean(dy * dy, axis=1, keepdims=True)
    yn = dy * lax.rsqrt(yv + RWKV_GN_EPS) * gng_ref[...][None] + gnb_ref[...][None]
    bonus = jnp.sum(r_ref[...] * km * per_dim(rk_ref), axis=0)
    y_ref[...] = yn + bonus[:, None, :] * v_ref[...]


def _rwkv_scan(rwka, v, k_k, k_a, r_k, gn_g, gn_b, bsz, seq, tt=64):
    n = B_HEAD_DIM
    lanes = bsz * B_HEADS

    def lane_param(p):
        return jnp.tile(p.reshape(B_HEADS, n).T, (1, bsz))

    op_spec = lambda a: pl.BlockSpec((None, n, tt, lanes), lambda c: (a, 0, c, 0))
    slab_spec = pl.BlockSpec((tt, n, lanes), lambda c: (c, 0, 0))
    par_spec = _const_spec((n, lanes))
    return pl.pallas_call(
        _rwkv_scan_body,
        grid=(seq // tt,),
        in_specs=[op_spec(a) for a in range(SCAN_OPERANDS - 1)] + [slab_spec] + [par_spec] * 5,
        out_specs=slab_spec,
        out_shape=jax.ShapeDtypeStruct((seq, n, lanes), F32),
        scratch_shapes=[pltpu.VMEM((n, n, lanes), F32)] + [pltpu.VMEM((n, tt, lanes), F32)] * 3,
        compiler_params=_params("arbitrary"),
        name="rwkv_scan",
    )(*([rwka] * (SCAN_OPERANDS - 1)), v, lane_param(k_k), lane_param(k_a), lane_param(r_k.reshape(-1)),
      lane_param(gn_g), lane_param(gn_b))


SCAN_TT = 128
SCAN_UNROLL = 8


def _time_to_lanes(x_ref, a_ref):
    for b in range(x_ref.shape[0]):
        for blk in range(B_WIDTH // V7X_LANES):
            cols = slice(blk * V7X_LANES, (blk + 1) * V7X_LANES)
            a_ref[b, cols, :] = x_ref[b, :, cols].T


def _to_scan_rows_body(x_ref, o_ref, a_ref):
    bsz = x_ref.shape[0]
    _time_to_lanes(x_ref, a_ref)

    def dims(i, carry):
        for u in range(SCAN_UNROLL):
            d = i * SCAN_UNROLL + u
            rows = pl.ds(pl.multiple_of(d * B_HEADS, B_HEADS), B_HEADS)
            z = jnp.concatenate([a_ref[b, rows, :] for b in range(bsz)], axis=0)
            o_ref[d] = z.T
        return carry

    lax.fori_loop(0, B_HEAD_DIM // SCAN_UNROLL, dims, 0)


def _to_scan_slab_body(x_ref, o_ref, a_ref):
    bsz = x_ref.shape[0]
    _time_to_lanes(x_ref, a_ref)

    def dims(i, carry):
        for u in range(SCAN_UNROLL):
            d = i * SCAN_UNROLL + u
            z = jnp.concatenate([a_ref[b, pl.ds(d, B_HEADS, stride=B_HEAD_DIM), :] for b in range(bsz)],
                                axis=0)
            o_ref[:, d, :] = z.T
        return carry

    lax.fori_loop(0, B_HEAD_DIM // SCAN_UNROLL, dims, 0)


def _to_scan_layout(ops, bsz, seq):
    tt = min(SCAN_TT, seq)
    lanes = bsz * B_HEADS
    ops4 = ops.reshape(SCAN_OPERANDS, bsz, seq, B_WIDTH)
    scratch = [pltpu.VMEM((bsz, B_WIDTH, tt), F32)]
    rwka = pl.pallas_call(
        _to_scan_rows_body,
        grid=(SCAN_OPERANDS - 1, seq // tt),
        in_specs=[pl.BlockSpec((None, bsz, tt, B_WIDTH), lambda a, i: (a, 0, i, 0))],
        out_specs=pl.BlockSpec((None, B_HEAD_DIM, tt, lanes), lambda a, i: (a, 0, i, 0)),
        out_shape=jax.ShapeDtypeStruct((SCAN_OPERANDS - 1, B_HEAD_DIM, seq, lanes), F32),
        scratch_shapes=scratch,
        compiler_params=_params("parallel", "arbitrary"),
        name="to_scan_rows",
    )(ops4)
    v = pl.pallas_call(
        _to_scan_slab_body,
        grid=(seq // tt,),
        in_specs=[pl.BlockSpec((None, bsz, tt, B_WIDTH), lambda i: (SCAN_OPERANDS - 1, 0, i, 0))],
        out_specs=pl.BlockSpec((tt, B_HEAD_DIM, lanes), lambda i: (i, 0, 0)),
        out_shape=jax.ShapeDtypeStruct((seq, B_HEAD_DIM, lanes), F32),
        scratch_shapes=scratch,
        compiler_params=_params("arbitrary"),
        name="to_scan_slab",
    )(ops4)
    return rwka, v


def _from_scan_body(y_ref, o_ref, a_ref):
    bsz = o_ref.shape[0]

    def dims(i, carry):
        for u in range(SCAN_UNROLL):
            d = i * SCAN_UNROLL + u
            zt = y_ref[:, d, :].T
            rows = pl.ds(pl.multiple_of(d * B_HEADS, B_HEADS), B_HEADS)
            for b in range(bsz):
                a_ref[b, rows, :] = zt[b * B_HEADS:(b + 1) * B_HEADS, :]
        return carry

    lax.fori_loop(0, B_HEAD_DIM // SCAN_UNROLL, dims, 0)
    for b in range(bsz):
        for blk in range(B_WIDTH // V7X_LANES):
            cols = slice(blk * V7X_LANES, (blk + 1) * V7X_LANES)
            o_ref[b, :, cols] = a_ref[b, cols, :].T


def _from_scan_layout(y, bsz, seq):
    tt = min(SCAN_TT, seq)
    return pl.pallas_call(
        _from_scan_body,
        grid=(seq // tt,),
        in_specs=[pl.BlockSpec((tt, B_HEAD_DIM, bsz * B_HEADS), lambda i: (i, 0, 0))],
        out_specs=pl.BlockSpec((bsz, tt, B_WIDTH), lambda i: (0, i, 0)),
        out_shape=jax.ShapeDtypeStruct((bsz, seq, B_WIDTH), F32),
        scratch_shapes=[pltpu.VMEM((bsz, B_WIDTH, tt), F32)],
        compiler_params=_params("arbitrary"),
        name="from_scan_layout",
    )(y).reshape(bsz * seq, B_WIDTH)


def _layer1_mid_body(h_ref, yc_ref, bcd_ref, prev_ref, cw_ref, w_ref, g_ref, wr_ref, br_ref, tri_ref,
                     o_ref, idx_o, gate_o, cnt_o, cnt_scr, *, tiles_per_seq):
    tm = h_ref.shape[0]

    @pl.when(pl.program_id(0) == 0)
    def _():
        cnt_scr[...] = jnp.zeros_like(cnt_scr)

    first = (pl.program_id(0) % tiles_per_seq) == 0
    z = bcd_ref[:, D_WIDTH:2 * D_WIDTH] * bcd_ref[:, 2 * D_WIDTH:]
    zp = jnp.where(first, 0.0, prev_ref[:, D_WIDTH:2 * D_WIDTH] * prev_ref[:, 2 * D_WIDTH:])
    rowid = lax.broadcasted_iota(jnp.int32, (tm, 1), 0)
    z1 = jnp.where(rowid == 0, zp[7:8, :], pltpu.roll(z, 1, axis=0))
    z2 = pltpu.roll(z, 2, axis=0)
    z2 = jnp.where(rowid == 0, zp[6:7, :], jnp.where(rowid == 1, zp[7:8, :], z2))
    y = cw_ref[0:1, :] * z2 + cw_ref[1:2, :] * z1 + cw_ref[2:3, :] * z
    yd = bcd_ref[:, :D_WIDTH] * y
    h_new = h_ref[...] + (_dot(yc_ref[...], w_ref[:C_WIDTH, :]) + _dot(yd, w_ref[C_WIDTH:, :]))
    o_ref[...] = h_new
    idx_o[...], gate_o[...] = _route(_rms(h_new, g_ref[...]), wr_ref, br_ref, tri_ref, cnt_scr)
    cnt_o[...] = cnt_scr[...]


def _layer1_mid(h, yc, bcd, conv_w, w_out, seq, gain, w_router, b_router, tm=512):
    n = h.shape[0]
    per8 = tm // V7X_SUBLANES
    wr = jnp.pad(w_router, ((0, 0), (0, V7X_LANES - N_EXPERTS)))
    br = jnp.pad(b_router.reshape(1, -1), ((0, 0), (0, V7X_LANES - N_EXPERTS)), constant_values=NEG_INF)
    tri = jnp.asarray(np.tril(np.ones((tm, tm), np.float32), k=-1)).astype(BF16)
    return pl.pallas_call(
        functools.partial(_layer1_mid_body, tiles_per_seq=seq // tm),
        grid=(n // tm,),
        in_specs=[_rows_spec(tm, D_MODEL), _rows_spec(tm, C_WIDTH), _rows_spec(tm, 3 * D_WIDTH),
                  pl.BlockSpec((V7X_SUBLANES, 3 * D_WIDTH), lambda i: (jnp.maximum(i * per8 - 1, 0), 0)),
                  _const_spec((CONV_W, D_WIDTH)), _const_spec(w_out.shape), _const_spec((1, D_MODEL)),
                  _const_spec(wr.shape), _const_spec(br.shape), _const_spec(tri.shape)],
        out_specs=[_rows_spec(tm, D_MODEL), _rows_spec(tm, V7X_LANES), _rows_spec(tm, V7X_LANES),
                   _const_spec((1, V7X_LANES))],
        out_shape=[jax.ShapeDtypeStruct((n, D_MODEL), F32), jax.ShapeDtypeStruct((n, V7X_LANES), jnp.int32),
                   jax.ShapeDtypeStruct((n, V7X_LANES), F32), jax.ShapeDtypeStruct((1, V7X_LANES), F32)],
        scratch_shapes=[pltpu.VMEM((1, V7X_LANES), F32)],
        compiler_params=_params("arbitrary"),
        name="layer1_mid",
    )(h, yc, bcd, bcd, conv_w, w_out, gain.reshape(1, -1), wr, br, tri)


def _layer0_back_body(h_ref, ya_ref, ys_ref, gm_ref, wo_ref, g_ref, wg_ref, wu_ref, wd_ref, o_ref):
    yb = ys_ref[...] * gm_ref[...]
    h = h_ref[...] + (_dot(ya_ref[...], wo_ref[:A_WIDTH, :]) + _dot(yb, wo_ref[A_WIDTH:, :]))
    hn = _rms(h, g_ref[...]).astype(BF16)
    gate = jnp.dot(hn, wg_ref[...], preferred_element_type=F32)
    up = jnp.dot(hn, wu_ref[...], preferred_element_type=F32)
    act = (gate * _sigmoid(gate)) * up
    o_ref[...] = h + _dot(act, wd_ref[...])


def _layer0_back(h, ya, ys, gm, w_out, gain, wg, wu, wd, tm=512):
    n = h.shape[0]
    perm = np.arange(B_WIDTH).reshape(B_HEADS, B_HEAD_DIM).T.reshape(-1)
    w_out = jnp.concatenate([w_out[:A_WIDTH], w_out[A_WIDTH:][perm]], axis=0)
    return pl.pallas_call(
        _layer0_back_body,
        grid=(n // tm,),
        in_specs=[_rows_spec(tm, D_MODEL), _rows_spec(tm, A_WIDTH), _rows_spec(tm, B_WIDTH),
                  _rows_spec(tm, B_WIDTH), _const_spec(w_out.shape), _const_spec((1, D_MODEL)),
                  _const_spec(wg.shape), _const_spec(wu.shape), _const_spec(wd.shape)],
        out_specs=_rows_spec(tm, D_MODEL),
        out_shape=jax.ShapeDtypeStruct((n, D_MODEL), F32),
        compiler_params=_params("parallel"),
        name="layer0_back",
    )(h, ya, ys, gm, w_out, gain.reshape(1, -1), wg, wu, wd)


def _rope(x, c, s_up, s_dn):
    half = ROT_DIM // 2
    return x * c + pltpu.roll(x, half, axis=1) * s_up + pltpu.roll(x, V7X_LANES - half, axis=1) * s_dn


def _lane_blocks(x):
    return [x[:, i * V7X_LANES:(i + 1) * V7X_LANES] for i in range(x.shape[1] // V7X_LANES)]


def _head_blocks(x):
    zeros = jnp.zeros((x.shape[0], V7X_LANES - C_HEAD_DIM), x.dtype)
    return [jnp.concatenate([x[:, h * C_HEAD_DIM:(h + 1) * C_HEAD_DIM], zeros], axis=1)
            for h in range(x.shape[1] // C_HEAD_DIM)]


def _in_proj1_body(x_ref, g_ref, c_ref, su_ref, sd_ref, kb_ref, one_ref, wq_ref, wkc_ref, wkv_ref,
                   wgl_ref, wbcd_ref, qn_o, qr_o, kc_o, ks_o, vs_o, kw_o, vw_o, gl_o, bcd_o):
    xn = _rms(x_ref[...], g_ref[...]).astype(BF16)
    d = functools.partial(jnp.dot, preferred_element_type=F32)
    c, su, sd = c_ref[...], su_ref[...], sd_ref[...]
    rope = lambda z: _rope(z, c, su, sd)
    q = _head_blocks(d(xn, wq_ref[...]) * (C_HEAD_DIM ** -0.5))
    qn_o[...] = jnp.concatenate(q, axis=1).astype(BF16)
    qr_o[...] = jnp.concatenate([rope(z) for z in q], axis=1).astype(BF16)
    kc_o[...] = d(xn, wkc_ref[...])
    kv = _head_blocks(d(xn, wkv_ref[...]))
    hk = C_KV_HEADS
    ks_o[...] = jnp.concatenate([rope(z) + kb_ref[...] for z in kv[:hk]], axis=1).astype(BF16)
    vs_o[...] = jnp.concatenate([z + one_ref[...] for z in kv[hk:2 * hk]], axis=1).astype(BF16)
    kw_o[...] = jnp.concatenate([rope(z) for z in kv[2 * hk:3 * hk]], axis=1).astype(BF16)
    vw_o[...] = jnp.concatenate([z + one_ref[...] for z in kv[3 * hk:]], axis=1).astype(BF16)
    gl_o[...] = d(xn, wgl_ref[...])
    bcd_o[...] = d(xn, wbcd_ref[...])


def _head_tables(seq):
    half = ROT_DIM // 2
    pos = jnp.arange(seq, dtype=F32)
    inv_freq = ROPE_THETA ** (-jnp.arange(0, ROT_DIM, 2, dtype=F32) / ROT_DIM)
    ang = pos[:, None] * inv_freq[None, :]
    cos, sin = jnp.cos(ang), jnp.sin(ang)
    pad = jnp.zeros((seq, V7X_LANES - ROT_DIM), F32)
    zeros = jnp.zeros((seq, half), F32)
    c = jnp.concatenate([cos, cos, pad + 1.0], axis=1)
    s_up = jnp.concatenate([zeros, sin, pad], axis=1)
    s_dn = jnp.concatenate([-sin, zeros, pad], axis=1)
    lane = jnp.arange(V7X_LANES)[None, :]
    blk = (jnp.arange(seq) // SLC_BLK)[:, None]
    k_bias = jnp.where(lane == C_HEAD_DIM + blk, NEG_INF, 0.0).astype(F32)
    ones = (lane == C_HEAD_DIM).astype(F32)
    return c, s_up, s_dn, k_bias, ones


def _pad_heads(w, n_heads):
    width = w.shape[1] // n_heads
    w = w.reshape(w.shape[0], n_heads, width)
    return jnp.pad(w, ((0, 0), (0, 0), (0, V7X_LANES - width))).reshape(w.shape[0], n_heads * V7X_LANES)


def _in_proj1(x, gain, w_in, seq, tm=256):
    n, d = x.shape
    o = np.cumsum([0, C_WIDTH] + [KV_WIDTH] * 6 + [C_HEADS * N_BRANCH] + [D_WIDTH] * 3)
    wq = w_in[:, o[0]:o[1]].astype(BF16)
    wkc = w_in[:, o[1]:o[3]].astype(BF16)
    wkv = w_in[:, o[3]:o[7]].astype(BF16)
    wgl = _pad_heads(w_in[:, o[7]:o[8]], C_KV_HEADS).astype(BF16)
    wbcd = w_in[:, o[8]:o[11]].astype(BF16)
    ws = [wq, wkc, wkv, wgl, wbcd]
    kvw = C_KV_HEADS * V7X_LANES
    widths = [C_HEADS * V7X_LANES] * 2 + [2 * KV_WIDTH] + [kvw] * 4 + [kvw, 3 * D_WIDTH]
    dts = [BF16, BF16, F32, BF16, BF16, BF16, BF16, F32, F32]
    tps = seq // tm
    tab_spec = pl.BlockSpec((tm, V7X_LANES), lambda i: (i % tps, 0))
    c, s_up, s_dn, k_bias, ones = _head_tables(seq)
    return pl.pallas_call(
        _in_proj1_body,
        grid=(n // tm,),
        in_specs=[_rows_spec(tm, d), _const_spec((1, d)), tab_spec, tab_spec, tab_spec, tab_spec,
                  _const_spec((1, V7X_LANES))] + [_const_spec(w.shape) for w in ws],
        out_specs=[_rows_spec(tm, wd) for wd in widths],
        out_shape=[jax.ShapeDtypeStruct((n, wd), dt) for wd, dt in zip(widths, dts)],
        compiler_params=_params("parallel"),
        name="in_proj1",
    )(x, gain.reshape(1, d), c, s_up, s_dn, k_bias, ones, *ws)


def _compress_body(z_ref, pk_ref, pv_ref, w1k_ref, w1v_ref, w2k_ref, w2v_ref, ko_ref, vo_ref, z_scr):
    half = CMP_STRIDE * C_HEAD_DIM
    n_rows = z_ref.shape[1] // CMP_STRIDE
    streams = ((pk_ref, w1k_ref, w2k_ref, ko_ref), (pv_ref, w1v_ref, w2v_ref, vo_ref))
    for s, (p_ref, w1_ref, w2_ref, o_ref) in enumerate(streams):
        z_scr[...] = z_ref[0, :, s * KV_WIDTH:(s + 1) * KV_WIDTH]
        every16 = [z_scr[pl.ds(l, n_rows, stride=CMP_STRIDE), :] for l in range(CMP_STRIDE)]
        for hk in range(C_KV_HEADS):
            cols = slice(hk * C_HEAD_DIM, (hk + 1) * C_HEAD_DIM)
            r = jnp.concatenate([z[:, cols] for z in every16], axis=1)
            lo = _dot(r + p_ref[:, :half], w1_ref[:half, :])
            hi = _dot(r + p_ref[:, half:], w1_ref[half:, :])
            hidden = _gelu(lo + pltpu.roll(hi, n_rows - 1, axis=0))
            o_ref[0, hk] = _dot(hidden, w2_ref[...]).astype(o_ref.dtype)


def _compress(kcvc, pos_k, pos_v, w1k, w1v, w2k, w2v, bsz, seq):
    n_rows = seq // CMP_STRIDE
    width = kcvc.shape[1]
    consts = [pos_k.reshape(1, -1), pos_v.reshape(1, -1), w1k.astype(BF16), w1v.astype(BF16),
              _pad_heads(w2k, 1).astype(BF16), _pad_heads(w2v, 1).astype(BF16)]
    o_spec = pl.BlockSpec((1, C_KV_HEADS, n_rows, V7X_LANES), lambda b: (b, 0, 0, 0))
    o_shape = jax.ShapeDtypeStruct((bsz, C_KV_HEADS, n_rows, V7X_LANES), BF16)
    return pl.pallas_call(
        _compress_body,
        grid=(bsz,),
        in_specs=[pl.BlockSpec((1, seq, width), lambda b: (b, 0, 0))] + [_const_spec(c.shape) for c in consts],
        out_specs=[o_spec, o_spec],
        out_shape=[o_shape, o_shape],
        scratch_shapes=[pltpu.VMEM((seq, KV_WIDTH), F32)],
        compiler_params=_params("parallel"),
        name="nsa_compress",
    )(kcvc.reshape(bsz, seq, width), *consts)


SEL_CHUNK = 512


def _stack_heads(blk):
    return jnp.concatenate(_lane_blocks(blk), axis=0)


def _rows4(x):
    return jnp.concatenate([x] * C_GROUP, axis=0)


def _nsa_body(qn_ref, qr_ref, gl_ref, kc_ref, vc_ref, ks_ref, vs_ref, kw_ref, vw_ref, ovl_ref, place_ref,
              o_ref):
    n_cmp = kc_ref.shape[2]
    n_slc = ovl_ref.shape[0]
    hd = C_HEAD_DIM
    qb = pl.program_id(2)
    s0 = qb * NSA_TQ
    t_pos = s0 + lax.broadcasted_iota(jnp.int32, (NSA_TQ, 1), 0)
    qn4 = _stack_heads(qn_ref[0])
    qr4 = _stack_heads(qr_ref[0])

    cmp_end = lax.broadcasted_iota(jnp.int32, (1, n_cmp), 1) * CMP_STRIDE + (CMP_LEN - 1)
    pc = _masked_softmax(_dot_nt(qn4, kc_ref[0, 0]), _rows4(cmp_end <= t_pos))
    o_c = _dot(pc, vc_ref[0, 0])
    pc_sum = (pc[:NSA_TQ] + pc[NSA_TQ:2 * NSA_TQ]) + (pc[2 * NSA_TQ:3 * NSA_TQ] + pc[3 * NSA_TQ:])

    p_hi, p_lo = _split_bf16(pc_sum)
    ovl = ovl_ref[...]
    imp = _dot_nt(ovl, p_hi) + _dot_nt(ovl, p_lo)
    jb = lax.broadcasted_iota(jnp.int32, (n_slc, 1), 0)
    t_row = s0 + lax.broadcasted_iota(jnp.int32, (1, NSA_TQ), 1)
    cur = t_row // SLC_BLK
    forced = (jb == 0) | (jb == cur) | (jb == cur - 1)
    imp = jnp.where(jb * SLC_BLK <= t_row, imp + jnp.where(forced, FORCE_BONUS, 0.0), NEG_INF)
    rank = jnp.zeros((n_slc, NSA_TQ), F32)
    for k in range(n_slc):
        ck = imp[k:k + 1, :]
        beats = (ck > imp) | ((ck == imp) & (jb > k))
        rank = rank + jnp.where(beats, 1.0, 0.0)
    not_sel = jnp.where(rank < float(min(SEL_TOPK, n_slc)), 0.0, 1.0).astype(BF16)
    not_sel_q = lax.dot_general(not_sel, place_ref[...], (((0,), (0,)), ((), ())),
                                preferred_element_type=F32)
    q_sel = qr4 + _rows4(not_sel_q.astype(BF16))

    def sel_chunk(c, m, acc, causal):
        k0 = pl.multiple_of(c * SEL_CHUNK, SEL_CHUNK)
        s = _dot_nt(q_sel, ks_ref[0, pl.ds(k0, SEL_CHUNK), :])
        if causal:
            key = k0 + lax.broadcasted_iota(jnp.int32, (1, SEL_CHUNK), 1)
            s = s + _rows4(jnp.where(key <= t_pos, 0.0, NEG_INF))
        m_new = jnp.maximum(m, jnp.max(s, axis=-1, keepdims=True))
        p = jnp.exp((s - m_new).astype(BF16))
        acc = jnp.exp(m - m_new) * acc + _dot(p, vs_ref[0, pl.ds(k0, SEL_CHUNK), :])
        return m_new, acc

    rows = C_GROUP * NSA_TQ
    last = qb // (SEL_CHUNK // NSA_TQ)
    m, acc = lax.fori_loop(0, last, lambda c, ma: sel_chunk(c, ma[0], ma[1], False),
                           (jnp.full((rows, 1), NEG_INF, F32), jnp.zeros((rows, V7X_LANES), F32)))
    _, acc = sel_chunk(last, m, acc, True)
    o_s = acc[:, :hd] / jnp.maximum(acc[:, hd:hd + 1], 1e-30)

    band = NSA_TQ + WIN
    w0 = pl.multiple_of(jnp.maximum(s0 - WIN, 0), NSA_TQ)
    diff = t_pos - (w0 + lax.broadcasted_iota(jnp.int32, (1, band), 1))
    s = _dot_nt(qr4, kw_ref[0, pl.ds(w0, band), :]) + _rows4(
        jnp.where((diff >= 0) & (diff < WIN), 0.0, NEG_INF))
    p = jnp.exp((s - jnp.max(s, axis=-1, keepdims=True)).astype(BF16))
    acc = _dot(p, vw_ref[0, pl.ds(w0, band), :])
    o_w = acc[:, :hd] / jnp.maximum(acc[:, hd:hd + 1], 1e-30)

    gate = _sigmoid(gl_ref[0])
    outs = []
    for g in range(C_GROUP):
        r = slice(g * NSA_TQ, (g + 1) * NSA_TQ)
        gc = g * N_BRANCH
        outs.append(gate[:, gc:gc + 1] * o_c[r, :hd] + gate[:, gc + 1:gc + 2] * o_s[r]
                    + gate[:, gc + 2:gc + 3] * o_w[r])
    o_ref[0] = jnp.concatenate(outs, axis=1).astype(o_ref.dtype)


def _nsa(qn, qr, gl, k_cmp, v_cmp, ks, vs, kw, vw, bsz, seq):
    n_cmp = seq // CMP_STRIDE
    n_slc = seq // SLC_BLK
    assert C_HEAD_DIM + n_slc <= V7X_LANES and seq % SEL_CHUNK == 0
    ci = np.arange(n_cmp)[None, :] * CMP_STRIDE
    sj = np.arange(n_slc)[:, None] * SLC_BLK
    overlap_t = jnp.asarray(((ci < sj + SLC_BLK) & (ci + CMP_LEN > sj)).astype(np.float32)).astype(BF16)
    place = jnp.asarray(np.eye(n_slc, V7X_LANES, k=C_HEAD_DIM, dtype=np.float32)).astype(BF16)
    gw = C_GROUP * V7X_LANES
    q_spec = pl.BlockSpec((1, NSA_TQ, gw), lambda b, h, i: (b, i, h))
    gl_spec = pl.BlockSpec((1, NSA_TQ, V7X_LANES), lambda b, h, i: (b, i, h))
    cmp_spec = pl.BlockSpec((1, 1, n_cmp, V7X_LANES), lambda b, h, i: (b, h, 0, 0))
    kv_spec = pl.BlockSpec((1, seq, V7X_LANES), lambda b, h, i: (b, 0, h))
    as3 = lambda z: z.reshape(bsz, seq, z.shape[-1])
    return pl.pallas_call(
        _nsa_body,
        grid=(bsz, C_KV_HEADS, seq // NSA_TQ),
        in_specs=[q_spec, q_spec, gl_spec, cmp_spec, cmp_spec, kv_spec, kv_spec, kv_spec, kv_spec,
                  _const_spec((n_slc, n_cmp)), _const_spec((n_slc, V7X_LANES))],
        out_specs=pl.BlockSpec((1, NSA_TQ, C_GROUP * C_HEAD_DIM), lambda b, h, i: (b, i, h)),
        out_shape=jax.ShapeDtypeStruct((bsz, seq, C_WIDTH), BF16),
        compiler_params=_params("parallel", "parallel", "arbitrary"),
        name="nsa_attention",
    )(as3(qn), as3(qr), as3(gl), k_cmp, v_cmp, as3(ks), as3(vs), as3(kw), as3(vw), overlap_t, place)


ROW_CHUNKS = D_MODEL // V7X_LANES
assert ROW_CHUNKS == V7X_SUBLANES
DMA_UNROLL = 8
PAD_SPANS = N_EXPERTS + 1


def _tile_rows(n_rows):
    return (n_rows * ROW_CHUNKS, V7X_LANES)


def _row_tile(ref, r):
    start = r * ROW_CHUNKS
    if not isinstance(start, int):
        start = pl.multiple_of(start, ROW_CHUNKS)
    return ref.at[pl.ds(start, ROW_CHUNKS), :]


def _store_row_tiles(ref, x):
    for c in range(ROW_CHUNKS):
        ref[pl.ds(c, x.shape[0], stride=ROW_CHUNKS), :] = x[:, c * V7X_LANES:(c + 1) * V7X_LANES]


def _load_row_tiles(ref):
    rows = ref.shape[0] // ROW_CHUNKS
    return jnp.concatenate([ref[pl.ds(c, rows, stride=ROW_CHUNKS), :] for c in range(ROW_CHUNKS)], axis=1)


def _route(hn, wr_ref, br_ref, tri_ref, cnt_scr):
    logits = _dot_f32(hn, wr_ref[...]) + br_ref[...]
    lane = lax.broadcasted_iota(jnp.int32, logits.shape, 1)
    m1 = jnp.max(logits, axis=-1, keepdims=True)
    i1 = jnp.min(jnp.where(logits == m1, lane, V7X_LANES), axis=-1, keepdims=True)
    rest = jnp.where(lane == i1, NEG_INF, logits)
    m2 = jnp.max(rest, axis=-1, keepdims=True)
    i2 = jnp.min(jnp.where(rest == m2, lane, V7X_LANES), axis=-1, keepdims=True)
    e2 = jnp.exp(m2 - m1)
    den = 1.0 + e2
    hit1, hit2 = lane == i1, lane == i2
    hits = jnp.where(hit1 | hit2, 1.0, 0.0)
    before = jnp.dot(tri_ref[...], hits.astype(BF16), preferred_element_type=F32) + cnt_scr[...]
    r1 = jnp.sum(jnp.where(hit1, before, 0.0), axis=-1, keepdims=True).astype(jnp.int32)
    r2 = jnp.sum(jnp.where(hit2, before, 0.0), axis=-1, keepdims=True).astype(jnp.int32)
    cnt_scr[...] = cnt_scr[...] + jnp.sum(hits, axis=0, keepdims=True)
    idx = jnp.where(lane == 0, i1, jnp.where(lane == 1, i2, jnp.where(lane == 2, r1,
                    jnp.where(lane == 3, r2, 0))))
    gate = jnp.where(lane == 0, 1.0 / den, jnp.where(lane == 1, e2 / den, 0.0))
    return idx, gate


def _row_copy(src, dst, sem):
    return pltpu.make_async_copy(src, dst, sem)


def _dispatch_body(pad_ref, dest_ref, h_ref, g_ref, xs_ref, hn_scr, zero_scr, sems, fill_sem):
    i = pl.program_id(0)
    slot = i % 2
    hn_ref = hn_scr.at[slot]
    _store_row_tiles(hn_ref, _rms(h_ref[...], g_ref[...]))

    @pl.when(i == 0)
    def _():
        zero_scr[...] = jnp.zeros_like(zero_scr)
        for e in range(PAD_SPANS):
            start, count = pad_ref[e], pad_ref[PAD_SPANS + e]

            def fill(r, c):
                _row_copy(zero_scr, _row_tile(xs_ref, start + r), fill_sem).start()
                return c

            def filled(r, c):
                _row_copy(zero_scr, _row_tile(xs_ref, 0), fill_sem).wait()
                return c

            lax.fori_loop(0, count, fill, 0)
            lax.fori_loop(0, count, filled, 0)

    def issue(blk, c):
        for u in range(DMA_UNROLL):
            r = blk * DMA_UNROLL + u
            for k in range(TOP_K):
                _row_copy(_row_tile(hn_ref, r), _row_tile(xs_ref, dest_ref[TOP_K * r + k]),
                          sems.at[slot]).start(priority=k)
        return c

    lax.fori_loop(0, DMA_ROWS // DMA_UNROLL, issue, 0)

    def drain(s):
        for _ in range(TOP_K):
            _row_copy(hn_scr.at[s], xs_ref.at[pl.ds(0, DMA_ROWS * ROW_CHUNKS), :], sems.at[s]).wait()

    @pl.when(i >= 1)
    def _():
        drain(1 - slot)

    @pl.when(i == pl.num_programs(0) - 1)
    def _():
        drain(slot)


def _dispatch(pad_info, dest, h, gain, p_rows):
    n = h.shape[0]
    return pl.pallas_call(
        _dispatch_body,
        grid_spec=pltpu.PrefetchScalarGridSpec(
            num_scalar_prefetch=1,
            grid=(n // DMA_ROWS,),
            in_specs=[pl.BlockSpec((TOP_K * DMA_ROWS,), lambda i, pad: (i,), memory_space=pltpu.SMEM),
                      pl.BlockSpec((DMA_ROWS, D_MODEL), lambda i, pad: (i, 0)),
                      pl.BlockSpec((1, D_MODEL), lambda i, pad: (0, 0))],
            out_specs=pl.BlockSpec(memory_space=pl.ANY),
            scratch_shapes=[pltpu.VMEM((2,) + _tile_rows(DMA_ROWS), F32), pltpu.VMEM(_tile_rows(1), F32),
                            pltpu.SemaphoreType.DMA((2,)), pltpu.SemaphoreType.DMA(())]),
        out_shape=jax.ShapeDtypeStruct(_tile_rows(p_rows), F32),
        compiler_params=_params("arbitrary"),
        name="moe_dispatch",
    )(pad_info, dest, h, gain.reshape(1, -1))


def _experts_body(blk_e_ref, n_used_ref, prev_ref, cur_ref, x_ref, wg_ref, wu_ref, wd_ref, g_ref,
                  o_scr, sems):
    i = pl.program_id(0)
    slot = i % 2

    def scatter(rows_ref, s):
        for r in range(MOE_ROWS):
            _row_copy(_row_tile(o_scr.at[s], r), _row_tile(g_ref, rows_ref[r]), sems.at[s]).start(priority=r % 2)

    def drain(s):
        _row_copy(o_scr.at[s], g_ref.at[pl.ds(0, MOE_ROWS * ROW_CHUNKS), :], sems.at[s]).wait()

    @pl.when(i == 0)
    def _():
        o_scr[...] = jnp.zeros_like(o_scr)

    @pl.when(i >= 1)
    def _():
        drain(slot)

    x = _load_row_tiles(x_ref).astype(BF16)
    scatter(prev_ref, 1 - slot)
    gate = jnp.dot(x, wg_ref[0], preferred_element_type=F32)
    up = jnp.dot(x, wu_ref[0], preferred_element_type=F32)
    act = (gate * _sigmoid(gate)) * up
    _store_row_tiles(o_scr.at[slot], _dot(act, wd_ref[0]))

    @pl.when(i == pl.num_programs(0) - 1)
    def _():
        scatter(cur_ref, slot)
        drain(1 - slot)
        drain(slot)


def _experts(blk_e, n_used, slots, xs, wg, wu, wd):
    p_rows = xs.shape[0] // ROW_CHUNKS
    out_rows = slots.shape[0]
    idx_spec = lambda off: pl.BlockSpec((MOE_ROWS,), lambda i, be, nu: (i + off,), memory_space=pltpu.SMEM)
    x_spec = pl.BlockSpec(_tile_rows(MOE_ROWS), lambda i, be, nu: (jnp.minimum(i, nu[0] - 1), 0))
    w_spec = lambda w: pl.BlockSpec((1,) + w.shape[1:], lambda i, be, nu: (be[i], 0, 0))
    return pl.pallas_call(
        _experts_body,
        grid_spec=pltpu.PrefetchScalarGridSpec(
            num_scalar_prefetch=2,
            grid=(p_rows // MOE_ROWS,),
            in_specs=[idx_spec(0), idx_spec(1), x_spec, w_spec(wg), w_spec(wu), w_spec(wd)],
            out_specs=pl.BlockSpec(memory_space=pl.ANY),
            scratch_shapes=[pltpu.VMEM((2,) + _tile_rows(MOE_ROWS), F32), pltpu.SemaphoreType.DMA((2,))]),
        out_shape=jax.ShapeDtypeStruct(_tile_rows(out_rows), F32),
        compiler_params=_params("arbitrary"),
        name="moe_experts",
    )(blk_e, n_used, slots, slots, xs, wg, wu, wd)


def _combine_body(h_ref, gate_ref, g_ref, y0_ref, y1_ref, o_ref):
    gate = gate_ref[...]
    moe = gate[:, 0:1] * _load_row_tiles(y0_ref) + gate[:, 1:2] * _load_row_tiles(y1_ref)
    o_ref[...] = _rms(h_ref[...] + moe, g_ref[...])


def _combine(h, gates, gain, y_rows, tm=512):
    n = h.shape[0]
    steps = n // tm
    return pl.pallas_call(
        _combine_body,
        grid=(steps,),
        in_specs=[_rows_spec(tm, D_MODEL), _rows_spec(tm, V7X_LANES), _const_spec((1, D_MODEL)),
                  pl.BlockSpec(_tile_rows(tm), lambda i: (i, 0)),
                  pl.BlockSpec(_tile_rows(tm), lambda i: (i + steps, 0))],
        out_specs=_rows_spec(tm, D_MODEL),
        out_shape=jax.ShapeDtypeStruct((n, D_MODEL), F32),
        compiler_params=_params("parallel"),
        name="moe_combine_norm",
    )(h, gates, gain.reshape(1, -1), y_rows, y_rows)


def _moe_layout(idx, counts):
    n = idx.shape[0]
    nk = n * TOP_K
    counts = counts[0, :N_EXPERTS].astype(jnp.int32)
    padded = ((counts + MOE_ROWS - 1) // MOE_ROWS) * MOE_ROWS
    p_end = jnp.cumsum(padded)
    p_start = p_end - padded
    experts = jnp.arange(N_EXPERTS, dtype=jnp.int32)[None, None, :]
    first = jnp.sum(jnp.where(idx[:, :TOP_K, None] == experts, p_start[None, None, :], 0), axis=-1)
    dest = (first + idx[:, TOP_K:2 * TOP_K]).reshape(nk)
    n_blk = (nk + MOE_ROWS - 1) // MOE_ROWS + N_EXPERTS
    blk_first = jnp.arange(n_blk, dtype=jnp.int32)[:, None] * MOE_ROWS
    blk_e = jnp.minimum(jnp.sum((p_end[None, :] <= blk_first).astype(jnp.int32), axis=1), N_EXPERTS - 1)
    n_used = (p_end[-1:] // MOE_ROWS).astype(jnp.int32)
    p_rows = n_blk * MOE_ROWS
    pad_info = jnp.concatenate([p_start + counts, p_end[-1:], padded - counts,
                                p_rows - p_end[-1:]]).astype(jnp.int32)
    pair_by_row = jnp.argsort(dest).astype(jnp.int32)
    rows = jnp.arange(p_rows, dtype=jnp.int32)
    row_e = jnp.repeat(blk_e, MOE_ROWS)
    offset = rows - p_start[row_e]
    before = (jnp.cumsum(counts) - counts)[row_e]
    occupied = offset < counts[row_e]
    pair = pair_by_row[jnp.clip(before + offset, 0, nk - 1)]
    spare = nk + rows - (before + jnp.minimum(offset, counts[row_e]))
    slots = jnp.where(occupied, (pair % TOP_K) * n + pair // TOP_K, spare)
    slots = jnp.concatenate([p_rows + jnp.arange(MOE_ROWS, dtype=jnp.int32), slots]).astype(jnp.int32)
    return dest.astype(jnp.int32), blk_e, n_used, pad_info, p_rows, slots


def _moe_final(h, idx, gates, counts, norm_g, wg, wu, wd, final_g):
    dest, blk_e, n_used, pad_info, p_rows, slots = _moe_layout(idx, counts)
    xs = _dispatch(pad_info, dest, h, norm_g, p_rows)
    y = _experts(blk_e, n_used, slots, xs, wg.astype(BF16), wu.astype(BF16), wd.astype(BF16))
    return _combine(h, gates, final_g, y)


def kernel(x, e_norm_mix, e_w_in, sgu_ln_g, sgu_ln_b, sgu_w, sgu_b, rwkv_mu, rwkv_w0, rwkv_w2,
           rwkv_a0, rwkv_a2, rwkv_g2, rwkv_k_k, rwkv_k_a, rwkv_r_k, rwkv_gn_g, rwkv_gn_b, e_w_out,
           e_norm_ffn, ffn_w_gate, ffn_w_up, ffn_w_down, o_norm_mix, o_w_in, nsa_cmp_pos_k,
           nsa_cmp_pos_v, nsa_cmp_k_w1, nsa_cmp_k_w2, nsa_cmp_v_w1, nsa_cmp_v_w2, conv_w, o_w_out,
           o_norm_ffn, moe_router, moe_router_b, moe_w_gate, moe_w_up, moe_w_down, final_norm):
    bsz, seq, d = x.shape
    n = bsz * seq
    h = x.reshape(n, d)

    ya, scan_ops, g = _layer0_front(h, seq, e_norm_mix[0], e_w_in[0].astype(BF16), sgu_ln_g[0], sgu_ln_b[0],
                                    sgu_w[0], sgu_b[0], rwkv_mu[0], rwkv_w0[0], rwkv_w2[0], rwkv_a0[0],
                                    rwkv_a2[0], rwkv_g2[0])
    rwka, v = _to_scan_layout(scan_ops, bsz, seq)
    ys = _rwkv_scan(rwka, v, rwkv_k_k[0], rwkv_k_a[0], rwkv_r_k[0], rwkv_gn_g[0], rwkv_gn_b[0], bsz, seq)
    h = _layer0_back(h, ya, _from_scan_layout(ys, bsz, seq), g, e_w_out[0].astype(BF16), e_norm_ffn[0],
                     ffn_w_gate[0].astype(BF16), ffn_w_up[0].astype(BF16), ffn_w_down[0].astype(BF16))

    qn, qr, kcvc, ks, vs, kw, vw, gl, bcd = _in_proj1(h, o_norm_mix[0], o_w_in[0], seq)
    k_cmp, v_cmp = _compress(kcvc, nsa_cmp_pos_k[0], nsa_cmp_pos_v[0], nsa_cmp_k_w1[0], nsa_cmp_v_w1[0],
                             nsa_cmp_k_w2[0], nsa_cmp_v_w2[0], bsz, seq)
    yc = _nsa(qn, qr, gl, k_cmp, v_cmp, ks, vs, kw, vw, bsz, seq).reshape(n, C_WIDTH)
    h, idx, gates, counts = _layer1_mid(h, yc, bcd, conv_w[0], o_w_out[0].astype(BF16), seq, o_norm_ffn[0],
                                        moe_router[0], moe_router_b[0])
    out = _moe_final(h, idx, gates, counts, o_norm_ffn[0], moe_w_gate[0], moe_w_up[0], moe_w_down[0],
                     final_norm)
    return out.reshape(bsz, seq, d)
```

```python
import functools

import jax
import jax.numpy as jnp
import numpy as np
from jax import lax
from jax.experimental import pallas as pl
from jax.experimental.pallas import tpu as pltpu

F32 = jnp.float32
BF16 = jnp.bfloat16

D_MODEL = 1024
A_GROUPS = 4
A_GROUP_DIM = 128
A_WIDTH = A_GROUPS * A_GROUP_DIM
CHUNK = 128
SGU_LN_EPS = 1e-5
B_HEADS = 8
B_HEAD_DIM = 64
B_WIDTH = B_HEADS * B_HEAD_DIM
DECAY_LORA = 64
ICLR_LORA = 64
GATE_LORA = 128
B_IN = 3 * B_WIDTH + DECAY_LORA + ICLR_LORA + GATE_LORA
RWKV_GN_EPS = 64e-5
C_HEADS = 8
C_KV_HEADS = 2
C_GROUP = C_HEADS // C_KV_HEADS
C_HEAD_DIM = 64
C_WIDTH = C_HEADS * C_HEAD_DIM
KV_WIDTH = C_KV_HEADS * C_HEAD_DIM
N_BRANCH = 3
CMP_LEN = 32
CMP_STRIDE = 16
CMP_HIDDEN = 256
SLC_BLK = 64
SEL_TOPK = 8
WIN = 512
NSA_TQ = 512
ROT_DIM = C_HEAD_DIM // 4
ROPE_THETA = 500000.0
D_WIDTH = 512
CONV_W = 3
FFN_DIM = 2816
N_EXPERTS = 8
TOP_K = 2
EXPERT_DIM = 1408
NORM_EPS = 1e-6
NEG_INF = -1e30
FORCE_BONUS = 1e6

V7X_LANES = 128
V7X_SUBLANES = 8
V7X_VMEM_LIMIT = 56 * 1024 * 1024

MOE_ROWS = 512
DMA_ROWS = 512


def _params(*sem):
    return pltpu.CompilerParams(dimension_semantics=sem, vmem_limit_bytes=V7X_VMEM_LIMIT)


def _const_spec(shape):
    zeros = (0,) * len(shape)
    return pl.BlockSpec(shape, lambda *_: zeros)


def _rows_spec(tm, width):
    return pl.BlockSpec((tm, width), lambda i: (i, 0))


def _rms(x, g):
    return x * lax.rsqrt(jnp.mean(x * x, axis=-1, keepdims=True) + NORM_EPS) * g


def _gelu(x):
    return x * (0.5 * (1.0 + jnp.tanh(0.7978845608028654 * (x + 0.044715 * (x * x * x)))))


def _sigmoid(x):
    return 1.0 / (1.0 + jnp.exp(-x))


def _dot(a, b):
    return jnp.dot(a.astype(BF16), b.astype(BF16), preferred_element_type=F32)


def _dot_nt(a, b):
    return lax.dot_general(a.astype(BF16), b.astype(BF16), (((1,), (1,)), ((), ())),
                           preferred_element_type=F32)


def _split_bf16(a):
    hi = a.astype(BF16)
    lo = (a - hi.astype(F32)).astype(BF16)
    return hi, lo


def _dot_f32(a, b):
    ah, al = _split_bf16(a)
    bh, bl = _split_bf16(b)
    d = functools.partial(jnp.dot, preferred_element_type=F32)
    return d(ah, bh) + (d(al, bh) + d(ah, bl))


def _masked_softmax(s, mask):
    s = jnp.where(mask, s, NEG_INF)
    m = jnp.max(s, axis=-1, keepdims=True)
    p = jnp.where(mask, jnp.exp(s - m), 0.0)
    return p / jnp.maximum(jnp.sum(p, axis=-1, keepdims=True), 1e-30)


def _sgu_chunk(p_uv, lng_ref, lnb_ref, w_ref, b_ref):
    row = lax.broadcasted_iota(jnp.int32, (CHUNK, CHUNK), 0)
    col = lax.broadcasted_iota(jnp.int32, (CHUNK, CHUNK), 1)
    causal = col <= row
    u = _gelu(p_uv[:, :A_WIDTH])
    v = _gelu(p_uv[:, A_WIDTH:])
    outs = []
    for g in range(A_GROUPS):
        cols = slice(g * A_GROUP_DIM, (g + 1) * A_GROUP_DIM)
        vg = v[:, cols]
        mu = jnp.mean(vg, axis=-1, keepdims=True)
        dv = vg - mu
        var = jnp.mean(dv * dv, axis=-1, keepdims=True)
        vn = dv * lax.rsqrt(var + SGU_LN_EPS) * lng_ref[:, cols] + lnb_ref[:, cols]
        wm = jnp.where(causal, w_ref[g], 0.0)
        mixed = _dot(wm, vn) + b_ref[:, g:g + 1]
        outs.append(u[:, cols] * mixed)
    return jnp.concatenate(outs, axis=1)


def _softplus(x):
    return jnp.maximum(x, 0.0) + jnp.log(1.0 + jnp.exp(-jnp.abs(x)))


SCAN_OPERANDS = 5


def _layer0_front_body(x_ref, gain_ref, win_ref, lng_ref, lnb_ref, sw_ref, sb_ref, mu_ref, w0_ref, w2_ref,
                       a0_ref, a2_ref, g2_ref, ya_o, s_o, g_o, prev_scr, *, tiles_per_seq):
    tm = x_ref.shape[0]

    @pl.when(pl.program_id(0) == 0)
    def _():
        prev_scr[...] = jnp.zeros_like(prev_scr)

    p = jnp.dot(_rms(x_ref[...], gain_ref[...]).astype(BF16), win_ref[...], preferred_element_type=F32)
    for c in range(tm // CHUNK):
        rows = slice(c * CHUNK, (c + 1) * CHUNK)
        ya_o[rows, :] = _sgu_chunk(p[rows, :2 * A_WIDTH], lng_ref, lnb_ref, sw_ref, sb_ref).astype(ya_o.dtype)
    x = p[:, 2 * A_WIDTH:]
    first = (pl.program_id(0) % tiles_per_seq) == 0
    prev_row = jnp.where(first, 0.0, prev_scr[...])
    rowid = lax.broadcasted_iota(jnp.int32, (tm, 1), 0)
    shifted = jnp.where(rowid == 0, prev_row, pltpu.roll(x, 1, axis=0))
    prev_scr[...] = x[tm - 1:tm, :]
    xm = x + (shifted - x) * mu_ref[...]
    o = 3 * B_WIDTH
    wl = xm[:, o:o + DECAY_LORA]
    al = xm[:, o + DECAY_LORA:o + DECAY_LORA + ICLR_LORA]
    gl = xm[:, o + DECAY_LORA + ICLR_LORA:]
    w = -_softplus(-(w0_ref[...] + _dot(jnp.tanh(wl), w2_ref[...]))) - 0.5
    s_o[0] = xm[:, :B_WIDTH]
    s_o[1] = jnp.exp(-jnp.exp(w))
    s_o[2] = xm[:, B_WIDTH:2 * B_WIDTH]
    s_o[3] = _sigmoid(a0_ref[...] + _dot(al, a2_ref[...]))
    s_o[4] = xm[:, 2 * B_WIDTH:3 * B_WIDTH]
    g_o[...] = _dot(_sigmoid(gl), g2_ref[...])


def _layer0_front(h, seq, gain, w_in, ln_g, ln_b, w_s, b_s, mu, w0, w2, a0, a2, g2, tm=512):
    n, d = h.shape
    perm = np.arange(B_WIDTH).reshape(B_HEADS, B_HEAD_DIM).T.reshape(-1)
    r0, k0 = 2 * A_WIDTH, 2 * A_WIDTH + B_WIDTH
    cols = np.concatenate([np.arange(r0), r0 + perm, k0 + perm, np.arange(k0 + B_WIDTH, w_in.shape[1])])
    w_in = w_in[:, cols]
    mu = mu[cols[r0:] - r0]
    w0, w2, a0, a2 = w0[perm], w2[:, perm], a0[perm], a2[:, perm]
    g2 = g2[:, perm]
    row = lambda v: v.reshape(1, -1)
    consts = [row(gain), w_in, row(ln_g), row(ln_b), w_s, b_s.T, row(mu), row(w0), w2, row(a0), a2, g2]
    outs = [jax.ShapeDtypeStruct((n, A_WIDTH), BF16), jax.ShapeDtypeStruct((SCAN_OPERANDS, n, B_WIDTH), F32),
            jax.ShapeDtypeStruct((n, B_WIDTH), F32)]
    return pl.pallas_call(
        functools.partial(_layer0_front_body, tiles_per_seq=seq // tm),
        grid=(n // tm,),
        in_specs=[_rows_spec(tm, d)] + [_const_spec(c.shape) for c in consts],
        out_specs=[_rows_spec(tm, A_WIDTH), pl.BlockSpec((SCAN_OPERANDS, tm, B_WIDTH), lambda i: (0, i, 0)),
                   _rows_spec(tm, B_WIDTH)],
        out_shape=outs,
        scratch_shapes=[pltpu.VMEM((1, B_IN), F32)],
        compiler_params=_params("arbitrary"),
        name="layer0_front",
    )(h, *consts)


def _rwkv_scan_body(r_ref, w_ref, k0_ref, a_ref, v_ref, kkp_ref, kap_ref, rk_ref, gng_ref, gnb_ref,
                    y_ref, s_ref, kkn_ref, ka_ref, km_ref):
    n, tt = r_ref.shape[0], r_ref.shape[1]

    @pl.when(pl.program_id(0) == 0)
    def _():
        s_ref[...] = jnp.zeros_like(s_ref)

    per_dim = lambda p_ref: p_ref[...][:, None, :]
    k0 = k0_ref[...]
    a = a_ref[...]
    kk = k0 * per_dim(kkp_ref)
    kkn = kk / jnp.maximum(jnp.sqrt(jnp.sum(kk * kk, axis=0, keepdims=True)), 1e-12)
    km = k0 * (1.0 + (a - 1.0) * per_dim(kap_ref))
    kkn_ref[...] = kkn
    ka_ref[...] = kkn * a
    km_ref[...] = km

    zero = jnp.zeros((n, r_ref.shape[2]), F32)

    def sa_init(j, acc):
        return acc + s_ref[j] * kkn_ref[j, pl.ds(0, 1), :]

    sa0 = lax.fori_loop(0, n, sa_init, zero, unroll=8)

    def step(t, sa):
        v_t = v_ref[t]
        now = pl.ds(t, 1)
        nxt = pl.ds(jnp.minimum(t + 1, tt - 1), 1)

        def jbody(j, carry):
            y, san = carry
            sn = s_ref[j] * w_ref[j, now, :] + (v_t * km_ref[j, now, :] - sa * ka_ref[j, now, :])
            s_ref[j] = sn
            return y + sn * r_ref[j, now, :], san + sn * kkn_ref[j, nxt, :]

        y, san = lax.fori_loop(0, n, jbody, (zero, zero), unroll=8)
        y_ref[t] = y
        return san

    lax.fori_loop(0, tt, step, sa0)

    y = y_ref[...]
    ym = jnp.mean(y, axis=1, keepdims=True)
    dy = y - ym
    yv = jnp.mean(dy * dy, axis=1, keepdims=True)
    yn = dy * lax.rsqrt(yv + RWKV_GN_EPS) * gng_ref[...][None] + gnb_ref[...][None]
    bonus = jnp.sum(r_ref[...] * km * per_dim(rk_ref), axis=0)
    y_ref[...] = yn + bonus[:, None, :] * v_ref[...]


def _rwkv_scan(rwka, v, k_k, k_a, r_k, gn_g, gn_b, bsz, seq, tt=64):
    n = B_HEAD_DIM
    lanes = bsz * B_HEADS

    def lane_param(p):
        return jnp.tile(p.reshape(B_HEADS, n).T, (1, bsz))

    op_spec = lambda a: pl.BlockSpec((None, n, tt, lanes), lambda c: (a, 0, c, 0))
    slab_spec = pl.BlockSpec((tt, n, lanes), lambda c: (c, 0, 0))
    par_spec = _const_spec((n, lanes))
    return pl.pallas_call(
        _rwkv_scan_body,
        grid=(seq // tt,),
        in_specs=[op_spec(a) for a in range(SCAN_OPERANDS - 1)] + [slab_spec] + [par_spec] * 5,
        out_specs=slab_spec,
        out_shape=jax.ShapeDtypeStruct((seq, n, lanes), F32),
        scratch_shapes=[pltpu.VMEM((n, n, lanes), F32)] + [pltpu.VMEM((n, tt, lanes), F32)] * 3,
        compiler_params=_params("arbitrary"),
        name="rwkv_scan",
    )(*([rwka] * (SCAN_OPERANDS - 1)), v, lane_param(k_k), lane_param(k_a), lane_param(r_k.reshape(-1)),
      lane_param(gn_g), lane_param(gn_b))


SCAN_TT = 128
SCAN_UNROLL = 8


def _time_to_lanes(x_ref, a_ref):
    for b in range(x_ref.shape[0]):
        for blk in range(B_WIDTH // V7X_LANES):
            cols = slice(blk * V7X_LANES, (blk + 1) * V7X_LANES)
            a_ref[b, cols, :] = x_ref[b, :, cols].T


def _to_scan_rows_body(x_ref, o_ref, a_ref):
    bsz = x_ref.shape[0]
    _time_to_lanes(x_ref, a_ref)

    def dims(i, carry):
        for u in range(SCAN_UNROLL):
            d = i * SCAN_UNROLL + u
            rows = pl.ds(pl.multiple_of(d * B_HEADS, B_HEADS), B_HEADS)
            z = jnp.concatenate([a_ref[b, rows, :] for b in range(bsz)], axis=0)
            o_ref[d] = z.T
        return carry

    lax.fori_loop(0, B_HEAD_DIM // SCAN_UNROLL, dims, 0)


def _to_scan_slab_body(x_ref, o_ref, a_ref):
    bsz = x_ref.shape[0]
    _time_to_lanes(x_ref, a_ref)

    def dims(i, carry):
        for u in range(SCAN_UNROLL):
            d = i * SCAN_UNROLL + u
            z = jnp.concatenate([a_ref[b, pl.ds(d, B_HEADS, stride=B_HEAD_DIM), :] for b in range(bsz)],
                                axis=0)
            o_ref[:, d, :] = z.T
        return carry

    lax.fori_loop(0, B_HEAD_DIM // SCAN_UNROLL, dims, 0)


def _to_scan_layout(ops, bsz, seq):
    tt = min(SCAN_TT, seq)
    lanes = bsz * B_HEADS
    ops4 = ops.reshape(SCAN_OPERANDS, bsz, seq, B_WIDTH)
    scratch = [pltpu.VMEM((bsz, B_WIDTH, tt), F32)]
    rwka = pl.pallas_call(
        _to_scan_rows_body,
        grid=(SCAN_OPERANDS - 1, seq // tt),
        in_specs=[pl.BlockSpec((None, bsz, tt, B_WIDTH), lambda a, i: (a, 0, i, 0))],
        out_specs=pl.BlockSpec((None, B_HEAD_DIM, tt, lanes), lambda a, i: (a, 0, i, 0)),
        out_shape=jax.ShapeDtypeStruct((SCAN_OPERANDS - 1, B_HEAD_DIM, seq, lanes), F32),
        scratch_shapes=scratch,
        compiler_params=_params("parallel", "arbitrary"),
        name="to_scan_rows",
    )(ops4)
    v = pl.pallas_call(
        _to_scan_slab_body,
        grid=(seq // tt,),
        in_specs=[pl.BlockSpec((None, bsz, tt, B_WIDTH), lambda i: (SCAN_OPERANDS - 1, 0, i, 0))],
        out_specs=pl.BlockSpec((tt, B_HEAD_DIM, lanes), lambda i: (i, 0, 0)),
        out_shape=jax.ShapeDtypeStruct((seq, B_HEAD_DIM, lanes), F32),
        scratch_shapes=scratch,
        compiler_params=_params("arbitrary"),
        name="to_scan_slab",
    )(ops4)
    return rwka, v


def _from_scan_body(y_ref, o_ref, a_ref):
    bsz = o_ref.shape[0]

    def dims(i, carry):
        for u in range(SCAN_UNROLL):
            d = i * SCAN_UNROLL + u
            zt = y_ref[:, d, :].T
            rows = pl.ds(pl.multiple_of(d * B_HEADS, B_HEADS), B_HEADS)
            for b in range(bsz):
                a_ref[b, rows, :] = zt[b * B_HEADS:(b + 1) * B_HEADS, :]
        return carry

    lax.fori_loop(0, B_HEAD_DIM // SCAN_UNROLL, dims, 0)
    for b in range(bsz):
        for blk in range(B_WIDTH // V7X_LANES):
            cols = slice(blk * V7X_LANES, (blk + 1) * V7X_LANES)
            o_ref[b, :, cols] = a_ref[b, cols, :].T


def _from_scan_layout(y, bsz, seq):
    tt = min(SCAN_TT, seq)
    return pl.pallas_call(
        _from_scan_body,
        grid=(seq // tt,),
        in_specs=[pl.BlockSpec((tt, B_HEAD_DIM, bsz * B_HEADS), lambda i: (i, 0, 0))],
        out_specs=pl.BlockSpec((bsz, tt, B_WIDTH), lambda i: (0, i, 0)),
        out_shape=jax.ShapeDtypeStruct((bsz, seq, B_WIDTH), F32),
        scratch_shapes=[pltpu.VMEM((bsz, B_WIDTH, tt), F32)],
        compiler_params=_params("arbitrary"),
        name="from_scan_layout",
    )(y).reshape(bsz * seq, B_WIDTH)


def _layer1_mid_body(h_ref, yc_ref, bcd_ref, prev_ref, cw_ref, w_ref, g_ref, wr_ref, br_ref, tri_ref,
                     o_ref, idx_o, gate_o, cnt_o, cnt_scr, *, tiles_per_seq):
    tm = h_ref.shape[0]

    @pl.when(pl.program_id(0) == 0)
    def _():
        cnt_scr[...] = jnp.zeros_like(cnt_scr)

    first = (pl.program_id(0) % tiles_per_seq) == 0
    z = bcd_ref[:, D_WIDTH:2 * D_WIDTH] * bcd_ref[:, 2 * D_WIDTH:]
    zp = jnp.where(first, 0.0, prev_ref[:, D_WIDTH:2 * D_WIDTH] * prev_ref[:, 2 * D_WIDTH:])
    rowid = lax.broadcasted_iota(jnp.int32, (tm, 1), 0)
    z1 = jnp.where(rowid == 0, zp[7:8, :], pltpu.roll(z, 1, axis=0))
    z2 = pltpu.roll(z, 2, axis=0)
    z2 = jnp.where(rowid == 0, zp[6:7, :], jnp.where(rowid == 1, zp[7:8, :], z2))
    y = cw_ref[0:1, :] * z2 + cw_ref[1:2, :] * z1 + cw_ref[2:3, :] * z
    yd = bcd_ref[:, :D_WIDTH] * y
    h_new = h_ref[...] + (_dot(yc_ref[...], w_ref[:C_WIDTH, :]) + _dot(yd, w_ref[C_WIDTH:, :]))
    o_ref[...] = h_new
    idx_o[...], gate_o[...] = _route(_rms(h_new, g_ref[...]), wr_ref, br_ref, tri_ref, cnt_scr)
    cnt_o[...] = cnt_scr[...]


def _layer1_mid(h, yc, bcd, conv_w, w_out, seq, gain, w_router, b_router, tm=512):
    n = h.shape[0]
    per8 = tm // V7X_SUBLANES
    wr = jnp.pad(w_router, ((0, 0), (0, V7X_LANES - N_EXPERTS)))
    br = jnp.pad(b_router.reshape(1, -1), ((0, 0), (0, V7X_LANES - N_EXPERTS)), constant_values=NEG_INF)
    tri = jnp.asarray(np.tril(np.ones((tm, tm), np.float32), k=-1)).astype(BF16)
    return pl.pallas_call(
        functools.partial(_layer1_mid_body, tiles_per_seq=seq // tm),
        grid=(n // tm,),
        in_specs=[_rows_spec(tm, D_MODEL), _rows_spec(tm, C_WIDTH), _rows_spec(tm, 3 * D_WIDTH),
                  pl.BlockSpec((V7X_SUBLANES, 3 * D_WIDTH), lambda i: (jnp.maximum(i * per8 - 1, 0), 0)),
                  _const_spec((CONV_W, D_WIDTH)), _const_spec(w_out.shape), _const_spec((1, D_MODEL)),
                  _const_spec(wr.shape), _const_spec(br.shape), _const_spec(tri.shape)],
        out_specs=[_rows_spec(tm, D_MODEL), _rows_spec(tm, V7X_LANES), _rows_spec(tm, V7X_LANES),
                   _const_spec((1, V7X_LANES))],
        out_shape=[jax.ShapeDtypeStruct((n, D_MODEL), F32), jax.ShapeDtypeStruct((n, V7X_LANES), jnp.int32),
                   jax.ShapeDtypeStruct((n, V7X_LANES), F32), jax.ShapeDtypeStruct((1, V7X_LANES), F32)],
        scratch_shapes=[pltpu.VMEM((1, V7X_LANES), F32)],
        compiler_params=_params("arbitrary"),
        name="layer1_mid",
    )(h, yc, bcd, bcd, conv_w, w_out, gain.reshape(1, -1), wr, br, tri)


def _layer0_back_body(h_ref, ya_ref, ys_ref, gm_ref, wo_ref, g_ref, wg_ref, wu_ref, wd_ref, o_ref):
    yb = ys_ref[...] * gm_ref[...]
    h = h_ref[...] + (_dot(ya_ref[...], wo_ref[:A_WIDTH, :]) + _dot(yb, wo_ref[A_WIDTH:, :]))
    hn = _rms(h, g_ref[...]).astype(BF16)
    gate = jnp.dot(hn, wg_ref[...], preferred_element_type=F32)
    up = jnp.dot(hn, wu_ref[...], preferred_element_type=F32)
    act = (gate * _sigmoid(gate)) * up
    o_ref[...] = h + _dot(act, wd_ref[...])


def _layer0_back(h, ya, ys, gm, w_out, gain, wg, wu, wd, tm=512):
    n = h.shape[0]
    perm = np.arange(B_WIDTH).reshape(B_HEADS, B_HEAD_DIM).T.reshape(-1)
    w_out = jnp.concatenate([w_out[:A_WIDTH], w_out[A_WIDTH:][perm]], axis=0)
    return pl.pallas_call(
        _layer0_back_body,
        grid=(n // tm,),
        in_specs=[_rows_spec(tm, D_MODEL), _rows_spec(tm, A_WIDTH), _rows_spec(tm, B_WIDTH),
                  _rows_spec(tm, B_WIDTH), _const_spec(w_out.shape), _const_spec((1, D_MODEL)),
                  _const_spec(wg.shape), _const_spec(wu.shape), _const_spec(wd.shape)],
        out_specs=_rows_spec(tm, D_MODEL),
        out_shape=jax.ShapeDtypeStruct((n, D_MODEL), F32),
        compiler_params=_params("parallel"),
        name="layer0_back",
    )(h, ya, ys, gm, w_out, gain.reshape(1, -1), wg, wu, wd)


def _rope(x, c, s_up, s_dn):
    half = ROT_DIM // 2
    return x * c + pltpu.roll(x, half, axis=1) * s_up + pltpu.roll(x, V7X_LANES - half, axis=1) * s_dn


def _lane_blocks(x):
    return [x[:, i * V7X_LANES:(i + 1) * V7X_LANES] for i in range(x.shape[1] // V7X_LANES)]


def _head_blocks(x):
    zeros = jnp.zeros((x.shape[0], V7X_LANES - C_HEAD_DIM), x.dtype)
    return [jnp.concatenate([x[:, h * C_HEAD_DIM:(h + 1) * C_HEAD_DIM], zeros], axis=1)
            for h in range(x.shape[1] // C_HEAD_DIM)]


def _in_proj1_body(x_ref, g_ref, c_ref, su_ref, sd_ref, kb_ref, one_ref, wq_ref, wkc_ref, wkv_ref,
                   wgl_ref, wbcd_ref, qn_o, qr_o, kc_o, ks_o, vs_o, kw_o, vw_o, gl_o, bcd_o):
    xn = _rms(x_ref[...], g_ref[...]).astype(BF16)
    d = functools.partial(jnp.dot, preferred_element_type=F32)
    c, su, sd = c_ref[...], su_ref[...], sd_ref[...]
    rope = lambda z: _rope(z, c, su, sd)
    q = _head_blocks(d(xn, wq_ref[...]) * (C_HEAD_DIM ** -0.5))
    qn_o[...] = jnp.concatenate(q, axis=1).astype(BF16)
    qr_o[...] = jnp.concatenate([rope(z) for z in q], axis=1).astype(BF16)
    kc_o[...] = d(xn, wkc_ref[...])
    kv = _head_blocks(d(xn, wkv_ref[...]))
    hk = C_KV_HEADS
    ks_o[...] = jnp.concatenate([rope(z) + kb_ref[...] for z in kv[:hk]], axis=1).astype(BF16)
    vs_o[...] = jnp.concatenate([z + one_ref[...] for z in kv[hk:2 * hk]], axis=1).astype(BF16)
    kw_o[...] = jnp.concatenate([rope(z) for z in kv[2 * hk:3 * hk]], axis=1).astype(BF16)
    vw_o[...] = jnp.concatenate([z + one_ref[...] for z in kv[3 * hk:]], axis=1).astype(BF16)
    gl_o[...] = d(xn, wgl_ref[...])
    bcd_o[...] = d(xn, wbcd_ref[...])


def _head_tables(seq):
    half = ROT_DIM // 2
    pos = jnp.arange(seq, dtype=F32)
    inv_freq = ROPE_THETA ** (-jnp.arange(0, ROT_DIM, 2, dtype=F32) / ROT_DIM)
    ang = pos[:, None] * inv_freq[None, :]
    cos, sin = jnp.cos(ang), jnp.sin(ang)
    pad = jnp.zeros((seq, V7X_LANES - ROT_DIM), F32)
    zeros = jnp.zeros((seq, half), F32)
    c = jnp.concatenate([cos, cos, pad + 1.0], axis=1)
    s_up = jnp.concatenate([zeros, sin, pad], axis=1)
    s_dn = jnp.concatenate([-sin, zeros, pad], axis=1)
    lane = jnp.arange(V7X_LANES)[None, :]
    blk = (jnp.arange(seq) // SLC_BLK)[:, None]
    k_bias = jnp.where(lane == C_HEAD_DIM + blk, NEG_INF, 0.0).astype(F32)
    ones = (lane == C_HEAD_DIM).astype(F32)
    return c, s_up, s_dn, k_bias, ones


def _pad_heads(w, n_heads):
    width = w.shape[1] // n_heads
    w = w.reshape(w.shape[0], n_heads, width)
    return jnp.pad(w, ((0, 0), (0, 0), (0, V7X_LANES - width))).reshape(w.shape[0], n_heads * V7X_LANES)


def _in_proj1(x, gain, w_in, seq, tm=512):
    n, d = x.shape
    o = np.cumsum([0, C_WIDTH] + [KV_WIDTH] * 6 + [C_HEADS * N_BRANCH] + [D_WIDTH] * 3)
    wq = w_in[:, o[0]:o[1]].astype(BF16)
    wkc = w_in[:, o[1]:o[3]].astype(BF16)
    wkv = w_in[:, o[3]:o[7]].astype(BF16)
    wgl = _pad_heads(w_in[:, o[7]:o[8]], C_KV_HEADS).astype(BF16)
    wbcd = w_in[:, o[8]:o[11]].astype(BF16)
    ws = [wq, wkc, wkv, wgl, wbcd]
    kvw = C_KV_HEADS * V7X_LANES
    widths = [C_HEADS * V7X_LANES] * 2 + [2 * KV_WIDTH] + [kvw] * 4 + [kvw, 3 * D_WIDTH]
    dts = [BF16, BF16, F32, BF16, BF16, BF16, BF16, F32, F32]
    tps = seq // tm
    tab_spec = pl.BlockSpec((tm, V7X_LANES), lambda i: (i % tps, 0))
    c, s_up, s_dn, k_bias, ones = _head_tables(seq)
    return pl.pallas_call(
        _in_proj1_body,
        grid=(n // tm,),
        in_specs=[_rows_spec(tm, d), _const_spec((1, d)), tab_spec, tab_spec, tab_spec, tab_spec,
                  _const_spec((1, V7X_LANES))] + [_const_spec(w.shape) for w in ws],
        out_specs=[_rows_spec(tm, wd) for wd in widths],
        out_shape=[jax.ShapeDtypeStruct((n, wd), dt) for wd, dt in zip(widths, dts)],
        compiler_params=_params("parallel"),
        name="in_proj1",
    )(x, gain.reshape(1, d), c, s_up, s_dn, k_bias, ones, *ws)


def _compress_body(z_ref, pk_ref, pv_ref, w1k_ref, w1v_ref, w2k_ref, w2v_ref, ko_ref, vo_ref, z_scr):
    half = CMP_STRIDE * C_HEAD_DIM
    n_rows = z_ref.shape[1] // CMP_STRIDE
    streams = ((pk_ref, w1k_ref, w2k_ref, ko_ref), (pv_ref, w1v_ref, w2v_ref, vo_ref))
    for s, (p_ref, w1_ref, w2_ref, o_ref) in enumerate(streams):
        z_scr[...] = z_ref[0, :, s * KV_WIDTH:(s + 1) * KV_WIDTH]
        every16 = [z_scr[pl.ds(l, n_rows, stride=CMP_STRIDE), :] for l in range(CMP_STRIDE)]
        for hk in range(C_KV_HEADS):
            cols = slice(hk * C_HEAD_DIM, (hk + 1) * C_HEAD_DIM)
            r = jnp.concatenate([z[:, cols] for z in every16], axis=1)
            lo = _dot(r + p_ref[:, :half], w1_ref[:half, :])
            hi = _dot(r + p_ref[:, half:], w1_ref[half:, :])
            hidden = _gelu(lo + pltpu.roll(hi, n_rows - 1, axis=0))
            o_ref[0, hk] = _dot(hidden, w2_ref[...]).astype(o_ref.dtype)


def _compress(kcvc, pos_k, pos_v, w1k, w1v, w2k, w2v, bsz, seq):
    n_rows = seq // CMP_STRIDE
    width = kcvc.shape[1]
    consts = [pos_k.reshape(1, -1), pos_v.reshape(1, -1), w1k.astype(BF16), w1v.astype(BF16),
              _pad_heads(w2k, 1).astype(BF16), _pad_heads(w2v, 1).astype(BF16)]
    o_spec = pl.BlockSpec((1, C_KV_HEADS, n_rows, V7X_LANES), lambda b: (b, 0, 0, 0))
    o_shape = jax.ShapeDtypeStruct((bsz, C_KV_HEADS, n_rows, V7X_LANES), BF16)
    return pl.pallas_call(
        _compress_body,
        grid=(bsz,),
        in_specs=[pl.BlockSpec((1, seq, width), lambda b: (b, 0, 0))] + [_const_spec(c.shape) for c in consts],
        out_specs=[o_spec, o_spec],
        out_shape=[o_shape, o_shape],
        scratch_shapes=[pltpu.VMEM((seq, KV_WIDTH), F32)],
        compiler_params=_params("parallel"),
        name="nsa_compress",
    )(kcvc.reshape(bsz, seq, width), *consts)


SEL_CHUNK = 512


def _stack_heads(blk):
    return jnp.concatenate(_lane_blocks(blk), axis=0)


def _rows4(x):
    return jnp.concatenate([x] * C_GROUP, axis=0)


def _nsa_body(qn_ref, qr_ref, gl_ref, kc_ref, vc_ref, ks_ref, vs_ref, kw_ref, vw_ref, ovl_ref, place_ref,
              o_ref):
    n_cmp = kc_ref.shape[2]
    n_slc = ovl_ref.shape[0]
    hd = C_HEAD_DIM
    qb = pl.program_id(2)
    s0 = qb * NSA_TQ
    t_pos = s0 + lax.broadcasted_iota(jnp.int32, (NSA_TQ, 1), 0)
    qn4 = _stack_heads(qn_ref[0])
    qr4 = _stack_heads(qr_ref[0])

    cmp_end = lax.broadcasted_iota(jnp.int32, (1, n_cmp), 1) * CMP_STRIDE + (CMP_LEN - 1)
    pc = _masked_softmax(_dot_nt(qn4, kc_ref[0, 0]), _rows4(cmp_end <= t_pos))
    o_c = _dot(pc, vc_ref[0, 0])
    pc_sum = (pc[:NSA_TQ] + pc[NSA_TQ:2 * NSA_TQ]) + (pc[2 * NSA_TQ:3 * NSA_TQ] + pc[3 * NSA_TQ:])

    p_hi, p_lo = _split_bf16(pc_sum)
    ovl = ovl_ref[...]
    imp = _dot_nt(ovl, p_hi) + _dot_nt(ovl, p_lo)
    jb = lax.broadcasted_iota(jnp.int32, (n_slc, 1), 0)
    t_row = s0 + lax.broadcasted_iota(jnp.int32, (1, NSA_TQ), 1)
    cur = t_row // SLC_BLK
    forced = (jb == 0) | (jb == cur) | (jb == cur - 1)
    imp = jnp.where(jb * SLC_BLK <= t_row, imp + jnp.where(forced, FORCE_BONUS, 0.0), NEG_INF)
    rank = jnp.zeros((n_slc, NSA_TQ), F32)
    for k in range(n_slc):
        ck = imp[k:k + 1, :]
        beats = (ck > imp) | ((ck == imp) & (jb > k))
        rank = rank + jnp.where(beats, 1.0, 0.0)
    not_sel = jnp.where(rank < float(min(SEL_TOPK, n_slc)), 0.0, 1.0).astype(BF16)
    not_sel_q = lax.dot_general(not_sel, place_ref[...], (((0,), (0,)), ((), ())),
                                preferred_element_type=F32)
    q_sel = qr4 + _rows4(not_sel_q.astype(BF16))

    def sel_chunk(c, m, acc, causal):
        k0 = pl.multiple_of(c * SEL_CHUNK, SEL_CHUNK)
        s = _dot_nt(q_sel, ks_ref[0, pl.ds(k0, SEL_CHUNK), :])
        if causal:
            key = k0 + lax.broadcasted_iota(jnp.int32, (1, SEL_CHUNK), 1)
            s = s + _rows4(jnp.where(key <= t_pos, 0.0, NEG_INF))
        m_new = jnp.maximum(m, jnp.max(s, axis=-1, keepdims=True))
        p = jnp.exp((s - m_new).astype(BF16))
        acc = jnp.exp(m - m_new) * acc + _dot(p, vs_ref[0, pl.ds(k0, SEL_CHUNK), :])
        return m_new, acc

    rows = C_GROUP * NSA_TQ
    last = qb // (SEL_CHUNK // NSA_TQ)
    m, acc = lax.fori_loop(0, last, lambda c, ma: sel_chunk(c, ma[0], ma[1], False),
                           (jnp.full((rows, 1), NEG_INF, F32), jnp.zeros((rows, V7X_LANES), F32)))
    _, acc = sel_chunk(last, m, acc, True)
    o_s = acc[:, :hd] / jnp.maximum(acc[:, hd:hd + 1], 1e-30)

    band = NSA_TQ + WIN
    w0 = pl.multiple_of(jnp.maximum(s0 - WIN, 0), NSA_TQ)
    diff = t_pos - (w0 + lax.broadcasted_iota(jnp.int32, (1, band), 1))
    s = _dot_nt(qr4, kw_ref[0, pl.ds(w0, band), :]) + _rows4(
        jnp.where((diff >= 0) & (diff < WIN), 0.0, NEG_INF))
    p = jnp.exp((s - jnp.max(s, axis=-1, keepdims=True)).astype(BF16))
    acc = _dot(p, vw_ref[0, pl.ds(w0, band), :])
    o_w = acc[:, :hd] / jnp.maximum(acc[:, hd:hd + 1], 1e-30)

    gate = _sigmoid(gl_ref[0])
    outs = []
    for g in range(C_GROUP):
        r = slice(g * NSA_TQ, (g + 1) * NSA_TQ)
        gc = g * N_BRANCH
        outs.append(gate[:, gc:gc + 1] * o_c[r, :hd] + gate[:, gc + 1:gc + 2] * o_s[r]
                    + gate[:, gc + 2:gc + 3] * o_w[r])
    o_ref[0] = jnp.concatenate(outs, axis=1).astype(o_ref.dtype)


def _nsa(qn, qr, gl, k_cmp, v_cmp, ks, vs, kw, vw, bsz, seq):
    n_cmp = seq // CMP_STRIDE
    n_slc = seq // SLC_BLK
    assert C_HEAD_DIM + n_slc <= V7X_LANES and seq % SEL_CHUNK == 0
    ci = np.arange(n_cmp)[None, :] * CMP_STRIDE
    sj = np.arange(n_slc)[:, None] * SLC_BLK
    overlap_t = jnp.asarray(((ci < sj + SLC_BLK) & (ci + CMP_LEN > sj)).astype(np.float32)).astype(BF16)
    place = jnp.asarray(np.eye(n_slc, V7X_LANES, k=C_HEAD_DIM, dtype=np.float32)).astype(BF16)
    gw = C_GROUP * V7X_LANES
    q_spec = pl.BlockSpec((1, NSA_TQ, gw), lambda b, h, i: (b, i, h))
    gl_spec = pl.BlockSpec((1, NSA_TQ, V7X_LANES), lambda b, h, i: (b, i, h))
    cmp_spec = pl.BlockSpec((1, 1, n_cmp, V7X_LANES), lambda b, h, i: (b, h, 0, 0))
    kv_spec = pl.BlockSpec((1, seq, V7X_LANES), lambda b, h, i: (b, 0, h))
    as3 = lambda z: z.reshape(bsz, seq, z.shape[-1])
    return pl.pallas_call(
        _nsa_body,
        grid=(bsz, C_KV_HEADS, seq // NSA_TQ),
        in_specs=[q_spec, q_spec, gl_spec, cmp_spec, cmp_spec, kv_spec, kv_spec, kv_spec, kv_spec,
                  _const_spec((n_slc, n_cmp)), _const_spec((n_slc, V7X_LANES))],
        out_specs=pl.BlockSpec((1, NSA_TQ, C_GROUP * C_HEAD_DIM), lambda b, h, i: (b, i, h)),
        out_shape=jax.ShapeDtypeStruct((bsz, seq, C_WIDTH), BF16),
        compiler_params=_params("parallel", "parallel", "arbitrary"),
        name="nsa_attention",
    )(as3(qn), as3(qr), as3(gl), k_cmp, v_cmp, as3(ks), as3(vs), as3(kw), as3(vw), overlap_t, place)


ROW_CHUNKS = D_MODEL // V7X_LANES
assert ROW_CHUNKS == V7X_SUBLANES
DMA_UNROLL = 8
PAD_SPANS = N_EXPERTS + 1


def _tile_rows(n_rows):
    return (n_rows * ROW_CHUNKS, V7X_LANES)


def _row_tile(ref, r):
    start = r * ROW_CHUNKS
    if not isinstance(start, int):
        start = pl.multiple_of(start, ROW_CHUNKS)
    return ref.at[pl.ds(start, ROW_CHUNKS), :]


def _store_row_tiles(ref, x):
    for c in range(ROW_CHUNKS):
        ref[pl.ds(c, x.shape[0], stride=ROW_CHUNKS), :] = x[:, c * V7X_LANES:(c + 1) * V7X_LANES]


def _load_row_tiles(ref):
    rows = ref.shape[0] // ROW_CHUNKS
    return jnp.concatenate([ref[pl.ds(c, rows, stride=ROW_CHUNKS), :] for c in range(ROW_CHUNKS)], axis=1)


def _route(hn, wr_ref, br_ref, tri_ref, cnt_scr):
    logits = _dot_f32(hn, wr_ref[...]) + br_ref[...]
    lane = lax.broadcasted_iota(jnp.int32, logits.shape, 1)
    m1 = jnp.max(logits, axis=-1, keepdims=True)
    i1 = jnp.min(jnp.where(logits == m1, lane, V7X_LANES), axis=-1, keepdims=True)
    rest = jnp.where(lane == i1, NEG_INF, logits)
    m2 = jnp.max(rest, axis=-1, keepdims=True)
    i2 = jnp.min(jnp.where(rest == m2, lane, V7X_LANES), axis=-1, keepdims=True)
    e2 = jnp.exp(m2 - m1)
    den = 1.0 + e2
    hit1, hit2 = lane == i1, lane == i2
    hits = jnp.where(hit1 | hit2, 1.0, 0.0)
    before = jnp.dot(tri_ref[...], hits.astype(BF16), preferred_element_type=F32) + cnt_scr[...]
    r1 = jnp.sum(jnp.where(hit1, before, 0.0), axis=-1, keepdims=True).astype(jnp.int32)
    r2 = jnp.sum(jnp.where(hit2, before, 0.0), axis=-1, keepdims=True).astype(jnp.int32)
    cnt_scr[...] = cnt_scr[...] + jnp.sum(hits, axis=0, keepdims=True)
    idx = jnp.where(lane == 0, i1, jnp.where(lane == 1, i2, jnp.where(lane == 2, r1,
                    jnp.where(lane == 3, r2, 0))))
    gate = jnp.where(lane == 0, 1.0 / den, jnp.where(lane == 1, e2 / den, 0.0))
    return idx, gate


def _row_copy(src, dst, sem):
    return pltpu.make_async_copy(src, dst, sem)


def _dispatch_body(pad_ref, dest_ref, h_ref, g_ref, xs_ref, hn_scr, zero_scr, sems, fill_sem):
    i = pl.program_id(0)
    slot = i % 2
    hn_ref = hn_scr.at[slot]
    _store_row_tiles(hn_ref, _rms(h_ref[...], g_ref[...]))

    @pl.when(i == 0)
    def _():
        zero_scr[...] = jnp.zeros_like(zero_scr)
        for e in range(PAD_SPANS):
            start, count = pad_ref[e], pad_ref[PAD_SPANS + e]

            def fill(r, c):
                _row_copy(zero_scr, _row_tile(xs_ref, start + r), fill_sem).start()
                return c

            def filled(r, c):
                _row_copy(zero_scr, _row_tile(xs_ref, 0), fill_sem).wait()
                return c

            lax.fori_loop(0, count, fill, 0)
            lax.fori_loop(0, count, filled, 0)

    def issue(blk, c):
        for u in range(DMA_UNROLL):
            r = blk * DMA_UNROLL + u
            for k in range(TOP_K):
                _row_copy(_row_tile(hn_ref, r), _row_tile(xs_ref, dest_ref[TOP_K * r + k]),
                          sems.at[slot]).start(priority=k)
        return c

    lax.fori_loop(0, DMA_ROWS // DMA_UNROLL, issue, 0)

    def drain(s):
        for _ in range(TOP_K):
            _row_copy(hn_scr.at[s], xs_ref.at[pl.ds(0, DMA_ROWS * ROW_CHUNKS), :], sems.at[s]).wait()

    @pl.when(i >= 1)
    def _():
        drain(1 - slot)

    @pl.when(i == pl.num_programs(0) - 1)
    def _():
        drain(slot)


def _dispatch(pad_info, dest, h, gain, p_rows):
    n = h.shape[0]
    return pl.pallas_call(
        _dispatch_body,
        grid_spec=pltpu.PrefetchScalarGridSpec(
            num_scalar_prefetch=1,
            grid=(n // DMA_ROWS,),
            in_specs=[pl.BlockSpec((TOP_K * DMA_ROWS,), lambda i, pad: (i,), memory_space=pltpu.SMEM),
                      pl.BlockSpec((DMA_ROWS, D_MODEL), lambda i, pad: (i, 0)),
                      pl.BlockSpec((1, D_MODEL), lambda i, pad: (0, 0))],
            out_specs=pl.BlockSpec(memory_space=pl.ANY),
            scratch_shapes=[pltpu.VMEM((2,) + _tile_rows(DMA_ROWS), F32), pltpu.VMEM(_tile_rows(1), F32),
                            pltpu.SemaphoreType.DMA((2,)), pltpu.SemaphoreType.DMA(())]),
        out_shape=jax.ShapeDtypeStruct(_tile_rows(p_rows), F32),
        compiler_params=_params("arbitrary"),
        name="moe_dispatch",
    )(pad_info, dest, h, gain.reshape(1, -1))


def _experts_body(blk_e_ref, n_used_ref, prev_ref, cur_ref, x_ref, wg_ref, wu_ref, wd_ref, g_ref,
                  o_scr, sems):
    i = pl.program_id(0)
    slot = i % 2

    def scatter(rows_ref, s):
        for r in range(MOE_ROWS):
            _row_copy(_row_tile(o_scr.at[s], r), _row_tile(g_ref, rows_ref[r]), sems.at[s]).start(priority=r % 2)

    def drain(s):
        _row_copy(o_scr.at[s], g_ref.at[pl.ds(0, MOE_ROWS * ROW_CHUNKS), :], sems.at[s]).wait()

    @pl.when(i == 0)
    def _():
        o_scr[...] = jnp.zeros_like(o_scr)

    @pl.when(i >= 1)
    def _():
        drain(slot)

    x = _load_row_tiles(x_ref).astype(BF16)
    scatter(prev_ref, 1 - slot)
    gate = jnp.dot(x, wg_ref[0], preferred_element_type=F32)
    up = jnp.dot(x, wu_ref[0], preferred_element_type=F32)
    act = (gate * _sigmoid(gate)) * up
    _store_row_tiles(o_scr.at[slot], _dot(act, wd_ref[0]))

    @pl.when(i == pl.num_programs(0) - 1)
    def _():
        scatter(cur_ref, slot)
        drain(1 - slot)
        drain(slot)


def _experts(blk_e, n_used, slots, xs, wg, wu, wd):
    p_rows = xs.shape[0] // ROW_CHUNKS
    out_rows = slots.shape[0]
    idx_spec = lambda off: pl.BlockSpec((MOE_ROWS,), lambda i, be, nu: (i + off,), memory_space=pltpu.SMEM)
    x_spec = pl.BlockSpec(_tile_rows(MOE_ROWS), lambda i, be, nu: (jnp.minimum(i, nu[0] - 1), 0))
    w_spec = lambda w: pl.BlockSpec((1,) + w.shape[1:], lambda i, be, nu: (be[i], 0, 0))
    return pl.pallas_call(
        _experts_body,
        grid_spec=pltpu.PrefetchScalarGridSpec(
            num_scalar_prefetch=2,
            grid=(p_rows // MOE_ROWS,),
            in_specs=[idx_spec(0), idx_spec(1), x_spec, w_spec(wg), w_spec(wu), w_spec(wd)],
            out_specs=pl.BlockSpec(memory_space=pl.ANY),
            scratch_shapes=[pltpu.VMEM((2,) + _tile_rows(MOE_ROWS), F32), pltpu.SemaphoreType.DMA((2,))]),
        out_shape=jax.ShapeDtypeStruct(_tile_rows(out_rows), F32),
        compiler_params=_params("arbitrary"),
        name="moe_experts",
    )(blk_e, n_used, slots, slots, xs, wg, wu, wd)


def _combine_body(h_ref, gate_ref, g_ref, y0_ref, y1_ref, o_ref):
    gate = gate_ref[...]
    moe = gate[:, 0:1] * _load_row_tiles(y0_ref) + gate[:, 1:2] * _load_row_tiles(y1_ref)
    o_ref[...] = _rms(h_ref[...] + moe, g_ref[...])


def _combine(h, gates, gain, y_rows, tm=512):
    n = h.shape[0]
    steps = n // tm
    return pl.pallas_call(
        _combine_body,
        grid=(steps,),
        in_specs=[_rows_spec(tm, D_MODEL), _rows_spec(tm, V7X_LANES), _const_spec((1, D_MODEL)),
                  pl.BlockSpec(_tile_rows(tm), lambda i: (i, 0)),
                  pl.BlockSpec(_tile_rows(tm), lambda i: (i + steps, 0))],
        out_specs=_rows_spec(tm, D_MODEL),
        out_shape=jax.ShapeDtypeStruct((n, D_MODEL), F32),
        compiler_params=_params("parallel"),
        name="moe_combine_norm",
    )(h, gates, gain.reshape(1, -1), y_rows, y_rows)


def _moe_layout(idx, counts):
    n = idx.shape[0]
    nk = n * TOP_K
    counts = counts[0, :N_EXPERTS].astype(jnp.int32)
    padded = ((counts + MOE_ROWS - 1) // MOE_ROWS) * MOE_ROWS
    p_end = jnp.cumsum(padded)
    p_start = p_end - padded
    experts = jnp.arange(N_EXPERTS, dtype=jnp.int32)[None, None, :]
    first = jnp.sum(jnp.where(idx[:, :TOP_K, None] == experts, p_start[None, None, :], 0), axis=-1)
    dest = (first + idx[:, TOP_K:2 * TOP_K]).reshape(nk)
    n_blk = (nk + MOE_ROWS - 1) // MOE_ROWS + N_EXPERTS
    blk_first = jnp.arange(n_blk, dtype=jnp.int32)[:, None] * MOE_ROWS
    blk_e = jnp.minimum(jnp.sum((p_end[None, :] <= blk_first).astype(jnp.int32), axis=1), N_EXPERTS - 1)
    n_used = (p_end[-1:] // MOE_ROWS).astype(jnp.int32)
    p_rows = n_blk * MOE_ROWS
    pad_info = jnp.concatenate([p_start + counts, p_end[-1:], padded - counts,
                                p_rows - p_end[-1:]]).astype(jnp.int32)
    pair_by_row = jnp.argsort(dest).astype(jnp.int32)
    rows = jnp.arange(p_rows, dtype=jnp.int32)
    row_e = jnp.repeat(blk_e, MOE_ROWS)
    offset = rows - p_start[row_e]
    before = (jnp.cumsum(counts) - counts)[row_e]
    occupied = offset < counts[row_e]
    pair = pair_by_row[jnp.clip(before + offset, 0, nk - 1)]
    spare = nk + rows - (before + jnp.minimum(offset, counts[row_e]))
    slots = jnp.where(occupied, (pair % TOP_K) * n + pair // TOP_K, spare)
    slots = jnp.concatenate([p_rows + jnp.arange(MOE_ROWS, dtype=jnp.int32), slots]).astype(jnp.int32)
    return dest.astype(jnp.int32), blk_e, n_used, pad_info, p_rows, slots


def _moe_final(h, idx, gates, counts, norm_g, wg, wu, wd, final_g):
    dest, blk_e, n_used, pad_info, p_rows, slots = _moe_layout(idx, counts)
    xs = _dispatch(pad_info, dest, h, norm_g, p_rows)
    y = _experts(blk_e, n_used, slots, xs, wg.astype(BF16), wu.astype(BF16), wd.astype(BF16))
    return _combine(h, gates, final_g, y)


def kernel(x, e_norm_mix, e_w_in, sgu_ln_g, sgu_ln_b, sgu_w, sgu_b, rwkv_mu, rwkv_w0, rwkv_w2,
           rwkv_a0, rwkv_a2, rwkv_g2, rwkv_k_k, rwkv_k_a, rwkv_r_k, rwkv_gn_g, rwkv_gn_b, e_w_out,
           e_norm_ffn, ffn_w_gate, ffn_w_up, ffn_w_down, o_norm_mix, o_w_in, nsa_cmp_pos_k,
           nsa_cmp_pos_v, nsa_cmp_k_w1, nsa_cmp_k_w2, nsa_cmp_v_w1, nsa_cmp_v_w2, conv_w, o_w_out,
           o_norm_ffn, moe_router, moe_router_b, moe_w_gate, moe_w_up, moe_w_down, final_norm):
    bsz, seq, d = x.shape
    n = bsz * seq
    h = x.reshape(n, d)

    ya, scan_ops, g = _layer0_front(h, seq, e_norm_mix[0], e_w_in[0].astype(BF16), sgu_ln_g[0], sgu_ln_b[0],
                                    sgu_w[0], sgu_b[0], rwkv_mu[0], rwkv_w0[0], rwkv_w2[0], rwkv_a0[0],
                                    rwkv_a2[0], rwkv_g2[0])
    rwka, v = _to_scan_layout(scan_ops, bsz, seq)
    ys = _rwkv_scan(rwka, v, rwkv_k_k[0], rwkv_k_a[0], rwkv_r_k[0], rwkv_gn_g[0], rwkv_gn_b[0], bsz, seq)
    h = _layer0_back(h, ya, _from_scan_layout(ys, bsz, seq), g, e_w_out[0].astype(BF16), e_norm_ffn[0],
                     ffn_w_gate[0].astype(BF16), ffn_w_up[0].astype(BF16), ffn_w_down[0].astype(BF16))

    qn, qr, kcvc, ks, vs, kw, vw, gl, bcd = _in_proj1(h, o_norm_mix[0], o_w_in[0], seq)
    k_cmp, v_cmp = _compress(kcvc, nsa_cmp_pos_k[0], nsa_cmp_pos_v[0], nsa_cmp_k_w1[0], nsa_cmp_v_w1[0],
                             nsa_cmp_k_w2[0], nsa_cmp_v_w2[0], bsz, seq)
    yc = _nsa(qn, qr, gl, k_cmp, v_cmp, ks, vs, kw, vw, bsz, seq).reshape(n, C_WIDTH)
    h, idx, gates, counts = _layer1_mid(h, yc, bcd, conv_w[0], o_w_out[0].astype(BF16), seq, o_norm_ffn[0],
                                        moe_router[0], moe_router_b[0])
    out = _moe_final(h, idx, gates, counts, o_norm_ffn[0], moe_w_gate[0], moe_w_up[0], moe_w_down[0],
                     final_norm)
    return out.reshape(bsz, seq, d)
```
